```python
import math
import jax, jax.numpy as jnp
from jax import lax
import numpy as np

D_MODEL = 2048
BATCH = 4
SEQ = 2048
DEPTH = 2
DEC_BATCH = 8
DEC_SEQ = 1
PAST_LEN = 16384
PAGE_SIZE = 128

HEAD_DIM = 128
D_NSA = D_MODEL // 2
D_SB = D_MODEL - D_NSA
H_NSA = D_NSA // HEAD_DIM
H_SB = D_SB // HEAD_DIM
G_NSA = 2
HPG = H_NSA // G_NSA
D_KV = G_NSA * HEAD_DIM
CMP_BLOCK = 64
SEL_BLOCK = CMP_BLOCK
TOP_N = 16
N_LOCAL = 2
WINDOW = 512
N_BUCKETS = 32
MAX_DISTANCE = 128
D_FF = -(-(8 * D_MODEL) // (3 * 256)) * 256
Q_BLOCK = 128
N_IN = D_NSA + 6 * D_KV + 3 * H_NSA + 3 * D_SB
EPS = 1e-6
FORCE_SCORE = 1e4
NEG_INF = -1e30

kernel_name = 'nsa_stickbreaking_hymba_decode_step'


def rmsnorm(x, g):
    xf = x.astype(jnp.float32)
    y = xf * lax.rsqrt(jnp.mean(xf * xf, axis=-1, keepdims=True) + EPS)
    return (y * g.astype(jnp.float32)).astype(x.dtype)


def masked_softmax(logits, mask):
    logits = jnp.where(mask, logits.astype(jnp.float32), NEG_INF)
    p = jax.nn.softmax(logits, axis=-1)
    return jnp.where(mask, p, 0.0)


def t5_bucket(dist):
    n = jnp.maximum(dist, 0)
    max_exact = N_BUCKETS // 2
    nf = jnp.maximum(n, 1).astype(jnp.float32)
    large = max_exact + (jnp.log(nf / max_exact) / math.log(MAX_DISTANCE / max_exact)
                         * (N_BUCKETS - max_exact)).astype(jnp.int32)
    large = jnp.minimum(large, N_BUCKETS - 1)
    return jnp.where(n < max_exact, n, large)


def project(h, w_in, g_q, g_k):
    lead = h.shape[:-1]
    z = h @ w_in
    sizes = [D_NSA, D_KV, D_KV, D_KV, D_KV, D_KV, D_KV, 3 * H_NSA, D_SB, D_SB, D_SB]
    cuts = np.cumsum(sizes)[:-1].tolist()
    zq, zkc, zvc, zks, zvs, zkw, zvw, zg, zqs, zksb, zvsb = jnp.split(z, cuts, axis=-1)
    kv = lambda t: t.reshape(*lead, G_NSA, HEAD_DIM)
    sb = lambda t: t.reshape(*lead, H_SB, HEAD_DIM)
    q = rmsnorm(zq.reshape(*lead, H_NSA, HEAD_DIM), g_q)
    gates = jax.nn.sigmoid(zg.astype(jnp.float32)).astype(h.dtype).reshape(*lead, 3, H_NSA)
    return (q, gates, kv(zkc), kv(zvc), rmsnorm(kv(zks), g_k[1]), kv(zvs),
            rmsnorm(kv(zkw), g_k[2]), kv(zvw), sb(zqs), sb(zksb), sb(zvsb))


def compress(rows, w, b):
    B, T = rows.shape[:2]
    blk = rows.reshape(B, T // CMP_BLOCK, CMP_BLOCK, G_NSA, HEAD_DIM)
    return jnp.einsum('bnlgd,lde->bnge', blk, w) + b


def nsa_core(q, gates, pos_q, kc, vc, cpos, ks_blk, vs_blk, kw, vw, wpos, rel_bias):
    B, Q = q.shape[:2]
    nb = kc.shape[1]
    scale = HEAD_DIM ** -0.5
    qg = q.reshape(B, Q, G_NSA, HPG, HEAD_DIM)

    def head_bias(dist):
        bias = rel_bias[t5_bucket(dist)]
        return jnp.moveaxis(bias.reshape(*dist.shape, G_NSA, HPG), 1, -1)

    dist_c = pos_q[:, None] - cpos[None, :]
    valid_c = (dist_c >= 0)[:, None, None, :]
    s_c = jnp.einsum('bqghd,bngd->bqghn', qg, kc, preferred_element_type=jnp.float32) * scale
    p_c = masked_softmax(s_c + head_bias(dist_c), valid_c)
    o_c = jnp.einsum('bqghn,bngd->bqghd', p_c.astype(vc.dtype), vc)

    imp = p_c.sum(axis=3)
    blk = jnp.arange(nb)[None, :]
    cur = (pos_q // SEL_BLOCK)[:, None]
    forced = (blk == 0) | (blk >= cur - (N_LOCAL - 1))
    score = jnp.where((blk > cur)[:, None, :], -jnp.inf,
                      jnp.where(forced[:, None, :], FORCE_SCORE, imp))
    n_sel = min(TOP_N, nb)
    top_s, idx = lax.top_k(score, n_sel)
    ok = top_s > -jnp.inf
    ks_t = jnp.moveaxis(ks_blk, 3, 1)
    vs_t = jnp.moveaxis(vs_blk, 3, 1)
    bi = jnp.arange(B)[:, None, None, None]
    gi = jnp.arange(G_NSA)[None, None, :, None]
    k_sel = ks_t[bi, gi, idx]
    v_sel = vs_t[bi, gi, idx]
    kpos = idx[..., None] * SEL_BLOCK + jnp.arange(SEL_BLOCK)
    dist_s = pos_q[None, :, None, None, None] - kpos
    valid_s = ok[..., None] & (dist_s >= 0)
    bias_s = jax.vmap(lambda t, bk: t[bk], in_axes=(1, 2), out_axes=2)(
        rel_bias.reshape(N_BUCKETS, G_NSA, HPG), t5_bucket(dist_s))
    bias_s = jnp.moveaxis(bias_s, -1, 3)
    s_s = jnp.einsum('bqghd,bqgnld->bqghnl', qg, k_sel, preferred_element_type=jnp.float32) * scale + bias_s
    p_s = masked_softmax(s_s.reshape(B, Q, G_NSA, HPG, -1),
                         valid_s.reshape(B, Q, G_NSA, 1, -1)).reshape(s_s.shape)
    o_s = jnp.einsum('bqghnl,bqgnld->bqghd', p_s.astype(v_sel.dtype), v_sel)

    dist_w = pos_q[:, None] - wpos[None, :]
    valid_w = ((dist_w >= 0) & (dist_w <= WINDOW) & (wpos >= 0)[None, :])[:, None, None, :]
    s_w = jnp.einsum('bqghd,bkgd->bqghk', qg, kw, preferred_element_type=jnp.float32) * scale
    p_w = masked_softmax(s_w + head_bias(dist_w), valid_w)
    o_w = jnp.einsum('bqghk,bkgd->bqghd', p_w.astype(vw.dtype), vw)

    g = gates.reshape(B, Q, 3, G_NSA, HPG, 1)
    o = g[:, :, 0] * o_c + g[:, :, 1] * o_s + g[:, :, 2] * o_w
    return o.reshape(B, Q, D_NSA)


def sb_core(q, k, v, q_pos, k_pos):
    B, Q = q.shape[:2]
    z = jnp.einsum('bqhd,bkhd->bhqk', q, k, preferred_element_type=jnp.float32) * (HEAD_DIM ** -0.5)
    causal = k_pos[None, :] < q_pos[:, None]
    log_1m = jnp.where(causal, jax.nn.log_sigmoid(-z), 0.0)
    after = lax.cumsum(log_1m, axis=3, reverse=True) - log_1m
    a = jnp.where(causal, jnp.exp(jax.nn.log_sigmoid(z) + after), 0.0)
    o = jnp.einsum('bhqk,bkhd->bqhd', a.astype(v.dtype), v)
    return o.reshape(B, Q, D_SB)


def layer_tail(x, o, g_out, w_out, norm_ffn, w_gate, w_up, w_down):
    o_n = rmsnorm(o[..., :D_NSA], g_out[:D_NSA])
    o_s = rmsnorm(o[..., D_NSA:], g_out[D_NSA:])
    x = x + jnp.concatenate([o_n, o_s], axis=-1) @ w_out
    h = rmsnorm(x, norm_ffn)
    return x + (jax.nn.silu(h @ w_gate) * (h @ w_up)) @ w_down


def prompt_layer(x, p, rel_bias):
    na, wi, gq, gk, wck, bck, wcv, bcv, go, wo, nf, wg, wu, wd = p
    B, S, _ = x.shape
    h = rmsnorm(x, na)
    q, gates, kc_raw, vc_raw, ks, vs, kw, vw, qs, ksb, vsb = project(h, wi, gq, gk)
    nb = S // CMP_BLOCK
    kc = rmsnorm(compress(kc_raw, wck, bck), gk[0])
    vc = compress(vc_raw, wcv, bcv)
    cpos = jnp.arange(nb) * CMP_BLOCK + (CMP_BLOCK - 1)
    ks_blk = ks.reshape(B, nb, SEL_BLOCK, G_NSA, HEAD_DIM)
    vs_blk = vs.reshape(B, nb, SEL_BLOCK, G_NSA, HEAD_DIM)
    pad_w = ((0, 0), (WINDOW, 0), (0, 0), (0, 0))
    kw_pad = jnp.pad(kw, pad_w)
    vw_pad = jnp.pad(vw, pad_w)
    k_pos = jnp.arange(S)

    def block(i):
        q0 = i * Q_BLOCK
        sl = lambda t: lax.dynamic_slice_in_dim(t, q0, Q_BLOCK, axis=1)
        wl = lambda t: lax.dynamic_slice_in_dim(t, q0, WINDOW + Q_BLOCK, axis=1)
        pos_q = q0 + jnp.arange(Q_BLOCK)
        wpos = q0 - WINDOW + jnp.arange(WINDOW + Q_BLOCK)
        o_n = nsa_core(sl(q), sl(gates), pos_q, kc, vc, cpos, ks_blk, vs_blk,
                       wl(kw_pad), wl(vw_pad), wpos, rel_bias)
        o_s = sb_core(sl(qs), ksb, vsb, pos_q, k_pos)
        return jnp.concatenate([o_n, o_s], axis=-1)

    o = lax.map(block, jnp.arange(S // Q_BLOCK))
    o = jnp.moveaxis(o, 0, 1).reshape(B, S, D_MODEL)
    x = layer_tail(x, o, go, wo, nf, wg, wu, wd)
    w_keep = min(WINDOW, S)
    return (x, jnp.stack([ksb, vsb], axis=2),
            jnp.stack([kc_raw, vc_raw, ks, vs], axis=2),
            jnp.stack([kw, vw], axis=2)[:, S - w_keep:])


def sample_layer(x, p, rel_bias, sb_past, nsa_past, win_buf, past_len):
    na, wi, gq, gk, wck, bck, wcv, bcv, go, wo, nf, wg, wu, wd = p
    B, T, _ = x.shape
    h = rmsnorm(x, na)
    q, gates, kc_raw, vc_raw, ks, vs, kw, vw, qs, ksb, vsb = project(h, wi, gq, gk)
    total = past_len + T
    nb = -(-total // CMP_BLOCK)
    pad = nb * CMP_BLOCK - total

    def with_past(past_rows, new_rows):
        rows = jnp.concatenate([past_rows, new_rows], axis=1)
        return jnp.pad(rows, ((0, 0), (0, pad), (0, 0), (0, 0)))

    kc = rmsnorm(compress(with_past(nsa_past[:, :, 0], kc_raw), wck, bck), gk[0])
    vc = compress(with_past(nsa_past[:, :, 1], vc_raw), wcv, bcv)
    cpos = jnp.arange(nb) * CMP_BLOCK + (CMP_BLOCK - 1)
    ks_blk = with_past(nsa_past[:, :, 2], ks).reshape(B, nb, SEL_BLOCK, G_NSA, HEAD_DIM)
    vs_blk = with_past(nsa_past[:, :, 3], vs).reshape(B, nb, SEL_BLOCK, G_NSA, HEAD_DIM)
    w_buf = win_buf.shape[1]
    kw_all = jnp.concatenate([win_buf[:, :, 0], kw], axis=1)
    vw_all = jnp.concatenate([win_buf[:, :, 1], vw], axis=1)
    wpos = past_len - w_buf + jnp.arange(w_buf + T)
    pos_q = past_len + jnp.arange(T)
    o_n = nsa_core(q, gates, pos_q, kc, vc, cpos, ks_blk, vs_blk, kw_all, vw_all, wpos, rel_bias)
    k_sb = jnp.concatenate([sb_past[:, :, 0], ksb], axis=1)
    v_sb = jnp.concatenate([sb_past[:, :, 1], vsb], axis=1)
    o_s = sb_core(qs, k_sb, v_sb, pos_q, jnp.arange(total))
    o = jnp.concatenate([o_n, o_s], axis=-1)
    x = layer_tail(x, o, go, wo, nf, wg, wu, wd)
    new_win = jnp.concatenate([win_buf, jnp.stack([kw, vw], axis=2)], axis=1)[:, T:]
    return (x, jnp.stack([ksb, vsb], axis=2),
            jnp.stack([kc_raw, vc_raw, ks, vs], axis=2), new_win)


def setup_inputs(seed: int = 0) -> dict:
    key = jax.random.key(seed)
    k = jax.random.split(key, 24)
    f32 = jnp.float32
    n_pages = PAST_LEN // PAGE_SIZE
    n_used = DEC_BATCH * n_pages
    n_pool = n_used + max(1, n_used // 4)
    w_buf = min(WINDOW, PAST_LEN)
    nrm = lambda kk, shape, s=1.0: jax.random.normal(kk, shape, f32) * s
    page_table = jax.random.permutation(k[5], n_pool)[:n_used].reshape(DEC_BATCH, n_pages).astype(jnp.int32)
    return {
        'x_prompt': nrm(k[0], (BATCH, SEQ, D_MODEL)),
        'x_sample': nrm(k[1], (DEC_BATCH, DEC_SEQ, D_MODEL)),
        'cache_sb_kv': nrm(k[2], (DEPTH, n_pool, PAGE_SIZE, 2, H_SB, HEAD_DIM)),
        'cache_nsa_kv': nrm(k[3], (DEPTH, n_pool, PAGE_SIZE, 4, G_NSA, HEAD_DIM)),
        'state_win_kv': nrm(k[4], (DEPTH, DEC_BATCH, w_buf, 2, G_NSA, HEAD_DIM)),
        'page_table': page_table,
        'rel_bias': nrm(k[6], (N_BUCKETS, H_NSA), 0.2),
        'norm_attn': 1.0 + nrm(k[7], (DEPTH, D_MODEL), 0.05),
        'w_in': nrm(k[8], (DEPTH, D_MODEL, N_IN), D_MODEL ** -0.5),
        'g_q': 1.0 + nrm(k[9], (DEPTH, HEAD_DIM), 0.05),
        'g_k': 1.0 + nrm(k[10], (DEPTH, 3, HEAD_DIM), 0.05),
        'w_cmp_k': nrm(k[11], (DEPTH, CMP_BLOCK, HEAD_DIM, HEAD_DIM), (CMP_BLOCK * HEAD_DIM) ** -0.5),
        'b_cmp_k': nrm(k[12], (DEPTH, HEAD_DIM), 0.01),
        'w_cmp_v': nrm(k[13], (DEPTH, CMP_BLOCK, HEAD_DIM, HEAD_DIM), (CMP_BLOCK * HEAD_DIM) ** -0.5),
        'b_cmp_v': nrm(k[14], (DEPTH, HEAD_DIM), 0.01),
        'g_out': 1.0 + nrm(k[15], (DEPTH, D_MODEL), 0.05),
        'w_out': nrm(k[16], (DEPTH, D_MODEL, D_MODEL), D_MODEL ** -0.5),
        'norm_ffn': 1.0 + nrm(k[17], (DEPTH, D_MODEL), 0.05),
        'w_gate': nrm(k[18], (DEPTH, D_MODEL, D_FF), D_MODEL ** -0.5),
        'w_up': nrm(k[19], (DEPTH, D_MODEL, D_FF), D_MODEL ** -0.5),
        'w_down': nrm(k[20], (DEPTH, D_FF, D_MODEL), D_FF ** -0.5),
    }


def reference(x_prompt, x_sample, cache_sb_kv, cache_nsa_kv, state_win_kv, page_table,
              rel_bias, norm_attn, w_in, g_q, g_k, w_cmp_k, b_cmp_k, w_cmp_v, b_cmp_v,
              g_out, w_out, norm_ffn, w_gate, w_up, w_down):
    db, n_pages = page_table.shape
    past_len = n_pages * cache_sb_kv.shape[2]
    yp, ys = x_prompt, x_sample
    sb_p, sb_s, nsa_p, nsa_s, win_p, win_s = [], [], [], [], [], []
    for l in range(DEPTH):
        p = (norm_attn[l], w_in[l], g_q[l], g_k[l], w_cmp_k[l], b_cmp_k[l], w_cmp_v[l],
             b_cmp_v[l], g_out[l], w_out[l], norm_ffn[l], w_gate[l], w_up[l], w_down[l])
        yp, a, b, c = prompt_layer(yp, p, rel_bias)
        sb_p.append(a); nsa_p.append(b); win_p.append(c)
        sb_past = cache_sb_kv[l, page_table].reshape(db, past_len, 2, H_SB, HEAD_DIM)
        nsa_past = cache_nsa_kv[l, page_table].reshape(db, past_len, 4, G_NSA, HEAD_DIM)
        ys, a, b, c = sample_layer(ys, p, rel_bias, sb_past, nsa_past, state_win_kv[l], past_len)
        sb_s.append(a); nsa_s.append(b); win_s.append(c)
    return (yp, ys, jnp.stack(sb_p), jnp.stack(sb_s), jnp.stack(nsa_p), jnp.stack(nsa_s),
            jnp.stack(win_p), jnp.stack(win_s))
```

```python
import functools
import math

import jax
import jax.numpy as jnp
from jax import lax
from jax.experimental import pallas as pl
from jax.experimental.pallas import tpu as pltpu

F32 = jnp.float32
BF16 = jnp.bfloat16
I32 = jnp.int32

LANE = 128
HEAD_DIM = 128
G_NSA = 2
HPG = 4
H_NSA = G_NSA * HPG
H_SB = 8
D_NSA = H_NSA * HEAD_DIM
D_SB = H_SB * HEAD_DIM
D_MODEL = D_NSA + D_SB
D_KV = G_NSA * HEAD_DIM
CMP_BLOCK = 64
TOP_N = 16
N_LOCAL = 2
WINDOW = 512
N_BUCKETS = 32
MAX_DISTANCE = 128
EPS = 1e-6
FORCE_SCORE = 1e4
NEG_INF = -1e30
SCALE = HEAD_DIM ** -0.5

COL_TILE = 512
N_MAIN = D_NSA + 6 * D_KV + 3 * D_SB
QS_BLK = (D_NSA + 6 * D_KV) // LANE
KSB_BLK = QS_BLK + D_SB // LANE
VSB_BLK = KSB_BLK + D_SB // LANE
KS_BLK = (D_NSA + 2 * D_KV) // LANE
VS_BLK = KS_BLK + G_NSA
KW_BLK = VS_BLK + G_NSA
VW_BLK = KW_BLK + G_NSA
VMEM_LIMIT = 56 * 1024 * 1024

SB_TQ = 256
SB_TK = 256
NSA_TQ = 128
SB_PAGES_PER_STEP = 8
CMP_PAGES_PER_STEP = 16


def _cparams(n_axes):
    return pltpu.CompilerParams(dimension_semantics=("arbitrary",) * n_axes,
                                vmem_limit_bytes=VMEM_LIMIT)


def _dot(a, b):
    return jnp.dot(a, b, preferred_element_type=F32)


def _dot_nt(a, b):
    return lax.dot_general(a, b, (((1,), (1,)), ((), ())), preferred_element_type=F32)


def _split_bf16(x):
    hi = x.astype(BF16)
    lo = (x - hi.astype(F32)).astype(BF16)
    return hi, lo


def _dot_split(x, u):
    hi, lo = _split_bf16(x)
    return _dot(hi, u) + _dot(lo, u)


def _rms(y, gain):
    ms = jnp.mean(y * y, axis=-1, keepdims=True)
    return y * lax.rsqrt(ms + EPS) * gain


def _neg_softplus(z):
    return -(jnp.maximum(z, 0.0) + jnp.log(1.0 + jnp.exp(-jnp.abs(z))))


def _t5_bucket(dist):
    n = jnp.maximum(dist, 0)
    max_exact = N_BUCKETS // 2
    nf = jnp.maximum(n, 1).astype(F32)
    large = max_exact + (jnp.log(nf / max_exact) / math.log(MAX_DISTANCE / max_exact)
                         * (N_BUCKETS - max_exact)).astype(I32)
    large = jnp.minimum(large, N_BUCKETS - 1)
    return jnp.where(n < max_exact, n, large)


def _pick_tile(m, candidates):
    for t in candidates:
        if m % t == 0:
            return t
    raise ValueError(f"no tile for {m}")


def _bias_body(rb_ref, toep_ref, bc_ref, sc_ref, sw_ref, *, seq, past_len, w_buf):
    h = pl.program_id(0)

    def table(dist):
        bk = _t5_bucket(dist)
        out = jnp.zeros(dist.shape, F32)
        for b in range(N_BUCKETS):
            out = jnp.where(bk == b, rb_ref[b, h], out)
        return out

    r = lax.broadcasted_iota(I32, (NSA_TQ, 2 * NSA_TQ), 0)
    c = lax.broadcasted_iota(I32, (NSA_TQ, 2 * NSA_TQ), 1)
    toep_ref[...] = table(r + NSA_TQ - c)
    t = lax.broadcasted_iota(I32, (seq, LANE), 0)
    n = lax.broadcasted_iota(I32, (seq, LANE), 1)
    bc_ref[...] = table(t - CMP_BLOCK * n - (CMP_BLOCK - 1))
    n1 = lax.broadcasted_iota(I32, sc_ref.shape, 1)
    sc_ref[...] = table(past_len - CMP_BLOCK * n1 - (CMP_BLOCK - 1))
    j1 = lax.broadcasted_iota(I32, sw_ref.shape, 1)
    sw_ref[...] = table(w_buf - j1)


def _bias_tables(rel_bias, seq, past_len, w_buf, nbp_s, wp_s):
    body = functools.partial(_bias_body, seq=seq, past_len=past_len, w_buf=w_buf)
    return pl.pallas_call(
        body,
        grid=(H_NSA,),
        in_specs=[pl.BlockSpec(memory_space=pltpu.SMEM)],
        out_specs=[
            pl.BlockSpec((None, NSA_TQ, 2 * NSA_TQ), lambda h: (h, 0, 0)),
            pl.BlockSpec((None, seq, LANE), lambda h: (h, 0, 0)),
            pl.BlockSpec((None, 1, nbp_s), lambda h: (h, 0, 0)),
            pl.BlockSpec((None, 1, wp_s), lambda h: (h, 0, 0)),
        ],
        out_shape=[
            jax.ShapeDtypeStruct((H_NSA, NSA_TQ, 2 * NSA_TQ), F32),
            jax.ShapeDtypeStruct((H_NSA, seq, LANE), F32),
            jax.ShapeDtypeStruct((H_NSA, 1, nbp_s), F32),
            jax.ShapeDtypeStruct((H_NSA, 1, wp_s), F32),
        ],
        compiler_params=_cparams(1),
        name="bias_tables",
    )(rel_bias)


def _inproj_body(x_ref, na_ref, w_ref, wg_ref, gain_ref, zf_ref, zb_ref, gate_ref, h_ref):
    j = pl.program_id(1)

    @pl.when(j == 0)
    def _():
        h = _rms(x_ref[...], na_ref[...]).astype(BF16)
        h_ref[...] = h
        gate_ref[...] = jax.nn.sigmoid(_dot(h, wg_ref[...]))

    acc = _dot(h_ref[...], w_ref[...])
    gain = gain_ref[...]

    def emit(norm_chunks):
        for c in range(COL_TILE // LANE):
            sl = slice(c * LANE, (c + 1) * LANE)
            y = acc[:, sl]
            if c in norm_chunks:
                y = _rms(y, gain[:, sl])
            zf_ref[:, sl] = y
            zb_ref[:, sl] = y.astype(BF16)

    q_tiles = D_NSA // COL_TILE

    @pl.when(j < q_tiles)
    def _():
        emit((0, 1, 2, 3))

    @pl.when((j == q_tiles + 1) | (j == q_tiles + 2))
    def _():
        emit((0, 1))

    @pl.when((j == q_tiles) | (j > q_tiles + 2))
    def _():
        emit(())


def _inproj(x2d, na, w_main, w_gate, gain_cols):
    m = x2d.shape[0]
    tm = _pick_tile(m, (512, 256, 128, 16))
    return pl.pallas_call(
        _inproj_body,
        grid=(m // tm, N_MAIN // COL_TILE),
        in_specs=[
            pl.BlockSpec((tm, D_MODEL), lambda i, j: (i, 0)),
            pl.BlockSpec((1, D_MODEL), lambda i, j: (0, 0)),
            pl.BlockSpec((D_MODEL, COL_TILE), lambda i, j: (0, j)),
            pl.BlockSpec((D_MODEL, 2 * LANE), lambda i, j: (0, 0)),
            pl.BlockSpec((1, COL_TILE), lambda i, j: (0, j)),
        ],
        out_specs=[
            pl.BlockSpec((tm, COL_TILE), lambda i, j: (i, j)),
            pl.BlockSpec((tm, COL_TILE), lambda i, j: (i, j)),
            pl.BlockSpec((tm, 2 * LANE), lambda i, j: (i, 0)),
        ],
        out_shape=[
            jax.ShapeDtypeStruct((m, N_MAIN), F32),
            jax.ShapeDtypeStruct((m, N_MAIN), BF16),
            jax.ShapeDtypeStruct((m, 2 * LANE), F32),
        ],
        scratch_shapes=[pltpu.VMEM((tm, D_MODEL), BF16)],
        compiler_params=_cparams(2),
        name="inproj",
    )(x2d, na, w_main, w_gate, gain_cols)


def _outproj_body(on_ref, os_ref, go_ref, w_ref, x_ref, y_ref, h_ref):
    j = pl.program_id(1)

    @pl.when(j == 0)
    def _():
        go = go_ref[...]
        h_ref[:, :D_NSA] = _rms(on_ref[...], go[:, :D_NSA]).astype(BF16)
        h_ref[:, D_NSA:] = _rms(os_ref[...], go[:, D_NSA:]).astype(BF16)

    y_ref[...] = x_ref[...] + _dot(h_ref[...], w_ref[...])


def _outproj(o_nsa, o_sb, g_out, w_out, x2d):
    m = x2d.shape[0]
    tm = _pick_tile(m, (512, 256, 128, 16))
    return pl.pallas_call(
        _outproj_body,
        grid=(m // tm, D_MODEL // COL_TILE),
        in_specs=[
            pl.BlockSpec((tm, D_NSA), lambda i, j: (i, 0)),
            pl.BlockSpec((tm, D_SB), lambda i, j: (i, 0)),
            pl.BlockSpec((1, D_MODEL), lambda i, j: (0, 0)),
            pl.BlockSpec((D_MODEL, COL_TILE), lambda i, j: (0, j)),
            pl.BlockSpec((tm, COL_TILE), lambda i, j: (i, j)),
        ],
        out_specs=pl.BlockSpec((tm, COL_TILE), lambda i, j: (i, j)),
        out_shape=jax.ShapeDtypeStruct((m, D_MODEL), F32),
        scratch_shapes=[pltpu.VMEM((tm, D_MODEL), BF16)],
        compiler_params=_cparams(2),
        name="outproj",
    )(o_nsa, o_sb, g_out, w_out, x2d)


def _ffn_body(x_ref, nf_ref, wg_ref, wu_ref, wd_ref, y_ref, h_ref, acc_ref):
    j = pl.program_id(1)

    @pl.when(j == 0)
    def _():
        h_ref[...] = _rms(x_ref[...], nf_ref[...]).astype(BF16)
        acc_ref[...] = jnp.zeros_like(acc_ref)

    h = h_ref[...]
    act = jax.nn.silu(_dot(h, wg_ref[...])) * _dot(h, wu_ref[...])
    acc_ref[...] += _dot(act.astype(BF16), wd_ref[...])

    @pl.when(j == pl.num_programs(1) - 1)
    def _():
        y_ref[...] = x_ref[...] + acc_ref[...]


def _ffn(x2d, norm_ffn, w_gate, w_up, w_down):
    m = x2d.shape[0]
    d_ff = w_gate.shape[1]
    tm = _pick_tile(m, (512, 256, 128, 16))
    tf = _pick_tile(d_ff, (512, 256))
    return pl.pallas_call(
        _ffn_body,
        grid=(m // tm, d_ff // tf),
        in_specs=[
            pl.BlockSpec((tm, D_MODEL), lambda i, j: (i, 0)),
            pl.BlockSpec((1, D_MODEL), lambda i, j: (0, 0)),
            pl.BlockSpec((D_MODEL, tf), lambda i, j: (0, j)),
            pl.BlockSpec((D_MODEL, tf), lambda i, j: (0, j)),
            pl.BlockSpec((tf, D_MODEL), lambda i, j: (j, 0)),
        ],
        out_specs=pl.BlockSpec((tm, D_MODEL), lambda i, j: (i, 0)),
        out_shape=jax.ShapeDtypeStruct((m, D_MODEL), F32),
        scratch_shapes=[pltpu.VMEM((tm, D_MODEL), BF16), pltpu.VMEM((tm, D_MODEL), F32)],
        compiler_params=_cparams(2),
        name="ffn",
    )(x2d, norm_ffn, w_gate, w_up, w_down)


def _sb_body(q_ref, k_ref, v_ref, u_ref, o_ref, acc_ref, run_ref):
    i = pl.program_id(2)
    q = q_ref[...]
    acc_ref[...] = jnp.zeros_like(acc_ref)
    run_ref[...] = jnp.zeros_like(run_ref)
    row = lax.broadcasted_iota(I32, (SB_TQ, SB_TK), 0)
    col = lax.broadcasted_iota(I32, (SB_TQ, SB_TK), 1)

    def step(it, carry):
        kb = i - it
        off = pl.multiple_of(kb * SB_TK, SB_TK)
        k = k_ref[pl.ds(off, SB_TK), :]
        v = v_ref[pl.ds(off, SB_TK), :]
        z = _dot_nt(q, k) * SCALE
        mask = (col < row) | (it > 0)
        log_1m = jnp.where(mask, _neg_softplus(z), 0.0)
        cs = _dot_split(log_1m, u_ref[...])
        run = run_ref[...]
        after = cs[:, :SB_TK] + jnp.concatenate([run] * (SB_TK // LANE), axis=1)
        a = jnp.where(mask, jnp.exp(z + log_1m + after), 0.0)
        acc_ref[...] += _dot(a.astype(BF16), v)
        run_ref[...] = run + cs[:, SB_TK:]
        return carry

    lax.fori_loop(0, i + 1, step, 0)
    o_ref[...] = acc_ref[...]


def _sb_prompt(zb, batch, seq):
    nq = seq // SB_TQ
    r = lax.broadcasted_iota(I32, (SB_TK, SB_TK + LANE), 0)
    c = lax.broadcasted_iota(I32, (SB_TK, SB_TK + LANE), 1)
    u = ((r > c) | (c >= SB_TK)).astype(BF16)
    return pl.pallas_call(
        _sb_body,
        grid=(batch, H_SB, nq),
        in_specs=[
            pl.BlockSpec((SB_TQ, HEAD_DIM), lambda b, h, i: (b * nq + i, QS_BLK + h)),
            pl.BlockSpec((seq, HEAD_DIM), lambda b, h, i: (b, KSB_BLK + h)),
            pl.BlockSpec((seq, HEAD_DIM), lambda b, h, i: (b, VSB_BLK + h)),
            pl.BlockSpec((SB_TK, SB_TK + LANE), lambda b, h, i: (0, 0)),
        ],
        out_specs=pl.BlockSpec((SB_TQ, HEAD_DIM), lambda b, h, i: (b * nq + i, h)),
        out_shape=jax.ShapeDtypeStruct((batch * seq, D_SB), F32),
        scratch_shapes=[pltpu.VMEM((SB_TQ, HEAD_DIM), F32), pltpu.VMEM((SB_TQ, LANE), F32)],
        compiler_params=_cparams(3),
        name="sb_prompt",
    )(zb, zb, zb, u)


def _cmp_prompt_body(x0_ref, x1_ref, x2_ref, x3_ref, wk_ref, wv_ref, bk_ref, bv_ref, gk_ref,
                     kc_ref, vc_ref, *, nb):
    kc_ref[...] = jnp.zeros_like(kc_ref)
    vc_ref[...] = jnp.zeros_like(vc_ref)
    for c, x_ref in enumerate((x0_ref, x1_ref, x2_ref, x3_ref)):
        rows = jnp.concatenate(
            [x_ref[pl.ds(l, nb, stride=CMP_BLOCK), :] for l in range(CMP_BLOCK)], axis=1)
        rows = rows.astype(BF16)
        if c < G_NSA:
            y = _rms(_dot(rows, wk_ref[...]) + bk_ref[...], gk_ref[...])
            kc_ref[c, 0:nb, :] = y.astype(BF16)
        else:
            y = _dot(rows, wv_ref[...]) + bv_ref[...]
            vc_ref[c - G_NSA, 0:nb, :] = y.astype(BF16)


def _cmp_prompt(zf, batch, seq, wk, wv, bk, bv, gk0):
    nb = seq // CMP_BLOCK
    body = functools.partial(_cmp_prompt_body, nb=nb)
    blk = D_NSA // LANE
    shape = jax.ShapeDtypeStruct((batch, G_NSA, LANE, HEAD_DIM), BF16)

    def x_spec(c):
        return pl.BlockSpec((seq, HEAD_DIM), lambda b: (b, blk + c))

    return pl.pallas_call(
        body,
        grid=(batch,),
        in_specs=[
            x_spec(0), x_spec(1), x_spec(2), x_spec(3),
            pl.BlockSpec(wk.shape, lambda b: (0, 0)),
            pl.BlockSpec(wv.shape, lambda b: (0, 0)),
            pl.BlockSpec((1, HEAD_DIM), lambda b: (0, 0)),
            pl.BlockSpec((1, HEAD_DIM), lambda b: (0, 0)),
            pl.BlockSpec((1, HEAD_DIM), lambda b: (0, 0)),
        ],
        out_specs=[pl.BlockSpec((None, G_NSA, LANE, HEAD_DIM), lambda b: (b, 0, 0, 0))] * 2,
        out_shape=[shape, shape],
        compiler_params=_cparams(1),
        name="cmp_prompt",
    )(zf, zf, zf, zf, wk, wv, bk, bv, gk0)


def _masked_softmax(s, valid):
    logit = jnp.where(valid, s, NEG_INF)
    e = jnp.exp(logit - jnp.max(logit, axis=-1, keepdims=True))
    return jnp.where(valid, e / jnp.sum(e, axis=-1, keepdims=True), 0.0)


class _Flash:
    def __init__(self, m_ref, l_ref, a_ref):
        self.m_ref, self.l_ref, self.a_ref = m_ref, l_ref, a_ref

    def first(self, logit, v):
        m = jnp.max(logit, axis=-1, keepdims=True)
        p = jnp.exp(logit - m)
        self.m_ref[...] = m
        self.l_ref[...] = jnp.sum(p, axis=-1, keepdims=True)
        self.a_ref[...] = _dot(p.astype(BF16), v)

    def update(self, logit, v):
        m_old = self.m_ref[...]
        m = jnp.maximum(m_old, jnp.max(logit, axis=-1, keepdims=True))
        alpha = jnp.exp(m_old - m)
        p = jnp.exp(logit - m)
        self.m_ref[...] = m
        self.l_ref[...] = alpha * self.l_ref[...] + jnp.sum(p, axis=-1, keepdims=True)
        self.a_ref[...] = alpha * self.a_ref[...] + _dot(p.astype(BF16), v)

    def result(self):
        return self.a_ref[...] / self.l_ref[...]


def _nsa_body(rb_ref, q_ref, kc_ref, vc_ref, ks_ref, vs_ref, kw_ref, vw_ref, g_ref, toep_ref,
              bc_ref, e_ref, o_ref, m_s, l_s, a_s, m_w, l_w, a_w, mask_ref, *, nb, seq):
    g = pl.program_id(1)
    i = pl.program_id(2)
    tq = NSA_TQ
    rows = HPG * tq
    q0 = i * tq
    qs = jnp.concatenate([q_ref[:, hh * LANE:(hh + 1) * LANE] for hh in range(HPG)], axis=0)
    r1 = lax.broadcasted_iota(I32, (tq, LANE), 0)
    c1 = lax.broadcasted_iota(I32, (tq, LANE), 1)
    rq = jnp.concatenate([r1] * HPG, axis=0)
    cq = lax.broadcasted_iota(I32, (rows, LANE), 1)

    s_c = _dot_nt(qs, kc_ref[...]) * SCALE + bc_ref[...].reshape(rows, LANE)
    valid_c = (CMP_BLOCK * cq + (CMP_BLOCK - 1) <= q0 + rq) & (cq < nb)
    p_c = _masked_softmax(s_c, valid_c)
    o_c = _dot(p_c.astype(BF16), vc_ref[...])
    imp = p_c[0:tq]
    for hh in range(1, HPG):
        imp = imp + p_c[hh * tq:(hh + 1) * tq]

    nbr = -(-nb // 8) * 8
    imp_t = imp.T[0:nbr]
    n_i = lax.broadcasted_iota(I32, (nbr, LANE), 0)
    cur = (q0 + lax.broadcasted_iota(I32, (nbr, LANE), 1)) // CMP_BLOCK
    forced = (n_i == 0) | (n_i >= cur - (N_LOCAL - 1))
    score = jnp.where(n_i > cur, -jnp.inf, jnp.where(forced, FORCE_SCORE, imp_t))
    rank = jnp.zeros((nbr, LANE), I32)
    for j in range(nb):
        sj = jnp.broadcast_to(score[j:j + 1, :], (nbr, LANE))
        ahead = (sj > score) | ((sj == score) & (n_i > j))
        rank = rank + ahead.astype(I32)
    sel_t = ((rank < min(TOP_N, nb)) & (score > -jnp.inf)).astype(F32)
    if nbr < LANE:
        sel_t = jnp.concatenate([sel_t, jnp.zeros((LANE - nbr, LANE), F32)], axis=0)
    sel = sel_t.T.astype(BF16)
    for kb in range(seq // LANE):
        mask_ref[kb] = _dot(sel, e_ref[:, kb * LANE:(kb + 1) * LANE])

    def stack_mask(kb):
        mk = mask_ref[kb] > 0.5
        return jnp.concatenate([mk] * HPG, axis=0)

    bias_diag = toep_ref[:, :, tq:2 * tq].reshape(rows, LANE)
    bias_prev = toep_ref[:, :, 0:tq].reshape(rows, LANE)
    bias_far = jnp.concatenate(
        [jnp.full((tq, 1), rb_ref[N_BUCKETS - 1, g * HPG + hh], F32) for hh in range(HPG)], axis=0)
    causal = cq <= rq

    def tile(ref, kb):
        return ref[pl.ds(pl.multiple_of(kb * LANE, LANE), LANE), :]

    fs = _Flash(m_s, l_s, a_s)
    s = _dot_nt(qs, tile(ks_ref, i)) * SCALE + bias_diag
    fs.first(jnp.where(stack_mask(i) & causal, s, NEG_INF), tile(vs_ref, i))

    @pl.when(i >= 1)
    def _():
        s = _dot_nt(qs, tile(ks_ref, i - 1)) * SCALE + bias_prev
        fs.update(jnp.where(stack_mask(i - 1), s, NEG_INF), tile(vs_ref, i - 1))

    def far_step(kb, carry):
        s = _dot_nt(qs, tile(ks_ref, kb)) * SCALE + bias_far
        fs.update(jnp.where(stack_mask(kb), s, NEG_INF), tile(vs_ref, kb))
        return carry

    lax.fori_loop(0, jnp.maximum(i - 1, 0), far_step, 0)

    fw = _Flash(m_w, l_w, a_w)
    s = _dot_nt(qs, tile(kw_ref, i)) * SCALE + bias_diag
    fw.first(jnp.where(causal, s, NEG_INF), tile(vw_ref, i))

    @pl.when(i >= 1)
    def _():
        s = _dot_nt(qs, tile(kw_ref, i - 1)) * SCALE + bias_prev
        fw.update(s, tile(vw_ref, i - 1))

    for d in range(2, WINDOW // tq):
        @pl.when(i >= d)
        def _():
            s = _dot_nt(qs, tile(kw_ref, i - d)) * SCALE + bias_far
            fw.update(s, tile(vw_ref, i - d))

    @pl.when(i >= WINDOW // tq)
    def _():
        kb = i - WINDOW // tq
        s = _dot_nt(qs, tile(kw_ref, kb)) * SCALE + bias_far
        fw.update(jnp.where(cq >= rq, s, NEG_INF), tile(vw_ref, kb))

    o_s = fs.result()
    o_w = fw.result()
    for hh in range(HPG):
        sl = slice(hh * tq, (hh + 1) * tq)
        o = (g_ref[:, hh:hh + 1] * o_c[sl]
             + g_ref[:, HPG + hh:HPG + hh + 1] * o_s[sl]
             + g_ref[:, 2 * HPG + hh:2 * HPG + hh + 1] * o_w[sl])
        o_ref[:, hh * LANE:(hh + 1) * LANE] = o


def _nsa_prompt(rel_bias, zb, kc, vc, gates, toep, bias_c, batch, seq):
    assert WINDOW % NSA_TQ == 0 and NSA_TQ == LANE
    nq = seq // NSA_TQ
    nb = seq // CMP_BLOCK
    assert nb <= LANE
    rows = HPG * NSA_TQ
    n = lax.broadcasted_iota(I32, (LANE, seq), 0)
    k = lax.broadcasted_iota(I32, (LANE, seq), 1)
    expand = (k // CMP_BLOCK == n).astype(BF16)
    body = functools.partial(_nsa_body, nb=nb, seq=seq)

    def kv_spec(blk):
        return pl.BlockSpec((seq, HEAD_DIM), lambda b, g, i: (b, blk + g))

    return pl.pallas_call(
        body,
        grid=(batch, G_NSA, nq),
        in_specs=[
            pl.BlockSpec(memory_space=pltpu.SMEM),
            pl.BlockSpec((NSA_TQ, HPG * HEAD_DIM), lambda b, g, i: (b * nq + i, g)),
            pl.BlockSpec((None, None, LANE, HEAD_DIM), lambda b, g, i: (b, g, 0, 0)),
            pl.BlockSpec((None, None, LANE, HEAD_DIM), lambda b, g, i: (b, g, 0, 0)),
            kv_spec(KS_BLK), kv_spec(VS_BLK), kv_spec(KW_BLK), kv_spec(VW_BLK),
            pl.BlockSpec((NSA_TQ, LANE), lambda b, g, i: (b * nq + i, g)),
            pl.BlockSpec((HPG, NSA_TQ, 2 * NSA_TQ), lambda b, g, i: (g, 0, 0)),
            pl.BlockSpec((HPG, NSA_TQ, LANE), lambda b, g, i: (g, i, 0)),
            pl.BlockSpec((LANE, seq), lambda b, g, i: (0, 0)),
        ],
        out_specs=pl.BlockSpec((NSA_TQ, HPG * HEAD_DIM), lambda b, g, i: (b * nq + i, g)),
        out_shape=jax.ShapeDtypeStruct((batch * seq, D_NSA), F32),
        scratch_shapes=[
            pltpu.VMEM((rows, 1), F32), pltpu.VMEM((rows, 1), F32), pltpu.VMEM((rows, HEAD_DIM), F32),
            pltpu.VMEM((rows, 1), F32), pltpu.VMEM((rows, 1), F32), pltpu.VMEM((rows, HEAD_DIM), F32),
            pltpu.VMEM((seq // LANE, NSA_TQ, LANE), F32),
        ],
        compiler_params=_cparams(3),
        name="nsa_prompt",
    )(rel_bias, zb, kc, vc, zb, zb, zb, zb, gates, toep, bias_c, expand)


def _prep_weights(l, w_in, g_q, g_k, w_cmp_k, w_cmp_v, w_out, w_gate, w_up, w_down):
    n_gate = 3 * H_NSA
    g0 = D_NSA + 6 * D_KV
    wi = w_in[l]
    w_main = jnp.concatenate([wi[:, :g0], wi[:, g0 + n_gate:]], axis=1).astype(BF16)
    wg = wi[:, g0:g0 + n_gate].reshape(D_MODEL, 3, G_NSA, HPG)
    wg = jnp.transpose(wg, (0, 2, 1, 3)).reshape(D_MODEL, G_NSA, 3 * HPG)
    wg = jnp.pad(wg, ((0, 0), (0, 0), (0, LANE - 3 * HPG))).reshape(D_MODEL, G_NSA * LANE)
    ones = jnp.ones((HEAD_DIM,), F32)
    gain = jnp.concatenate(
        [jnp.tile(g_q[l], H_NSA), jnp.tile(ones, 2 * G_NSA),
         jnp.tile(g_k[l, 1], G_NSA), jnp.tile(ones, G_NSA),
         jnp.tile(g_k[l, 2], G_NSA), jnp.tile(ones, G_NSA),
         jnp.tile(ones, 3 * H_SB)])[None, :]
    wk = w_cmp_k[l].reshape(CMP_BLOCK * HEAD_DIM, HEAD_DIM).astype(BF16)
    wv = w_cmp_v[l].reshape(CMP_BLOCK * HEAD_DIM, HEAD_DIM).astype(BF16)
    return dict(
        w_main=w_main, w_gate=wg.astype(BF16), gain=gain, wk=wk, wv=wv,
        w_kv=jnp.concatenate([wk, wv], axis=1),
        w_out=w_out[l].astype(BF16), w_gate_ffn=w_gate[l].astype(BF16),
        w_up=w_up[l].astype(BF16), w_down=w_down[l].astype(BF16))


def _prompt_layer(x2d, wt, p, tables, rel_bias, batch, seq):
    na, bk, bv, gk0, go, nf = p
    zf, zb, gates = _inproj(x2d, na, wt["w_main"], wt["w_gate"], wt["gain"])
    kc, vc = _cmp_prompt(zf, batch, seq, wt["wk"], wt["wv"], bk, bv, gk0)
    o_nsa = _nsa_prompt(rel_bias, zb, kc, vc, gates, tables[0], tables[1], batch, seq)
    o_sb = _sb_prompt(zb, batch, seq)
    x1 = _outproj(o_nsa, o_sb, go, wt["w_out"], x2d)
    x2 = _ffn(x1, nf, wt["w_gate_ffn"], wt["w_up"], wt["w_down"])
    return x2, zf


ROWS_S = 16


def _sb_dec_body(pt_ref, q_ref, *refs, npg, pps, page):
    page_refs = refs[:pps]
    up_ref, upp_ref, o_ref, z_ref, acc_ref = refs[pps:]
    ph = pl.program_id(1)
    c = pl.program_id(2)
    width = page * H_SB
    r = lax.broadcasted_iota(I32, (ROWS_S, width), 0)
    lane = lax.broadcasted_iota(I32, (ROWS_S, width), 1)
    own_head = (lane % H_SB) == r

    @pl.when(ph == 0)
    def _():
        q = q_ref[...]
        for u in range(pps):
            ks = page_refs[u][...].reshape(width, HEAD_DIM).astype(BF16)
            zt = _dot_nt(q, ks)
            z_ref[pl.ds(c * pps + u, 1), :] = jnp.sum(jnp.where(own_head, zt, 0.0), axis=0,
                                                      keepdims=True)

    @pl.when((ph == 1) & (c == 0))
    def _():
        z = z_ref[...] * SCALE
        log_1m = _neg_softplus(z)
        cs = _dot_split(log_1m, up_ref[...])
        hi, lo = _split_bf16(cs[:, width:])
        later_pages = _dot(upp_ref[...], hi) + _dot(upp_ref[...], lo)
        z_ref[...] = jnp.exp(z + log_1m + cs[:, :width] + later_pages)
        acc_ref[...] = jnp.zeros_like(acc_ref)

    @pl.when(ph == 1)
    def _():
        for u in range(pps):
            vs = page_refs[u][...].reshape(width, HEAD_DIM).astype(BF16)
            a = jnp.broadcast_to(z_ref[pl.ds(c * pps + u, 1), :], (ROWS_S, width))
            acc_ref[...] += _dot(jnp.where(own_head, a, 0.0).astype(BF16), vs)

    @pl.when((ph == 1) & (c == pl.num_programs(2) - 1))
    def _():
        o_ref[...] = acc_ref[0:H_SB, :]


def _sb_decode(l, pt_flat, q_sb, cache_sb):
    dec_b = q_sb.shape[0]
    page = cache_sb.shape[2]
    npg = pt_flat.shape[0] // dec_b
    pps = SB_PAGES_PER_STEP
    assert npg % pps == 0 and npg % 8 == 0
    width = page * H_SB
    r = lax.broadcasted_iota(I32, (width, 2 * width), 0)
    c = lax.broadcasted_iota(I32, (width, 2 * width), 1)
    same_head = (r % H_SB) == (c % H_SB)
    up = (same_head & ((c >= width) | (r // H_SB > c // H_SB))).astype(BF16)
    pr = lax.broadcasted_iota(I32, (npg, npg), 0)
    pc = lax.broadcasted_iota(I32, (npg, npg), 1)
    upp = (pc > pr).astype(BF16)
    body = functools.partial(_sb_dec_body, npg=npg, pps=pps, page=page)

    def page_spec(u):
        return pl.BlockSpec((None, None, page, None, H_SB, HEAD_DIM),
                            lambda b, ph, c, pt: (l, pt[b * npg + c * pps + u], 0, ph, 0, 0))

    grid_spec = pltpu.PrefetchScalarGridSpec(
        num_scalar_prefetch=1,
        grid=(dec_b, 2, npg // pps),
        in_specs=[pl.BlockSpec((None, ROWS_S, HEAD_DIM), lambda b, ph, c, pt: (b, 0, 0))]
        + [page_spec(u) for u in range(pps)]
        + [pl.BlockSpec((width, 2 * width), lambda b, ph, c, pt: (0, 0)),
           pl.BlockSpec((npg, npg), lambda b, ph, c, pt: (0, 0))],
        out_specs=pl.BlockSpec((None, H_SB, HEAD_DIM), lambda b, ph, c, pt: (b, 0, 0)),
        scratch_shapes=[pltpu.VMEM((npg, width), F32), pltpu.VMEM((ROWS_S, HEAD_DIM), F32)],
    )
    return pl.pallas_call(
        body,
        grid_spec=grid_spec,
        out_shape=jax.ShapeDtypeStruct((dec_b, H_SB, HEAD_DIM), F32),
        compiler_params=_cparams(3),
        name="sb_decode",
    )(pt_flat, q_sb, *([cache_sb] * pps), up, upp)


NSA_CH = 4 * G_NSA


def _cmp_dec_body(pt_ref, *refs, pps, page):
    page_refs = refs[:pps]
    w_ref, kc_ref, vc_ref, yk_ref, yv_ref = refs[pps:]
    halves = page // CMP_BLOCK
    nblk = pps * halves
    acc = jnp.zeros((nblk * NSA_CH, 2 * HEAD_DIM), F32)
    for l in range(CMP_BLOCK):
        x_l = jnp.concatenate(
            [page_refs[u][pl.ds((hf * CMP_BLOCK + l) * NSA_CH, NSA_CH), :]
             for u in range(pps) for hf in range(halves)], axis=0).astype(BF16)
        acc = acc + _dot(x_l, w_ref[l * HEAD_DIM:(l + 1) * HEAD_DIM, :])
    yk_ref[...] = acc[:, :HEAD_DIM]
    yv_ref[...] = acc[:, HEAD_DIM:]
    for g in range(G_NSA):
        kc_ref[g] = yk_ref[pl.ds(g, nblk, stride=NSA_CH), :]
        vc_ref[g] = yv_ref[pl.ds(G_NSA + g, nblk, stride=NSA_CH), :]


def _cmp_decode(l, pt_flat, cache_nsa2d, w_kv, dec_b, page):
    npg = pt_flat.shape[0] // dec_b
    pps = min(CMP_PAGES_PER_STEP, npg)
    assert npg % pps == 0
    halves = page // CMP_BLOCK
    nblk = pps * halves
    body = functools.partial(_cmp_dec_body, pps=pps, page=page)

    def page_spec(u):
        return pl.BlockSpec((None, None, page * NSA_CH, HEAD_DIM),
                            lambda b, c, pt: (l, pt[b * npg + c * pps + u], 0, 0))

    out_spec = pl.BlockSpec((None, G_NSA, nblk, HEAD_DIM), lambda b, c, pt: (b, 0, c, 0))
    shape = jax.ShapeDtypeStruct((dec_b, G_NSA, npg * halves, HEAD_DIM), F32)
    grid_spec = pltpu.PrefetchScalarGridSpec(
        num_scalar_prefetch=1,
        grid=(dec_b, npg // pps),
        in_specs=[page_spec(u) for u in range(pps)]
        + [pl.BlockSpec(w_kv.shape, lambda b, c, pt: (0, 0))],
        out_specs=[out_spec, out_spec],
        scratch_shapes=[pltpu.VMEM((nblk * NSA_CH, HEAD_DIM), F32)] * 2,
    )
    return pl.pallas_call(
        body,
        grid_spec=grid_spec,
        out_shape=[shape, shape],
        compiler_params=_cparams(2),
        name="cmp_decode",
    )(pt_flat, *([cache_nsa2d] * pps), w_kv)


def _nsa_sel_body(q_ref, kcr_ref, vcr_ref, new_ref, wk0_ref, wv0_ref, bk_ref, bv_ref, gk_ref,
                  sbc_ref, kw_ref, vw_ref, sbw_ref, idx_ref, oc_ref, ow_ref,
                  *, nb0, nbp, past_len, w_buf, wp):
    nb = nb0 + 1
    new = new_ref[...].astype(BF16)
    new_k = _rms(_dot(new, wk0_ref[...]) + bk_ref[...], gk_ref[...])
    new_v = _dot(new, wv0_ref[...]) + bv_ref[...]
    row8 = lax.broadcasted_iota(I32, (8, HEAD_DIM), 0)
    n1 = lax.broadcasted_iota(I32, (1, nbp), 1)
    nq = lax.broadcasted_iota(I32, (ROWS_S, nbp), 1)
    pos_q = past_len
    pad = jnp.zeros((nbp - nb0 - 8, HEAD_DIM), F32)
    for g in range(G_NSA):
        q = q_ref[g]
        kc = jnp.concatenate(
            [_rms(kcr_ref[g] + bk_ref[...], gk_ref[...]),
             jnp.where(row8 == 0, jnp.broadcast_to(new_k[g:g + 1], (8, HEAD_DIM)), 0.0), pad],
            axis=0).astype(BF16)
        vc = jnp.concatenate(
            [vcr_ref[g] + bv_ref[...],
             jnp.where(row8 == 0, jnp.broadcast_to(new_v[G_NSA + g:G_NSA + g + 1], (8, HEAD_DIM)),
                       0.0), pad], axis=0).astype(BF16)
        s_c = _dot_nt(q, kc) * SCALE + sbc_ref[g]
        valid_c = (CMP_BLOCK * nq + (CMP_BLOCK - 1) <= pos_q) & (nq < nb)
        p_c = _masked_softmax(s_c, valid_c)
        oc_ref[g] = _dot(p_c.astype(BF16), vc)
        imp = p_c[0:1]
        for hh in range(1, HPG):
            imp = imp + p_c[hh:hh + 1]

        cur = pos_q // CMP_BLOCK
        forced = (n1 == 0) | (n1 >= cur - (N_LOCAL - 1))
        score = jnp.where(n1 > cur, -jnp.inf, jnp.where(forced, FORCE_SCORE, imp))
        sq = jnp.broadcast_to(score, (LANE, nbp))
        col = jnp.concatenate([sq[:, k * LANE:(k + 1) * LANE].T for k in range(nbp // LANE)],
                              axis=0)
        j_i = lax.broadcasted_iota(I32, (nbp, LANE), 0)
        ranks = []
        for k in range(nbp // LANE):
            mine = jnp.broadcast_to(score[:, k * LANE:(k + 1) * LANE], (nbp, LANE))
            n_i = k * LANE + lax.broadcasted_iota(I32, (nbp, LANE), 1)
            ahead = (col > mine) | ((col == mine) & (j_i < n_i))
            ranks.append(jnp.sum(ahead.astype(F32), axis=0, keepdims=True))
        rank = jnp.concatenate(ranks, axis=1)
        slot = lax.broadcasted_iota(I32, (TOP_N, nbp), 0).astype(F32)
        hit = jnp.broadcast_to(rank, (TOP_N, nbp)) == slot
        ids = jnp.sum(jnp.where(hit, lax.broadcasted_iota(I32, (TOP_N, nbp), 1).astype(F32), 0.0),
                      axis=1, keepdims=True)
        idx_ref[g] = jnp.broadcast_to(ids, (TOP_N, LANE)).astype(I32)

        jw = lax.broadcasted_iota(I32, (ROWS_S, wp), 1)
        dist_w = w_buf - jw
        valid_w = (dist_w >= 0) & (dist_w <= WINDOW) & (past_len - w_buf + jw >= 0)
        s_w = _dot_nt(q, kw_ref[g]) * SCALE + sbw_ref[g]
        p_w = _masked_softmax(s_w, valid_w)
        ow_ref[g] = _dot(p_w.astype(BF16), vw_ref[g])


def _nsa_select(q_nsa, kcr, vcr, new_c, wk0, wv0, bk, bv, gk0, sbc, kw, vw, sbw, past_len, w_buf):
    dec_b, _, nb0, _ = kcr.shape
    nbp = sbc.shape[-1]
    wp = sbw.shape[-1]
    assert nb0 + 1 >= TOP_N and nbp >= nb0 + 8 and nb0 % 8 == 0
    body = functools.partial(_nsa_sel_body, nb0=nb0, nbp=nbp, past_len=past_len, w_buf=w_buf, wp=wp)

    def per_b(shape):
        nd = len(shape)
        return pl.BlockSpec((None,) + tuple(shape), lambda b: (b,) + (0,) * nd)

    def whole(arr):
        nd = arr.ndim
        return pl.BlockSpec(arr.shape, lambda b: (0,) * nd)

    vec = (G_NSA, ROWS_S, HEAD_DIM)
    return pl.pallas_call(
        body,
        grid=(dec_b,),
        in_specs=[per_b(vec), per_b((G_NSA, nb0, HEAD_DIM)), per_b((G_NSA, nb0, HEAD_DIM)),
                  per_b((8, HEAD_DIM)), whole(wk0), whole(wv0), whole(bk), whole(bv), whole(gk0),
                  whole(sbc), per_b((G_NSA, wp, HEAD_DIM)), per_b((G_NSA, wp, HEAD_DIM)), whole(sbw)],
        out_specs=[per_b((G_NSA, TOP_N, LANE)), per_b(vec), per_b(vec)],
        out_shape=[jax.ShapeDtypeStruct((dec_b, G_NSA, TOP_N, LANE), I32),
                   jax.ShapeDtypeStruct((dec_b,) + vec, F32),
                   jax.ShapeDtypeStruct((dec_b,) + vec, F32)],
        compiler_params=_cparams(1),
        name="nsa_select",
    )(q_nsa, kcr, vcr, new_c, wk0, wv0, bk, bv, gk0, sbc, kw, vw, sbw)


def _nsa_gather_body(pt_ref, idx_ref, rb_ref, q_ref, p0_ref, p1_ref, new_ref, gs_ref, oc_ref,
                     ow_ref, o_ref, m_ref, l_ref, a_ref, *, nb0, past_len):
    b = pl.program_id(0)
    k = pl.program_id(1)
    pos_q = past_len
    row = lax.broadcasted_iota(I32, (CMP_BLOCK, HEAD_DIM), 0)
    l1 = lax.broadcasted_iota(I32, (1, CMP_BLOCK), 1)
    hrow = lax.broadcasted_iota(I32, (ROWS_S, CMP_BLOCK), 0)
    for g, page_ref in enumerate((p0_ref, p1_ref)):
        blk = idx_ref[(b * G_NSA + g) * TOP_N + k]
        is_new = blk >= nb0
        new_k = jnp.broadcast_to(new_ref[g:g + 1, :], (CMP_BLOCK, HEAD_DIM))
        new_v = jnp.broadcast_to(new_ref[G_NSA + g:G_NSA + g + 1, :], (CMP_BLOCK, HEAD_DIM))
        ks = page_ref[pl.ds(2 * G_NSA + g, CMP_BLOCK, stride=NSA_CH), :]
        vs = page_ref[pl.ds(3 * G_NSA + g, CMP_BLOCK, stride=NSA_CH), :]
        ks = jnp.where(is_new, jnp.where(row == 0, new_k, 0.0), ks).astype(BF16)
        vs = jnp.where(is_new, jnp.where(row == 0, new_v, 0.0), vs).astype(BF16)
        dist = pos_q - (blk * CMP_BLOCK + l1)
        bucket = jnp.broadcast_to(_t5_bucket(dist), (ROWS_S, CMP_BLOCK))
        bias = jnp.zeros((ROWS_S, CMP_BLOCK), F32)
        for hh in range(HPG):
            for bk in range(N_BUCKETS):
                bias = jnp.where((hrow == hh) & (bucket == bk), rb_ref[bk, g * HPG + hh], bias)
        s = _dot_nt(q_ref[g], ks) * SCALE + bias
        logit = jnp.where(jnp.broadcast_to(dist >= 0, (ROWS_S, CMP_BLOCK)), s, NEG_INF)
        fl = _Flash(m_ref.at[g], l_ref.at[g], a_ref.at[g])

        @pl.when(k == 0)
        def _():
            fl.first(logit, vs)

        @pl.when(k > 0)
        def _():
            fl.update(logit, vs)

        @pl.when(k == pl.num_programs(1) - 1)
        def _():
            o_ref[g] = gs_ref[g, 0] * oc_ref[g] + gs_ref[g, 1] * fl.result() + gs_ref[g, 2] * ow_ref[g]


def _nsa_gather(l, pt_flat, idx_flat, rel_bias, q_nsa, cache_nsa2d, new_s, gs, oc, ow, nb0,
                past_len, page):
    dec_b = q_nsa.shape[0]
    npg = pt_flat.shape[0] // dec_b
    halves = page // CMP_BLOCK
    body = functools.partial(_nsa_gather_body, nb0=nb0, past_len=past_len)

    def page_spec(g):
        def imap(b, k, pt, idx):
            blk = jnp.minimum(idx[(b * G_NSA + g) * TOP_N + k], nb0 - 1)
            return (l, pt[b * npg + blk // halves], blk % halves, 0)
        return pl.BlockSpec((None, None, CMP_BLOCK * NSA_CH, HEAD_DIM), imap)

    def per_b(shape):
        nd = len(shape)
        return pl.BlockSpec((None,) + tuple(shape), lambda b, k, pt, idx: (b,) + (0,) * nd)

    vec = (G_NSA, ROWS_S, HEAD_DIM)
    grid_spec = pltpu.PrefetchScalarGridSpec(
        num_scalar_prefetch=2,
        grid=(dec_b, TOP_N),
        in_specs=[pl.BlockSpec(memory_space=pltpu.SMEM), per_b(vec), page_spec(0), page_spec(1),
                  per_b((8, HEAD_DIM)), per_b((G_NSA, 3, ROWS_S, HEAD_DIM)), per_b(vec), per_b(vec)],
        out_specs=per_b(vec),
        scratch_shapes=[pltpu.VMEM((G_NSA, ROWS_S, 1), F32), pltpu.VMEM((G_NSA, ROWS_S, 1), F32),
                        pltpu.VMEM((G_NSA, ROWS_S, HEAD_DIM), F32)],
    )
    return pl.pallas_call(
        body,
        grid_spec=grid_spec,
        out_shape=jax.ShapeDtypeStruct((dec_b,) + vec, F32),
        compiler_params=_cparams(2),
        name="nsa_gather",
    )(pt_flat, idx_flat, rel_bias, q_nsa, cache_nsa2d, cache_nsa2d, new_s, gs, oc, ow)


def _pad_rows(x, rows):
    return jnp.pad(x, ((0, rows - x.shape[0]),) + ((0, 0),) * (x.ndim - 1))


def _sample_layer(l, xs, wt, p, sb_tabs, rel_bias, pt_flat, cache_sb, cache_nsa2d, win_l, w_cmp0,
                  dec_b, past_len, page):
    na, bk, bv, gk0, go, nf = p
    sbc, sbw = sb_tabs
    w_buf = win_l.shape[1]
    zf, zb, gates = _inproj(xs, na, wt["w_main"], wt["w_gate"], wt["gain"])
    zf8, zb8 = zf[:dec_b], zb[:dec_b]

    q_sb = zb8[:, QS_BLK * LANE:KSB_BLK * LANE].reshape(dec_b, H_SB, HEAD_DIM)
    q_sb = jnp.pad(q_sb, ((0, 0), (0, ROWS_S - H_SB), (0, 0)))
    o_sb = _sb_decode(l, pt_flat, q_sb, cache_sb).reshape(dec_b, D_SB)

    q_nsa = zb8[:, :D_NSA].reshape(dec_b, G_NSA, HPG, HEAD_DIM)
    q_nsa = jnp.pad(q_nsa, ((0, 0), (0, 0), (0, ROWS_S - HPG), (0, 0)))
    kcr, vcr = _cmp_decode(l, pt_flat, cache_nsa2d, wt["w_kv"], dec_b, page)
    new_c = jnp.pad(zf8[:, D_NSA:D_NSA + 2 * D_KV].reshape(dec_b, 2 * G_NSA, HEAD_DIM),
                    ((0, 0), (0, 8 - 2 * G_NSA), (0, 0)))
    new_s = jnp.pad(zf8[:, D_NSA + 2 * D_KV:D_NSA + 4 * D_KV].reshape(dec_b, 2 * G_NSA, HEAD_DIM),
                    ((0, 0), (0, 8 - 2 * G_NSA), (0, 0)))
    new_w = zf8[:, D_NSA + 4 * D_KV:D_NSA + 6 * D_KV].reshape(dec_b, 1, 2, G_NSA, HEAD_DIM)
    win_all = jnp.concatenate([win_l, new_w], axis=1)
    wp = sbw.shape[-1]
    win_t = jnp.transpose(win_all, (2, 0, 3, 1, 4))
    win_t = jnp.pad(win_t, ((0, 0), (0, 0), (0, 0), (0, wp - w_buf - 1), (0, 0))).astype(BF16)
    idx, oc, ow = _nsa_select(q_nsa, kcr, vcr, new_c, w_cmp0[0], w_cmp0[1], bk, bv, gk0, sbc,
                              win_t[0], win_t[1], sbw, past_len, w_buf)
    gs = gates[:dec_b].reshape(dec_b, G_NSA, LANE)[:, :, :3 * HPG].reshape(dec_b, G_NSA, 3, HPG)
    gs = jnp.pad(gs, ((0, 0), (0, 0), (0, 0), (0, ROWS_S - HPG)))
    gs = jnp.broadcast_to(gs[..., None], gs.shape + (HEAD_DIM,))
    o_nsa = _nsa_gather(l, pt_flat, idx[:, :, :, 0].reshape(-1), rel_bias, q_nsa, cache_nsa2d,
                        new_s, gs, oc, ow, kcr.shape[2], past_len, page)
    o_nsa = o_nsa[:, :, :HPG].reshape(dec_b, D_NSA)

    rows = xs.shape[0]
    x1 = _outproj(_pad_rows(o_nsa, rows), _pad_rows(o_sb, rows), go, wt["w_out"], xs)
    x2 = _ffn(x1, nf, wt["w_gate_ffn"], wt["w_up"], wt["w_down"])
    new_win = win_all[:, 1:]
    return x2, zf8, new_win


def kernel(x_prompt, x_sample, cache_sb_kv, cache_nsa_kv, state_win_kv, page_table, rel_bias,
           norm_attn, w_in, g_q, g_k, w_cmp_k, b_cmp_k, w_cmp_v, b_cmp_v, g_out, w_out,
           norm_ffn, w_gate, w_up, w_down):
    batch, seq, _ = x_prompt.shape
    dec_b, dec_t, _ = x_sample.shape
    assert dec_t == 1
    depth, n_pool, page = cache_sb_kv.shape[:3]
    npg = page_table.shape[1]
    past_len = npg * page
    w_buf = state_win_kv.shape[2]
    nb_s = past_len // CMP_BLOCK + 1
    nbp_s = -(-(nb_s + 7) // LANE) * LANE
    wp_s = -(-(w_buf + 1) // LANE) * LANE

    toep, bias_c, sc, sw = _bias_tables(rel_bias, seq, past_len, w_buf, nbp_s, wp_s)
    sbc = jnp.pad(sc.reshape(G_NSA, HPG, nbp_s), ((0, 0), (0, ROWS_S - HPG), (0, 0)))
    sbw = jnp.pad(sw.reshape(G_NSA, HPG, wp_s), ((0, 0), (0, ROWS_S - HPG), (0, 0)))
    pt_flat = page_table.reshape(-1).astype(I32)
    cache_nsa2d = cache_nsa_kv.reshape(depth, n_pool, page * NSA_CH, HEAD_DIM)

    xp = x_prompt.reshape(batch * seq, D_MODEL)
    xs = _pad_rows(x_sample.reshape(dec_b, D_MODEL), ROWS_S)
    sb_p, sb_s, nsa_p, nsa_s, win_p, win_s = [], [], [], [], [], []
    w_keep = min(WINDOW, seq)
    nsa0, nsa1, win1 = D_NSA, D_NSA + 4 * D_KV, D_NSA + 6 * D_KV
    for l in range(depth):
        wt = _prep_weights(l, w_in, g_q, g_k, w_cmp_k, w_cmp_v, w_out, w_gate, w_up, w_down)
        p = (norm_attn[l][None], b_cmp_k[l][None], b_cmp_v[l][None], g_k[l, 0][None],
             g_out[l][None], norm_ffn[l][None])
        xp, zf = _prompt_layer(xp, wt, p, (toep, bias_c), rel_bias, batch, seq)
        sb_p.append(zf[:, KSB_BLK * LANE:].reshape(batch, seq, 2, H_SB, HEAD_DIM))
        nsa_p.append(zf[:, nsa0:nsa1].reshape(batch, seq, 4, G_NSA, HEAD_DIM))
        win_p.append(zf[:, nsa1:win1].reshape(batch, seq, 2, G_NSA, HEAD_DIM)[:, seq - w_keep:])

        w_cmp0 = (w_cmp_k[l, 0].astype(BF16), w_cmp_v[l, 0].astype(BF16))
        xs, zs, new_win = _sample_layer(l, xs, wt, p, (sbc, sbw), rel_bias, pt_flat, cache_sb_kv,
                                        cache_nsa2d, state_win_kv[l], w_cmp0, dec_b, past_len, page)
        sb_s.append(zs[:, KSB_BLK * LANE:].reshape(dec_b, 1, 2, H_SB, HEAD_DIM))
        nsa_s.append(zs[:, nsa0:nsa1].reshape(dec_b, 1, 4, G_NSA, HEAD_DIM))
        win_s.append(new_win)
    return (xp.reshape(batch, seq, D_MODEL), xs[:dec_b].reshape(dec_b, 1, D_MODEL),
            jnp.stack(sb_p), jnp.stack(sb_s), jnp.stack(nsa_p), jnp.stack(nsa_s),
            jnp.stack(win_p), jnp.stack(win_s))
```

```python
import functools
import math

import jax
import jax.numpy as jnp
from jax import lax
from jax.experimental import pallas as pl
from jax.experimental.pallas import tpu as pltpu

F32 = jnp.float32
BF16 = jnp.bfloat16
I32 = jnp.int32

LANE = 128
HEAD_DIM = 128
G_NSA = 2
HPG = 4
H_NSA = G_NSA * HPG
H_SB = 8
D_NSA = H_NSA * HEAD_DIM
D_SB = H_SB * HEAD_DIM
D_MODEL = D_NSA + D_SB
D_KV = G_NSA * HEAD_DIM
CMP_BLOCK = 64
TOP_N = 16
N_LOCAL = 2
WINDOW = 512
N_BUCKETS = 32
MAX_DISTANCE = 128
EPS = 1e-6
FORCE_SCORE = 1e4
NEG_INF = -1e30
SCALE = HEAD_DIM ** -0.5

COL_TILE = 512
N_MAIN = D_NSA + 6 * D_KV + 3 * D_SB
QS_BLK = (D_NSA + 6 * D_KV) // LANE
KSB_BLK = QS_BLK + D_SB // LANE
VSB_BLK = KSB_BLK + D_SB // LANE
KS_BLK = (D_NSA + 2 * D_KV) // LANE
VS_BLK = KS_BLK + G_NSA
KW_BLK = VS_BLK + G_NSA
VW_BLK = KW_BLK + G_NSA
VMEM_LIMIT = 56 * 1024 * 1024

SB_TQ = 256
SB_TK = 256
SB_HB = 4
NSA_TQ = 256
SB_PAGES_PER_STEP = 8
CMP_PAGES_PER_STEP = 16


def _cparams(n_axes):
    return pltpu.CompilerParams(dimension_semantics=("arbitrary",) * n_axes,
                                vmem_limit_bytes=VMEM_LIMIT)


def _dot(a, b):
    return jnp.dot(a, b, preferred_element_type=F32)


def _dot_nt(a, b):
    return lax.dot_general(a, b, (((1,), (1,)), ((), ())), preferred_element_type=F32)


def _split_bf16(x):
    hi = x.astype(BF16)
    lo = (x - hi.astype(F32)).astype(BF16)
    return hi, lo


def _dot_split(x, u):
    hi, lo = _split_bf16(x)
    return _dot(hi, u) + _dot(lo, u)


def _rms(y, gain):
    ms = jnp.mean(y * y, axis=-1, keepdims=True)
    return y * lax.rsqrt(ms + EPS) * gain


def _neg_softplus(z):
    return -(jnp.maximum(z, 0.0) + jnp.log(1.0 + jnp.exp(-jnp.abs(z))))


def _t5_bucket(dist):
    n = jnp.maximum(dist, 0)
    max_exact = N_BUCKETS // 2
    nf = jnp.maximum(n, 1).astype(F32)
    large = max_exact + (jnp.log(nf / max_exact) / math.log(MAX_DISTANCE / max_exact)
                         * (N_BUCKETS - max_exact)).astype(I32)
    large = jnp.minimum(large, N_BUCKETS - 1)
    return jnp.where(n < max_exact, n, large)


def _pick_tile(m, candidates):
    for t in candidates:
        if m % t == 0:
            return t
    raise ValueError(f"no tile for {m}")


def _bias_body(rb_ref, toep_ref, bc_ref, sc_ref, sw_ref, *, seq, past_len, w_buf):
    h = pl.program_id(0)
    tq = NSA_TQ

    def table(dist):
        bk = _t5_bucket(dist)
        out = jnp.zeros(dist.shape, F32)
        for b in range(N_BUCKETS):
            out = jnp.where(bk == b, rb_ref[b, h], out)
        return out

    k = lax.broadcasted_iota(I32, (tq, tq), 0)
    q = lax.broadcasted_iota(I32, (tq, tq), 1)
    toep_ref[0] = table(tq + q - k)
    toep_ref[1] = table(q - k)
    n = lax.broadcasted_iota(I32, (LANE, tq), 0)
    qc = lax.broadcasted_iota(I32, (LANE, tq), 1)
    for i in range(seq // tq):
        bc_ref[i] = table(i * tq + qc - CMP_BLOCK * n - (CMP_BLOCK - 1))
    n1 = lax.broadcasted_iota(I32, sc_ref.shape, 1)
    sc_ref[...] = table(past_len - CMP_BLOCK * n1 - (CMP_BLOCK - 1))
    j1 = lax.broadcasted_iota(I32, sw_ref.shape, 1)
    sw_ref[...] = table(w_buf - j1)


def _bias_tables(rel_bias, seq, past_len, w_buf, nbp_s, wp_s):
    body = functools.partial(_bias_body, seq=seq, past_len=past_len, w_buf=w_buf)
    nq = seq // NSA_TQ
    return pl.pallas_call(
        body,
        grid=(H_NSA,),
        in_specs=[pl.BlockSpec(memory_space=pltpu.SMEM)],
        out_specs=[
            pl.BlockSpec((2, NSA_TQ, NSA_TQ), lambda h: (0, 0, h)),
            pl.BlockSpec((nq, LANE, NSA_TQ), lambda h: (0, 0, h)),
            pl.BlockSpec((None, 1, nbp_s), lambda h: (h, 0, 0)),
            pl.BlockSpec((None, 1, wp_s), lambda h: (h, 0, 0)),
        ],
        out_shape=[
            jax.ShapeDtypeStruct((2, NSA_TQ, H_NSA * NSA_TQ), F32),
            jax.ShapeDtypeStruct((nq, LANE, H_NSA * NSA_TQ), F32),
            jax.ShapeDtypeStruct((H_NSA, 1, nbp_s), F32),
            jax.ShapeDtypeStruct((H_NSA, 1, wp_s), F32),
        ],
        compiler_params=_cparams(1),
        name="bias_tables",
    )(rel_bias)


Q_TILES = D_NSA // COL_TILE
NSA_TILE0 = Q_TILES
WIN_TILE = NSA_TILE0 + 4 * D_KV // COL_TILE
SB_TILE0 = WIN_TILE + 2 * D_KV // COL_TILE + D_SB // COL_TILE


def _inproj_body(x_ref, na_ref, w_ref, wg_ref, gain_ref, zb_ref, sb_ref, nsa_ref, win_ref, gate_ref,
                 h_ref, *, w_keep):
    j = pl.program_id(1)
    tm = x_ref.shape[0]

    @pl.when(j == 0)
    def _():
        h = _rms(x_ref[...], na_ref[...]).astype(BF16)
        h_ref[...] = h
        gate_ref[...] = jax.nn.sigmoid(_dot(h, wg_ref[...]))

    acc = _dot(h_ref[...], w_ref[...])
    gain = gain_ref[...]

    def emit(norm_chunks, f32_ref=None, row0=0):
        for c in range(COL_TILE // LANE):
            sl = slice(c * LANE, (c + 1) * LANE)
            y = acc[:, sl]
            if c in norm_chunks:
                y = _rms(y, gain[:, sl])
            zb_ref[:, sl] = y.astype(BF16)
            if f32_ref is not None:
                f32_ref[:, sl] = y[row0:, :]

    @pl.when(j < Q_TILES)
    def _():
        emit((0, 1, 2, 3))

    @pl.when(j == NSA_TILE0)
    def _():
        emit((), nsa_ref)

    @pl.when(j == NSA_TILE0 + 1)
    def _():
        emit((0, 1), nsa_ref)

    @pl.when(j == WIN_TILE)
    def _():
        emit((0, 1), win_ref, tm - w_keep)

    @pl.when((j > WIN_TILE) & (j < SB_TILE0))
    def _():
        emit(())

    @pl.when(j >= SB_TILE0)
    def _():
        emit((), sb_ref)


def _inproj(x2d, na, w_main, w_gate, gain_cols, seq, w_keep):
    m = x2d.shape[0]
    tm = _pick_tile(seq, (1024, 512, 256, 128, 16))
    assert w_keep <= tm
    tpb = seq // tm
    n_sb = 2 * D_SB // COL_TILE
    n_nsa = 4 * D_KV // COL_TILE
    body = functools.partial(_inproj_body, w_keep=w_keep)
    return pl.pallas_call(
        body,
        grid=(m // tm, N_MAIN // COL_TILE),
        in_specs=[
            pl.BlockSpec((tm, D_MODEL), lambda i, j: (i, 0)),
            pl.BlockSpec((1, D_MODEL), lambda i, j: (0, 0)),
            pl.BlockSpec((D_MODEL, COL_TILE), lambda i, j: (0, j)),
            pl.BlockSpec((D_MODEL, 2 * LANE), lambda i, j: (0, 0)),
            pl.BlockSpec((1, COL_TILE), lambda i, j: (0, j)),
        ],
        out_specs=[
            pl.BlockSpec((tm, COL_TILE), lambda i, j: (i, j)),
            pl.BlockSpec((tm, COL_TILE), lambda i, j: (i, jnp.clip(j - SB_TILE0, 0, n_sb - 1))),
            pl.BlockSpec((tm, COL_TILE), lambda i, j: (i, jnp.clip(j - NSA_TILE0, 0, n_nsa - 1))),
            pl.BlockSpec((w_keep, COL_TILE), lambda i, j: (i // tpb, 0)),
            pl.BlockSpec((tm, 2 * LANE), lambda i, j: (i, 0)),
        ],
        out_shape=[
            jax.ShapeDtypeStruct((m, N_MAIN), BF16),
            jax.ShapeDtypeStruct((m, 2 * D_SB), F32),
            jax.ShapeDtypeStruct((m, 4 * D_KV), F32),
            jax.ShapeDtypeStruct((m // seq * w_keep, 2 * D_KV), F32),
            jax.ShapeDtypeStruct((m, 2 * LANE), F32),
        ],
        scratch_shapes=[pltpu.VMEM((tm, D_MODEL), BF16)],
        compiler_params=_cparams(2),
        name="inproj",
    )(x2d, na, w_main, w_gate, gain_cols)


def _outproj_body(on_ref, os_ref, go_ref, w_ref, x_ref, y_ref, h_ref):
    j = pl.program_id(1)

    @pl.when(j == 0)
    def _():
        go = go_ref[...]
        h_ref[:, :D_NSA] = _rms(on_ref[...], go[:, :D_NSA]).astype(BF16)
        h_ref[:, D_NSA:] = _rms(os_ref[...], go[:, D_NSA:]).astype(BF16)

    y_ref[...] = x_ref[...] + _dot(h_ref[...], w_ref[...])


def _outproj(o_nsa, o_sb, g_out, w_out, x2d):
    m = x2d.shape[0]
    tm = _pick_tile(m, (1024, 512, 256, 128, 16))
    return pl.pallas_call(
        _outproj_body,
        grid=(m // tm, D_MODEL // COL_TILE),
        in_specs=[
            pl.BlockSpec((tm, D_NSA), lambda i, j: (i, 0)),
            pl.BlockSpec((tm, D_SB), lambda i, j: (i, 0)),
            pl.BlockSpec((1, D_MODEL), lambda i, j: (0, 0)),
            pl.BlockSpec((D_MODEL, COL_TILE), lambda i, j: (0, j)),
            pl.BlockSpec((tm, COL_TILE), lambda i, j: (i, j)),
        ],
        out_specs=pl.BlockSpec((tm, COL_TILE), lambda i, j: (i, j)),
        out_shape=jax.ShapeDtypeStruct((m, D_MODEL), F32),
        scratch_shapes=[pltpu.VMEM((tm, D_MODEL), BF16)],
        compiler_params=_cparams(2),
        name="outproj",
    )(o_nsa, o_sb, g_out, w_out, x2d)


def _ffn_body(x_ref, nf_ref, wg_ref, wu_ref, wd_ref, y_ref, h_ref, acc_ref):
    j = pl.program_id(1)

    @pl.when(j == 0)
    def _():
        h_ref[...] = _rms(x_ref[...], nf_ref[...]).astype(BF16)
        acc_ref[...] = jnp.zeros_like(acc_ref)

    h = h_ref[...]
    act = jax.nn.silu(_dot(h, wg_ref[...])) * _dot(h, wu_ref[...])
    acc_ref[...] += _dot(act.astype(BF16), wd_ref[...])

    @pl.when(j == pl.num_programs(1) - 1)
    def _():
        y_ref[...] = x_ref[...] + acc_ref[...]


def _ffn(x2d, norm_ffn, w_gate, w_up, w_down):
    m = x2d.shape[0]
    d_ff = w_gate.shape[1]
    tm = _pick_tile(m, (512, 256, 128, 16))
    tf = _pick_tile(d_ff, (512, 256))
    return pl.pallas_call(
        _ffn_body,
        grid=(m // tm, d_ff // tf),
        in_specs=[
            pl.BlockSpec((tm, D_MODEL), lambda i, j: (i, 0)),
            pl.BlockSpec((1, D_MODEL), lambda i, j: (0, 0)),
            pl.BlockSpec((D_MODEL, tf), lambda i, j: (0, j)),
            pl.BlockSpec((D_MODEL, tf), lambda i, j: (0, j)),
            pl.BlockSpec((tf, D_MODEL), lambda i, j: (j, 0)),
        ],
        out_specs=pl.BlockSpec((tm, D_MODEL), lambda i, j: (i, 0)),
        out_shape=jax.ShapeDtypeStruct((m, D_MODEL), F32),
        scratch_shapes=[pltpu.VMEM((tm, D_MODEL), BF16), pltpu.VMEM((tm, D_MODEL), F32)],
        compiler_params=_cparams(2),
        name="ffn",
    )(x2d, norm_ffn, w_gate, w_up, w_down)


def _sb_body(q_ref, k_ref, v_ref, u_ref, o_ref, acc_ref, run_ref):
    i = pl.program_id(2)
    row = lax.broadcasted_iota(I32, (SB_TQ, SB_TK), 0)
    col = lax.broadcasted_iota(I32, (SB_TQ, SB_TK), 1)
    strict = col < row

    def tile(kb, diagonal):
        off = pl.multiple_of(kb * SB_TK, SB_TK)
        heads = range(SB_HB)
        sls = [slice(h * HEAD_DIM, (h + 1) * HEAD_DIM) for h in heads]
        z = [_dot_nt(q_ref[:, sl], k_ref[pl.ds(off, SB_TK), sl]) * SCALE for sl in sls]
        log_1m = [_neg_softplus(zh) for zh in z]
        if diagonal:
            log_1m = [jnp.where(strict, lh, 0.0) for lh in log_1m]
        parts = [_split_bf16(lh) for lh in log_1m]
        u = u_ref[...]
        later = [_dot(hi, u) + _dot(lo, u) for hi, lo in parts]
        total = [jnp.sum(lh, axis=-1, keepdims=True) for lh in log_1m]
        if diagonal:
            a = [jnp.where(strict, jnp.exp(z[h] + log_1m[h] + later[h]), 0.0) for h in heads]
        else:
            run = [run_ref[h] for h in heads]
            a = [jnp.exp(z[h] + log_1m[h] + later[h] + run[h]) for h in heads]
        pv = [_dot(a[h].astype(BF16), v_ref[pl.ds(off, SB_TK), sls[h]]) for h in heads]
        for h in heads:
            if diagonal:
                acc_ref[h] = pv[h]
                run_ref[h] = total[h]
            else:
                acc_ref[h] += pv[h]
                run_ref[h] = run[h] + total[h]

    tile(i, True)

    def step(it, carry):
        tile(i - 1 - it, False)
        return carry

    lax.fori_loop(0, i, step, 0)
    for h in range(SB_HB):
        o_ref[:, h * HEAD_DIM:(h + 1) * HEAD_DIM] = acc_ref[h]


def _sb_prompt(zb, batch, seq):
    nq = seq // SB_TQ
    width = SB_HB * HEAD_DIM
    blk = LANE // HEAD_DIM * SB_HB
    assert QS_BLK % blk == 0 and KSB_BLK % blk == 0 and VSB_BLK % blk == 0
    r = lax.broadcasted_iota(I32, (SB_TK, SB_TK), 0)
    c = lax.broadcasted_iota(I32, (SB_TK, SB_TK), 1)
    u = (r > c).astype(BF16)
    return pl.pallas_call(
        _sb_body,
        grid=(batch, H_SB // SB_HB, nq),
        in_specs=[
            pl.BlockSpec((SB_TQ, width), lambda b, h, i: (b * nq + i, QS_BLK // blk + h)),
            pl.BlockSpec((seq, width), lambda b, h, i: (b, KSB_BLK // blk + h)),
            pl.BlockSpec((seq, width), lambda b, h, i: (b, VSB_BLK // blk + h)),
            pl.BlockSpec((SB_TK, SB_TK), lambda b, h, i: (0, 0)),
        ],
        out_specs=pl.BlockSpec((SB_TQ, width), lambda b, h, i: (b * nq + i, h)),
        out_shape=jax.ShapeDtypeStruct((batch * seq, D_SB), F32),
        scratch_shapes=[pltpu.VMEM((SB_HB, SB_TQ, HEAD_DIM), F32), pltpu.VMEM((SB_HB, SB_TQ, 1), F32)],
        compiler_params=_cparams(3),
        name="sb_prompt",
    )(zb, zb, zb, u)


def _cmp_prompt_body(x0_ref, x1_ref, x2_ref, x3_ref, wk_ref, wv_ref, bk_ref, bv_ref, gk_ref,
                     kc_ref, vct_ref, *, nb):
    pad = jnp.zeros((LANE - nb, HEAD_DIM), F32)
    for c, x_ref in enumerate((x0_ref, x1_ref, x2_ref, x3_ref)):
        rows = jnp.concatenate(
            [x_ref[pl.ds(l, nb, stride=CMP_BLOCK), :] for l in range(CMP_BLOCK)], axis=1)
        rows = rows.astype(BF16)
        if c < G_NSA:
            y = _rms(_dot(rows, wk_ref[...]) + bk_ref[...], gk_ref[...])
            kc_ref[c] = jnp.concatenate([y, pad], axis=0).astype(BF16)
        else:
            y = _dot(rows, wv_ref[...]) + bv_ref[...]
            vct_ref[c - G_NSA] = jnp.concatenate([y, pad], axis=0).T.astype(BF16)


def _cmp_prompt(nsa_rows, batch, seq, wk, wv, bk, bv, gk0):
    nb = seq // CMP_BLOCK
    assert nb % 8 == 0 and nb < LANE
    body = functools.partial(_cmp_prompt_body, nb=nb)
    shape = jax.ShapeDtypeStruct((batch, G_NSA, LANE, HEAD_DIM), BF16)

    def x_spec(c):
        return pl.BlockSpec((seq, HEAD_DIM), lambda b: (b, c))

    return pl.pallas_call(
        body,
        grid=(batch,),
        in_specs=[
            x_spec(0), x_spec(1), x_spec(2), x_spec(3),
            pl.BlockSpec(wk.shape, lambda b: (0, 0)),
            pl.BlockSpec(wv.shape, lambda b: (0, 0)),
            pl.BlockSpec((1, HEAD_DIM), lambda b: (0, 0)),
            pl.BlockSpec((1, HEAD_DIM), lambda b: (0, 0)),
            pl.BlockSpec((1, HEAD_DIM), lambda b: (0, 0)),
        ],
        out_specs=[pl.BlockSpec((None, G_NSA, LANE, HEAD_DIM), lambda b: (b, 0, 0, 0))] * 2,
        out_shape=[shape, shape],
        compiler_params=_cparams(1),
        name="cmp_prompt",
    )(nsa_rows, nsa_rows, nsa_rows, nsa_rows, wk, wv, bk, bv, gk0)


def _masked_softmax(s, valid):
    logit = jnp.where(valid, s, NEG_INF)
    e = jnp.exp(logit - jnp.max(logit, axis=-1, keepdims=True))
    return jnp.where(valid, e / jnp.sum(e, axis=-1, keepdims=True), 0.0)


class _Flash:
    def __init__(self, m_ref, l_ref, a_ref):
        self.m_ref, self.l_ref, self.a_ref = m_ref, l_ref, a_ref

    def first(self, logit, v):
        m = jnp.max(logit, axis=-1, keepdims=True)
        p = jnp.exp(logit - m)
        self.m_ref[...] = m
        self.l_ref[...] = jnp.sum(p, axis=-1, keepdims=True)
        self.a_ref[...] = _dot(p.astype(BF16), v)

    def update(self, logit, v):
        m_old = self.m_ref[...]
        m = jnp.maximum(m_old, jnp.max(logit, axis=-1, keepdims=True))
        alpha = jnp.exp(m_old - m)
        p = jnp.exp(logit - m)
        self.m_ref[...] = m
        self.l_ref[...] = alpha * self.l_ref[...] + jnp.sum(p, axis=-1, keepdims=True)
        self.a_ref[...] = alpha * self.a_ref[...] + _dot(p.astype(BF16), v)

    def result(self):
        return self.a_ref[...] / self.l_ref[...]


class _FlashT:
    def __init__(self, m_ref, l_ref, a_ref):
        self.m_ref, self.l_ref, self.a_ref = m_ref, l_ref, a_ref

    def first(self, logit, v_t):
        m = jnp.max(logit, axis=0, keepdims=True)
        p = jnp.exp(logit - m)
        self.m_ref[...] = m
        self.l_ref[...] = jnp.sum(p, axis=0, keepdims=True)
        self.a_ref[...] = _dot(v_t, p.astype(BF16))

    def update(self, logit, v_t):
        m_old = self.m_ref[...]
        m = jnp.maximum(m_old, jnp.max(logit, axis=0, keepdims=True))
        alpha = jnp.exp(m_old - m)
        p = jnp.exp(logit - m)
        self.m_ref[...] = m
        self.l_ref[...] = alpha * self.l_ref[...] + jnp.sum(p, axis=0, keepdims=True)
        self.a_ref[...] = alpha * self.a_ref[...] + _dot(v_t, p.astype(BF16))

    def result(self):
        return self.a_ref[...] / self.l_ref[...]


def _nsa_body(rb_ref, q_ref, kc_ref, vct_ref, ks_ref, vs_ref, kw_ref, vw_ref, g_ref, toep_ref,
              bc_ref, et_ref, o_ref, vst_ref, vwt_ref, m_s, l_s, a_s, m_w, l_w, a_w, *, nb, seq):
    g = pl.program_id(1)
    i = pl.program_id(2)
    tq = NSA_TQ
    sub = tq // LANE
    width = HPG * tq
    q0 = i * tq

    @pl.when(i == 0)
    def _():
        for c in range(seq // LANE):
            rows = slice(c * LANE, (c + 1) * LANE)
            cols = slice((c % sub) * LANE, (c % sub + 1) * LANE)
            vst_ref[c // sub, :, cols] = vs_ref[rows, :].astype(F32).T.astype(BF16)
            vwt_ref[c // sub, :, cols] = vw_ref[rows, :].astype(F32).T.astype(BF16)

    qs = jnp.concatenate([q_ref[:, hh * LANE:(hh + 1) * LANE] for hh in range(HPG)], axis=0)

    def q_off(rows):
        return lax.broadcasted_iota(I32, (rows, width), 1) % tq

    n_r = lax.broadcasted_iota(I32, (LANE, width), 0)
    s_c = _dot_nt(kc_ref[...], qs) * SCALE + bc_ref[...]
    valid_c = (CMP_BLOCK * n_r + (CMP_BLOCK - 1) <= q0 + q_off(LANE)) & (n_r < nb)
    logit = jnp.where(valid_c, s_c, NEG_INF)
    e = jnp.exp(logit - jnp.max(logit, axis=0, keepdims=True))
    p_c = jnp.where(valid_c, e / jnp.sum(e, axis=0, keepdims=True), 0.0)
    oc_t = _dot(vct_ref[...], p_c.astype(BF16))
    imp_t = p_c[:, 0:tq]
    for hh in range(1, HPG):
        imp_t = imp_t + p_c[:, hh * tq:(hh + 1) * tq]

    nbr = -(-nb // 8) * 8
    n_i = lax.broadcasted_iota(I32, (nbr, tq), 0)
    cur = (q0 + lax.broadcasted_iota(I32, (nbr, tq), 1)) // CMP_BLOCK
    forced = (n_i == 0) | (n_i >= cur - (N_LOCAL - 1))
    score = jnp.where(n_i > cur, -jnp.inf, jnp.where(forced, FORCE_SCORE, imp_t[0:nbr]))
    rank = jnp.zeros((nbr, tq), I32)
    for j in range(nb):
        sj = jnp.broadcast_to(score[j:j + 1, :], (nbr, tq))
        ahead = (sj > score) | ((sj == score) & (n_i > j))
        rank = rank + ahead.astype(I32)
    sel_t = ((rank < min(TOP_N, nb)) & (score > -jnp.inf)).astype(F32)
    if nbr < LANE:
        sel_t = jnp.concatenate([sel_t, jnp.zeros((LANE - nbr, tq), F32)], axis=0)
    sel_t = sel_t.astype(BF16)

    k_r = lax.broadcasted_iota(I32, (tq, width), 0)
    q_c = q_off(tq)
    causal = k_r <= q_c
    bias_far = jnp.concatenate(
        [jnp.full((1, tq), rb_ref[N_BUCKETS - 1, g * HPG + hh], F32) for hh in range(HPG)], axis=1)

    def chunk(ref, c):
        return ref[pl.ds(pl.multiple_of(c * tq, tq), tq), :]

    def scores(ref, c):
        return _dot_nt(chunk(ref, c), qs) * SCALE

    def selected(c):
        hit = _dot(chunk(et_ref, c), sel_t)
        return jnp.concatenate([hit] * HPG, axis=1) > 0.5

    fs = _FlashT(m_s, l_s, a_s)
    fs.first(jnp.where(selected(i) & causal, scores(ks_ref, i) + toep_ref[1], NEG_INF), vst_ref[i])

    @pl.when(i >= 1)
    def _():
        fs.update(jnp.where(selected(i - 1), scores(ks_ref, i - 1) + toep_ref[0], NEG_INF),
                  vst_ref[i - 1])

    def far_step(c, carry):
        fs.update(jnp.where(selected(c), scores(ks_ref, c) + bias_far, NEG_INF), vst_ref[c])
        return carry

    lax.fori_loop(0, jnp.maximum(i - 1, 0), far_step, 0)

    fw = _FlashT(m_w, l_w, a_w)
    fw.first(jnp.where(causal, scores(kw_ref, i) + toep_ref[1], NEG_INF), vwt_ref[i])

    @pl.when(i >= 1)
    def _():
        fw.update(scores(kw_ref, i - 1) + toep_ref[0], vwt_ref[i - 1])

    @pl.when(i >= 2)
    def _():
        fw.update(jnp.where(k_r >= q_c, scores(kw_ref, i - 2) + bias_far, NEG_INF), vwt_ref[i - 2])

    g_t = jnp.concatenate([g_ref[t * LANE:(t + 1) * LANE, :].T for t in range(sub)], axis=1)

    def gate(branch):
        return jnp.concatenate(
            [g_t[branch * HPG + hh:branch * HPG + hh + 1, :] for hh in range(HPG)], axis=1)

    o_t = gate(0) * oc_t + gate(1) * fs.result() + gate(2) * fw.result()
    for hh in range(HPG):
        for t in range(sub):
            lanes = slice(hh * tq + t * LANE, hh * tq + (t + 1) * LANE)
            o_ref[t * LANE:(t + 1) * LANE, hh * LANE:(hh + 1) * LANE] = o_t[:, lanes].T


def _nsa_prompt(rel_bias, zb, kc, vct, gates, toep, bias_c, batch, seq):
    tq = NSA_TQ
    assert WINDOW == 2 * tq and tq % LANE == 0 and seq % tq == 0
    nq = seq // tq
    nb = seq // CMP_BLOCK
    assert nb <= LANE
    width = HPG * tq
    k = lax.broadcasted_iota(I32, (seq, LANE), 0)
    n = lax.broadcasted_iota(I32, (seq, LANE), 1)
    expand_t = (k // CMP_BLOCK == n).astype(BF16)
    body = functools.partial(_nsa_body, nb=nb, seq=seq)

    def kv_spec(blk):
        return pl.BlockSpec((seq, HEAD_DIM), lambda b, g, i: (b, blk + g))

    stat = pltpu.VMEM((1, width), F32)
    accum = pltpu.VMEM((HEAD_DIM, width), F32)
    v_t = pltpu.VMEM((nq, HEAD_DIM, tq), BF16)
    return pl.pallas_call(
        body,
        grid=(batch, G_NSA, nq),
        in_specs=[
            pl.BlockSpec(memory_space=pltpu.SMEM),
            pl.BlockSpec((tq, HPG * HEAD_DIM), lambda b, g, i: (b * nq + i, g)),
            pl.BlockSpec((None, None, LANE, HEAD_DIM), lambda b, g, i: (b, g, 0, 0)),
            pl.BlockSpec((None, None, HEAD_DIM, LANE), lambda b, g, i: (b, g, 0, 0)),
            kv_spec(KS_BLK), kv_spec(VS_BLK), kv_spec(KW_BLK), kv_spec(VW_BLK),
            pl.BlockSpec((tq, LANE), lambda b, g, i: (b * nq + i, g)),
            pl.BlockSpec((2, tq, width), lambda b, g, i: (0, 0, g)),
            pl.BlockSpec((None, LANE, width), lambda b, g, i: (i, 0, g)),
            pl.BlockSpec((seq, LANE), lambda b, g, i: (0, 0)),
        ],
        out_specs=pl.BlockSpec((tq, HPG * HEAD_DIM), lambda b, g, i: (b * nq + i, g)),
        out_shape=jax.ShapeDtypeStruct((batch * seq, D_NSA), F32),
        scratch_shapes=[v_t, v_t, stat, stat, accum, stat, stat, accum],
        compiler_params=_cparams(3),
        name="nsa_prompt",
    )(rel_bias, zb, kc, vct, zb, zb, zb, zb, gates, toep, bias_c, expand_t)


def _prep_weights(l, w_in, g_q, g_k, w_cmp_k, w_cmp_v, w_out, w_gate, w_up, w_down):
    n_gate = 3 * H_NSA
    g0 = D_NSA + 6 * D_KV
    wi = w_in[l]
    w_main = jnp.concatenate([wi[:, :g0], wi[:, g0 + n_gate:]], axis=1).astype(BF16)
    wg = wi[:, g0:g0 + n_gate].reshape(D_MODEL, 3, G_NSA, HPG)
    wg = jnp.transpose(wg, (0, 2, 1, 3)).reshape(D_MODEL, G_NSA, 3 * HPG)
    wg = jnp.pad(wg, ((0, 0), (0, 0), (0, LANE - 3 * HPG))).reshape(D_MODEL, G_NSA * LANE)
    ones = jnp.ones((HEAD_DIM,), F32)
    gain = jnp.concatenate(
        [jnp.tile(g_q[l], H_NSA), jnp.tile(ones, 2 * G_NSA),
         jnp.tile(g_k[l, 1], G_NSA), jnp.tile(ones, G_NSA),
         jnp.tile(g_k[l, 2], G_NSA), jnp.tile(ones, G_NSA),
         jnp.tile(ones, 3 * H_SB)])[None, :]
    wk = w_cmp_k[l].reshape(CMP_BLOCK * HEAD_DIM, HEAD_DIM).astype(BF16)
    wv = w_cmp_v[l].reshape(CMP_BLOCK * HEAD_DIM, HEAD_DIM).astype(BF16)
    return dict(
        w_main=w_main, w_gate=wg.astype(BF16), gain=gain, wk=wk, wv=wv,
        w_kv=jnp.concatenate([wk, wv], axis=1),
        w_out=w_out[l].astype(BF16), w_gate_ffn=w_gate[l].astype(BF16),
        w_up=w_up[l].astype(BF16), w_down=w_down[l].astype(BF16))


def _prompt_layer(x2d, wt, p, tables, rel_bias, batch, seq):
    na, bk, bv, gk0, go, nf = p
    zb, sb_rows, nsa_rows, win_rows, gates = _inproj(
        x2d, na, wt["w_main"], wt["w_gate"], wt["gain"], seq, min(WINDOW, seq))
    kc, vct = _cmp_prompt(nsa_rows, batch, seq, wt["wk"], wt["wv"], bk, bv, gk0)
    o_nsa = _nsa_prompt(rel_bias, zb, kc, vct, gates, tables[0], tables[1], batch, seq)
    o_sb = _sb_prompt(zb, batch, seq)
    x1 = _outproj(o_nsa, o_sb, go, wt["w_out"], x2d)
    x2 = _ffn(x1, nf, wt["w_gate_ffn"], wt["w_up"], wt["w_down"])
    return x2, sb_rows, nsa_rows, win_rows


ROWS_S = 16


def _sb_dec_body(pt_ref, q_ref, *refs, npg, pps, page):
    page_refs = refs[:pps]
    up_ref, upp_ref, o_ref, z_ref, acc_ref = refs[pps:]
    ph = pl.program_id(1)
    c = pl.program_id(2)
    width = page * H_SB
    r = lax.broadcasted_iota(I32, (ROWS_S, width), 0)
    lane = lax.broadcasted_iota(I32, (ROWS_S, width), 1)
    own_head = (lane % H_SB) == r

    @pl.when(ph == 0)
    def _():
        q = q_ref[...]
        for u in range(pps):
            ks = page_refs[u][...].reshape(width, HEAD_DIM).astype(BF16)
            zt = _dot_nt(q, ks)
            z_ref[pl.ds(c * pps + u, 1), :] = jnp.sum(jnp.where(own_head, zt, 0.0), axis=0,
                                                      keepdims=True)

    @pl.when((ph == 1) & (c == 0))
    def _():
        z = z_ref[...] * SCALE
        log_1m = _neg_softplus(z)
        cs = _dot_split(log_1m, up_ref[...])
        hi, lo = _split_bf16(cs[:, width:])
        later_pages = _dot(upp_ref[...], hi) + _dot(upp_ref[...], lo)
        z_ref[...] = jnp.exp(z + log_1m + cs[:, :width] + later_pages)
        acc_ref[...] = jnp.zeros_like(acc_ref)

    @pl.when(ph == 1)
    def _():
        for u in range(pps):
            vs = page_refs[u][...].reshape(width, HEAD_DIM).astype(BF16)
            a = jnp.broadcast_to(z_ref[pl.ds(c * pps + u, 1), :], (ROWS_S, width))
            acc_ref[...] += _dot(jnp.where(own_head, a, 0.0).astype(BF16), vs)

    @pl.when((ph == 1) & (c == pl.num_programs(2) - 1))
    def _():
        o_ref[...] = acc_ref[0:H_SB, :]


def _sb_decode(l, pt_flat, q_sb, cache_sb):
    dec_b = q_sb.shape[0]
    page = cache_sb.shape[2]
    npg = pt_flat.shape[0] // dec_b
    pps = SB_PAGES_PER_STEP
    assert npg % pps == 0 and npg % 8 == 0
    width = page * H_SB
    r = lax.broadcasted_iota(I32, (width, 2 * width), 0)
    c = lax.broadcasted_iota(I32, (width, 2 * width), 1)
    same_head = (r % H_SB) == (c % H_SB)
    up = (same_head & ((c >= width) | (r // H_SB > c // H_SB))).astype(BF16)
    pr = lax.broadcasted_iota(I32, (npg, npg), 0)
    pc = lax.broadcasted_iota(I32, (npg, npg), 1)
    upp = (pc > pr).astype(BF16)
    body = functools.partial(_sb_dec_body, npg=npg, pps=pps, page=page)

    def page_spec(u):
        return pl.BlockSpec((None, None, page, None, H_SB, HEAD_DIM),
                            lambda b, ph, c, pt: (l, pt[b * npg + c * pps + u], 0, ph, 0, 0))

    grid_spec = pltpu.PrefetchScalarGridSpec(
        num_scalar_prefetch=1,
        grid=(dec_b, 2, npg // pps),
        in_specs=[pl.BlockSpec((None, ROWS_S, HEAD_DIM), lambda b, ph, c, pt: (b, 0, 0))]
        + [page_spec(u) for u in range(pps)]
        + [pl.BlockSpec((width, 2 * width), lambda b, ph, c, pt: (0, 0)),
           pl.BlockSpec((npg, npg), lambda b, ph, c, pt: (0, 0))],
        out_specs=pl.BlockSpec((None, H_SB, HEAD_DIM), lambda b, ph, c, pt: (b, 0, 0)),
        scratch_shapes=[pltpu.VMEM((npg, width), F32), pltpu.VMEM((ROWS_S, HEAD_DIM), F32)],
    )
    return pl.pallas_call(
        body,
        grid_spec=grid_spec,
        out_shape=jax.ShapeDtypeStruct((dec_b, H_SB, HEAD_DIM), F32),
        compiler_params=_cparams(3),
        name="sb_decode",
    )(pt_flat, q_sb, *([cache_sb] * pps), up, upp)


NSA_CH = 4 * G_NSA


def _cmp_dec_body(pt_ref, *refs, pps, page):
    page_refs = refs[:pps]
    w_ref, kc_ref, vc_ref, yk_ref, yv_ref = refs[pps:]
    halves = page // CMP_BLOCK
    nblk = pps * halves
    acc = jnp.zeros((nblk * NSA_CH, 2 * HEAD_DIM), F32)
    for l in range(CMP_BLOCK):
        x_l = jnp.concatenate(
            [page_refs[u][pl.ds((hf * CMP_BLOCK + l) * NSA_CH, NSA_CH), :]
             for u in range(pps) for hf in range(halves)], axis=0).astype(BF16)
        acc = acc + _dot(x_l, w_ref[l * HEAD_DIM:(l + 1) * HEAD_DIM, :])
    yk_ref[...] = acc[:, :HEAD_DIM]
    yv_ref[...] = acc[:, HEAD_DIM:]
    for g in range(G_NSA):
        kc_ref[g] = yk_ref[pl.ds(g, nblk, stride=NSA_CH), :]
        vc_ref[g] = yv_ref[pl.ds(G_NSA + g, nblk, stride=NSA_CH), :]


def _cmp_decode(l, pt_flat, cache_nsa2d, w_kv, dec_b, page):
    npg = pt_flat.shape[0] // dec_b
    pps = min(CMP_PAGES_PER_STEP, npg)
    assert npg % pps == 0
    halves = page // CMP_BLOCK
    nblk = pps * halves
    body = functools.partial(_cmp_dec_body, pps=pps, page=page)

    def page_spec(u):
        return pl.BlockSpec((None, None, page * NSA_CH, HEAD_DIM),
                            lambda b, c, pt: (l, pt[b * npg + c * pps + u], 0, 0))

    out_spec = pl.BlockSpec((None, G_NSA, nblk, HEAD_DIM), lambda b, c, pt: (b, 0, c, 0))
    shape = jax.ShapeDtypeStruct((dec_b, G_NSA, npg * halves, HEAD_DIM), F32)
    grid_spec = pltpu.PrefetchScalarGridSpec(
        num_scalar_prefetch=1,
        grid=(dec_b, npg // pps),
        in_specs=[page_spec(u) for u in range(pps)]
        + [pl.BlockSpec(w_kv.shape, lambda b, c, pt: (0, 0))],
        out_specs=[out_spec, out_spec],
        scratch_shapes=[pltpu.VMEM((nblk * NSA_CH, HEAD_DIM), F32)] * 2,
    )
    return pl.pallas_call(
        body,
        grid_spec=grid_spec,
        out_shape=[shape, shape],
        compiler_params=_cparams(2),
        name="cmp_decode",
    )(pt_flat, *([cache_nsa2d] * pps), w_kv)


def _nsa_sel_body(q_ref, kcr_ref, vcr_ref, new_ref, wk0_ref, wv0_ref, bk_ref, bv_ref, gk_ref,
                  sbc_ref, kw_ref, vw_ref, sbw_ref, idx_ref, oc_ref, ow_ref,
                  *, nb0, nbp, past_len, w_buf, wp):
    nb = nb0 + 1
    new = new_ref[...].astype(BF16)
    new_k = _rms(_dot(new, wk0_ref[...]) + bk_ref[...], gk_ref[...])
    new_v = _dot(new, wv0_ref[...]) + bv_ref[...]
    row8 = lax.broadcasted_iota(I32, (8, HEAD_DIM), 0)
    n1 = lax.broadcasted_iota(I32, (1, nbp), 1)
    nq = lax.broadcasted_iota(I32, (ROWS_S, nbp), 1)
    pos_q = past_len
    pad = jnp.zeros((nbp - nb0 - 8, HEAD_DIM), F32)
    for g in range(G_NSA):
        q = q_ref[g]
        kc = jnp.concatenate(
            [_rms(kcr_ref[g] + bk_ref[...], gk_ref[...]),
             jnp.where(row8 == 0, jnp.broadcast_to(new_k[g:g + 1], (8, HEAD_DIM)), 0.0), pad],
            axis=0).astype(BF16)
        vc = jnp.concatenate(
            [vcr_ref[g] + bv_ref[...],
             jnp.where(row8 == 0, jnp.broadcast_to(new_v[G_NSA + g:G_NSA + g + 1], (8, HEAD_DIM)),
                       0.0), pad], axis=0).astype(BF16)
        s_c = _dot_nt(q, kc) * SCALE + sbc_ref[g]
        valid_c = (CMP_BLOCK * nq + (CMP_BLOCK - 1) <= pos_q) & (nq < nb)
        p_c = _masked_softmax(s_c, valid_c)
        oc_ref[g] = _dot(p_c.astype(BF16), vc)
        imp = p_c[0:1]
        for hh in range(1, HPG):
            imp = imp + p_c[hh:hh + 1]

        cur = pos_q // CMP_BLOCK
        forced = (n1 == 0) | (n1 >= cur - (N_LOCAL - 1))
        score = jnp.where(n1 > cur, -jnp.inf, jnp.where(forced, FORCE_SCORE, imp))
        sq = jnp.broadcast_to(score, (LANE, nbp))
        col = jnp.concatenate([sq[:, k * LANE:(k + 1) * LANE].T for k in range(nbp // LANE)],
                              axis=0)
        j_i = lax.broadcasted_iota(I32, (nbp, LANE), 0)
        ranks = []
        for k in range(nbp // LANE):
            mine = jnp.broadcast_to(score[:, k * LANE:(k + 1) * LANE], (nbp, LANE))
            n_i = k * LANE + lax.broadcasted_iota(I32, (nbp, LANE), 1)
            ahead = (col > mine) | ((col == mine) & (j_i < n_i))
            ranks.append(jnp.sum(ahead.astype(F32), axis=0, keepdims=True))
        rank = jnp.concatenate(ranks, axis=1)
        slot = lax.broadcasted_iota(I32, (TOP_N, nbp), 0).astype(F32)
        hit = jnp.broadcast_to(rank, (TOP_N, nbp)) == slot
        ids = jnp.sum(jnp.where(hit, lax.broadcasted_iota(I32, (TOP_N, nbp), 1).astype(F32), 0.0),
                      axis=1, keepdims=True)
        idx_ref[g] = jnp.broadcast_to(ids, (TOP_N, LANE)).astype(I32)

        jw = lax.broadcasted_iota(I32, (ROWS_S, wp), 1)
        dist_w = w_buf - jw
        valid_w = (dist_w >= 0) & (dist_w <= WINDOW) & (past_len - w_buf + jw >= 0)
        s_w = _dot_nt(q, kw_ref[g]) * SCALE + sbw_ref[g]
        p_w = _masked_softmax(s_w, valid_w)
        ow_ref[g] = _dot(p_w.astype(BF16), vw_ref[g])


def _nsa_select(q_nsa, kcr, vcr, new_c, wk0, wv0, bk, bv, gk0, sbc, kw, vw, sbw, past_len, w_buf):
    dec_b, _, nb0, _ = kcr.shape
    nbp = sbc.shape[-1]
    wp = sbw.shape[-1]
    assert nb0 + 1 >= TOP_N and nbp >= nb0 + 8 and nb0 % 8 == 0
    body = functools.partial(_nsa_sel_body, nb0=nb0, nbp=nbp, past_len=past_len, w_buf=w_buf, wp=wp)

    def per_b(shape):
        nd = len(shape)
        return pl.BlockSpec((None,) + tuple(shape), lambda b: (b,) + (0,) * nd)

    def whole(arr):
        nd = arr.ndim
        return pl.BlockSpec(arr.shape, lambda b: (0,) * nd)

    vec = (G_NSA, ROWS_S, HEAD_DIM)
    return pl.pallas_call(
        body,
        grid=(dec_b,),
        in_specs=[per_b(vec), per_b((G_NSA, nb0, HEAD_DIM)), per_b((G_NSA, nb0, HEAD_DIM)),
                  per_b((8, HEAD_DIM)), whole(wk0), whole(wv0), whole(bk), whole(bv), whole(gk0),
                  whole(sbc), per_b((G_NSA, wp, HEAD_DIM)), per_b((G_NSA, wp, HEAD_DIM)), whole(sbw)],
        out_specs=[per_b((G_NSA, TOP_N, LANE)), per_b(vec), per_b(vec)],
        out_shape=[jax.ShapeDtypeStruct((dec_b, G_NSA, TOP_N, LANE), I32),
                   jax.ShapeDtypeStruct((dec_b,) + vec, F32),
                   jax.ShapeDtypeStruct((dec_b,) + vec, F32)],
        compiler_params=_cparams(1),
        name="nsa_select",
    )(q_nsa, kcr, vcr, new_c, wk0, wv0, bk, bv, gk0, sbc, kw, vw, sbw)


def _nsa_gather_body(pt_ref, idx_ref, rb_ref, q_ref, p0_ref, p1_ref, new_ref, gs_ref, oc_ref,
                     ow_ref, o_ref, m_ref, l_ref, a_ref, *, nb0, past_len):
    b = pl.program_id(0)
    k = pl.program_id(1)
    pos_q = past_len
    row = lax.broadcasted_iota(I32, (CMP_BLOCK, HEAD_DIM), 0)
    l1 = lax.broadcasted_iota(I32, (1, CMP_BLOCK), 1)
    hrow = lax.broadcasted_iota(I32, (ROWS_S, CMP_BLOCK), 0)
    for g, page_ref in enumerate((p0_ref, p1_ref)):
        blk = idx_ref[(b * G_NSA + g) * TOP_N + k]
        is_new = blk >= nb0
        new_k = jnp.broadcast_to(new_ref[g:g + 1, :], (CMP_BLOCK, HEAD_DIM))
        new_v = jnp.broadcast_to(new_ref[G_NSA + g:G_NSA + g + 1, :], (CMP_BLOCK, HEAD_DIM))
        ks = page_ref[pl.ds(2 * G_NSA + g, CMP_BLOCK, stride=NSA_CH), :]
        vs = page_ref[pl.ds(3 * G_NSA + g, CMP_BLOCK, stride=NSA_CH), :]
        ks = jnp.where(is_new, jnp.where(row == 0, new_k, 0.0), ks).astype(BF16)
        vs = jnp.where(is_new, jnp.where(row == 0, new_v, 0.0), vs).astype(BF16)
        dist = pos_q - (blk * CMP_BLOCK + l1)
        bucket = jnp.broadcast_to(_t5_bucket(dist), (ROWS_S, CMP_BLOCK))
        bias = jnp.zeros((ROWS_S, CMP_BLOCK), F32)
        for hh in range(HPG):
            for bk in range(N_BUCKETS):
                bias = jnp.where((hrow == hh) & (bucket == bk), rb_ref[bk, g * HPG + hh], bias)
        s = _dot_nt(q_ref[g], ks) * SCALE + bias
        logit = jnp.where(jnp.broadcast_to(dist >= 0, (ROWS_S, CMP_BLOCK)), s, NEG_INF)
        fl = _Flash(m_ref.at[g], l_ref.at[g], a_ref.at[g])

        @pl.when(k == 0)
        def _():
            fl.first(logit, vs)

        @pl.when(k > 0)
        def _():
            fl.update(logit, vs)

        @pl.when(k == pl.num_programs(1) - 1)
        def _():
            o_ref[g] = gs_ref[g, 0] * oc_ref[g] + gs_ref[g, 1] * fl.result() + gs_ref[g, 2] * ow_ref[g]


def _nsa_gather(l, pt_flat, idx_flat, rel_bias, q_nsa, cache_nsa2d, new_s, gs, oc, ow, nb0,
                past_len, page):
    dec_b = q_nsa.shape[0]
    npg = pt_flat.shape[0] // dec_b
    halves = page // CMP_BLOCK
    body = functools.partial(_nsa_gather_body, nb0=nb0, past_len=past_len)

    def page_spec(g):
        def imap(b, k, pt, idx):
            blk = jnp.minimum(idx[(b * G_NSA + g) * TOP_N + k], nb0 - 1)
            return (l, pt[b * npg + blk // halves], blk % halves, 0)
        return pl.BlockSpec((None, None, CMP_BLOCK * NSA_CH, HEAD_DIM), imap)

    def per_b(shape):
        nd = len(shape)
        return pl.BlockSpec((None,) + tuple(shape), lambda b, k, pt, idx: (b,) + (0,) * nd)

    vec = (G_NSA, ROWS_S, HEAD_DIM)
    grid_spec = pltpu.PrefetchScalarGridSpec(
        num_scalar_prefetch=2,
        grid=(dec_b, TOP_N),
        in_specs=[pl.BlockSpec(memory_space=pltpu.SMEM), per_b(vec), page_spec(0), page_spec(1),
                  per_b((8, HEAD_DIM)), per_b((G_NSA, 3, ROWS_S, HEAD_DIM)), per_b(vec), per_b(vec)],
        out_specs=per_b(vec),
        scratch_shapes=[pltpu.VMEM((G_NSA, ROWS_S, 1), F32), pltpu.VMEM((G_NSA, ROWS_S, 1), F32),
                        pltpu.VMEM((G_NSA, ROWS_S, HEAD_DIM), F32)],
    )
    return pl.pallas_call(
        body,
        grid_spec=grid_spec,
        out_shape=jax.ShapeDtypeStruct((dec_b,) + vec, F32),
        compiler_params=_cparams(2),
        name="nsa_gather",
    )(pt_flat, idx_flat, rel_bias, q_nsa, cache_nsa2d, cache_nsa2d, new_s, gs, oc, ow)


def _pad_rows(x, rows):
    return jnp.pad(x, ((0, rows - x.shape[0]),) + ((0, 0),) * (x.ndim - 1))


def _sample_layer(l, xs, wt, p, sb_tabs, rel_bias, pt_flat, cache_sb, cache_nsa2d, win_l, w_cmp0,
                  dec_b, past_len, page):
    na, bk, bv, gk0, go, nf = p
    sbc, sbw = sb_tabs
    w_buf = win_l.shape[1]
    rows = xs.shape[0]
    zb, sb_rows, nsa_rows, win_rows, gates = _inproj(
        xs, na, wt["w_main"], wt["w_gate"], wt["gain"], rows, rows)
    zb8, sb_rows, nsa_rows, win_rows = zb[:dec_b], sb_rows[:dec_b], nsa_rows[:dec_b], win_rows[:dec_b]

    q_sb = zb8[:, QS_BLK * LANE:KSB_BLK * LANE].reshape(dec_b, H_SB, HEAD_DIM)
    q_sb = jnp.pad(q_sb, ((0, 0), (0, ROWS_S - H_SB), (0, 0)))
    o_sb = _sb_decode(l, pt_flat, q_sb, cache_sb).reshape(dec_b, D_SB)

    q_nsa = zb8[:, :D_NSA].reshape(dec_b, G_NSA, HPG, HEAD_DIM)
    q_nsa = jnp.pad(q_nsa, ((0, 0), (0, 0), (0, ROWS_S - HPG), (0, 0)))
    kcr, vcr = _cmp_decode(l, pt_flat, cache_nsa2d, wt["w_kv"], dec_b, page)
    new_c = jnp.pad(nsa_rows[:, :2 * D_KV].reshape(dec_b, 2 * G_NSA, HEAD_DIM),
                    ((0, 0), (0, 8 - 2 * G_NSA), (0, 0)))
    new_s = jnp.pad(nsa_rows[:, 2 * D_KV:].reshape(dec_b, 2 * G_NSA, HEAD_DIM),
                    ((0, 0), (0, 8 - 2 * G_NSA), (0, 0)))
    new_w = win_rows.reshape(dec_b, 1, 2, G_NSA, HEAD_DIM)
    win_all = jnp.concatenate([win_l, new_w], axis=1)
    wp = sbw.shape[-1]
    win_t = jnp.transpose(win_all, (2, 0, 3, 1, 4))
    win_t = jnp.pad(win_t, ((0, 0), (0, 0), (0, 0), (0, wp - w_buf - 1), (0, 0))).astype(BF16)
    idx, oc, ow = _nsa_select(q_nsa, kcr, vcr, new_c, w_cmp0[0], w_cmp0[1], bk, bv, gk0, sbc,
                              win_t[0], win_t[1], sbw, past_len, w_buf)
    gs = gates[:dec_b].reshape(dec_b, G_NSA, LANE)[:, :, :3 * HPG].reshape(dec_b, G_NSA, 3, HPG)
    gs = jnp.pad(gs, ((0, 0), (0, 0), (0, 0), (0, ROWS_S - HPG)))
    gs = jnp.broadcast_to(gs[..., None], gs.shape + (HEAD_DIM,))
    o_nsa = _nsa_gather(l, pt_flat, idx[:, :, :, 0].reshape(-1), rel_bias, q_nsa, cache_nsa2d,
                        new_s, gs, oc, ow, kcr.shape[2], past_len, page)
    o_nsa = o_nsa[:, :, :HPG].reshape(dec_b, D_NSA)

    x1 = _outproj(_pad_rows(o_nsa, rows), _pad_rows(o_sb, rows), go, wt["w_out"], xs)
    x2 = _ffn(x1, nf, wt["w_gate_ffn"], wt["w_up"], wt["w_down"])
    new_win = win_all[:, 1:]
    return x2, sb_rows, nsa_rows, new_win


def kernel(x_prompt, x_sample, cache_sb_kv, cache_nsa_kv, state_win_kv, page_table, rel_bias,
           norm_attn, w_in, g_q, g_k, w_cmp_k, b_cmp_k, w_cmp_v, b_cmp_v, g_out, w_out,
           norm_ffn, w_gate, w_up, w_down):
    batch, seq, _ = x_prompt.shape
    dec_b, dec_t, _ = x_sample.shape
    assert dec_t == 1
    depth, n_pool, page = cache_sb_kv.shape[:3]
    npg = page_table.shape[1]
    past_len = npg * page
    w_buf = state_win_kv.shape[2]
    nb_s = past_len // CMP_BLOCK + 1
    nbp_s = -(-(nb_s + 7) // LANE) * LANE
    wp_s = -(-(w_buf + 1) // LANE) * LANE

    toep, bias_c, sc, sw = _bias_tables(rel_bias, seq, past_len, w_buf, nbp_s, wp_s)
    sbc = jnp.pad(sc.reshape(G_NSA, HPG, nbp_s), ((0, 0), (0, ROWS_S - HPG), (0, 0)))
    sbw = jnp.pad(sw.reshape(G_NSA, HPG, wp_s), ((0, 0), (0, ROWS_S - HPG), (0, 0)))
    pt_flat = page_table.reshape(-1).astype(I32)
    cache_nsa2d = cache_nsa_kv.reshape(depth, n_pool, page * NSA_CH, HEAD_DIM)

    xp = x_prompt.reshape(batch * seq, D_MODEL)
    xs = _pad_rows(x_sample.reshape(dec_b, D_MODEL), ROWS_S)
    sb_p, sb_s, nsa_p, nsa_s, win_p, win_s = [], [], [], [], [], []
    w_keep = min(WINDOW, seq)
    for l in range(depth):
        wt = _prep_weights(l, w_in, g_q, g_k, w_cmp_k, w_cmp_v, w_out, w_gate, w_up, w_down)
        p = (norm_attn[l][None], b_cmp_k[l][None], b_cmp_v[l][None], g_k[l, 0][None],
             g_out[l][None], norm_ffn[l][None])
        xp, sb_rows, nsa_rows, win_rows = _prompt_layer(xp, wt, p, (toep, bias_c), rel_bias,
                                                        batch, seq)
        sb_p.append(sb_rows.reshape(batch, seq, 2, H_SB, HEAD_DIM))
        nsa_p.append(nsa_rows.reshape(batch, seq, 4, G_NSA, HEAD_DIM))
        win_p.append(win_rows.reshape(batch, w_keep, 2, G_NSA, HEAD_DIM))

        w_cmp0 = (w_cmp_k[l, 0].astype(BF16), w_cmp_v[l, 0].astype(BF16))
        xs, sb_rows, nsa_rows, new_win = _sample_layer(
            l, xs, wt, p, (sbc, sbw), rel_bias, pt_flat, cache_sb_kv, cache_nsa2d, state_win_kv[l],
            w_cmp0, dec_b, past_len, page)
        sb_s.append(sb_rows.reshape(dec_b, 1, 2, H_SB, HEAD_DIM))
        nsa_s.append(nsa_rows.reshape(dec_b, 1, 4, G_NSA, HEAD_DIM))
        win_s.append(new_win)
    return (xp.reshape(batch, seq, D_MODEL), xs[:dec_b].reshape(dec_b, 1, D_MODEL),
            jnp.stack(sb_p), jnp.stack(sb_s), jnp.stack(nsa_p), jnp.stack(nsa_s),
            jnp.stack(win_p), jnp.stack(win_s))
```

```python
import functools
import math

import jax
import jax.numpy as jnp
from jax import lax
from jax.experimental import pallas as pl
from jax.experimental.pallas import tpu as pltpu

F32 = jnp.float32
BF16 = jnp.bfloat16
I32 = jnp.int32

LANE = 128
HEAD_DIM = 128
G_NSA = 2
HPG = 4
H_NSA = G_NSA * HPG
H_SB = 8
D_NSA = H_NSA * HEAD_DIM
D_SB = H_SB * HEAD_DIM
D_MODEL = D_NSA + D_SB
D_KV = G_NSA * HEAD_DIM
CMP_BLOCK = 64
TOP_N = 16
N_LOCAL = 2
WINDOW = 512
N_BUCKETS = 32
MAX_DISTANCE = 128
EPS = 1e-6
FORCE_SCORE = 1e4
NEG_INF = -1e30
SCALE = HEAD_DIM ** -0.5

COL_TILE = 512
N_MAIN = D_NSA + 6 * D_KV + 3 * D_SB
QS_BLK = (D_NSA + 6 * D_KV) // LANE
KSB_BLK = QS_BLK + D_SB // LANE
VSB_BLK = KSB_BLK + D_SB // LANE
KS_BLK = (D_NSA + 2 * D_KV) // LANE
VS_BLK = KS_BLK + G_NSA
KW_BLK = VS_BLK + G_NSA
VW_BLK = KW_BLK + G_NSA
VMEM_LIMIT = 56 * 1024 * 1024

SB_TQ = 256
SB_TK = 256
SB_HB = 4
NSA_TQ = 256
SB_PAGES_PER_STEP = 16
CMP_PAGES_PER_STEP = 16


def _cparams(n_axes):
    return pltpu.CompilerParams(dimension_semantics=("arbitrary",) * n_axes,
                                vmem_limit_bytes=VMEM_LIMIT)


def _dot(a, b):
    return jnp.dot(a, b, preferred_element_type=F32)


def _dot_nt(a, b):
    return lax.dot_general(a, b, (((1,), (1,)), ((), ())), preferred_element_type=F32)


def _split_bf16(x):
    hi = x.astype(BF16)
    lo = (x - hi.astype(F32)).astype(BF16)
    return hi, lo


def _dot_split(x, u):
    hi, lo = _split_bf16(x)
    return _dot(hi, u) + _dot(lo, u)


def _rms(y, gain):
    ms = jnp.mean(y * y, axis=-1, keepdims=True)
    return y * lax.rsqrt(ms + EPS) * gain


def _neg_softplus(z):
    return -(jnp.maximum(z, 0.0) + jnp.log(1.0 + jnp.exp(-jnp.abs(z))))


def _t5_bucket(dist):
    n = jnp.maximum(dist, 0)
    max_exact = N_BUCKETS // 2
    nf = jnp.maximum(n, 1).astype(F32)
    large = max_exact + (jnp.log(nf / max_exact) / math.log(MAX_DISTANCE / max_exact)
                         * (N_BUCKETS - max_exact)).astype(I32)
    large = jnp.minimum(large, N_BUCKETS - 1)
    return jnp.where(n < max_exact, n, large)


def _pick_tile(m, candidates):
    for t in candidates:
        if m % t == 0:
            return t
    raise ValueError(f"no tile for {m}")


def _bias_body(rb_ref, toep_ref, bc_ref, sc_ref, sw_ref, *, seq, past_len, w_buf):
    h = pl.program_id(0)
    tq = NSA_TQ

    def table(dist):
        bk = _t5_bucket(dist)
        out = jnp.zeros(dist.shape, F32)
        for b in range(N_BUCKETS):
            out = jnp.where(bk == b, rb_ref[b, h], out)
        return out

    k = lax.broadcasted_iota(I32, (tq, tq), 0)
    q = lax.broadcasted_iota(I32, (tq, tq), 1)
    toep_ref[0] = table(tq + q - k)
    toep_ref[1] = table(q - k)
    n = lax.broadcasted_iota(I32, (LANE, tq), 0)
    qc = lax.broadcasted_iota(I32, (LANE, tq), 1)
    for i in range(seq // tq):
        bc_ref[i] = table(i * tq + qc - CMP_BLOCK * n - (CMP_BLOCK - 1))
    n1 = lax.broadcasted_iota(I32, sc_ref.shape, 1)
    sc_ref[...] = table(past_len - CMP_BLOCK * n1 - (CMP_BLOCK - 1))
    j1 = lax.broadcasted_iota(I32, sw_ref.shape, 1)
    sw_ref[...] = table(w_buf - j1)


def _bias_tables(rel_bias, seq, past_len, w_buf, nbp_s, wp_s):
    body = functools.partial(_bias_body, seq=seq, past_len=past_len, w_buf=w_buf)
    nq = seq // NSA_TQ
    return pl.pallas_call(
        body,
        grid=(H_NSA,),
        in_specs=[pl.BlockSpec(memory_space=pltpu.SMEM)],
        out_specs=[
            pl.BlockSpec((2, NSA_TQ, NSA_TQ), lambda h: (0, 0, h)),
            pl.BlockSpec((nq, LANE, NSA_TQ), lambda h: (0, 0, h)),
            pl.BlockSpec((None, 1, nbp_s), lambda h: (h, 0, 0)),
            pl.BlockSpec((None, 1, wp_s), lambda h: (h, 0, 0)),
        ],
        out_shape=[
            jax.ShapeDtypeStruct((2, NSA_TQ, H_NSA * NSA_TQ), F32),
            jax.ShapeDtypeStruct((nq, LANE, H_NSA * NSA_TQ), F32),
            jax.ShapeDtypeStruct((H_NSA, 1, nbp_s), F32),
            jax.ShapeDtypeStruct((H_NSA, 1, wp_s), F32),
        ],
        compiler_params=_cparams(1),
        name="bias_tables",
    )(rel_bias)


Q_TILES = D_NSA // COL_TILE
NSA_TILE0 = Q_TILES
WIN_TILE = NSA_TILE0 + 4 * D_KV // COL_TILE
SB_TILE0 = WIN_TILE + 2 * D_KV // COL_TILE + D_SB // COL_TILE


def _inproj_body(x_ref, na_ref, wa_ref, wb_ref, wg_ref, gain_ref, zb_ref, sb_ref, nsa_ref, win_ref,
                 gate_ref, h_ref, *, w_keep):
    j = pl.program_id(1)
    tm = x_ref.shape[0]
    half = COL_TILE // 2

    @pl.when(j == 0)
    def _():
        h = _rms(x_ref[...], na_ref[...]).astype(BF16)
        h_ref[...] = h
        gate_ref[...] = jax.nn.sigmoid(_dot(h, wg_ref[...]))

    gain = gain_ref[...]

    def emit(w_ref, norm_chunks, f32_ref=None, row0=0):
        h = h_ref[...]
        accs = [_dot(h, w_ref[:, :half]), _dot(h, w_ref[:, half:])]
        for c in range(COL_TILE // LANE):
            sl = slice(c * LANE, (c + 1) * LANE)
            y = accs[c * LANE // half][:, (c * LANE) % half:(c * LANE) % half + LANE]
            if c in norm_chunks:
                y = _rms(y, gain[:, sl])
            zb_ref[:, sl] = y.astype(BF16)
            if f32_ref is not None:
                f32_ref[:, sl] = y[row0:, :]

    @pl.when(j < Q_TILES)
    def _():
        emit(wa_ref, (0, 1, 2, 3))

    @pl.when(j == NSA_TILE0)
    def _():
        emit(wa_ref, (), nsa_ref)

    @pl.when(j == NSA_TILE0 + 1)
    def _():
        emit(wa_ref, (0, 1), nsa_ref)

    @pl.when(j == WIN_TILE)
    def _():
        emit(wa_ref, (0, 1), win_ref, tm - w_keep)

    @pl.when((j > WIN_TILE) & (j < SB_TILE0))
    def _():
        emit(wb_ref, ())

    @pl.when(j >= SB_TILE0)
    def _():
        emit(wb_ref, (), sb_ref)


def _inproj(x2d, na, w_a, w_b, w_gate, gain_cols, seq, w_keep):
    m = x2d.shape[0]
    tm = _pick_tile(seq, (1024, 512, 256, 128, 16))
    assert w_keep <= tm
    tpb = seq // tm
    n_sb = 2 * D_SB // COL_TILE
    n_nsa = 4 * D_KV // COL_TILE
    n_a = w_a.shape[1] // COL_TILE
    n_b = w_b.shape[1] // COL_TILE
    assert n_a == WIN_TILE + 1 and n_a + n_b == N_MAIN // COL_TILE
    body = functools.partial(_inproj_body, w_keep=w_keep)
    return pl.pallas_call(
        body,
        grid=(m // tm, n_a + n_b),
        in_specs=[
            pl.BlockSpec((tm, D_MODEL), lambda i, j: (i, 0)),
            pl.BlockSpec((1, D_MODEL), lambda i, j: (0, 0)),
            pl.BlockSpec((D_MODEL, COL_TILE), lambda i, j: (0, jnp.minimum(j, n_a - 1))),
            pl.BlockSpec((D_MODEL, COL_TILE), lambda i, j: (0, jnp.maximum(j - n_a, 0))),
            pl.BlockSpec((D_MODEL, 2 * LANE), lambda i, j: (0, 0)),
            pl.BlockSpec((1, COL_TILE), lambda i, j: (0, j)),
        ],
        out_specs=[
            pl.BlockSpec((tm, COL_TILE), lambda i, j: (i, j)),
            pl.BlockSpec((tm, COL_TILE), lambda i, j: (i, jnp.clip(j - SB_TILE0, 0, n_sb - 1))),
            pl.BlockSpec((tm, COL_TILE), lambda i, j: (i, jnp.clip(j - NSA_TILE0, 0, n_nsa - 1))),
            pl.BlockSpec((w_keep, COL_TILE), lambda i, j: (i // tpb, 0)),
            pl.BlockSpec((tm, 2 * LANE), lambda i, j: (i, 0)),
        ],
        out_shape=[
            jax.ShapeDtypeStruct((m, N_MAIN), BF16),
            jax.ShapeDtypeStruct((m, 2 * D_SB), F32),
            jax.ShapeDtypeStruct((m, 4 * D_KV), F32),
            jax.ShapeDtypeStruct((m // seq * w_keep, 2 * D_KV), F32),
            jax.ShapeDtypeStruct((m, 2 * LANE), F32),
        ],
        scratch_shapes=[pltpu.VMEM((tm, D_MODEL), BF16)],
        compiler_params=_cparams(2),
        name="inproj",
    )(x2d, na, w_a, w_b, w_gate, gain_cols)


def _outproj_body(on_ref, os_ref, go_ref, w_ref, x_ref, y_ref, h_ref):
    j = pl.program_id(1)

    @pl.when(j == 0)
    def _():
        go = go_ref[...]
        h_ref[:, :D_NSA] = _rms(on_ref[...], go[:, :D_NSA]).astype(BF16)
        h_ref[:, D_NSA:] = _rms(os_ref[...], go[:, D_NSA:]).astype(BF16)

    y_ref[...] = x_ref[...] + _dot(h_ref[...], w_ref[...])


def _outproj(o_nsa, o_sb, g_out, w_out, x2d):
    m = x2d.shape[0]
    tm = _pick_tile(m, (1024, 512, 256, 128, 16))
    return pl.pallas_call(
        _outproj_body,
        grid=(m // tm, D_MODEL // COL_TILE),
        in_specs=[
            pl.BlockSpec((tm, D_NSA), lambda i, j: (i, 0)),
            pl.BlockSpec((tm, D_SB), lambda i, j: (i, 0)),
            pl.BlockSpec((1, D_MODEL), lambda i, j: (0, 0)),
            pl.BlockSpec((D_MODEL, COL_TILE), lambda i, j: (0, j)),
            pl.BlockSpec((tm, COL_TILE), lambda i, j: (i, j)),
        ],
        out_specs=pl.BlockSpec((tm, COL_TILE), lambda i, j: (i, j)),
        out_shape=jax.ShapeDtypeStruct((m, D_MODEL), F32),
        scratch_shapes=[pltpu.VMEM((tm, D_MODEL), BF16)],
        compiler_params=_cparams(2),
        name="outproj",
    )(o_nsa, o_sb, g_out, w_out, x2d)


def _ffn_body(x_ref, nf_ref, wg_ref, wu_ref, wd_ref, y_ref, h_ref, acc_ref):
    j = pl.program_id(1)

    @pl.when(j == 0)
    def _():
        h_ref[...] = _rms(x_ref[...], nf_ref[...]).astype(BF16)
        acc_ref[...] = jnp.zeros_like(acc_ref)

    h = h_ref[...]
    half = wg_ref.shape[1] // 2
    cols = (slice(0, half), slice(half, 2 * half))
    gate = [_dot(h, wg_ref[:, c]) for c in cols]
    up = [_dot(h, wu_ref[:, c]) for c in cols]
    act = [(jax.nn.silu(gt) * u).astype(BF16) for gt, u in zip(gate, up)]
    acc_ref[...] += _dot(act[0], wd_ref[cols[0], :]) + _dot(act[1], wd_ref[cols[1], :])

    @pl.when(j == pl.num_programs(1) - 1)
    def _():
        y_ref[...] = x_ref[...] + acc_ref[...]


def _ffn(x2d, norm_ffn, w_gate, w_up, w_down):
    m = x2d.shape[0]
    d_ff = w_gate.shape[1]
    tm = _pick_tile(m, (512, 256, 128, 16))
    tf = _pick_tile(d_ff, (512, 256))
    return pl.pallas_call(
        _ffn_body,
        grid=(m // tm, d_ff // tf),
        in_specs=[
            pl.BlockSpec((tm, D_MODEL), lambda i, j: (i, 0)),
            pl.BlockSpec((1, D_MODEL), lambda i, j: (0, 0)),
            pl.BlockSpec((D_MODEL, tf), lambda i, j: (0, j)),
            pl.BlockSpec((D_MODEL, tf), lambda i, j: (0, j)),
            pl.BlockSpec((tf, D_MODEL), lambda i, j: (j, 0)),
        ],
        out_specs=pl.BlockSpec((tm, D_MODEL), lambda i, j: (i, 0)),
        out_shape=jax.ShapeDtypeStruct((m, D_MODEL), F32),
        scratch_shapes=[pltpu.VMEM((tm, D_MODEL), BF16), pltpu.VMEM((tm, D_MODEL), F32)],
        compiler_params=_cparams(2),
        name="ffn",
    )(x2d, norm_ffn, w_gate, w_up, w_down)


def _sb_body(q_ref, k_ref, v_ref, u_ref, o_ref, acc_ref, run_ref):
    i = pl.program_id(2)
    row = lax.broadcasted_iota(I32, (SB_TQ, SB_TK), 0)
    col = lax.broadcasted_iota(I32, (SB_TQ, SB_TK), 1)
    strict = col < row

    def tile(kb, diagonal):
        off = pl.multiple_of(kb * SB_TK, SB_TK)
        heads = range(SB_HB)
        sls = [slice(h * HEAD_DIM, (h + 1) * HEAD_DIM) for h in heads]
        z = [_dot_nt(q_ref[:, sl], k_ref[pl.ds(off, SB_TK), sl]) * SCALE for sl in sls]
        log_1m = [_neg_softplus(zh) for zh in z]
        if diagonal:
            log_1m = [jnp.where(strict, lh, 0.0) for lh in log_1m]
        parts = [_split_bf16(lh) for lh in log_1m]
        u = u_ref[...]
        later = [_dot(hi, u) + _dot(lo, u) for hi, lo in parts]
        total = [jnp.sum(lh, axis=-1, keepdims=True) for lh in log_1m]
        if diagonal:
            a = [jnp.where(strict, jnp.exp(z[h] + log_1m[h] + later[h]), 0.0) for h in heads]
        else:
            run = [run_ref[h] for h in heads]
            a = [jnp.exp(z[h] + log_1m[h] + later[h] + run[h]) for h in heads]
        pv = [_dot(a[h].astype(BF16), v_ref[pl.ds(off, SB_TK), sls[h]]) for h in heads]
        for h in heads:
            if diagonal:
                acc_ref[h] = pv[h]
                run_ref[h] = total[h]
            else:
                acc_ref[h] += pv[h]
                run_ref[h] = run[h] + total[h]

    tile(i, True)

    def step(it, carry):
        tile(i - 1 - it, False)
        return carry

    lax.fori_loop(0, i, step, 0)
    for h in range(SB_HB):
        o_ref[:, h * HEAD_DIM:(h + 1) * HEAD_DIM] = acc_ref[h]


def _sb_prompt(zb, batch, seq):
    nq = seq // SB_TQ
    width = SB_HB * HEAD_DIM
    blk = LANE // HEAD_DIM * SB_HB
    assert QS_BLK % blk == 0 and KSB_BLK % blk == 0 and VSB_BLK % blk == 0
    r = lax.broadcasted_iota(I32, (SB_TK, SB_TK), 0)
    c = lax.broadcasted_iota(I32, (SB_TK, SB_TK), 1)
    u = (r > c).astype(BF16)
    return pl.pallas_call(
        _sb_body,
        grid=(batch, H_SB // SB_HB, nq),
        in_specs=[
            pl.BlockSpec((SB_TQ, width), lambda b, h, i: (b * nq + i, QS_BLK // blk + h)),
            pl.BlockSpec((seq, width), lambda b, h, i: (b, KSB_BLK // blk + h)),
            pl.BlockSpec((seq, width), lambda b, h, i: (b, VSB_BLK // blk + h)),
            pl.BlockSpec((SB_TK, SB_TK), lambda b, h, i: (0, 0)),
        ],
        out_specs=pl.BlockSpec((SB_TQ, width), lambda b, h, i: (b * nq + i, h)),
        out_shape=jax.ShapeDtypeStruct((batch * seq, D_SB), F32),
        scratch_shapes=[pltpu.VMEM((SB_HB, SB_TQ, HEAD_DIM), F32), pltpu.VMEM((SB_HB, SB_TQ, 1), F32)],
        compiler_params=_cparams(3),
        name="sb_prompt",
    )(zb, zb, zb, u)


def _cmp_prompt_body(x0_ref, x1_ref, x2_ref, x3_ref, wk_ref, wv_ref, bk_ref, bv_ref, gk_ref,
                     kc_ref, vct_ref, *, nb):
    pad = jnp.zeros((LANE - nb, HEAD_DIM), F32)
    for c, x_ref in enumerate((x0_ref, x1_ref, x2_ref, x3_ref)):
        rows = jnp.concatenate(
            [x_ref[pl.ds(l, nb, stride=CMP_BLOCK), :] for l in range(CMP_BLOCK)], axis=1)
        rows = rows.astype(BF16)
        if c < G_NSA:
            y = _rms(_dot(rows, wk_ref[...]) + bk_ref[...], gk_ref[...])
            kc_ref[c] = jnp.concatenate([y, pad], axis=0).astype(BF16)
        else:
            y = _dot(rows, wv_ref[...]) + bv_ref[...]
            vct_ref[c - G_NSA] = jnp.concatenate([y, pad], axis=0).T.astype(BF16)


def _cmp_prompt(nsa_rows, batch, seq, wk, wv, bk, bv, gk0):
    nb = seq // CMP_BLOCK
    assert nb % 8 == 0 and nb < LANE
    body = functools.partial(_cmp_prompt_body, nb=nb)
    shape = jax.ShapeDtypeStruct((batch, G_NSA, LANE, HEAD_DIM), BF16)

    def x_spec(c):
        return pl.BlockSpec((seq, HEAD_DIM), lambda b: (b, c))

    return pl.pallas_call(
        body,
        grid=(batch,),
        in_specs=[
            x_spec(0), x_spec(1), x_spec(2), x_spec(3),
            pl.BlockSpec(wk.shape, lambda b: (0, 0)),
            pl.BlockSpec(wv.shape, lambda b: (0, 0)),
            pl.BlockSpec((1, HEAD_DIM), lambda b: (0, 0)),
            pl.BlockSpec((1, HEAD_DIM), lambda b: (0, 0)),
            pl.BlockSpec((1, HEAD_DIM), lambda b: (0, 0)),
        ],
        out_specs=[pl.BlockSpec((None, G_NSA, LANE, HEAD_DIM), lambda b: (b, 0, 0, 0))] * 2,
        out_shape=[shape, shape],
        compiler_params=_cparams(1),
        name="cmp_prompt",
    )(nsa_rows, nsa_rows, nsa_rows, nsa_rows, wk, wv, bk, bv, gk0)


def _masked_softmax(s, valid):
    logit = jnp.where(valid, s, NEG_INF)
    e = jnp.exp(logit - jnp.max(logit, axis=-1, keepdims=True))
    return jnp.where(valid, e / jnp.sum(e, axis=-1, keepdims=True), 0.0)


class _Flash:
    def __init__(self, m_ref, l_ref, a_ref):
        self.m_ref, self.l_ref, self.a_ref = m_ref, l_ref, a_ref

    def first(self, logit, v):
        m = jnp.max(logit, axis=-1, keepdims=True)
        p = jnp.exp(logit - m)
        self.m_ref[...] = m
        self.l_ref[...] = jnp.sum(p, axis=-1, keepdims=True)
        self.a_ref[...] = _dot(p.astype(BF16), v)

    def update(self, logit, v):
        m_old = self.m_ref[...]
        m = jnp.maximum(m_old, jnp.max(logit, axis=-1, keepdims=True))
        alpha = jnp.exp(m_old - m)
        p = jnp.exp(logit - m)
        self.m_ref[...] = m
        self.l_ref[...] = alpha * self.l_ref[...] + jnp.sum(p, axis=-1, keepdims=True)
        self.a_ref[...] = alpha * self.a_ref[...] + _dot(p.astype(BF16), v)

    def result(self):
        return self.a_ref[...] / self.l_ref[...]


class _FlashT:
    def __init__(self, m_ref, l_ref, a_ref):
        self.m_ref, self.l_ref, self.a_ref = m_ref, l_ref, a_ref

    @staticmethod
    def step(states, logits, v_ts, first):
        ms = [jnp.max(lg, axis=0, keepdims=True) for lg in logits]
        if not first:
            olds = [st.m_ref[...] for st in states]
            ms = [jnp.maximum(o, m) for o, m in zip(olds, ms)]
            alphas = [jnp.exp(o - m) for o, m in zip(olds, ms)]
        ps = [jnp.exp(lg - m) for lg, m in zip(logits, ms)]
        sums = [jnp.sum(p, axis=0, keepdims=True) for p in ps]
        pvs = [_dot(v_t, p.astype(BF16)) for v_t, p in zip(v_ts, ps)]
        for n, st in enumerate(states):
            st.m_ref[...] = ms[n]
            if first:
                st.l_ref[...] = sums[n]
                st.a_ref[...] = pvs[n]
            else:
                st.l_ref[...] = alphas[n] * st.l_ref[...] + sums[n]
                st.a_ref[...] = alphas[n] * st.a_ref[...] + pvs[n]

    def update(self, logit, v_t):
        _FlashT.step([self], [logit], [v_t], False)

    def result(self):
        return self.a_ref[...] / self.l_ref[...]


def _nsa_body(rb_ref, q_ref, kc_ref, vct_ref, ks_ref, vs_ref, kw_ref, vw_ref, g_ref, toep_ref,
              bc_ref, et_ref, o_ref, vst_ref, vwt_ref, m_s, l_s, a_s, m_w, l_w, a_w, *, nb, seq):
    g = pl.program_id(1)
    i = pl.program_id(2)
    tq = NSA_TQ
    sub = tq // LANE
    width = HPG * tq
    q0 = i * tq

    @pl.when(i == 0)
    def _():
        for c in range(seq // LANE):
            rows = slice(c * LANE, (c + 1) * LANE)
            cols = slice((c % sub) * LANE, (c % sub + 1) * LANE)
            vst_ref[c // sub, :, cols] = vs_ref[rows, :].astype(F32).T.astype(BF16)
            vwt_ref[c // sub, :, cols] = vw_ref[rows, :].astype(F32).T.astype(BF16)

    qs = jnp.concatenate([q_ref[:, hh * LANE:(hh + 1) * LANE] for hh in range(HPG)], axis=0)

    def q_off(rows):
        return lax.broadcasted_iota(I32, (rows, width), 1) % tq

    n_r = lax.broadcasted_iota(I32, (LANE, width), 0)
    s_c = _dot_nt(kc_ref[...], qs) * SCALE + bc_ref[...]
    valid_c = (CMP_BLOCK * n_r + (CMP_BLOCK - 1) <= q0 + q_off(LANE)) & (n_r < nb)
    logit = jnp.where(valid_c, s_c, NEG_INF)
    e = jnp.exp(logit - jnp.max(logit, axis=0, keepdims=True))
    p_c = jnp.where(valid_c, e / jnp.sum(e, axis=0, keepdims=True), 0.0)
    oc_t = _dot(vct_ref[...], p_c.astype(BF16))
    imp_t = p_c[:, 0:tq]
    for hh in range(1, HPG):
        imp_t = imp_t + p_c[:, hh * tq:(hh + 1) * tq]

    nbr = -(-nb // 8) * 8
    n_i = lax.broadcasted_iota(I32, (nbr, tq), 0)
    cur = (q0 + lax.broadcasted_iota(I32, (nbr, tq), 1)) // CMP_BLOCK
    forced = (n_i == 0) | (n_i >= cur - (N_LOCAL - 1))
    score = jnp.where(n_i > cur, -jnp.inf, jnp.where(forced, FORCE_SCORE, imp_t[0:nbr]))
    rank = jnp.zeros((nbr, tq), I32)
    for j in range(nb):
        sj = jnp.broadcast_to(score[j:j + 1, :], (nbr, tq))
        ahead = (sj > score) | ((sj == score) & (n_i > j))
        rank = rank + ahead.astype(I32)
    sel_t = ((rank < min(TOP_N, nb)) & (score > -jnp.inf)).astype(F32)
    if nbr < LANE:
        sel_t = jnp.concatenate([sel_t, jnp.zeros((LANE - nbr, tq), F32)], axis=0)
    sel_t = sel_t.astype(BF16)

    k_r = lax.broadcasted_iota(I32, (tq, width), 0)
    q_c = q_off(tq)
    causal = k_r <= q_c
    bias_far = jnp.concatenate(
        [jnp.full((1, tq), rb_ref[N_BUCKETS - 1, g * HPG + hh], F32) for hh in range(HPG)], axis=1)

    def chunk(ref, c):
        return ref[pl.ds(pl.multiple_of(c * tq, tq), tq), :]

    def scores(ref, c):
        return _dot_nt(chunk(ref, c), qs) * SCALE

    def selected(c):
        hit = _dot(chunk(et_ref, c), sel_t)
        return jnp.concatenate([hit] * HPG, axis=1) > 0.5

    fs = _FlashT(m_s, l_s, a_s)
    fw = _FlashT(m_w, l_w, a_w)

    def both(c, bias, sel_extra, win_mask, first):
        sel_mask = selected(c) if sel_extra is None else selected(c) & sel_extra
        lg_s = jnp.where(sel_mask, scores(ks_ref, c) + bias, NEG_INF)
        lg_w = scores(kw_ref, c) + bias
        if win_mask is not None:
            lg_w = jnp.where(win_mask, lg_w, NEG_INF)
        _FlashT.step([fs, fw], [lg_s, lg_w], [vst_ref[c], vwt_ref[c]], first)

    both(i, toep_ref[1], causal, causal, True)

    @pl.when(i >= 1)
    def _():
        both(i - 1, toep_ref[0], None, None, False)

    @pl.when(i >= 2)
    def _():
        both(i - 2, bias_far, None, k_r >= q_c, False)

    def far_step(c, carry):
        fs.update(jnp.where(selected(c), scores(ks_ref, c) + bias_far, NEG_INF), vst_ref[c])
        return carry

    lax.fori_loop(0, jnp.maximum(i - 2, 0), far_step, 0)

    g_t = jnp.concatenate([g_ref[t * LANE:(t + 1) * LANE, :].T for t in range(sub)], axis=1)

    def gate(branch):
        return jnp.concatenate(
            [g_t[branch * HPG + hh:branch * HPG + hh + 1, :] for hh in range(HPG)], axis=1)

    o_t = gate(0) * oc_t + gate(1) * fs.result() + gate(2) * fw.result()
    for hh in range(HPG):
        for t in range(sub):
            lanes = slice(hh * tq + t * LANE, hh * tq + (t + 1) * LANE)
            o_ref[t * LANE:(t + 1) * LANE, hh * LANE:(hh + 1) * LANE] = o_t[:, lanes].T


def _nsa_prompt(rel_bias, zb, kc, vct, gates, toep, bias_c, batch, seq):
    tq = NSA_TQ
    assert WINDOW == 2 * tq and tq % LANE == 0 and seq % tq == 0
    nq = seq // tq
    nb = seq // CMP_BLOCK
    assert nb <= LANE
    width = HPG * tq
    k = lax.broadcasted_iota(I32, (seq, LANE), 0)
    n = lax.broadcasted_iota(I32, (seq, LANE), 1)
    expand_t = (k // CMP_BLOCK == n).astype(BF16)
    body = functools.partial(_nsa_body, nb=nb, seq=seq)

    def kv_spec(blk):
        return pl.BlockSpec((seq, HEAD_DIM), lambda b, g, i: (b, blk + g))

    stat = pltpu.VMEM((1, width), F32)
    accum = pltpu.VMEM((HEAD_DIM, width), F32)
    v_t = pltpu.VMEM((nq, HEAD_DIM, tq), BF16)
    return pl.pallas_call(
        body,
        grid=(batch, G_NSA, nq),
        in_specs=[
            pl.BlockSpec(memory_space=pltpu.SMEM),
            pl.BlockSpec((tq, HPG * HEAD_DIM), lambda b, g, i: (b * nq + i, g)),
            pl.BlockSpec((None, None, LANE, HEAD_DIM), lambda b, g, i: (b, g, 0, 0)),
            pl.BlockSpec((None, None, HEAD_DIM, LANE), lambda b, g, i: (b, g, 0, 0)),
            kv_spec(KS_BLK), kv_spec(VS_BLK), kv_spec(KW_BLK), kv_spec(VW_BLK),
            pl.BlockSpec((tq, LANE), lambda b, g, i: (b * nq + i, g)),
            pl.BlockSpec((2, tq, width), lambda b, g, i: (0, 0, g)),
            pl.BlockSpec((None, LANE, width), lambda b, g, i: (i, 0, g)),
            pl.BlockSpec((seq, LANE), lambda b, g, i: (0, 0)),
        ],
        out_specs=pl.BlockSpec((tq, HPG * HEAD_DIM), lambda b, g, i: (b * nq + i, g)),
        out_shape=jax.ShapeDtypeStruct((batch * seq, D_NSA), F32),
        scratch_shapes=[v_t, v_t, stat, stat, accum, stat, stat, accum],
        compiler_params=_cparams(3),
        name="nsa_prompt",
    )(rel_bias, zb, kc, vct, zb, zb, zb, zb, gates, toep, bias_c, expand_t)


def _cast_body(w_ref, o_ref):
    o_ref[...] = w_ref[...].astype(BF16)


def _cast_layer_bf16(w, l, cols=None):
    _, rows, n = w.shape
    cols = n if cols is None else cols
    assert cols % LANE == 0
    tr = _pick_tile(rows, (256, 128))
    return pl.pallas_call(
        _cast_body,
        grid=(rows // tr,),
        in_specs=[pl.BlockSpec((None, tr, cols), lambda i: (l, i, 0))],
        out_specs=pl.BlockSpec((tr, cols), lambda i: (i, 0)),
        out_shape=jax.ShapeDtypeStruct((rows, cols), BF16),
        compiler_params=_cparams(1),
        name="cast_bf16",
    )(w)


def _prep_weights(l, w_in, g_q, g_k, w_cmp_k, w_cmp_v, w_out, w_gate, w_up, w_down):
    n_gate = 3 * H_NSA
    g0 = D_NSA + 6 * D_KV
    wi = w_in[l]
    w_a = _cast_layer_bf16(w_in, l, g0)
    w_b = wi[:, g0 + n_gate:].astype(BF16)
    wg = wi[:, g0:g0 + n_gate].reshape(D_MODEL, 3, G_NSA, HPG)
    wg = jnp.transpose(wg, (0, 2, 1, 3)).reshape(D_MODEL, G_NSA, 3 * HPG)
    wg = jnp.pad(wg, ((0, 0), (0, 0), (0, LANE - 3 * HPG))).reshape(D_MODEL, G_NSA * LANE)
    ones = jnp.ones((HEAD_DIM,), F32)
    gain = jnp.concatenate(
        [jnp.tile(g_q[l], H_NSA), jnp.tile(ones, 2 * G_NSA),
         jnp.tile(g_k[l, 1], G_NSA), jnp.tile(ones, G_NSA),
         jnp.tile(g_k[l, 2], G_NSA), jnp.tile(ones, G_NSA),
         jnp.tile(ones, 3 * H_SB)])[None, :]
    wk = w_cmp_k[l].reshape(CMP_BLOCK * HEAD_DIM, HEAD_DIM).astype(BF16)
    wv = w_cmp_v[l].reshape(CMP_BLOCK * HEAD_DIM, HEAD_DIM).astype(BF16)
    return dict(
        w_a=w_a, w_b=w_b, w_gate=wg.astype(BF16), gain=gain, wk=wk, wv=wv,
        w_kv=jnp.concatenate([wk, wv], axis=1),
        w_out=_cast_layer_bf16(w_out, l), w_gate_ffn=_cast_layer_bf16(w_gate, l),
        w_up=_cast_layer_bf16(w_up, l), w_down=_cast_layer_bf16(w_down, l))


def _prompt_layer(x2d, wt, p, tables, rel_bias, batch, seq):
    na, bk, bv, gk0, go, nf = p
    zb, sb_rows, nsa_rows, win_rows, gates = _inproj(
        x2d, na, wt["w_a"], wt["w_b"], wt["w_gate"], wt["gain"], seq, min(WINDOW, seq))
    kc, vct = _cmp_prompt(nsa_rows, batch, seq, wt["wk"], wt["wv"], bk, bv, gk0)
    o_nsa = _nsa_prompt(rel_bias, zb, kc, vct, gates, tables[0], tables[1], batch, seq)
    o_sb = _sb_prompt(zb, batch, seq)
    x1 = _outproj(o_nsa, o_sb, go, wt["w_out"], x2d)
    x2 = _ffn(x1, nf, wt["w_gate_ffn"], wt["w_up"], wt["w_down"])
    return x2, sb_rows, nsa_rows, win_rows


ROWS_S = 16


def _sb_dec_body(pt_ref, q_ref, *refs, npg, pps, page):
    page_refs = refs[:pps]
    up_ref, upp_ref, o_ref, z_ref, acc_ref = refs[pps:]
    ph = pl.program_id(1)
    c = pl.program_id(2)
    width = page * H_SB
    r = lax.broadcasted_iota(I32, (ROWS_S, width), 0)
    lane = lax.broadcasted_iota(I32, (ROWS_S, width), 1)
    own_head = (lane % H_SB) == r

    @pl.when(ph == 0)
    def _():
        q = q_ref[...]
        for u in range(pps):
            ks = page_refs[u][...].reshape(width, HEAD_DIM).astype(BF16)
            zt = _dot_nt(q, ks)
            z_ref[pl.ds(c * pps + u, 1), :] = jnp.sum(jnp.where(own_head, zt, 0.0), axis=0,
                                                      keepdims=True)

    @pl.when((ph == 1) & (c == 0))
    def _():
        z = z_ref[...] * SCALE
        log_1m = _neg_softplus(z)
        cs = _dot_split(log_1m, up_ref[...])
        hi, lo = _split_bf16(cs[:, width:])
        later_pages = _dot(upp_ref[...], hi) + _dot(upp_ref[...], lo)
        z_ref[...] = jnp.exp(z + log_1m + cs[:, :width] + later_pages)
        acc_ref[...] = jnp.zeros_like(acc_ref)

    @pl.when(ph == 1)
    def _():
        for u in range(pps):
            vs = page_refs[u][...].reshape(width, HEAD_DIM).astype(BF16)
            a = jnp.broadcast_to(z_ref[pl.ds(c * pps + u, 1), :], (ROWS_S, width))
            acc_ref[...] += _dot(jnp.where(own_head, a, 0.0).astype(BF16), vs)

    @pl.when((ph == 1) & (c == pl.num_programs(2) - 1))
    def _():
        o_ref[...] = acc_ref[0:H_SB, :]


def _sb_decode(l, pt_flat, q_sb, cache_sb):
    dec_b = q_sb.shape[0]
    page = cache_sb.shape[2]
    npg = pt_flat.shape[0] // dec_b
    pps = SB_PAGES_PER_STEP
    assert npg % pps == 0 and npg % 8 == 0
    width = page * H_SB
    r = lax.broadcasted_iota(I32, (width, 2 * width), 0)
    c = lax.broadcasted_iota(I32, (width, 2 * width), 1)
    same_head = (r % H_SB) == (c % H_SB)
    up = (same_head & ((c >= width) | (r // H_SB > c // H_SB))).astype(BF16)
    pr = lax.broadcasted_iota(I32, (npg, npg), 0)
    pc = lax.broadcasted_iota(I32, (npg, npg), 1)
    upp = (pc > pr).astype(BF16)
    body = functools.partial(_sb_dec_body, npg=npg, pps=pps, page=page)

    def page_spec(u):
        return pl.BlockSpec((None, None, page, None, H_SB, HEAD_DIM),
                            lambda b, ph, c, pt: (l, pt[b * npg + c * pps + u], 0, ph, 0, 0))

    grid_spec = pltpu.PrefetchScalarGridSpec(
        num_scalar_prefetch=1,
        grid=(dec_b, 2, npg // pps),
        in_specs=[pl.BlockSpec((None, ROWS_S, HEAD_DIM), lambda b, ph, c, pt: (b, 0, 0))]
        + [page_spec(u) for u in range(pps)]
        + [pl.BlockSpec((width, 2 * width), lambda b, ph, c, pt: (0, 0)),
           pl.BlockSpec((npg, npg), lambda b, ph, c, pt: (0, 0))],
        out_specs=pl.BlockSpec((None, H_SB, HEAD_DIM), lambda b, ph, c, pt: (b, 0, 0)),
        scratch_shapes=[pltpu.VMEM((npg, width), F32), pltpu.VMEM((ROWS_S, HEAD_DIM), F32)],
    )
    return pl.pallas_call(
        body,
        grid_spec=grid_spec,
        out_shape=jax.ShapeDtypeStruct((dec_b, H_SB, HEAD_DIM), F32),
        compiler_params=_cparams(3),
        name="sb_decode",
    )(pt_flat, q_sb, *([cache_sb] * pps), up, upp)


NSA_CH = 4 * G_NSA


def _cmp_dec_body(pt_ref, *refs, pps, page):
    page_refs = refs[:pps]
    w_ref, kc_ref, vc_ref, yk_ref, yv_ref = refs[pps:]
    halves = page // CMP_BLOCK
    nblk = pps * halves
    acc = jnp.zeros((nblk * NSA_CH, 2 * HEAD_DIM), F32)

    def token_rows(u, hf, l):
        return page_refs[u][pl.ds((hf * CMP_BLOCK + l) * NSA_CH, NSA_CH), :]

    for l in range(0, CMP_BLOCK, 2):
        x_l = jnp.concatenate(
            [jnp.concatenate([token_rows(u, hf, l), token_rows(u, hf, l + 1)], axis=1)
             for u in range(pps) for hf in range(halves)], axis=0).astype(BF16)
        acc = acc + _dot(x_l, w_ref[l * HEAD_DIM:(l + 2) * HEAD_DIM, :])
    yk_ref[...] = acc[:, :HEAD_DIM]
    yv_ref[...] = acc[:, HEAD_DIM:]
    for g in range(G_NSA):
        kc_ref[g] = yk_ref[pl.ds(g, nblk, stride=NSA_CH), :]
        vc_ref[g] = yv_ref[pl.ds(G_NSA + g, nblk, stride=NSA_CH), :]


def _cmp_decode(l, pt_flat, cache_nsa2d, w_kv, dec_b, page):
    npg = pt_flat.shape[0] // dec_b
    pps = min(CMP_PAGES_PER_STEP, npg)
    assert npg % pps == 0
    halves = page // CMP_BLOCK
    nblk = pps * halves
    body = functools.partial(_cmp_dec_body, pps=pps, page=page)

    def page_spec(u):
        return pl.BlockSpec((None, None, page * NSA_CH, HEAD_DIM),
                            lambda b, c, pt: (l, pt[b * npg + c * pps + u], 0, 0))

    out_spec = pl.BlockSpec((None, G_NSA, nblk, HEAD_DIM), lambda b, c, pt: (b, 0, c, 0))
    shape = jax.ShapeDtypeStruct((dec_b, G_NSA, npg * halves, HEAD_DIM), F32)
    grid_spec = pltpu.PrefetchScalarGridSpec(
        num_scalar_prefetch=1,
        grid=(dec_b, npg // pps),
        in_specs=[page_spec(u) for u in range(pps)]
        + [pl.BlockSpec(w_kv.shape, lambda b, c, pt: (0, 0))],
        out_specs=[out_spec, out_spec],
        scratch_shapes=[pltpu.VMEM((nblk * NSA_CH, HEAD_DIM), F32)] * 2,
    )
    return pl.pallas_call(
        body,
        grid_spec=grid_spec,
        out_shape=[shape, shape],
        compiler_params=_cparams(2),
        name="cmp_decode",
    )(pt_flat, *([cache_nsa2d] * pps), w_kv)


def _nsa_sel_body(q_ref, kcr_ref, vcr_ref, new_ref, wk0_ref, wv0_ref, bk_ref, bv_ref, gk_ref,
                  sbc_ref, kw_ref, vw_ref, sbw_ref, idx_ref, oc_ref, ow_ref,
                  *, nb0, nbp, past_len, w_buf, wp):
    nb = nb0 + 1
    new = new_ref[...].astype(BF16)
    new_k = _rms(_dot(new, wk0_ref[...]) + bk_ref[...], gk_ref[...])
    new_v = _dot(new, wv0_ref[...]) + bv_ref[...]
    row8 = lax.broadcasted_iota(I32, (8, HEAD_DIM), 0)
    n1 = lax.broadcasted_iota(I32, (1, nbp), 1)
    nq = lax.broadcasted_iota(I32, (ROWS_S, nbp), 1)
    pos_q = past_len
    pad = jnp.zeros((nbp - nb0 - 8, HEAD_DIM), F32)
    for g in range(G_NSA):
        q = q_ref[g]
        kc = jnp.concatenate(
            [_rms(kcr_ref[g] + bk_ref[...], gk_ref[...]),
             jnp.where(row8 == 0, jnp.broadcast_to(new_k[g:g + 1], (8, HEAD_DIM)), 0.0), pad],
            axis=0).astype(BF16)
        vc = jnp.concatenate(
            [vcr_ref[g] + bv_ref[...],
             jnp.where(row8 == 0, jnp.broadcast_to(new_v[G_NSA + g:G_NSA + g + 1], (8, HEAD_DIM)),
                       0.0), pad], axis=0).astype(BF16)
        s_c = _dot_nt(q, kc) * SCALE + sbc_ref[g]
        valid_c = (CMP_BLOCK * nq + (CMP_BLOCK - 1) <= pos_q) & (nq < nb)
        p_c = _masked_softmax(s_c, valid_c)
        oc_ref[g] = _dot(p_c.astype(BF16), vc)
        imp = p_c[0:1]
        for hh in range(1, HPG):
            imp = imp + p_c[hh:hh + 1]

        cur = pos_q // CMP_BLOCK
        forced = (n1 == 0) | (n1 >= cur - (N_LOCAL - 1))
        score = jnp.where(n1 > cur, -jnp.inf, jnp.where(forced, FORCE_SCORE, imp))
        sq = jnp.broadcast_to(score, (LANE, nbp))
        col = jnp.concatenate([sq[:, k * LANE:(k + 1) * LANE].T for k in range(nbp // LANE)],
                              axis=0)
        j_i = lax.broadcasted_iota(I32, (nbp, LANE), 0)
        ranks = []
        for k in range(nbp // LANE):
            mine = jnp.broadcast_to(score[:, k * LANE:(k + 1) * LANE], (nbp, LANE))
            n_i = k * LANE + lax.broadcasted_iota(I32, (nbp, LANE), 1)
            ahead = (col > mine) | ((col == mine) & (j_i < n_i))
            ranks.append(jnp.sum(ahead.astype(F32), axis=0, keepdims=True))
        rank = jnp.concatenate(ranks, axis=1)
        slot = lax.broadcasted_iota(I32, (TOP_N, nbp), 0).astype(F32)
        hit = jnp.broadcast_to(rank, (TOP_N, nbp)) == slot
        ids = jnp.sum(jnp.where(hit, lax.broadcasted_iota(I32, (TOP_N, nbp), 1).astype(F32), 0.0),
                      axis=1, keepdims=True)
        idx_ref[g] = jnp.broadcast_to(ids, (TOP_N, LANE)).astype(I32)

        jw = lax.broadcasted_iota(I32, (ROWS_S, wp), 1)
        dist_w = w_buf - jw
        valid_w = (dist_w >= 0) & (dist_w <= WINDOW) & (past_len - w_buf + jw >= 0)
        s_w = _dot_nt(q, kw_ref[g]) * SCALE + sbw_ref[g]
        p_w = _masked_softmax(s_w, valid_w)
        ow_ref[g] = _dot(p_w.astype(BF16), vw_ref[g])


def _nsa_select(q_nsa, kcr, vcr, new_c, wk0, wv0, bk, bv, gk0, sbc, kw, vw, sbw, past_len, w_buf):
    dec_b, _, nb0, _ = kcr.shape
    nbp = sbc.shape[-1]
    wp = sbw.shape[-1]
    assert nb0 + 1 >= TOP_N and nbp >= nb0 + 8 and nb0 % 8 == 0
    body = functools.partial(_nsa_sel_body, nb0=nb0, nbp=nbp, past_len=past_len, w_buf=w_buf, wp=wp)

    def per_b(shape):
        nd = len(shape)
        return pl.BlockSpec((None,) + tuple(shape), lambda b: (b,) + (0,) * nd)

    def whole(arr):
        nd = arr.ndim
        return pl.BlockSpec(arr.shape, lambda b: (0,) * nd)

    vec = (G_NSA, ROWS_S, HEAD_DIM)
    return pl.pallas_call(
        body,
        grid=(dec_b,),
        in_specs=[per_b(vec), per_b((G_NSA, nb0, HEAD_DIM)), per_b((G_NSA, nb0, HEAD_DIM)),
                  per_b((8, HEAD_DIM)), whole(wk0), whole(wv0), whole(bk), whole(bv), whole(gk0),
                  whole(sbc), per_b((G_NSA, wp, HEAD_DIM)), per_b((G_NSA, wp, HEAD_DIM)), whole(sbw)],
        out_specs=[per_b((G_NSA, TOP_N, LANE)), per_b(vec), per_b(vec)],
        out_shape=[jax.ShapeDtypeStruct((dec_b, G_NSA, TOP_N, LANE), I32),
                   jax.ShapeDtypeStruct((dec_b,) + vec, F32),
                   jax.ShapeDtypeStruct((dec_b,) + vec, F32)],
        compiler_params=_cparams(1),
        name="nsa_select",
    )(q_nsa, kcr, vcr, new_c, wk0, wv0, bk, bv, gk0, sbc, kw, vw, sbw)


def _nsa_gather_body(pt_ref, idx_ref, rb_ref, q_ref, p0_ref, p1_ref, new_ref, gs_ref, oc_ref,
                     ow_ref, o_ref, m_ref, l_ref, a_ref, *, nb0, past_len):
    b = pl.program_id(0)
    k = pl.program_id(1)
    pos_q = past_len
    row = lax.broadcasted_iota(I32, (CMP_BLOCK, HEAD_DIM), 0)
    l1 = lax.broadcasted_iota(I32, (1, CMP_BLOCK), 1)
    hrow = lax.broadcasted_iota(I32, (ROWS_S, CMP_BLOCK), 0)
    for g, page_ref in enumerate((p0_ref, p1_ref)):
        blk = idx_ref[(b * G_NSA + g) * TOP_N + k]
        is_new = blk >= nb0
        new_k = jnp.broadcast_to(new_ref[g:g + 1, :], (CMP_BLOCK, HEAD_DIM))
        new_v = jnp.broadcast_to(new_ref[G_NSA + g:G_NSA + g + 1, :], (CMP_BLOCK, HEAD_DIM))
        ks = page_ref[pl.ds(2 * G_NSA + g, CMP_BLOCK, stride=NSA_CH), :]
        vs = page_ref[pl.ds(3 * G_NSA + g, CMP_BLOCK, stride=NSA_CH), :]
        ks = jnp.where(is_new, jnp.where(row == 0, new_k, 0.0), ks).astype(BF16)
        vs = jnp.where(is_new, jnp.where(row == 0, new_v, 0.0), vs).astype(BF16)
        dist = pos_q - (blk * CMP_BLOCK + l1)
        bucket = jnp.broadcast_to(_t5_bucket(dist), (ROWS_S, CMP_BLOCK))
        bias = jnp.zeros((ROWS_S, CMP_BLOCK), F32)
        for hh in range(HPG):
            for bk in range(N_BUCKETS):
                bias = jnp.where((hrow == hh) & (bucket == bk), rb_ref[bk, g * HPG + hh], bias)
        s = _dot_nt(q_ref[g], ks) * SCALE + bias
        logit = jnp.where(jnp.broadcast_to(dist >= 0, (ROWS_S, CMP_BLOCK)), s, NEG_INF)
        fl = _Flash(m_ref.at[g], l_ref.at[g], a_ref.at[g])

        @pl.when(k == 0)
        def _():
            fl.first(logit, vs)

        @pl.when(k > 0)
        def _():
            fl.update(logit, vs)

        @pl.when(k == pl.num_programs(1) - 1)
        def _():
            o_ref[g] = gs_ref[g, 0] * oc_ref[g] + gs_ref[g, 1] * fl.result() + gs_ref[g, 2] * ow_ref[g]


def _nsa_gather(l, pt_flat, idx_flat, rel_bias, q_nsa, cache_nsa2d, new_s, gs, oc, ow, nb0,
                past_len, page):
    dec_b = q_nsa.shape[0]
    npg = pt_flat.shape[0] // dec_b
    halves = page // CMP_BLOCK
    body = functools.partial(_nsa_gather_body, nb0=nb0, past_len=past_len)

    def page_spec(g):
        def imap(b, k, pt, idx):
            blk = jnp.minimum(idx[(b * G_NSA + g) * TOP_N + k], nb0 - 1)
            return (l, pt[b * npg + blk // halves], blk % halves, 0)
        return pl.BlockSpec((None, None, CMP_BLOCK * NSA_CH, HEAD_DIM), imap)

    def per_b(shape):
        nd = len(shape)
        return pl.BlockSpec((None,) + tuple(shape), lambda b, k, pt, idx: (b,) + (0,) * nd)

    vec = (G_NSA, ROWS_S, HEAD_DIM)
    grid_spec = pltpu.PrefetchScalarGridSpec(
        num_scalar_prefetch=2,
        grid=(dec_b, TOP_N),
        in_specs=[pl.BlockSpec(memory_space=pltpu.SMEM), per_b(vec), page_spec(0), page_spec(1),
                  per_b((8, HEAD_DIM)), per_b((G_NSA, 3, ROWS_S, HEAD_DIM)), per_b(vec), per_b(vec)],
        out_specs=per_b(vec),
        scratch_shapes=[pltpu.VMEM((G_NSA, ROWS_S, 1), F32), pltpu.VMEM((G_NSA, ROWS_S, 1), F32),
                        pltpu.VMEM((G_NSA, ROWS_S, HEAD_DIM), F32)],
    )
    return pl.pallas_call(
        body,
        grid_spec=grid_spec,
        out_shape=jax.ShapeDtypeStruct((dec_b,) + vec, F32),
        compiler_params=_cparams(2),
        name="nsa_gather",
    )(pt_flat, idx_flat, rel_bias, q_nsa, cache_nsa2d, cache_nsa2d, new_s, gs, oc, ow)


def _pad_rows(x, rows):
    return jnp.pad(x, ((0, rows - x.shape[0]),) + ((0, 0),) * (x.ndim - 1))


def _sample_layer(l, xs, wt, p, sb_tabs, rel_bias, pt_flat, cache_sb, cache_nsa2d, win_l, w_cmp0,
                  dec_b, past_len, page):
    na, bk, bv, gk0, go, nf = p
    sbc, sbw = sb_tabs
    w_buf = win_l.shape[1]
    rows = xs.shape[0]
    zb, sb_rows, nsa_rows, win_rows, gates = _inproj(
        xs, na, wt["w_a"], wt["w_b"], wt["w_gate"], wt["gain"], rows, rows)
    zb8, sb_rows, nsa_rows, win_rows = zb[:dec_b], sb_rows[:dec_b], nsa_rows[:dec_b], win_rows[:dec_b]

    q_sb = zb8[:, QS_BLK * LANE:KSB_BLK * LANE].reshape(dec_b, H_SB, HEAD_DIM)
    q_sb = jnp.pad(q_sb, ((0, 0), (0, ROWS_S - H_SB), (0, 0)))
    o_sb = _sb_decode(l, pt_flat, q_sb, cache_sb).reshape(dec_b, D_SB)

    q_nsa = zb8[:, :D_NSA].reshape(dec_b, G_NSA, HPG, HEAD_DIM)
    q_nsa = jnp.pad(q_nsa, ((0, 0), (0, 0), (0, ROWS_S - HPG), (0, 0)))
    kcr, vcr = _cmp_decode(l, pt_flat, cache_nsa2d, wt["w_kv"], dec_b, page)
    new_c = jnp.pad(nsa_rows[:, :2 * D_KV].reshape(dec_b, 2 * G_NSA, HEAD_DIM),
                    ((0, 0), (0, 8 - 2 * G_NSA), (0, 0)))
    new_s = jnp.pad(nsa_rows[:, 2 * D_KV:].reshape(dec_b, 2 * G_NSA, HEAD_DIM),
                    ((0, 0), (0, 8 - 2 * G_NSA), (0, 0)))
    new_w = win_rows.reshape(dec_b, 1, 2, G_NSA, HEAD_DIM)
    win_all = jnp.concatenate([win_l, new_w], axis=1)
    wp = sbw.shape[-1]
    win_t = jnp.transpose(win_all, (2, 0, 3, 1, 4))
    win_t = jnp.pad(win_t, ((0, 0), (0, 0), (0, 0), (0, wp - w_buf - 1), (0, 0))).astype(BF16)
    idx, oc, ow = _nsa_select(q_nsa, kcr, vcr, new_c, w_cmp0[0], w_cmp0[1], bk, bv, gk0, sbc,
                              win_t[0], win_t[1], sbw, past_len, w_buf)
    gs = gates[:dec_b].reshape(dec_b, G_NSA, LANE)[:, :, :3 * HPG].reshape(dec_b, G_NSA, 3, HPG)
    gs = jnp.pad(gs, ((0, 0), (0, 0), (0, 0), (0, ROWS_S - HPG)))
    gs = jnp.broadcast_to(gs[..., None], gs.shape + (HEAD_DIM,))
    o_nsa = _nsa_gather(l, pt_flat, idx[:, :, :, 0].reshape(-1), rel_bias, q_nsa, cache_nsa2d,
                        new_s, gs, oc, ow, kcr.shape[2], past_len, page)
    o_nsa = o_nsa[:, :, :HPG].reshape(dec_b, D_NSA)

    x1 = _outproj(_pad_rows(o_nsa, rows), _pad_rows(o_sb, rows), go, wt["w_out"], xs)
    x2 = _ffn(x1, nf, wt["w_gate_ffn"], wt["w_up"], wt["w_down"])
    new_win = win_all[:, 1:]
    return x2, sb_rows, nsa_rows, new_win


def kernel(x_prompt, x_sample, cache_sb_kv, cache_nsa_kv, state_win_kv, page_table, rel_bias,
           norm_attn, w_in, g_q, g_k, w_cmp_k, b_cmp_k, w_cmp_v, b_cmp_v, g_out, w_out,
           norm_ffn, w_gate, w_up, w_down):
    batch, seq, _ = x_prompt.shape
    dec_b, dec_t, _ = x_sample.shape
    assert dec_t == 1
    depth, n_pool, page = cache_sb_kv.shape[:3]
    npg = page_table.shape[1]
    past_len = npg * page
    w_buf = state_win_kv.shape[2]
    nb_s = past_len // CMP_BLOCK + 1
    nbp_s = -(-(nb_s + 7) // LANE) * LANE
    wp_s = -(-(w_buf + 1) // LANE) * LANE

    toep, bias_c, sc, sw = _bias_tables(rel_bias, seq, past_len, w_buf, nbp_s, wp_s)
    sbc = jnp.pad(sc.reshape(G_NSA, HPG, nbp_s), ((0, 0), (0, ROWS_S - HPG), (0, 0)))
    sbw = jnp.pad(sw.reshape(G_NSA, HPG, wp_s), ((0, 0), (0, ROWS_S - HPG), (0, 0)))
    pt_flat = page_table.reshape(-1).astype(I32)
    cache_nsa2d = cache_nsa_kv.reshape(depth, n_pool, page * NSA_CH, HEAD_DIM)

    xp = x_prompt.reshape(batch * seq, D_MODEL)
    xs = _pad_rows(x_sample.reshape(dec_b, D_MODEL), ROWS_S)
    sb_p, sb_s, nsa_p, nsa_s, win_p, win_s = [], [], [], [], [], []
    w_keep = min(WINDOW, seq)
    for l in range(depth):
        wt = _prep_weights(l, w_in, g_q, g_k, w_cmp_k, w_cmp_v, w_out, w_gate, w_up, w_down)
        p = (norm_attn[l][None], b_cmp_k[l][None], b_cmp_v[l][None], g_k[l, 0][None],
             g_out[l][None], norm_ffn[l][None])
        xp, sb_rows, nsa_rows, win_rows = _prompt_layer(xp, wt, p, (toep, bias_c), rel_bias,
                                                        batch, seq)
        sb_p.append(sb_rows.reshape(batch, seq, 2, H_SB, HEAD_DIM))
        nsa_p.append(nsa_rows.reshape(batch, seq, 4, G_NSA, HEAD_DIM))
        win_p.append(win_rows.reshape(batch, w_keep, 2, G_NSA, HEAD_DIM))

        w_cmp0 = (w_cmp_k[l, 0].astype(BF16), w_cmp_v[l, 0].astype(BF16))
        xs, sb_rows, nsa_rows, new_win = _sample_layer(
            l, xs, wt, p, (sbc, sbw), rel_bias, pt_flat, cache_sb_kv, cache_nsa2d, state_win_kv[l],
            w_cmp0, dec_b, past_len, page)
        sb_s.append(sb_rows.reshape(dec_b, 1, 2, H_SB, HEAD_DIM))
        nsa_s.append(nsa_rows.reshape(dec_b, 1, 4, G_NSA, HEAD_DIM))
        win_s.append(new_win)
    return (xp.reshape(batch, seq, D_MODEL), xs[:dec_b].reshape(dec_b, 1, D_MODEL),
            jnp.stack(sb_p), jnp.stack(sb_s), jnp.stack(nsa_p), jnp.stack(nsa_s),
            jnp.stack(win_p), jnp.stack(win_s))
```

```python
import functools
import math

import jax
import jax.numpy as jnp
from jax import lax
from jax.experimental import pallas as pl
from jax.experimental.pallas import tpu as pltpu

F32 = jnp.float32
BF16 = jnp.bfloat16
I32 = jnp.int32

LANE = 128
HEAD_DIM = 128
G_NSA = 2
HPG = 4
H_NSA = G_NSA * HPG
H_SB = 8
D_NSA = H_NSA * HEAD_DIM
D_SB = H_SB * HEAD_DIM
D_MODEL = D_NSA + D_SB
D_KV = G_NSA * HEAD_DIM
CMP_BLOCK = 64
TOP_N = 16
N_LOCAL = 2
WINDOW = 512
N_BUCKETS = 32
MAX_DISTANCE = 128
EPS = 1e-6
FORCE_SCORE = 1e4
NEG_INF = -1e30
SCALE = HEAD_DIM ** -0.5

COL_TILE = 512
N_MAIN = D_NSA + 6 * D_KV + 3 * D_SB
QS_BLK = (D_NSA + 6 * D_KV) // LANE
KSB_BLK = QS_BLK + D_SB // LANE
VSB_BLK = KSB_BLK + D_SB // LANE
KS_BLK = (D_NSA + 2 * D_KV) // LANE
VS_BLK = KS_BLK + G_NSA
KW_BLK = VS_BLK + G_NSA
VW_BLK = KW_BLK + G_NSA
VMEM_LIMIT = 56 * 1024 * 1024

SB_TQ = 256
SB_TK = 256
SB_HB = 4
NSA_TQ = 256
SB_PAGES_PER_STEP = 16
CMP_PAGES_PER_STEP = 16


VMEM_LIMIT_FUSED = 60 * 1024 * 1024


def _cparams(n_axes, vmem_limit=VMEM_LIMIT):
    return pltpu.CompilerParams(dimension_semantics=("arbitrary",) * n_axes,
                                vmem_limit_bytes=vmem_limit)


def _dot(a, b):
    return jnp.dot(a, b, preferred_element_type=F32)


def _dot_nt(a, b):
    return lax.dot_general(a, b, (((1,), (1,)), ((), ())), preferred_element_type=F32)


def _split_bf16(x):
    hi = x.astype(BF16)
    lo = (x - hi.astype(F32)).astype(BF16)
    return hi, lo


def _dot_split(x, u):
    hi, lo = _split_bf16(x)
    return _dot(hi, u) + _dot(lo, u)


def _rms(y, gain):
    ms = jnp.mean(y * y, axis=-1, keepdims=True)
    return y * lax.rsqrt(ms + EPS) * gain


def _neg_softplus(z):
    return -(jnp.maximum(z, 0.0) + jnp.log(1.0 + jnp.exp(-jnp.abs(z))))


def _t5_bucket(dist):
    n = jnp.maximum(dist, 0)
    max_exact = N_BUCKETS // 2
    nf = jnp.maximum(n, 1).astype(F32)
    large = max_exact + (jnp.log(nf / max_exact) / math.log(MAX_DISTANCE / max_exact)
                         * (N_BUCKETS - max_exact)).astype(I32)
    large = jnp.minimum(large, N_BUCKETS - 1)
    return jnp.where(n < max_exact, n, large)


def _pick_tile(m, candidates):
    for t in candidates:
        if m % t == 0:
            return t
    raise ValueError(f"no tile for {m}")


def _bias_body(rb_ref, toep_ref, bc_ref, sc_ref, sw_ref, *, seq, past_len, w_buf):
    h = pl.program_id(0)
    tq = NSA_TQ

    def table(dist):
        bk = _t5_bucket(dist)
        out = jnp.zeros(dist.shape, F32)
        for b in range(N_BUCKETS):
            out = jnp.where(bk == b, rb_ref[b, h], out)
        return out

    k = lax.broadcasted_iota(I32, (tq, tq), 0)
    q = lax.broadcasted_iota(I32, (tq, tq), 1)
    toep_ref[0] = table(tq + q - k)
    toep_ref[1] = table(q - k)
    n = lax.broadcasted_iota(I32, (LANE, tq), 0)
    qc = lax.broadcasted_iota(I32, (LANE, tq), 1)
    for i in range(seq // tq):
        bc_ref[i] = table(i * tq + qc - CMP_BLOCK * n - (CMP_BLOCK - 1))
    n1 = lax.broadcasted_iota(I32, sc_ref.shape, 1)
    sc_ref[...] = table(past_len - CMP_BLOCK * n1 - (CMP_BLOCK - 1))
    j1 = lax.broadcasted_iota(I32, sw_ref.shape, 1)
    sw_ref[...] = table(w_buf - j1)


def _bias_tables(rel_bias, seq, past_len, w_buf, nbp_s, wp_s):
    body = functools.partial(_bias_body, seq=seq, past_len=past_len, w_buf=w_buf)
    nq = seq // NSA_TQ
    return pl.pallas_call(
        body,
        grid=(H_NSA,),
        in_specs=[pl.BlockSpec(memory_space=pltpu.SMEM)],
        out_specs=[
            pl.BlockSpec((2, NSA_TQ, NSA_TQ), lambda h: (0, 0, h)),
            pl.BlockSpec((nq, LANE, NSA_TQ), lambda h: (0, 0, h)),
            pl.BlockSpec((None, 1, nbp_s), lambda h: (h, 0, 0)),
            pl.BlockSpec((None, 1, wp_s), lambda h: (h, 0, 0)),
        ],
        out_shape=[
            jax.ShapeDtypeStruct((2, NSA_TQ, H_NSA * NSA_TQ), F32),
            jax.ShapeDtypeStruct((nq, LANE, H_NSA * NSA_TQ), F32),
            jax.ShapeDtypeStruct((H_NSA, 1, nbp_s), F32),
            jax.ShapeDtypeStruct((H_NSA, 1, wp_s), F32),
        ],
        compiler_params=_cparams(1),
        name="bias_tables",
    )(rel_bias)


Q_TILES = D_NSA // COL_TILE
NSA_TILE0 = Q_TILES
WIN_TILE = NSA_TILE0 + 4 * D_KV // COL_TILE
SB_TILE0 = WIN_TILE + 2 * D_KV // COL_TILE + D_SB // COL_TILE


def _inproj_body(x_ref, na_ref, wa_ref, wb_ref, wg_ref, gain_ref, zb_ref, sb_ref, nsa_ref, win_ref,
                 gate_ref, h_ref, *, w_keep):
    j = pl.program_id(1)
    tm = x_ref.shape[0]
    half = COL_TILE // 2

    @pl.when(j == 0)
    def _():
        h = _rms(x_ref[...], na_ref[...]).astype(BF16)
        h_ref[...] = h
        gate_ref[...] = jax.nn.sigmoid(_dot(h, wg_ref[...]))

    gain = gain_ref[...]

    def emit(w_ref, norm_chunks, f32_ref=None, row0=0):
        h = h_ref[...]
        accs = [_dot(h, w_ref[:, :half]), _dot(h, w_ref[:, half:])]
        for c in range(COL_TILE // LANE):
            sl = slice(c * LANE, (c + 1) * LANE)
            y = accs[c * LANE // half][:, (c * LANE) % half:(c * LANE) % half + LANE]
            if c in norm_chunks:
                y = _rms(y, gain[:, sl])
            zb_ref[:, sl] = y.astype(BF16)
            if f32_ref is not None:
                f32_ref[:, sl] = y[row0:, :]

    @pl.when(j < Q_TILES)
    def _():
        emit(wa_ref, (0, 1, 2, 3))

    @pl.when(j == NSA_TILE0)
    def _():
        emit(wa_ref, (), nsa_ref)

    @pl.when(j == NSA_TILE0 + 1)
    def _():
        emit(wa_ref, (0, 1), nsa_ref)

    @pl.when(j == WIN_TILE)
    def _():
        emit(wa_ref, (0, 1), win_ref, tm - w_keep)

    @pl.when((j > WIN_TILE) & (j < SB_TILE0))
    def _():
        emit(wb_ref, ())

    @pl.when(j >= SB_TILE0)
    def _():
        emit(wb_ref, (), sb_ref)


def _inproj(x2d, na, w_a, w_b, w_gate, gain_cols, seq, w_keep):
    m = x2d.shape[0]
    tm = _pick_tile(seq, (1024, 512, 256, 128, 16))
    assert w_keep <= tm
    tpb = seq // tm
    n_sb = 2 * D_SB // COL_TILE
    n_nsa = 4 * D_KV // COL_TILE
    n_a = w_a.shape[1] // COL_TILE
    n_b = w_b.shape[1] // COL_TILE
    assert n_a == WIN_TILE + 1 and n_a + n_b == N_MAIN // COL_TILE
    body = functools.partial(_inproj_body, w_keep=w_keep)
    return pl.pallas_call(
        body,
        grid=(m // tm, n_a + n_b),
        in_specs=[
            pl.BlockSpec((tm, D_MODEL), lambda i, j: (i, 0)),
            pl.BlockSpec((1, D_MODEL), lambda i, j: (0, 0)),
            pl.BlockSpec((D_MODEL, COL_TILE), lambda i, j: (0, jnp.minimum(j, n_a - 1))),
            pl.BlockSpec((D_MODEL, COL_TILE), lambda i, j: (0, jnp.maximum(j - n_a, 0))),
            pl.BlockSpec((D_MODEL, 2 * LANE), lambda i, j: (0, 0)),
            pl.BlockSpec((1, COL_TILE), lambda i, j: (0, j)),
        ],
        out_specs=[
            pl.BlockSpec((tm, COL_TILE), lambda i, j: (i, j)),
            pl.BlockSpec((tm, COL_TILE), lambda i, j: (i, jnp.clip(j - SB_TILE0, 0, n_sb - 1))),
            pl.BlockSpec((tm, COL_TILE), lambda i, j: (i, jnp.clip(j - NSA_TILE0, 0, n_nsa - 1))),
            pl.BlockSpec((w_keep, COL_TILE), lambda i, j: (i // tpb, 0)),
            pl.BlockSpec((tm, 2 * LANE), lambda i, j: (i, 0)),
        ],
        out_shape=[
            jax.ShapeDtypeStruct((m, N_MAIN), BF16),
            jax.ShapeDtypeStruct((m, 2 * D_SB), F32),
            jax.ShapeDtypeStruct((m, 4 * D_KV), F32),
            jax.ShapeDtypeStruct((m // seq * w_keep, 2 * D_KV), F32),
            jax.ShapeDtypeStruct((m, 2 * LANE), F32),
        ],
        scratch_shapes=[pltpu.VMEM((tm, D_MODEL), BF16)],
        compiler_params=_cparams(2),
        name="inproj",
    )(x2d, na, w_a, w_b, w_gate, gain_cols)


def _outproj_body(on_ref, os_ref, go_ref, w_ref, x_ref, y_ref, h_ref):
    j = pl.program_id(1)

    @pl.when(j == 0)
    def _():
        go = go_ref[...]
        h_ref[:, :D_NSA] = _rms(on_ref[...], go[:, :D_NSA]).astype(BF16)
        h_ref[:, D_NSA:] = _rms(os_ref[...], go[:, D_NSA:]).astype(BF16)

    y_ref[...] = x_ref[...] + _dot(h_ref[...], w_ref[...])


def _outproj(o_nsa, o_sb, g_out, w_out, x2d):
    m = x2d.shape[0]
    tm = _pick_tile(m, (1024, 512, 256, 128, 16))
    return pl.pallas_call(
        _outproj_body,
        grid=(m // tm, D_MODEL // COL_TILE),
        in_specs=[
            pl.BlockSpec((tm, D_NSA), lambda i, j: (i, 0)),
            pl.BlockSpec((tm, D_SB), lambda i, j: (i, 0)),
            pl.BlockSpec((1, D_MODEL), lambda i, j: (0, 0)),
            pl.BlockSpec((D_MODEL, COL_TILE), lambda i, j: (0, j)),
            pl.BlockSpec((tm, COL_TILE), lambda i, j: (i, j)),
        ],
        out_specs=pl.BlockSpec((tm, COL_TILE), lambda i, j: (i, j)),
        out_shape=jax.ShapeDtypeStruct((m, D_MODEL), F32),
        scratch_shapes=[pltpu.VMEM((tm, D_MODEL), BF16)],
        compiler_params=_cparams(2),
        name="outproj",
    )(o_nsa, o_sb, g_out, w_out, x2d)


def _ffn_body(*refs, sb):
    if sb is None:
        x_ref, nf_ref, wg_ref, wu_ref, wd_ref, y_ref, h_ref = refs
    else:
        pps = sb["pps"]
        x_ref, nf_ref, wg_ref, wu_ref, wd_ref, q_ref = refs[1:7]
        page_refs = refs[7:7 + pps]
        up_ref, upp_ref, y_ref, o_ref, h_ref, z_ref, sacc_ref = refs[7 + pps:]
    i = pl.program_id(0)
    j = pl.program_id(1)

    @pl.when(j == 0)
    def _():
        x = x_ref[...]
        h_ref[...] = _rms(x, nf_ref[...]).astype(BF16)
        y_ref[...] = x

    h = h_ref[...]
    half = wg_ref.shape[1] // 2
    cols = (slice(0, half), slice(half, 2 * half))
    gate = [_dot(h, wg_ref[:, c]) for c in cols]
    up = [_dot(h, wu_ref[:, c]) for c in cols]
    act = [(jax.nn.silu(gt) * u).astype(BF16) for gt, u in zip(gate, up)]
    y_ref[...] += _dot(act[0], wd_ref[cols[0], :]) + _dot(act[1], wd_ref[cols[1], :])

    if sb is not None:
        nc = sb["n_chunks"]
        s = i * pl.num_programs(1) + j

        @pl.when(s < sb["n_steps"])
        def _():
            _sb_dec_step((s // nc) % 2, s % nc, nc - 1, q_ref, page_refs, up_ref, upp_ref, o_ref,
                         z_ref, sacc_ref, sb["page"])


def _ffn(x2d, norm_ffn, w_gate, w_up, w_down, sb_args=None):
    m = x2d.shape[0]
    d_ff = w_gate.shape[1]
    tm = _pick_tile(m, (512, 256, 128, 16))
    tf = _pick_tile(d_ff, (512, 256))
    nf = d_ff // tf
    in_specs = [
        pl.BlockSpec((tm, D_MODEL), lambda i, j, *_: (i, 0)),
        pl.BlockSpec((1, D_MODEL), lambda i, j, *_: (0, 0)),
        pl.BlockSpec((D_MODEL, tf), lambda i, j, *_: (0, j)),
        pl.BlockSpec((D_MODEL, tf), lambda i, j, *_: (0, j)),
        pl.BlockSpec((tf, D_MODEL), lambda i, j, *_: (j, 0)),
    ]
    y_spec = pl.BlockSpec((tm, D_MODEL), lambda i, j, *_: (i, 0))
    y_shape = jax.ShapeDtypeStruct((m, D_MODEL), F32)
    h_scratch = pltpu.VMEM((tm, D_MODEL), BF16)
    if sb_args is None:
        return pl.pallas_call(
            functools.partial(_ffn_body, sb=None),
            grid=(m // tm, nf),
            in_specs=in_specs,
            out_specs=y_spec,
            out_shape=y_shape,
            scratch_shapes=[h_scratch],
            compiler_params=_cparams(2),
            name="ffn",
        )(x2d, norm_ffn, w_gate, w_up, w_down)

    l, pt_flat, q_sb, cache_sb = sb_args
    dec_b = q_sb.shape[0]
    page = cache_sb.shape[2]
    npg = pt_flat.shape[0] // dec_b
    pps = SB_PAGES_PER_STEP
    nc = npg // pps
    n_steps = dec_b * 2 * nc
    assert n_steps <= (m // tm) * nf
    width = page * H_SB
    up, upp = _sb_dec_consts(npg, page)

    def pos(i, j):
        s = jnp.minimum(i * nf + j, n_steps - 1)
        return s // (2 * nc), (s // nc) % 2, s % nc

    def page_spec(u):
        def imap(i, j, pt):
            b, ph, c = pos(i, j)
            return (l, pt[b * npg + c * pps + u], 0, ph, 0, 0)
        return pl.BlockSpec((None, None, page, None, H_SB, HEAD_DIM), imap)

    once = pl.Buffered(1)
    grid_spec = pltpu.PrefetchScalarGridSpec(
        num_scalar_prefetch=1,
        grid=(m // tm, nf),
        in_specs=in_specs
        + [pl.BlockSpec((None, ROWS_S, HEAD_DIM), lambda i, j, pt: (pos(i, j)[0], 0, 0))]
        + [page_spec(u) for u in range(pps)]
        + [pl.BlockSpec((width, 2 * width), lambda i, j, pt: (0, 0), pipeline_mode=once),
           pl.BlockSpec((npg, npg), lambda i, j, pt: (0, 0), pipeline_mode=once)],
        out_specs=[y_spec,
                   pl.BlockSpec((None, H_SB, HEAD_DIM), lambda i, j, pt: (pos(i, j)[0], 0, 0))],
        scratch_shapes=[h_scratch, pltpu.VMEM((npg, width), F32),
                        pltpu.VMEM((ROWS_S, HEAD_DIM), F32)],
    )
    sb = dict(pps=pps, page=page, n_chunks=nc, n_steps=n_steps)
    return pl.pallas_call(
        functools.partial(_ffn_body, sb=sb),
        grid_spec=grid_spec,
        out_shape=[y_shape, jax.ShapeDtypeStruct((dec_b, H_SB, HEAD_DIM), F32)],
        compiler_params=_cparams(2, VMEM_LIMIT_FUSED),
        name="ffn_sb_decode",
    )(pt_flat, x2d, norm_ffn, w_gate, w_up, w_down, q_sb, *([cache_sb] * pps), up, upp)


def _sb_body(q_ref, k_ref, v_ref, u_ref, o_ref, acc_ref, run_ref):
    i = pl.program_id(2)
    row = lax.broadcasted_iota(I32, (SB_TQ, SB_TK), 0)
    col = lax.broadcasted_iota(I32, (SB_TQ, SB_TK), 1)
    strict = col < row

    def tile(kb, diagonal):
        off = pl.multiple_of(kb * SB_TK, SB_TK)
        heads = range(SB_HB)
        sls = [slice(h * HEAD_DIM, (h + 1) * HEAD_DIM) for h in heads]
        z = [_dot_nt(q_ref[:, sl], k_ref[pl.ds(off, SB_TK), sl]) * SCALE for sl in sls]
        log_1m = [_neg_softplus(zh) for zh in z]
        if diagonal:
            log_1m = [jnp.where(strict, lh, 0.0) for lh in log_1m]
        parts = [_split_bf16(lh) for lh in log_1m]
        u = u_ref[...]
        later = [_dot(hi, u) + _dot(lo, u) for hi, lo in parts]
        total = [jnp.sum(lh, axis=-1, keepdims=True) for lh in log_1m]
        if diagonal:
            a = [jnp.where(strict, jnp.exp(z[h] + log_1m[h] + later[h]), 0.0) for h in heads]
        else:
            run = [run_ref[h] for h in heads]
            a = [jnp.exp(z[h] + log_1m[h] + later[h] + run[h]) for h in heads]
        pv = [_dot(a[h].astype(BF16), v_ref[pl.ds(off, SB_TK), sls[h]]) for h in heads]
        for h in heads:
            if diagonal:
                acc_ref[h] = pv[h]
                run_ref[h] = total[h]
            else:
                acc_ref[h] += pv[h]
                run_ref[h] = run[h] + total[h]

    tile(i, True)

    def step(it, carry):
        tile(i - 1 - it, False)
        return carry

    lax.fori_loop(0, i, step, 0)
    for h in range(SB_HB):
        o_ref[:, h * HEAD_DIM:(h + 1) * HEAD_DIM] = acc_ref[h]


def _sb_prompt(zb, batch, seq):
    nq = seq // SB_TQ
    width = SB_HB * HEAD_DIM
    blk = LANE // HEAD_DIM * SB_HB
    assert QS_BLK % blk == 0 and KSB_BLK % blk == 0 and VSB_BLK % blk == 0
    r = lax.broadcasted_iota(I32, (SB_TK, SB_TK), 0)
    c = lax.broadcasted_iota(I32, (SB_TK, SB_TK), 1)
    u = (r > c).astype(BF16)
    return pl.pallas_call(
        _sb_body,
        grid=(batch, H_SB // SB_HB, nq),
        in_specs=[
            pl.BlockSpec((SB_TQ, width), lambda b, h, i: (b * nq + i, QS_BLK // blk + h)),
            pl.BlockSpec((seq, width), lambda b, h, i: (b, KSB_BLK // blk + h)),
            pl.BlockSpec((seq, width), lambda b, h, i: (b, VSB_BLK // blk + h)),
            pl.BlockSpec((SB_TK, SB_TK), lambda b, h, i: (0, 0)),
        ],
        out_specs=pl.BlockSpec((SB_TQ, width), lambda b, h, i: (b * nq + i, h)),
        out_shape=jax.ShapeDtypeStruct((batch * seq, D_SB), F32),
        scratch_shapes=[pltpu.VMEM((SB_HB, SB_TQ, HEAD_DIM), F32), pltpu.VMEM((SB_HB, SB_TQ, 1), F32)],
        compiler_params=_cparams(3),
        name="sb_prompt",
    )(zb, zb, zb, u)


def _cmp_prompt_body(x0_ref, x1_ref, x2_ref, x3_ref, wk_ref, wv_ref, bk_ref, bv_ref, gk_ref,
                     kc_ref, vct_ref, *, nb):
    pad = jnp.zeros((LANE - nb, HEAD_DIM), F32)
    for c, x_ref in enumerate((x0_ref, x1_ref, x2_ref, x3_ref)):
        rows = jnp.concatenate(
            [x_ref[pl.ds(l, nb, stride=CMP_BLOCK), :] for l in range(CMP_BLOCK)], axis=1)
        rows = rows.astype(BF16)
        if c < G_NSA:
            y = _rms(_dot(rows, wk_ref[...]) + bk_ref[...], gk_ref[...])
            kc_ref[c] = jnp.concatenate([y, pad], axis=0).astype(BF16)
        else:
            y = _dot(rows, wv_ref[...]) + bv_ref[...]
            vct_ref[c - G_NSA] = jnp.concatenate([y, pad], axis=0).T.astype(BF16)


def _cmp_prompt(nsa_rows, batch, seq, wk, wv, bk, bv, gk0):
    nb = seq // CMP_BLOCK
    assert nb % 8 == 0 and nb < LANE
    body = functools.partial(_cmp_prompt_body, nb=nb)
    shape = jax.ShapeDtypeStruct((batch, G_NSA, LANE, HEAD_DIM), BF16)

    def x_spec(c):
        return pl.BlockSpec((seq, HEAD_DIM), lambda b: (b, c))

    return pl.pallas_call(
        body,
        grid=(batch,),
        in_specs=[
            x_spec(0), x_spec(1), x_spec(2), x_spec(3),
            pl.BlockSpec(wk.shape, lambda b: (0, 0)),
            pl.BlockSpec(wv.shape, lambda b: (0, 0)),
            pl.BlockSpec((1, HEAD_DIM), lambda b: (0, 0)),
            pl.BlockSpec((1, HEAD_DIM), lambda b: (0, 0)),
            pl.BlockSpec((1, HEAD_DIM), lambda b: (0, 0)),
        ],
        out_specs=[pl.BlockSpec((None, G_NSA, LANE, HEAD_DIM), lambda b: (b, 0, 0, 0))] * 2,
        out_shape=[shape, shape],
        compiler_params=_cparams(1),
        name="cmp_prompt",
    )(nsa_rows, nsa_rows, nsa_rows, nsa_rows, wk, wv, bk, bv, gk0)


def _masked_softmax(s, valid):
    logit = jnp.where(valid, s, NEG_INF)
    e = jnp.exp(logit - jnp.max(logit, axis=-1, keepdims=True))
    return jnp.where(valid, e / jnp.sum(e, axis=-1, keepdims=True), 0.0)


class _Flash:
    def __init__(self, m_ref, l_ref, a_ref):
        self.m_ref, self.l_ref, self.a_ref = m_ref, l_ref, a_ref

    def first(self, logit, v):
        m = jnp.max(logit, axis=-1, keepdims=True)
        p = jnp.exp(logit - m)
        self.m_ref[...] = m
        self.l_ref[...] = jnp.sum(p, axis=-1, keepdims=True)
        self.a_ref[...] = _dot(p.astype(BF16), v)

    def update(self, logit, v):
        m_old = self.m_ref[...]
        m = jnp.maximum(m_old, jnp.max(logit, axis=-1, keepdims=True))
        alpha = jnp.exp(m_old - m)
        p = jnp.exp(logit - m)
        self.m_ref[...] = m
        self.l_ref[...] = alpha * self.l_ref[...] + jnp.sum(p, axis=-1, keepdims=True)
        self.a_ref[...] = alpha * self.a_ref[...] + _dot(p.astype(BF16), v)

    def result(self):
        return self.a_ref[...] / self.l_ref[...]


class _FlashT:
    def __init__(self, m_ref, l_ref, a_ref):
        self.m_ref, self.l_ref, self.a_ref = m_ref, l_ref, a_ref

    @staticmethod
    def step(states, logits, v_ts, first):
        ms = [jnp.max(lg, axis=0, keepdims=True) for lg in logits]
        if not first:
            olds = [st.m_ref[...] for st in states]
            ms = [jnp.maximum(o, m) for o, m in zip(olds, ms)]
            alphas = [jnp.exp(o - m) for o, m in zip(olds, ms)]
        ps = [jnp.exp(lg - m) for lg, m in zip(logits, ms)]
        sums = [jnp.sum(p, axis=0, keepdims=True) for p in ps]
        pvs = [_dot(v_t, p.astype(BF16)) for v_t, p in zip(v_ts, ps)]
        for n, st in enumerate(states):
            st.m_ref[...] = ms[n]
            if first:
                st.l_ref[...] = sums[n]
                st.a_ref[...] = pvs[n]
            else:
                st.l_ref[...] = alphas[n] * st.l_ref[...] + sums[n]
                st.a_ref[...] = alphas[n] * st.a_ref[...] + pvs[n]

    def update(self, logit, v_t):
        _FlashT.step([self], [logit], [v_t], False)

    def result(self):
        return self.a_ref[...] / self.l_ref[...]


def _nsa_body(rb_ref, q_ref, kc_ref, vct_ref, ks_ref, vs_ref, kw_ref, vw_ref, g_ref, toep_ref,
              bc_ref, et_ref, o_ref, vst_ref, vwt_ref, m_s, l_s, a_s, m_w, l_w, a_w, *, nb, seq):
    g = pl.program_id(1)
    i = pl.program_id(2)
    tq = NSA_TQ
    sub = tq // LANE
    width = HPG * tq
    q0 = i * tq

    @pl.when(i == 0)
    def _():
        for c in range(seq // LANE):
            rows = slice(c * LANE, (c + 1) * LANE)
            cols = slice((c % sub) * LANE, (c % sub + 1) * LANE)
            vst_ref[c // sub, :, cols] = vs_ref[rows, :].astype(F32).T.astype(BF16)
            vwt_ref[c // sub, :, cols] = vw_ref[rows, :].astype(F32).T.astype(BF16)

    qs = jnp.concatenate([q_ref[:, hh * LANE:(hh + 1) * LANE] for hh in range(HPG)], axis=0)

    def q_off(rows):
        return lax.broadcasted_iota(I32, (rows, width), 1) % tq

    n_r = lax.broadcasted_iota(I32, (LANE, width), 0)
    s_c = _dot_nt(kc_ref[...], qs) * SCALE + bc_ref[...]
    valid_c = (CMP_BLOCK * n_r + (CMP_BLOCK - 1) <= q0 + q_off(LANE)) & (n_r < nb)
    logit = jnp.where(valid_c, s_c, NEG_INF)
    e = jnp.exp(logit - jnp.max(logit, axis=0, keepdims=True))
    p_c = jnp.where(valid_c, e / jnp.sum(e, axis=0, keepdims=True), 0.0)
    oc_t = _dot(vct_ref[...], p_c.astype(BF16))
    imp_t = p_c[:, 0:tq]
    for hh in range(1, HPG):
        imp_t = imp_t + p_c[:, hh * tq:(hh + 1) * tq]

    nbr = -(-nb // 8) * 8
    n_i = lax.broadcasted_iota(I32, (nbr, tq), 0)
    cur = (q0 + lax.broadcasted_iota(I32, (nbr, tq), 1)) // CMP_BLOCK
    forced = (n_i == 0) | (n_i >= cur - (N_LOCAL - 1))
    score = jnp.where(n_i > cur, -jnp.inf, jnp.where(forced, FORCE_SCORE, imp_t[0:nbr]))
    rank = jnp.zeros((nbr, tq), I32)
    for j in range(nb):
        sj = jnp.broadcast_to(score[j:j + 1, :], (nbr, tq))
        ahead = (sj > score) | ((sj == score) & (n_i > j))
        rank = rank + ahead.astype(I32)
    sel_t = ((rank < min(TOP_N, nb)) & (score > -jnp.inf)).astype(F32)
    if nbr < LANE:
        sel_t = jnp.concatenate([sel_t, jnp.zeros((LANE - nbr, tq), F32)], axis=0)
    sel_t = sel_t.astype(BF16)

    k_r = lax.broadcasted_iota(I32, (tq, width), 0)
    q_c = q_off(tq)
    causal = k_r <= q_c
    bias_far = jnp.concatenate(
        [jnp.full((1, tq), rb_ref[N_BUCKETS - 1, g * HPG + hh], F32) for hh in range(HPG)], axis=1)

    def chunk(ref, c):
        return ref[pl.ds(pl.multiple_of(c * tq, tq), tq), :]

    def scores(ref, c):
        return _dot_nt(chunk(ref, c), qs) * SCALE

    def selected(c):
        hit = _dot(chunk(et_ref, c), sel_t)
        return jnp.concatenate([hit] * HPG, axis=1) > 0.5

    fs = _FlashT(m_s, l_s, a_s)
    fw = _FlashT(m_w, l_w, a_w)

    def both(c, bias, sel_extra, win_mask, first):
        sel_mask = selected(c) if sel_extra is None else selected(c) & sel_extra
        lg_s = jnp.where(sel_mask, scores(ks_ref, c) + bias, NEG_INF)
        lg_w = scores(kw_ref, c) + bias
        if win_mask is not None:
            lg_w = jnp.where(win_mask, lg_w, NEG_INF)
        _FlashT.step([fs, fw], [lg_s, lg_w], [vst_ref[c], vwt_ref[c]], first)

    both(i, toep_ref[1], causal, causal, True)

    @pl.when(i >= 1)
    def _():
        both(i - 1, toep_ref[0], None, None, False)

    @pl.when(i >= 2)
    def _():
        both(i - 2, bias_far, None, k_r >= q_c, False)

    def far_step(c, carry):
        fs.update(jnp.where(selected(c), scores(ks_ref, c) + bias_far, NEG_INF), vst_ref[c])
        return carry

    lax.fori_loop(0, jnp.maximum(i - 2, 0), far_step, 0)

    g_t = jnp.concatenate([g_ref[t * LANE:(t + 1) * LANE, :].T for t in range(sub)], axis=1)

    def gate(branch):
        return jnp.concatenate(
            [g_t[branch * HPG + hh:branch * HPG + hh + 1, :] for hh in range(HPG)], axis=1)

    o_t = gate(0) * oc_t + gate(1) * fs.result() + gate(2) * fw.result()
    for hh in range(HPG):
        for t in range(sub):
            lanes = slice(hh * tq + t * LANE, hh * tq + (t + 1) * LANE)
            o_ref[t * LANE:(t + 1) * LANE, hh * LANE:(hh + 1) * LANE] = o_t[:, lanes].T


def _nsa_prompt(rel_bias, zb, kc, vct, gates, toep, bias_c, batch, seq):
    tq = NSA_TQ
    assert WINDOW == 2 * tq and tq % LANE == 0 and seq % tq == 0
    nq = seq // tq
    nb = seq // CMP_BLOCK
    assert nb <= LANE
    width = HPG * tq
    k = lax.broadcasted_iota(I32, (seq, LANE), 0)
    n = lax.broadcasted_iota(I32, (seq, LANE), 1)
    expand_t = (k // CMP_BLOCK == n).astype(BF16)
    body = functools.partial(_nsa_body, nb=nb, seq=seq)

    def kv_spec(blk):
        return pl.BlockSpec((seq, HEAD_DIM), lambda b, g, i: (b, blk + g))

    stat = pltpu.VMEM((1, width), F32)
    accum = pltpu.VMEM((HEAD_DIM, width), F32)
    v_t = pltpu.VMEM((nq, HEAD_DIM, tq), BF16)
    return pl.pallas_call(
        body,
        grid=(batch, G_NSA, nq),
        in_specs=[
            pl.BlockSpec(memory_space=pltpu.SMEM),
            pl.BlockSpec((tq, HPG * HEAD_DIM), lambda b, g, i: (b * nq + i, g)),
            pl.BlockSpec((None, None, LANE, HEAD_DIM), lambda b, g, i: (b, g, 0, 0)),
            pl.BlockSpec((None, None, HEAD_DIM, LANE), lambda b, g, i: (b, g, 0, 0)),
            kv_spec(KS_BLK), kv_spec(VS_BLK), kv_spec(KW_BLK), kv_spec(VW_BLK),
            pl.BlockSpec((tq, LANE), lambda b, g, i: (b * nq + i, g)),
            pl.BlockSpec((2, tq, width), lambda b, g, i: (0, 0, g)),
            pl.BlockSpec((None, LANE, width), lambda b, g, i: (i, 0, g)),
            pl.BlockSpec((seq, LANE), lambda b, g, i: (0, 0)),
        ],
        out_specs=pl.BlockSpec((tq, HPG * HEAD_DIM), lambda b, g, i: (b * nq + i, g)),
        out_shape=jax.ShapeDtypeStruct((batch * seq, D_NSA), F32),
        scratch_shapes=[v_t, v_t, stat, stat, accum, stat, stat, accum],
        compiler_params=_cparams(3),
        name="nsa_prompt",
    )(rel_bias, zb, kc, vct, zb, zb, zb, zb, gates, toep, bias_c, expand_t)


def _cast_body(w_ref, o_ref):
    o_ref[...] = w_ref[...].astype(BF16)


def _cast_layer_bf16(w, l, cols=None):
    _, rows, n = w.shape
    cols = n if cols is None else cols
    assert cols % LANE == 0
    tr = _pick_tile(rows, (256, 128))
    return pl.pallas_call(
        _cast_body,
        grid=(rows // tr,),
        in_specs=[pl.BlockSpec((None, tr, cols), lambda i: (l, i, 0))],
        out_specs=pl.BlockSpec((tr, cols), lambda i: (i, 0)),
        out_shape=jax.ShapeDtypeStruct((rows, cols), BF16),
        compiler_params=_cparams(1),
        name="cast_bf16",
    )(w)


def _prep_weights(l, w_in, g_q, g_k, w_cmp_k, w_cmp_v, w_out, w_gate, w_up, w_down):
    n_gate = 3 * H_NSA
    g0 = D_NSA + 6 * D_KV
    wi = w_in[l]
    w_a = wi[:, :g0].astype(BF16)
    w_b = wi[:, g0 + n_gate:].astype(BF16)
    wg = wi[:, g0:g0 + n_gate].reshape(D_MODEL, 3, G_NSA, HPG)
    wg = jnp.transpose(wg, (0, 2, 1, 3)).reshape(D_MODEL, G_NSA, 3 * HPG)
    wg = jnp.pad(wg, ((0, 0), (0, 0), (0, LANE - 3 * HPG))).reshape(D_MODEL, G_NSA * LANE)
    ones = jnp.ones((HEAD_DIM,), F32)
    gain = jnp.concatenate(
        [jnp.tile(g_q[l], H_NSA), jnp.tile(ones, 2 * G_NSA),
         jnp.tile(g_k[l, 1], G_NSA), jnp.tile(ones, G_NSA),
         jnp.tile(g_k[l, 2], G_NSA), jnp.tile(ones, G_NSA),
         jnp.tile(ones, 3 * H_SB)])[None, :]
    wk = w_cmp_k[l].reshape(CMP_BLOCK * HEAD_DIM, HEAD_DIM).astype(BF16)
    wv = w_cmp_v[l].reshape(CMP_BLOCK * HEAD_DIM, HEAD_DIM).astype(BF16)
    return dict(
        w_a=w_a, w_b=w_b, w_gate=wg.astype(BF16), gain=gain, wk=wk, wv=wv,
        w_kv=jnp.concatenate([wk, wv], axis=1),
        w_out=_cast_layer_bf16(w_out, l), w_gate_ffn=_cast_layer_bf16(w_gate, l),
        w_up=_cast_layer_bf16(w_up, l), w_down=_cast_layer_bf16(w_down, l))


def _prompt_layer(x2d, wt, p, tables, rel_bias, batch, seq, sb_args):
    na, bk, bv, gk0, go, nf = p
    zb, sb_rows, nsa_rows, win_rows, gates = _inproj(
        x2d, na, wt["w_a"], wt["w_b"], wt["w_gate"], wt["gain"], seq, min(WINDOW, seq))
    kc, vct = _cmp_prompt(nsa_rows, batch, seq, wt["wk"], wt["wv"], bk, bv, gk0)
    o_nsa = _nsa_prompt(rel_bias, zb, kc, vct, gates, tables[0], tables[1], batch, seq)
    o_sb = _sb_prompt(zb, batch, seq)
    x1 = _outproj(o_nsa, o_sb, go, wt["w_out"], x2d)
    x2, o_sb_sample = _ffn(x1, nf, wt["w_gate_ffn"], wt["w_up"], wt["w_down"], sb_args)
    return x2, sb_rows, nsa_rows, win_rows, o_sb_sample


ROWS_S = 16


def _sb_dec_step(ph, c, last_c, q_ref, page_refs, up_ref, upp_ref, o_ref, z_ref, acc_ref, page):
    pps = len(page_refs)
    width = page * H_SB
    r = lax.broadcasted_iota(I32, (ROWS_S, width), 0)
    lane = lax.broadcasted_iota(I32, (ROWS_S, width), 1)
    own_head = (lane % H_SB) == r

    @pl.when(ph == 0)
    def _():
        q = q_ref[...]
        for u in range(pps):
            ks = page_refs[u][...].reshape(width, HEAD_DIM).astype(BF16)
            zt = _dot_nt(q, ks)
            z_ref[pl.ds(c * pps + u, 1), :] = jnp.sum(jnp.where(own_head, zt, 0.0), axis=0,
                                                      keepdims=True)

    @pl.when((ph == 1) & (c == 0))
    def _():
        z = z_ref[...] * SCALE
        log_1m = _neg_softplus(z)
        cs = _dot_split(log_1m, up_ref[...])
        hi, lo = _split_bf16(cs[:, width:])
        later_pages = _dot(upp_ref[...], hi) + _dot(upp_ref[...], lo)
        z_ref[...] = jnp.exp(z + log_1m + cs[:, :width] + later_pages)
        acc_ref[...] = jnp.zeros_like(acc_ref)

    @pl.when(ph == 1)
    def _():
        for u in range(pps):
            vs = page_refs[u][...].reshape(width, HEAD_DIM).astype(BF16)
            a = jnp.broadcast_to(z_ref[pl.ds(c * pps + u, 1), :], (ROWS_S, width))
            acc_ref[...] += _dot(jnp.where(own_head, a, 0.0).astype(BF16), vs)

    @pl.when((ph == 1) & (c == last_c))
    def _():
        o_ref[...] = acc_ref[0:H_SB, :]


def _sb_dec_consts(npg, page):
    assert npg % SB_PAGES_PER_STEP == 0 and npg % 8 == 0
    width = page * H_SB
    r = lax.broadcasted_iota(I32, (width, 2 * width), 0)
    c = lax.broadcasted_iota(I32, (width, 2 * width), 1)
    same_head = (r % H_SB) == (c % H_SB)
    up = (same_head & ((c >= width) | (r // H_SB > c // H_SB))).astype(BF16)
    pr = lax.broadcasted_iota(I32, (npg, npg), 0)
    pc = lax.broadcasted_iota(I32, (npg, npg), 1)
    upp = (pc > pr).astype(BF16)
    return up, upp


NSA_CH = 4 * G_NSA


def _cmp_dec_body(pt_ref, *refs, pps, page):
    page_refs = refs[:pps]
    w_ref, kc_ref, vc_ref, yk_ref, yv_ref = refs[pps:]
    halves = page // CMP_BLOCK
    nblk = pps * halves
    acc = jnp.zeros((nblk * NSA_CH, 2 * HEAD_DIM), F32)

    def token_rows(u, hf, l):
        return page_refs[u][pl.ds((hf * CMP_BLOCK + l) * NSA_CH, NSA_CH), :]

    for l in range(0, CMP_BLOCK, 2):
        x_l = jnp.concatenate(
            [jnp.concatenate([token_rows(u, hf, l), token_rows(u, hf, l + 1)], axis=1)
             for u in range(pps) for hf in range(halves)], axis=0).astype(BF16)
        acc = acc + _dot(x_l, w_ref[l * HEAD_DIM:(l + 2) * HEAD_DIM, :])
    yk_ref[...] = acc[:, :HEAD_DIM]
    yv_ref[...] = acc[:, HEAD_DIM:]
    for g in range(G_NSA):
        kc_ref[g] = yk_ref[pl.ds(g, nblk, stride=NSA_CH), :]
        vc_ref[g] = yv_ref[pl.ds(G_NSA + g, nblk, stride=NSA_CH), :]


def _cmp_decode(l, pt_flat, cache_nsa2d, w_kv, dec_b, page):
    npg = pt_flat.shape[0] // dec_b
    pps = min(CMP_PAGES_PER_STEP, npg)
    assert npg % pps == 0
    halves = page // CMP_BLOCK
    nblk = pps * halves
    body = functools.partial(_cmp_dec_body, pps=pps, page=page)

    def page_spec(u):
        return pl.BlockSpec((None, None, page * NSA_CH, HEAD_DIM),
                            lambda b, c, pt: (l, pt[b * npg + c * pps + u], 0, 0))

    out_spec = pl.BlockSpec((None, G_NSA, nblk, HEAD_DIM), lambda b, c, pt: (b, 0, c, 0))
    shape = jax.ShapeDtypeStruct((dec_b, G_NSA, npg * halves, HEAD_DIM), F32)
    grid_spec = pltpu.PrefetchScalarGridSpec(
        num_scalar_prefetch=1,
        grid=(dec_b, npg // pps),
        in_specs=[page_spec(u) for u in range(pps)]
        + [pl.BlockSpec(w_kv.shape, lambda b, c, pt: (0, 0))],
        out_specs=[out_spec, out_spec],
        scratch_shapes=[pltpu.VMEM((nblk * NSA_CH, HEAD_DIM), F32)] * 2,
    )
    return pl.pallas_call(
        body,
        grid_spec=grid_spec,
        out_shape=[shape, shape],
        compiler_params=_cparams(2),
        name="cmp_decode",
    )(pt_flat, *([cache_nsa2d] * pps), w_kv)


def _nsa_sel_body(q_ref, kcr_ref, vcr_ref, new_ref, wk0_ref, wv0_ref, bk_ref, bv_ref, gk_ref,
                  sbc_ref, kw_ref, vw_ref, sbw_ref, idx_ref, oc_ref, ow_ref,
                  *, nb0, nbp, past_len, w_buf, wp):
    nb = nb0 + 1
    new = new_ref[...].astype(BF16)
    new_k = _rms(_dot(new, wk0_ref[...]) + bk_ref[...], gk_ref[...])
    new_v = _dot(new, wv0_ref[...]) + bv_ref[...]
    row8 = lax.broadcasted_iota(I32, (8, HEAD_DIM), 0)
    n1 = lax.broadcasted_iota(I32, (1, nbp), 1)
    nq = lax.broadcasted_iota(I32, (ROWS_S, nbp), 1)
    pos_q = past_len
    pad = jnp.zeros((nbp - nb0 - 8, HEAD_DIM), F32)
    for g in range(G_NSA):
        q = q_ref[g]
        kc = jnp.concatenate(
            [_rms(kcr_ref[g] + bk_ref[...], gk_ref[...]),
             jnp.where(row8 == 0, jnp.broadcast_to(new_k[g:g + 1], (8, HEAD_DIM)), 0.0), pad],
            axis=0).astype(BF16)
        vc = jnp.concatenate(
            [vcr_ref[g] + bv_ref[...],
             jnp.where(row8 == 0, jnp.broadcast_to(new_v[G_NSA + g:G_NSA + g + 1], (8, HEAD_DIM)),
                       0.0), pad], axis=0).astype(BF16)
        s_c = _dot_nt(q, kc) * SCALE + sbc_ref[g]
        valid_c = (CMP_BLOCK * nq + (CMP_BLOCK - 1) <= pos_q) & (nq < nb)
        p_c = _masked_softmax(s_c, valid_c)
        oc_ref[g] = _dot(p_c.astype(BF16), vc)
        imp = p_c[0:1]
        for hh in range(1, HPG):
            imp = imp + p_c[hh:hh + 1]

        cur = pos_q // CMP_BLOCK
        forced = (n1 == 0) | (n1 >= cur - (N_LOCAL - 1))
        score = jnp.where(n1 > cur, -jnp.inf, jnp.where(forced, FORCE_SCORE, imp))
        sq = jnp.broadcast_to(score, (LANE, nbp))
        col = jnp.concatenate([sq[:, k * LANE:(k + 1) * LANE].T for k in range(nbp // LANE)],
                              axis=0)
        j_i = lax.broadcasted_iota(I32, (nbp, LANE), 0)
        ranks = []
        for k in range(nbp // LANE):
            mine = jnp.broadcast_to(score[:, k * LANE:(k + 1) * LANE], (nbp, LANE))
            n_i = k * LANE + lax.broadcasted_iota(I32, (nbp, LANE), 1)
            ahead = (col > mine) | ((col == mine) & (j_i < n_i))
            ranks.append(jnp.sum(ahead.astype(F32), axis=0, keepdims=True))
        rank = jnp.concatenate(ranks, axis=1)
        slot = lax.broadcasted_iota(I32, (TOP_N, nbp), 0).astype(F32)
        hit = jnp.broadcast_to(rank, (TOP_N, nbp)) == slot
        ids = jnp.sum(jnp.where(hit, lax.broadcasted_iota(I32, (TOP_N, nbp), 1).astype(F32), 0.0),
                      axis=1, keepdims=True)
        idx_ref[g] = jnp.broadcast_to(ids, (TOP_N, LANE)).astype(I32)

        jw = lax.broadcasted_iota(I32, (ROWS_S, wp), 1)
        dist_w = w_buf - jw
        valid_w = (dist_w >= 0) & (dist_w <= WINDOW) & (past_len - w_buf + jw >= 0)
        s_w = _dot_nt(q, kw_ref[g]) * SCALE + sbw_ref[g]
        p_w = _masked_softmax(s_w, valid_w)
        ow_ref[g] = _dot(p_w.astype(BF16), vw_ref[g])


def _nsa_select(q_nsa, kcr, vcr, new_c, wk0, wv0, bk, bv, gk0, sbc, kw, vw, sbw, past_len, w_buf):
    dec_b, _, nb0, _ = kcr.shape
    nbp = sbc.shape[-1]
    wp = sbw.shape[-1]
    assert nb0 + 1 >= TOP_N and nbp >= nb0 + 8 and nb0 % 8 == 0
    body = functools.partial(_nsa_sel_body, nb0=nb0, nbp=nbp, past_len=past_len, w_buf=w_buf, wp=wp)

    def per_b(shape):
        nd = len(shape)
        return pl.BlockSpec((None,) + tuple(shape), lambda b: (b,) + (0,) * nd)

    def whole(arr):
        nd = arr.ndim
        return pl.BlockSpec(arr.shape, lambda b: (0,) * nd)

    vec = (G_NSA, ROWS_S, HEAD_DIM)
    return pl.pallas_call(
        body,
        grid=(dec_b,),
        in_specs=[per_b(vec), per_b((G_NSA, nb0, HEAD_DIM)), per_b((G_NSA, nb0, HEAD_DIM)),
                  per_b((8, HEAD_DIM)), whole(wk0), whole(wv0), whole(bk), whole(bv), whole(gk0),
                  whole(sbc), per_b((G_NSA, wp, HEAD_DIM)), per_b((G_NSA, wp, HEAD_DIM)), whole(sbw)],
        out_specs=[per_b((G_NSA, TOP_N, LANE)), per_b(vec), per_b(vec)],
        out_shape=[jax.ShapeDtypeStruct((dec_b, G_NSA, TOP_N, LANE), I32),
                   jax.ShapeDtypeStruct((dec_b,) + vec, F32),
                   jax.ShapeDtypeStruct((dec_b,) + vec, F32)],
        compiler_params=_cparams(1),
        name="nsa_select",
    )(q_nsa, kcr, vcr, new_c, wk0, wv0, bk, bv, gk0, sbc, kw, vw, sbw)


def _nsa_gather_body(pt_ref, idx_ref, rb_ref, q_ref, p0_ref, p1_ref, new_ref, gs_ref, oc_ref,
                     ow_ref, o_ref, m_ref, l_ref, a_ref, *, nb0, past_len):
    b = pl.program_id(0)
    k = pl.program_id(1)
    pos_q = past_len
    row = lax.broadcasted_iota(I32, (CMP_BLOCK, HEAD_DIM), 0)
    l1 = lax.broadcasted_iota(I32, (1, CMP_BLOCK), 1)
    hrow = lax.broadcasted_iota(I32, (ROWS_S, CMP_BLOCK), 0)
    for g, page_ref in enumerate((p0_ref, p1_ref)):
        blk = idx_ref[(b * G_NSA + g) * TOP_N + k]
        is_new = blk >= nb0
        new_k = jnp.broadcast_to(new_ref[g:g + 1, :], (CMP_BLOCK, HEAD_DIM))
        new_v = jnp.broadcast_to(new_ref[G_NSA + g:G_NSA + g + 1, :], (CMP_BLOCK, HEAD_DIM))
        ks = page_ref[pl.ds(2 * G_NSA + g, CMP_BLOCK, stride=NSA_CH), :]
        vs = page_ref[pl.ds(3 * G_NSA + g, CMP_BLOCK, stride=NSA_CH), :]
        ks = jnp.where(is_new, jnp.where(row == 0, new_k, 0.0), ks).astype(BF16)
        vs = jnp.where(is_new, jnp.where(row == 0, new_v, 0.0), vs).astype(BF16)
        dist = pos_q - (blk * CMP_BLOCK + l1)
        bucket = jnp.broadcast_to(_t5_bucket(dist), (ROWS_S, CMP_BLOCK))
        bias = jnp.zeros((ROWS_S, CMP_BLOCK), F32)
        for hh in range(HPG):
            for bk in range(N_BUCKETS):
                bias = jnp.where((hrow == hh) & (bucket == bk), rb_ref[bk, g * HPG + hh], bias)
        s = _dot_nt(q_ref[g], ks) * SCALE + bias
        logit = jnp.where(jnp.broadcast_to(dist >= 0, (ROWS_S, CMP_BLOCK)), s, NEG_INF)
        fl = _Flash(m_ref.at[g], l_ref.at[g], a_ref.at[g])

        @pl.when(k == 0)
        def _():
            fl.first(logit, vs)

        @pl.when(k > 0)
        def _():
            fl.update(logit, vs)

        @pl.when(k == pl.num_programs(1) - 1)
        def _():
            o_ref[g] = gs_ref[g, 0] * oc_ref[g] + gs_ref[g, 1] * fl.result() + gs_ref[g, 2] * ow_ref[g]


def _nsa_gather(l, pt_flat, idx_flat, rel_bias, q_nsa, cache_nsa2d, new_s, gs, oc, ow, nb0,
                past_len, page):
    dec_b = q_nsa.shape[0]
    npg = pt_flat.shape[0] // dec_b
    halves = page // CMP_BLOCK
    body = functools.partial(_nsa_gather_body, nb0=nb0, past_len=past_len)

    def page_spec(g):
        def imap(b, k, pt, idx):
            blk = jnp.minimum(idx[(b * G_NSA + g) * TOP_N + k], nb0 - 1)
            return (l, pt[b * npg + blk // halves], blk % halves, 0)
        return pl.BlockSpec((None, None, CMP_BLOCK * NSA_CH, HEAD_DIM), imap)

    def per_b(shape):
        nd = len(shape)
        return pl.BlockSpec((None,) + tuple(shape), lambda b, k, pt, idx: (b,) + (0,) * nd)

    vec = (G_NSA, ROWS_S, HEAD_DIM)
    grid_spec = pltpu.PrefetchScalarGridSpec(
        num_scalar_prefetch=2,
        grid=(dec_b, TOP_N),
        in_specs=[pl.BlockSpec(memory_space=pltpu.SMEM), per_b(vec), page_spec(0), page_spec(1),
                  per_b((8, HEAD_DIM)), per_b((G_NSA, 3, ROWS_S, HEAD_DIM)), per_b(vec), per_b(vec)],
        out_specs=per_b(vec),
        scratch_shapes=[pltpu.VMEM((G_NSA, ROWS_S, 1), F32), pltpu.VMEM((G_NSA, ROWS_S, 1), F32),
                        pltpu.VMEM((G_NSA, ROWS_S, HEAD_DIM), F32)],
    )
    return pl.pallas_call(
        body,
        grid_spec=grid_spec,
        out_shape=jax.ShapeDtypeStruct((dec_b,) + vec, F32),
        compiler_params=_cparams(2),
        name="nsa_gather",
    )(pt_flat, idx_flat, rel_bias, q_nsa, cache_nsa2d, cache_nsa2d, new_s, gs, oc, ow)


def _pad_rows(x, rows):
    return jnp.pad(x, ((0, rows - x.shape[0]),) + ((0, 0),) * (x.ndim - 1))


def _sample_inproj(xs, wt, p, dec_b):
    rows = xs.shape[0]
    zb, sb_rows, nsa_rows, win_rows, gates = _inproj(
        xs, p[0], wt["w_a"], wt["w_b"], wt["w_gate"], wt["gain"], rows, rows)
    zb8 = zb[:dec_b]
    q_sb = zb8[:, QS_BLK * LANE:KSB_BLK * LANE].reshape(dec_b, H_SB, HEAD_DIM)
    q_sb = jnp.pad(q_sb, ((0, 0), (0, ROWS_S - H_SB), (0, 0)))
    return q_sb, (zb8, sb_rows[:dec_b], nsa_rows[:dec_b], win_rows[:dec_b], gates)


def _sample_layer(l, xs, projected, o_sb, wt, p, sb_tabs, rel_bias, pt_flat, cache_nsa2d, win_l,
                  w_cmp0, dec_b, past_len, page):
    na, bk, bv, gk0, go, nf = p
    sbc, sbw = sb_tabs
    w_buf = win_l.shape[1]
    rows = xs.shape[0]
    zb8, sb_rows, nsa_rows, win_rows, gates = projected
    o_sb = o_sb.reshape(dec_b, D_SB)

    q_nsa = zb8[:, :D_NSA].reshape(dec_b, G_NSA, HPG, HEAD_DIM)
    q_nsa = jnp.pad(q_nsa, ((0, 0), (0, 0), (0, ROWS_S - HPG), (0, 0)))
    kcr, vcr = _cmp_decode(l, pt_flat, cache_nsa2d, wt["w_kv"], dec_b, page)
    new_c = jnp.pad(nsa_rows[:, :2 * D_KV].reshape(dec_b, 2 * G_NSA, HEAD_DIM),
                    ((0, 0), (0, 8 - 2 * G_NSA), (0, 0)))
    new_s = jnp.pad(nsa_rows[:, 2 * D_KV:].reshape(dec_b, 2 * G_NSA, HEAD_DIM),
                    ((0, 0), (0, 8 - 2 * G_NSA), (0, 0)))
    new_w = win_rows.reshape(dec_b, 1, 2, G_NSA, HEAD_DIM)
    win_all = jnp.concatenate([win_l, new_w], axis=1)
    wp = sbw.shape[-1]
    win_t = jnp.transpose(win_all, (2, 0, 3, 1, 4))
    win_t = jnp.pad(win_t, ((0, 0), (0, 0), (0, 0), (0, wp - w_buf - 1), (0, 0))).astype(BF16)
    idx, oc, ow = _nsa_select(q_nsa, kcr, vcr, new_c, w_cmp0[0], w_cmp0[1], bk, bv, gk0, sbc,
                              win_t[0], win_t[1], sbw, past_len, w_buf)
    gs = gates[:dec_b].reshape(dec_b, G_NSA, LANE)[:, :, :3 * HPG].reshape(dec_b, G_NSA, 3, HPG)
    gs = jnp.pad(gs, ((0, 0), (0, 0), (0, 0), (0, ROWS_S - HPG)))
    gs = jnp.broadcast_to(gs[..., None], gs.shape + (HEAD_DIM,))
    o_nsa = _nsa_gather(l, pt_flat, idx[:, :, :, 0].reshape(-1), rel_bias, q_nsa, cache_nsa2d,
                        new_s, gs, oc, ow, kcr.shape[2], past_len, page)
    o_nsa = o_nsa[:, :, :HPG].reshape(dec_b, D_NSA)

    x1 = _outproj(_pad_rows(o_nsa, rows), _pad_rows(o_sb, rows), go, wt["w_out"], xs)
    x2 = _ffn(x1, nf, wt["w_gate_ffn"], wt["w_up"], wt["w_down"])
    new_win = win_all[:, 1:]
    return x2, sb_rows, nsa_rows, new_win


def kernel(x_prompt, x_sample, cache_sb_kv, cache_nsa_kv, state_win_kv, page_table, rel_bias,
           norm_attn, w_in, g_q, g_k, w_cmp_k, b_cmp_k, w_cmp_v, b_cmp_v, g_out, w_out,
           norm_ffn, w_gate, w_up, w_down):
    batch, seq, _ = x_prompt.shape
    dec_b, dec_t, _ = x_sample.shape
    assert dec_t == 1
    depth, n_pool, page = cache_sb_kv.shape[:3]
    npg = page_table.shape[1]
    past_len = npg * page
    w_buf = state_win_kv.shape[2]
    nb_s = past_len // CMP_BLOCK + 1
    nbp_s = -(-(nb_s + 7) // LANE) * LANE
    wp_s = -(-(w_buf + 1) // LANE) * LANE

    toep, bias_c, sc, sw = _bias_tables(rel_bias, seq, past_len, w_buf, nbp_s, wp_s)
    sbc = jnp.pad(sc.reshape(G_NSA, HPG, nbp_s), ((0, 0), (0, ROWS_S - HPG), (0, 0)))
    sbw = jnp.pad(sw.reshape(G_NSA, HPG, wp_s), ((0, 0), (0, ROWS_S - HPG), (0, 0)))
    pt_flat = page_table.reshape(-1).astype(I32)
    cache_nsa2d = cache_nsa_kv.reshape(depth, n_pool, page * NSA_CH, HEAD_DIM)

    xp = x_prompt.reshape(batch * seq, D_MODEL)
    xs = _pad_rows(x_sample.reshape(dec_b, D_MODEL), ROWS_S)
    sb_p, sb_s, nsa_p, nsa_s, win_p, win_s = [], [], [], [], [], []
    w_keep = min(WINDOW, seq)
    for l in range(depth):
        wt = _prep_weights(l, w_in, g_q, g_k, w_cmp_k, w_cmp_v, w_out, w_gate, w_up, w_down)
        p = (norm_attn[l][None], b_cmp_k[l][None], b_cmp_v[l][None], g_k[l, 0][None],
             g_out[l][None], norm_ffn[l][None])
        q_sb, projected = _sample_inproj(xs, wt, p, dec_b)
        xp, sb_rows, nsa_rows, win_rows, o_sb = _prompt_layer(
            xp, wt, p, (toep, bias_c), rel_bias, batch, seq, (l, pt_flat, q_sb, cache_sb_kv))
        sb_p.append(sb_rows.reshape(batch, seq, 2, H_SB, HEAD_DIM))
        nsa_p.append(nsa_rows.reshape(batch, seq, 4, G_NSA, HEAD_DIM))
        win_p.append(win_rows.reshape(batch, w_keep, 2, G_NSA, HEAD_DIM))

        w_cmp0 = (w_cmp_k[l, 0].astype(BF16), w_cmp_v[l, 0].astype(BF16))
        xs, sb_rows, nsa_rows, new_win = _sample_layer(
            l, xs, projected, o_sb, wt, p, (sbc, sbw), rel_bias, pt_flat, cache_nsa2d,
            state_win_kv[l], w_cmp0, dec_b, past_len, page)
        sb_s.append(sb_rows.reshape(dec_b, 1, 2, H_SB, HEAD_DIM))
        nsa_s.append(nsa_rows.reshape(dec_b, 1, 4, G_NSA, HEAD_DIM))
        win_s.append(new_win)
    return (xp.reshape(batch, seq, D_MODEL), xs[:dec_b].reshape(dec_b, 1, D_MODEL),
            jnp.stack(sb_p), jnp.stack(sb_s), jnp.stack(nsa_p), jnp.stack(nsa_s),
            jnp.stack(win_p), jnp.stack(win_s))
```

```python
import functools
import math

import jax
import jax.numpy as jnp
from jax import lax
from jax.experimental import pallas as pl
from jax.experimental.pallas import tpu as pltpu

F32 = jnp.float32
BF16 = jnp.bfloat16
I32 = jnp.int32

LANE = 128
HEAD_DIM = 128
G_NSA = 2
HPG = 4
H_NSA = G_NSA * HPG
H_SB = 8
D_NSA = H_NSA * HEAD_DIM
D_SB = H_SB * HEAD_DIM
D_MODEL = D_NSA + D_SB
D_KV = G_NSA * HEAD_DIM
CMP_BLOCK = 64
TOP_N = 16
N_LOCAL = 2
WINDOW = 512
N_BUCKETS = 32
MAX_DISTANCE = 128
EPS = 1e-6
FORCE_SCORE = 1e4
NEG_INF = -1e30
SCALE = HEAD_DIM ** -0.5

COL_TILE = 512
N_MAIN = D_NSA + 6 * D_KV + 3 * D_SB
QS_BLK = (D_NSA + 6 * D_KV) // LANE
KSB_BLK = QS_BLK + D_SB // LANE
VSB_BLK = KSB_BLK + D_SB // LANE
KS_BLK = (D_NSA + 2 * D_KV) // LANE
VS_BLK = KS_BLK + G_NSA
KW_BLK = VS_BLK + G_NSA
VW_BLK = KW_BLK + G_NSA
VMEM_LIMIT = 56 * 1024 * 1024

SB_TQ = 256
SB_TK = 256
SB_HB = 8
NSA_TQ = 256
SB_PAGES_PER_STEP = 16
CMP_PAGES_PER_STEP = 16


VMEM_LIMIT_FUSED = 62 * 1024 * 1024


def _cparams(n_axes, vmem_limit=VMEM_LIMIT):
    return pltpu.CompilerParams(dimension_semantics=("arbitrary",) * n_axes,
                                vmem_limit_bytes=vmem_limit)


def _dot(a, b):
    return jnp.dot(a, b, preferred_element_type=F32)


def _dot_nt(a, b):
    return lax.dot_general(a, b, (((1,), (1,)), ((), ())), preferred_element_type=F32)


def _split_bf16(x):
    hi = x.astype(BF16)
    lo = (x - hi.astype(F32)).astype(BF16)
    return hi, lo


def _dot_split(x, u):
    hi, lo = _split_bf16(x)
    return _dot(hi, u) + _dot(lo, u)


def _rms(y, gain):
    ms = jnp.mean(y * y, axis=-1, keepdims=True)
    return y * lax.rsqrt(ms + EPS) * gain


def _neg_softplus(z):
    return -(jnp.maximum(z, 0.0) + jnp.log(1.0 + jnp.exp(-jnp.abs(z))))


def _t5_bucket(dist):
    n = jnp.maximum(dist, 0)
    max_exact = N_BUCKETS // 2
    nf = jnp.maximum(n, 1).astype(F32)
    large = max_exact + (jnp.log(nf / max_exact) / math.log(MAX_DISTANCE / max_exact)
                         * (N_BUCKETS - max_exact)).astype(I32)
    large = jnp.minimum(large, N_BUCKETS - 1)
    return jnp.where(n < max_exact, n, large)


def _pick_tile(m, candidates):
    for t in candidates:
        if m % t == 0:
            return t
    raise ValueError(f"no tile for {m}")


def _bias_body(rb_ref, toep_ref, bc_ref, sc_ref, sw_ref, *, seq, past_len, w_buf):
    h = pl.program_id(0)
    tq = NSA_TQ

    def table(dist):
        bk = _t5_bucket(dist)
        out = jnp.zeros(dist.shape, F32)
        for b in range(N_BUCKETS):
            out = jnp.where(bk == b, rb_ref[b, h], out)
        return out

    k = lax.broadcasted_iota(I32, (tq, tq), 0)
    q = lax.broadcasted_iota(I32, (tq, tq), 1)
    toep_ref[0] = table(tq + q - k)
    toep_ref[1] = table(q - k)
    n = lax.broadcasted_iota(I32, (LANE, tq), 0)
    qc = lax.broadcasted_iota(I32, (LANE, tq), 1)
    for i in range(seq // tq):
        bc_ref[i] = table(i * tq + qc - CMP_BLOCK * n - (CMP_BLOCK - 1))
    n1 = lax.broadcasted_iota(I32, sc_ref.shape, 1)
    sc_ref[...] = table(past_len - CMP_BLOCK * n1 - (CMP_BLOCK - 1))
    j1 = lax.broadcasted_iota(I32, sw_ref.shape, 1)
    sw_ref[...] = table(w_buf - j1)


def _bias_tables(rel_bias, seq, past_len, w_buf, nbp_s, wp_s):
    body = functools.partial(_bias_body, seq=seq, past_len=past_len, w_buf=w_buf)
    nq = seq // NSA_TQ
    return pl.pallas_call(
        body,
        grid=(H_NSA,),
        in_specs=[pl.BlockSpec(memory_space=pltpu.SMEM)],
        out_specs=[
            pl.BlockSpec((2, NSA_TQ, NSA_TQ), lambda h: (0, 0, h)),
            pl.BlockSpec((nq, LANE, NSA_TQ), lambda h: (0, 0, h)),
            pl.BlockSpec((None, 1, nbp_s), lambda h: (h, 0, 0)),
            pl.BlockSpec((None, 1, wp_s), lambda h: (h, 0, 0)),
        ],
        out_shape=[
            jax.ShapeDtypeStruct((2, NSA_TQ, H_NSA * NSA_TQ), F32),
            jax.ShapeDtypeStruct((nq, LANE, H_NSA * NSA_TQ), F32),
            jax.ShapeDtypeStruct((H_NSA, 1, nbp_s), F32),
            jax.ShapeDtypeStruct((H_NSA, 1, wp_s), F32),
        ],
        compiler_params=_cparams(1),
        name="bias_tables",
    )(rel_bias)


Q_TILES = D_NSA // COL_TILE
NSA_TILE0 = Q_TILES
WIN_TILE = NSA_TILE0 + 4 * D_KV // COL_TILE
SB_TILE0 = WIN_TILE + 2 * D_KV // COL_TILE + D_SB // COL_TILE


def _inproj_body(x_ref, na_ref, wa_ref, wb_ref, wg_ref, gain_ref, zb_ref, sb_ref, nsa_ref, win_ref,
                 gate_ref, h_ref, *, w_keep):
    j = pl.program_id(1)
    tm = x_ref.shape[0]
    half = COL_TILE // 2

    @pl.when(j == 0)
    def _():
        h = _rms(x_ref[...], na_ref[...]).astype(BF16)
        h_ref[...] = h
        gate_ref[...] = jax.nn.sigmoid(_dot(h, wg_ref[...]))

    gain = gain_ref[...]

    def emit(w_ref, norm_chunks, f32_ref=None, row0=0):
        h = h_ref[...]
        accs = [_dot(h, w_ref[:, :half]), _dot(h, w_ref[:, half:])]
        for c in range(COL_TILE // LANE):
            sl = slice(c * LANE, (c + 1) * LANE)
            y = accs[c * LANE // half][:, (c * LANE) % half:(c * LANE) % half + LANE]
            if c in norm_chunks:
                y = _rms(y, gain[:, sl])
            zb_ref[:, sl] = y.astype(BF16)
            if f32_ref is not None:
                f32_ref[:, sl] = y[row0:, :]

    @pl.when(j < Q_TILES)
    def _():
        emit(wa_ref, (0, 1, 2, 3))

    @pl.when(j == NSA_TILE0)
    def _():
        emit(wa_ref, (), nsa_ref)

    @pl.when(j == NSA_TILE0 + 1)
    def _():
        emit(wa_ref, (0, 1), nsa_ref)

    @pl.when(j == WIN_TILE)
    def _():
        emit(wa_ref, (0, 1), win_ref, tm - w_keep)

    @pl.when((j > WIN_TILE) & (j < SB_TILE0))
    def _():
        emit(wb_ref, ())

    @pl.when(j >= SB_TILE0)
    def _():
        emit(wb_ref, (), sb_ref)


def _inproj(x2d, na, w_a, w_b, w_gate, gain_cols, seq, w_keep):
    m = x2d.shape[0]
    tm = _pick_tile(seq, (1024, 512, 256, 128, 16))
    assert w_keep <= tm
    tpb = seq // tm
    n_sb = 2 * D_SB // COL_TILE
    n_nsa = 4 * D_KV // COL_TILE
    n_a = w_a.shape[1] // COL_TILE
    n_b = w_b.shape[1] // COL_TILE
    assert n_a == WIN_TILE + 1 and n_a + n_b == N_MAIN // COL_TILE
    body = functools.partial(_inproj_body, w_keep=w_keep)
    return pl.pallas_call(
        body,
        grid=(m // tm, n_a + n_b),
        in_specs=[
            pl.BlockSpec((tm, D_MODEL), lambda i, j: (i, 0)),
            pl.BlockSpec((1, D_MODEL), lambda i, j: (0, 0)),
            pl.BlockSpec((D_MODEL, COL_TILE), lambda i, j: (0, jnp.minimum(j, n_a - 1))),
            pl.BlockSpec((D_MODEL, COL_TILE), lambda i, j: (0, jnp.maximum(j - n_a, 0))),
            pl.BlockSpec((D_MODEL, 2 * LANE), lambda i, j: (0, 0)),
            pl.BlockSpec((1, COL_TILE), lambda i, j: (0, j)),
        ],
        out_specs=[
            pl.BlockSpec((tm, COL_TILE), lambda i, j: (i, j)),
            pl.BlockSpec((tm, COL_TILE), lambda i, j: (i, jnp.clip(j - SB_TILE0, 0, n_sb - 1))),
            pl.BlockSpec((tm, COL_TILE), lambda i, j: (i, jnp.clip(j - NSA_TILE0, 0, n_nsa - 1))),
            pl.BlockSpec((w_keep, COL_TILE), lambda i, j: (i // tpb, 0)),
            pl.BlockSpec((tm, 2 * LANE), lambda i, j: (i, 0)),
        ],
        out_shape=[
            jax.ShapeDtypeStruct((m, N_MAIN), BF16),
            jax.ShapeDtypeStruct((m, 2 * D_SB), F32),
            jax.ShapeDtypeStruct((m, 4 * D_KV), F32),
            jax.ShapeDtypeStruct((m // seq * w_keep, 2 * D_KV), F32),
            jax.ShapeDtypeStruct((m, 2 * LANE), F32),
        ],
        scratch_shapes=[pltpu.VMEM((tm, D_MODEL), BF16)],
        compiler_params=_cparams(2),
        name="inproj",
    )(x2d, na, w_a, w_b, w_gate, gain_cols)


def _outproj_body(on_ref, os_ref, go_ref, w_ref, x_ref, y_ref, h_ref):
    j = pl.program_id(1)

    @pl.when(j == 0)
    def _():
        go = go_ref[...]
        h_ref[:, :D_NSA] = _rms(on_ref[...], go[:, :D_NSA]).astype(BF16)
        h_ref[:, D_NSA:] = _rms(os_ref[...], go[:, D_NSA:]).astype(BF16)

    y_ref[...] = x_ref[...] + _dot(h_ref[...], w_ref[...])


def _outproj(o_nsa, o_sb, g_out, w_out, x2d):
    m = x2d.shape[0]
    tm = _pick_tile(m, (1024, 512, 256, 128, 16))
    return pl.pallas_call(
        _outproj_body,
        grid=(m // tm, D_MODEL // COL_TILE),
        in_specs=[
            pl.BlockSpec((tm, D_NSA), lambda i, j: (i, 0)),
            pl.BlockSpec((tm, D_SB), lambda i, j: (i, 0)),
            pl.BlockSpec((1, D_MODEL), lambda i, j: (0, 0)),
            pl.BlockSpec((D_MODEL, COL_TILE), lambda i, j: (0, j)),
            pl.BlockSpec((tm, COL_TILE), lambda i, j: (i, j)),
        ],
        out_specs=pl.BlockSpec((tm, COL_TILE), lambda i, j: (i, j)),
        out_shape=jax.ShapeDtypeStruct((m, D_MODEL), F32),
        scratch_shapes=[pltpu.VMEM((tm, D_MODEL), BF16)],
        compiler_params=_cparams(2),
        name="outproj",
    )(o_nsa, o_sb, g_out, w_out, x2d)


def _ffn_body(*refs, sb):
    if sb is None:
        x_ref, nf_ref, wg_ref, wu_ref, wd_ref, y_ref, h_ref = refs
    else:
        pps = sb["pps"]
        x_ref, nf_ref, wg_ref, wu_ref, wd_ref, q_ref = refs[1:7]
        page_refs = refs[7:7 + pps]
        up_ref, upp_ref, y_ref, o_ref, h_ref, z_ref, sacc_ref = refs[7 + pps:]
    i = pl.program_id(0)
    j = pl.program_id(1)

    @pl.when(j == 0)
    def _():
        x = x_ref[...]
        h_ref[...] = _rms(x, nf_ref[...]).astype(BF16)
        y_ref[...] = x

    h = h_ref[...]
    act = jax.nn.silu(_dot(h, wg_ref[...])) * _dot(h, wu_ref[...])
    y_ref[...] += _dot(act.astype(BF16), wd_ref[...])

    if sb is not None:
        nc = sb["n_chunks"]
        s = i * pl.num_programs(1) + j

        @pl.when(s < sb["n_steps"])
        def _():
            _sb_dec_step((s // nc) % 2, s % nc, nc - 1, q_ref, page_refs, up_ref, upp_ref, o_ref,
                         z_ref, sacc_ref, sb["page"])


def _ffn(x2d, norm_ffn, w_gate, w_up, w_down, sb_args=None):
    m = x2d.shape[0]
    d_ff = w_gate.shape[1]
    if sb_args is None:
        tm = _pick_tile(m, (512, 256, 128, 16))
        tf = _pick_tile(d_ff, (512, 256))
        x_mode = None
    else:
        tm = _pick_tile(m, (1024, 512, 256, 128))
        tf = _pick_tile(d_ff, (256,))
        x_mode = pl.Buffered(1)
    nf = d_ff // tf
    in_specs = [
        pl.BlockSpec((tm, D_MODEL), lambda i, j, *_: (i, 0), pipeline_mode=x_mode),
        pl.BlockSpec((1, D_MODEL), lambda i, j, *_: (0, 0)),
        pl.BlockSpec((D_MODEL, tf), lambda i, j, *_: (0, j)),
        pl.BlockSpec((D_MODEL, tf), lambda i, j, *_: (0, j)),
        pl.BlockSpec((tf, D_MODEL), lambda i, j, *_: (j, 0)),
    ]
    y_spec = pl.BlockSpec((tm, D_MODEL), lambda i, j, *_: (i, 0))
    y_shape = jax.ShapeDtypeStruct((m, D_MODEL), F32)
    h_scratch = pltpu.VMEM((tm, D_MODEL), BF16)
    if sb_args is None:
        return pl.pallas_call(
            functools.partial(_ffn_body, sb=None),
            grid=(m // tm, nf),
            in_specs=in_specs,
            out_specs=y_spec,
            out_shape=y_shape,
            scratch_shapes=[h_scratch],
            compiler_params=_cparams(2),
            name="ffn",
        )(x2d, norm_ffn, w_gate, w_up, w_down)

    l, pt_flat, q_sb, cache_sb = sb_args
    dec_b = q_sb.shape[0]
    page = cache_sb.shape[2]
    npg = pt_flat.shape[0] // dec_b
    pps = SB_PAGES_PER_STEP
    nc = npg // pps
    n_steps = dec_b * 2 * nc
    assert n_steps <= (m // tm) * nf
    width = page * H_SB
    up, upp = _sb_dec_consts(npg, page)

    def pos(i, j):
        s = jnp.minimum(i * nf + j, n_steps - 1)
        return s // (2 * nc), (s // nc) % 2, s % nc

    def page_spec(u):
        def imap(i, j, pt):
            b, ph, c = pos(i, j)
            return (l, pt[b * npg + c * pps + u], 0, ph, 0, 0)
        return pl.BlockSpec((None, None, page, None, H_SB, HEAD_DIM), imap)

    once = pl.Buffered(1)
    grid_spec = pltpu.PrefetchScalarGridSpec(
        num_scalar_prefetch=1,
        grid=(m // tm, nf),
        in_specs=in_specs
        + [pl.BlockSpec((None, ROWS_S, HEAD_DIM), lambda i, j, pt: (pos(i, j)[0], 0, 0))]
        + [page_spec(u) for u in range(pps)]
        + [pl.BlockSpec((width, 2 * width), lambda i, j, pt: (0, 0), pipeline_mode=once),
           pl.BlockSpec((npg, npg), lambda i, j, pt: (0, 0), pipeline_mode=once)],
        out_specs=[y_spec,
                   pl.BlockSpec((None, H_SB, HEAD_DIM), lambda i, j, pt: (pos(i, j)[0], 0, 0))],
        scratch_shapes=[h_scratch, pltpu.VMEM((npg, width), F32),
                        pltpu.VMEM((ROWS_S, HEAD_DIM), F32)],
    )
    sb = dict(pps=pps, page=page, n_chunks=nc, n_steps=n_steps)
    return pl.pallas_call(
        functools.partial(_ffn_body, sb=sb),
        grid_spec=grid_spec,
        out_shape=[y_shape, jax.ShapeDtypeStruct((dec_b, H_SB, HEAD_DIM), F32)],
        compiler_params=_cparams(2, VMEM_LIMIT_FUSED),
        name="ffn_sb_decode",
    )(pt_flat, x2d, norm_ffn, w_gate, w_up, w_down, q_sb, *([cache_sb] * pps), up, upp)


SB_SUB = 4


def _sb_body(*refs):
    n_sub = SB_HB // SB_SUB
    q_refs, k_refs, v_refs = refs[:n_sub], refs[n_sub:2 * n_sub], refs[2 * n_sub:3 * n_sub]
    u_ref, o_ref, acc_ref, run_ref = refs[3 * n_sub:]
    i = pl.program_id(2)
    row = lax.broadcasted_iota(I32, (SB_TQ, SB_TK), 0)
    col = lax.broadcasted_iota(I32, (SB_TQ, SB_TK), 1)
    strict = col < row

    def tile(kb, diagonal):
        off = pl.multiple_of(kb * SB_TK, SB_TK)
        heads = range(SB_HB)
        sls = [slice(h % SB_SUB * HEAD_DIM, (h % SB_SUB + 1) * HEAD_DIM) for h in heads]
        q_ref = [q_refs[h // SB_SUB] for h in heads]
        k_ref = [k_refs[h // SB_SUB] for h in heads]
        v_ref = [v_refs[h // SB_SUB] for h in heads]
        z = [_dot_nt(q_ref[h][:, sls[h]], k_ref[h][pl.ds(off, SB_TK), sls[h]]) * SCALE
             for h in heads]
        log_1m = [_neg_softplus(zh) for zh in z]
        if diagonal:
            log_1m = [jnp.where(strict, lh, 0.0) for lh in log_1m]
        parts = [_split_bf16(lh) for lh in log_1m]
        u = u_ref[...]
        later = [_dot(hi, u) + _dot(lo, u) for hi, lo in parts]
        total = [jnp.sum(lh, axis=-1, keepdims=True) for lh in log_1m]
        if diagonal:
            a = [jnp.where(strict, jnp.exp(z[h] + log_1m[h] + later[h]), 0.0) for h in heads]
        else:
            run = [run_ref[h] for h in heads]
            a = [jnp.exp(z[h] + log_1m[h] + later[h] + run[h]) for h in heads]
        pv = [_dot(a[h].astype(BF16), v_ref[h][pl.ds(off, SB_TK), sls[h]]) for h in heads]
        for h in heads:
            if diagonal:
                acc_ref[h] = pv[h]
                run_ref[h] = total[h]
            else:
                acc_ref[h] += pv[h]
                run_ref[h] = run[h] + total[h]

    tile(i, True)

    def step(it, carry):
        tile(i - 1 - it, False)
        return carry

    lax.fori_loop(0, i, step, 0)
    for h in range(SB_HB):
        o_ref[:, h * HEAD_DIM:(h + 1) * HEAD_DIM] = acc_ref[h]


def _sb_prompt(zb, batch, seq):
    nq = seq // SB_TQ
    width = SB_HB * HEAD_DIM
    sub_w = SB_SUB * HEAD_DIM
    n_sub = SB_HB // SB_SUB
    assert QS_BLK % SB_SUB == 0 and KSB_BLK % SB_SUB == 0 and VSB_BLK % SB_SUB == 0
    r = lax.broadcasted_iota(I32, (SB_TK, SB_TK), 0)
    c = lax.broadcasted_iota(I32, (SB_TK, SB_TK), 1)
    u = (r > c).astype(BF16)

    def q_spec(n):
        return pl.BlockSpec((SB_TQ, sub_w),
                            lambda b, h, i: (b * nq + i, QS_BLK // SB_SUB + h * n_sub + n))

    def kv_spec(blk, n):
        return pl.BlockSpec((seq, sub_w), lambda b, h, i: (b, blk // SB_SUB + h * n_sub + n))

    subs = range(n_sub)
    return pl.pallas_call(
        _sb_body,
        grid=(batch, H_SB // SB_HB, nq),
        in_specs=[q_spec(n) for n in subs] + [kv_spec(KSB_BLK, n) for n in subs]
        + [kv_spec(VSB_BLK, n) for n in subs]
        + [pl.BlockSpec((SB_TK, SB_TK), lambda b, h, i: (0, 0))],
        out_specs=pl.BlockSpec((SB_TQ, width), lambda b, h, i: (b * nq + i, h)),
        out_shape=jax.ShapeDtypeStruct((batch * seq, D_SB), F32),
        scratch_shapes=[pltpu.VMEM((SB_HB, SB_TQ, HEAD_DIM), F32), pltpu.VMEM((SB_HB, SB_TQ, 1), F32)],
        compiler_params=_cparams(3),
        name="sb_prompt",
    )(*([zb] * (3 * n_sub)), u)


def _cmp_prompt_body(x0_ref, x1_ref, x2_ref, x3_ref, wk_ref, wv_ref, bk_ref, bv_ref, gk_ref,
                     kc_ref, vct_ref, *, nb):
    pad = jnp.zeros((LANE - nb, HEAD_DIM), F32)
    for c, x_ref in enumerate((x0_ref, x1_ref, x2_ref, x3_ref)):
        rows = jnp.concatenate(
            [x_ref[pl.ds(l, nb, stride=CMP_BLOCK), :] for l in range(CMP_BLOCK)], axis=1)
        rows = rows.astype(BF16)
        if c < G_NSA:
            y = _rms(_dot(rows, wk_ref[...]) + bk_ref[...], gk_ref[...])
            kc_ref[c] = jnp.concatenate([y, pad], axis=0).astype(BF16)
        else:
            y = _dot(rows, wv_ref[...]) + bv_ref[...]
            vct_ref[c - G_NSA] = jnp.concatenate([y, pad], axis=0).T.astype(BF16)


def _cmp_prompt(nsa_rows, batch, seq, wk, wv, bk, bv, gk0):
    nb = seq // CMP_BLOCK
    assert nb % 8 == 0 and nb < LANE
    body = functools.partial(_cmp_prompt_body, nb=nb)
    shape = jax.ShapeDtypeStruct((batch, G_NSA, LANE, HEAD_DIM), BF16)

    def x_spec(c):
        return pl.BlockSpec((seq, HEAD_DIM), lambda b: (b, c))

    return pl.pallas_call(
        body,
        grid=(batch,),
        in_specs=[
            x_spec(0), x_spec(1), x_spec(2), x_spec(3),
            pl.BlockSpec(wk.shape, lambda b: (0, 0)),
            pl.BlockSpec(wv.shape, lambda b: (0, 0)),
            pl.BlockSpec((1, HEAD_DIM), lambda b: (0, 0)),
            pl.BlockSpec((1, HEAD_DIM), lambda b: (0, 0)),
            pl.BlockSpec((1, HEAD_DIM), lambda b: (0, 0)),
        ],
        out_specs=[pl.BlockSpec((None, G_NSA, LANE, HEAD_DIM), lambda b: (b, 0, 0, 0))] * 2,
        out_shape=[shape, shape],
        compiler_params=_cparams(1),
        name="cmp_prompt",
    )(nsa_rows, nsa_rows, nsa_rows, nsa_rows, wk, wv, bk, bv, gk0)


def _masked_softmax(s, valid):
    logit = jnp.where(valid, s, NEG_INF)
    e = jnp.exp(logit - jnp.max(logit, axis=-1, keepdims=True))
    return jnp.where(valid, e / jnp.sum(e, axis=-1, keepdims=True), 0.0)


class _Flash:
    def __init__(self, m_ref, l_ref, a_ref):
        self.m_ref, self.l_ref, self.a_ref = m_ref, l_ref, a_ref

    def first(self, logit, v):
        m = jnp.max(logit, axis=-1, keepdims=True)
        p = jnp.exp(logit - m)
        self.m_ref[...] = m
        self.l_ref[...] = jnp.sum(p, axis=-1, keepdims=True)
        self.a_ref[...] = _dot(p.astype(BF16), v)

    def update(self, logit, v):
        m_old = self.m_ref[...]
        m = jnp.maximum(m_old, jnp.max(logit, axis=-1, keepdims=True))
        alpha = jnp.exp(m_old - m)
        p = jnp.exp(logit - m)
        self.m_ref[...] = m
        self.l_ref[...] = alpha * self.l_ref[...] + jnp.sum(p, axis=-1, keepdims=True)
        self.a_ref[...] = alpha * self.a_ref[...] + _dot(p.astype(BF16), v)

    def result(self):
        return self.a_ref[...] / self.l_ref[...]


class _FlashT:
    def __init__(self, m_ref, l_ref, a_ref):
        self.m_ref, self.l_ref, self.a_ref = m_ref, l_ref, a_ref

    @staticmethod
    def step(states, logits, v_ts, first):
        ms = [jnp.max(lg, axis=0, keepdims=True) for lg in logits]
        if not first:
            olds = [st.m_ref[...] for st in states]
            ms = [jnp.maximum(o, m) for o, m in zip(olds, ms)]
            alphas = [jnp.exp(o - m) for o, m in zip(olds, ms)]
        ps = [jnp.exp(lg - m) for lg, m in zip(logits, ms)]
        sums = [jnp.sum(p, axis=0, keepdims=True) for p in ps]
        pvs = [_dot(v_t, p.astype(BF16)) for v_t, p in zip(v_ts, ps)]
        for n, st in enumerate(states):
            st.m_ref[...] = ms[n]
            if first:
                st.l_ref[...] = sums[n]
                st.a_ref[...] = pvs[n]
            else:
                st.l_ref[...] = alphas[n] * st.l_ref[...] + sums[n]
                st.a_ref[...] = alphas[n] * st.a_ref[...] + pvs[n]

    def update(self, logit, v_t):
        _FlashT.step([self], [logit], [v_t], False)

    def result(self):
        return self.a_ref[...] / self.l_ref[...]


def _nsa_body(rb_ref, q_ref, kc_ref, vct_ref, ks_ref, vs_ref, kw_ref, vw_ref, g_ref, toep_ref,
              bc_ref, et_ref, o_ref, vst_ref, vwt_ref, m_s, l_s, a_s, m_w, l_w, a_w, *, nb, seq):
    g = pl.program_id(1)
    i = pl.program_id(2)
    tq = NSA_TQ
    sub = tq // LANE
    width = HPG * tq
    q0 = i * tq

    @pl.when(i == 0)
    def _():
        for c in range(seq // LANE):
            rows = slice(c * LANE, (c + 1) * LANE)
            cols = slice((c % sub) * LANE, (c % sub + 1) * LANE)
            vst_ref[c // sub, :, cols] = vs_ref[rows, :].astype(F32).T.astype(BF16)
            vwt_ref[c // sub, :, cols] = vw_ref[rows, :].astype(F32).T.astype(BF16)

    qs = jnp.concatenate([q_ref[:, hh * LANE:(hh + 1) * LANE] for hh in range(HPG)], axis=0)

    def q_off(rows):
        return lax.broadcasted_iota(I32, (rows, width), 1) % tq

    n_r = lax.broadcasted_iota(I32, (LANE, width), 0)
    s_c = _dot_nt(kc_ref[...], qs) * SCALE + bc_ref[...]
    valid_c = (CMP_BLOCK * n_r + (CMP_BLOCK - 1) <= q0 + q_off(LANE)) & (n_r < nb)
    logit = jnp.where(valid_c, s_c, NEG_INF)
    e = jnp.exp(logit - jnp.max(logit, axis=0, keepdims=True))
    p_c = jnp.where(valid_c, e / jnp.sum(e, axis=0, keepdims=True), 0.0)
    oc_t = _dot(vct_ref[...], p_c.astype(BF16))
    imp_t = p_c[:, 0:tq]
    for hh in range(1, HPG):
        imp_t = imp_t + p_c[:, hh * tq:(hh + 1) * tq]

    nbr = -(-nb // 8) * 8
    n_i = lax.broadcasted_iota(I32, (nbr, tq), 0)
    cur = (q0 + lax.broadcasted_iota(I32, (nbr, tq), 1)) // CMP_BLOCK
    forced = (n_i == 0) | (n_i >= cur - (N_LOCAL - 1))
    score = jnp.where(n_i > cur, -jnp.inf, jnp.where(forced, FORCE_SCORE, imp_t[0:nbr]))
    rank = jnp.zeros((nbr, tq), I32)
    for j in range(nb):
        sj = jnp.broadcast_to(score[j:j + 1, :], (nbr, tq))
        ahead = (sj > score) | ((sj == score) & (n_i > j))
        rank = rank + ahead.astype(I32)
    sel_t = ((rank < min(TOP_N, nb)) & (score > -jnp.inf)).astype(F32)
    if nbr < LANE:
        sel_t = jnp.concatenate([sel_t, jnp.zeros((LANE - nbr, tq), F32)], axis=0)
    sel_t = sel_t.astype(BF16)

    k_r = lax.broadcasted_iota(I32, (tq, width), 0)
    q_c = q_off(tq)
    causal = k_r <= q_c
    bias_far = jnp.concatenate(
        [jnp.full((1, tq), rb_ref[N_BUCKETS - 1, g * HPG + hh], F32) for hh in range(HPG)], axis=1)

    def chunk(ref, c):
        return ref[pl.ds(pl.multiple_of(c * tq, tq), tq), :]

    def scores(ref, c):
        return _dot_nt(chunk(ref, c), qs) * SCALE

    def selected(c):
        hit = _dot(chunk(et_ref, c), sel_t)
        return jnp.concatenate([hit] * HPG, axis=1) > 0.5

    fs = _FlashT(m_s, l_s, a_s)
    fw = _FlashT(m_w, l_w, a_w)

    def both(c, bias, sel_extra, win_mask, first):
        sel_mask = selected(c) if sel_extra is None else selected(c) & sel_extra
        lg_s = jnp.where(sel_mask, scores(ks_ref, c) + bias, NEG_INF)
        lg_w = scores(kw_ref, c) + bias
        if win_mask is not None:
            lg_w = jnp.where(win_mask, lg_w, NEG_INF)
        _FlashT.step([fs, fw], [lg_s, lg_w], [vst_ref[c], vwt_ref[c]], first)

    both(i, toep_ref[1], causal, causal, True)

    @pl.when(i >= 1)
    def _():
        both(i - 1, toep_ref[0], None, None, False)

    @pl.when(i >= 2)
    def _():
        both(i - 2, bias_far, None, k_r >= q_c, False)

    def far_step(c, carry):
        fs.update(jnp.where(selected(c), scores(ks_ref, c) + bias_far, NEG_INF), vst_ref[c])
        return carry

    lax.fori_loop(0, jnp.maximum(i - 2, 0), far_step, 0)

    g_t = jnp.concatenate([g_ref[t * LANE:(t + 1) * LANE, :].T for t in range(sub)], axis=1)

    def gate(branch):
        return jnp.concatenate(
            [g_t[branch * HPG + hh:branch * HPG + hh + 1, :] for hh in range(HPG)], axis=1)

    o_t = gate(0) * oc_t + gate(1) * fs.result() + gate(2) * fw.result()
    for hh in range(HPG):
        for t in range(sub):
            lanes = slice(hh * tq + t * LANE, hh * tq + (t + 1) * LANE)
            o_ref[t * LANE:(t + 1) * LANE, hh * LANE:(hh + 1) * LANE] = o_t[:, lanes].T


def _nsa_prompt(rel_bias, zb, kc, vct, gates, toep, bias_c, batch, seq):
    tq = NSA_TQ
    assert WINDOW == 2 * tq and tq % LANE == 0 and seq % tq == 0
    nq = seq // tq
    nb = seq // CMP_BLOCK
    assert nb <= LANE
    width = HPG * tq
    k = lax.broadcasted_iota(I32, (seq, LANE), 0)
    n = lax.broadcasted_iota(I32, (seq, LANE), 1)
    expand_t = (k // CMP_BLOCK == n).astype(BF16)
    body = functools.partial(_nsa_body, nb=nb, seq=seq)

    def kv_spec(blk):
        return pl.BlockSpec((seq, HEAD_DIM), lambda b, g, i: (b, blk + g))

    stat = pltpu.VMEM((1, width), F32)
    accum = pltpu.VMEM((HEAD_DIM, width), F32)
    v_t = pltpu.VMEM((nq, HEAD_DIM, tq), BF16)
    return pl.pallas_call(
        body,
        grid=(batch, G_NSA, nq),
        in_specs=[
            pl.BlockSpec(memory_space=pltpu.SMEM),
            pl.BlockSpec((tq, HPG * HEAD_DIM), lambda b, g, i: (b * nq + i, g)),
            pl.BlockSpec((None, None, LANE, HEAD_DIM), lambda b, g, i: (b, g, 0, 0)),
            pl.BlockSpec((None, None, HEAD_DIM, LANE), lambda b, g, i: (b, g, 0, 0)),
            kv_spec(KS_BLK), kv_spec(VS_BLK), kv_spec(KW_BLK), kv_spec(VW_BLK),
            pl.BlockSpec((tq, LANE), lambda b, g, i: (b * nq + i, g)),
            pl.BlockSpec((2, tq, width), lambda b, g, i: (0, 0, g)),
            pl.BlockSpec((None, LANE, width), lambda b, g, i: (i, 0, g)),
            pl.BlockSpec((seq, LANE), lambda b, g, i: (0, 0)),
        ],
        out_specs=pl.BlockSpec((tq, HPG * HEAD_DIM), lambda b, g, i: (b * nq + i, g)),
        out_shape=jax.ShapeDtypeStruct((batch * seq, D_NSA), F32),
        scratch_shapes=[v_t, v_t, stat, stat, accum, stat, stat, accum],
        compiler_params=_cparams(3),
        name="nsa_prompt",
    )(rel_bias, zb, kc, vct, zb, zb, zb, zb, gates, toep, bias_c, expand_t)


def _cast_body(w_ref, o_ref):
    o_ref[...] = w_ref[...].astype(BF16)


def _cast_layer_bf16(w, l, cols=None):
    _, rows, n = w.shape
    cols = n if cols is None else cols
    assert cols % LANE == 0
    tr = _pick_tile(rows, (256, 128))
    return pl.pallas_call(
        _cast_body,
        grid=(rows // tr,),
        in_specs=[pl.BlockSpec((None, tr, cols), lambda i: (l, i, 0))],
        out_specs=pl.BlockSpec((tr, cols), lambda i: (i, 0)),
        out_shape=jax.ShapeDtypeStruct((rows, cols), BF16),
        compiler_params=_cparams(1),
        name="cast_bf16",
    )(w)


def _prep_weights(l, w_in, g_q, g_k, w_cmp_k, w_cmp_v, w_out, w_gate, w_up, w_down):
    n_gate = 3 * H_NSA
    g0 = D_NSA + 6 * D_KV
    wi = w_in[l]
    w_a = wi[:, :g0].astype(BF16)
    w_b = wi[:, g0 + n_gate:].astype(BF16)
    wg = wi[:, g0:g0 + n_gate].reshape(D_MODEL, 3, G_NSA, HPG)
    wg = jnp.transpose(wg, (0, 2, 1, 3)).reshape(D_MODEL, G_NSA, 3 * HPG)
    wg = jnp.pad(wg, ((0, 0), (0, 0), (0, LANE - 3 * HPG))).reshape(D_MODEL, G_NSA * LANE)
    ones = jnp.ones((HEAD_DIM,), F32)
    gain = jnp.concatenate(
        [jnp.tile(g_q[l], H_NSA), jnp.tile(ones, 2 * G_NSA),
         jnp.tile(g_k[l, 1], G_NSA), jnp.tile(ones, G_NSA),
         jnp.tile(g_k[l, 2], G_NSA), jnp.tile(ones, G_NSA),
         jnp.tile(ones, 3 * H_SB)])[None, :]
    wk = w_cmp_k[l].reshape(CMP_BLOCK * HEAD_DIM, HEAD_DIM).astype(BF16)
    wv = w_cmp_v[l].reshape(CMP_BLOCK * HEAD_DIM, HEAD_DIM).astype(BF16)
    return dict(
        w_a=w_a, w_b=w_b, w_gate=wg.astype(BF16), gain=gain, wk=wk, wv=wv,
        w_kv=jnp.concatenate([wk, wv], axis=1),
        w_out=_cast_layer_bf16(w_out, l), w_gate_ffn=_cast_layer_bf16(w_gate, l),
        w_up=_cast_layer_bf16(w_up, l), w_down=_cast_layer_bf16(w_down, l))


def _prompt_layer(x2d, wt, p, tables, rel_bias, batch, seq, sb_args):
    na, bk, bv, gk0, go, nf = p
    zb, sb_rows, nsa_rows, win_rows, gates = _inproj(
        x2d, na, wt["w_a"], wt["w_b"], wt["w_gate"], wt["gain"], seq, min(WINDOW, seq))
    kc, vct = _cmp_prompt(nsa_rows, batch, seq, wt["wk"], wt["wv"], bk, bv, gk0)
    o_nsa = _nsa_prompt(rel_bias, zb, kc, vct, gates, tables[0], tables[1], batch, seq)
    o_sb = _sb_prompt(zb, batch, seq)
    x1 = _outproj(o_nsa, o_sb, go, wt["w_out"], x2d)
    x2, o_sb_sample = _ffn(x1, nf, wt["w_gate_ffn"], wt["w_up"], wt["w_down"], sb_args)
    return x2, sb_rows, nsa_rows, win_rows, o_sb_sample


ROWS_S = 16


def _sb_dec_step(ph, c, last_c, q_ref, page_refs, up_ref, upp_ref, o_ref, z_ref, acc_ref, page):
    pps = len(page_refs)
    width = page * H_SB
    r = lax.broadcasted_iota(I32, (ROWS_S, width), 0)
    lane = lax.broadcasted_iota(I32, (ROWS_S, width), 1)
    own_head = (lane % H_SB) == r

    @pl.when(ph == 0)
    def _():
        q = q_ref[...]
        for u in range(pps):
            ks = page_refs[u][...].reshape(width, HEAD_DIM).astype(BF16)
            zt = _dot_nt(q, ks)
            z_ref[pl.ds(c * pps + u, 1), :] = jnp.sum(jnp.where(own_head, zt, 0.0), axis=0,
                                                      keepdims=True)

    @pl.when((ph == 1) & (c == 0))
    def _():
        z = z_ref[...] * SCALE
        log_1m = _neg_softplus(z)
        cs = _dot_split(log_1m, up_ref[...])
        hi, lo = _split_bf16(cs[:, width:])
        later_pages = _dot(upp_ref[...], hi) + _dot(upp_ref[...], lo)
        z_ref[...] = jnp.exp(z + log_1m + cs[:, :width] + later_pages)
        acc_ref[...] = jnp.zeros_like(acc_ref)

    @pl.when(ph == 1)
    def _():
        for u in range(pps):
            vs = page_refs[u][...].reshape(width, HEAD_DIM).astype(BF16)
            a = jnp.broadcast_to(z_ref[pl.ds(c * pps + u, 1), :], (ROWS_S, width))
            acc_ref[...] += _dot(jnp.where(own_head, a, 0.0).astype(BF16), vs)

    @pl.when((ph == 1) & (c == last_c))
    def _():
        o_ref[...] = acc_ref[0:H_SB, :]


def _sb_dec_consts(npg, page):
    assert npg % SB_PAGES_PER_STEP == 0 and npg % 8 == 0
    width = page * H_SB
    r = lax.broadcasted_iota(I32, (width, 2 * width), 0)
    c = lax.broadcasted_iota(I32, (width, 2 * width), 1)
    same_head = (r % H_SB) == (c % H_SB)
    up = (same_head & ((c >= width) | (r // H_SB > c // H_SB))).astype(BF16)
    pr = lax.broadcasted_iota(I32, (npg, npg), 0)
    pc = lax.broadcasted_iota(I32, (npg, npg), 1)
    upp = (pc > pr).astype(BF16)
    return up, upp


NSA_CH = 4 * G_NSA


def _cmp_dec_body(pt_ref, *refs, pps, page):
    page_refs = refs[:pps]
    w_ref, kc_ref, vc_ref, yk_ref, yv_ref = refs[pps:]
    halves = page // CMP_BLOCK
    nblk = pps * halves
    acc = jnp.zeros((nblk * NSA_CH, 2 * HEAD_DIM), F32)

    def token_rows(u, hf, l):
        return page_refs[u][pl.ds((hf * CMP_BLOCK + l) * NSA_CH, NSA_CH), :]

    for l in range(0, CMP_BLOCK, 2):
        x_l = jnp.concatenate(
            [jnp.concatenate([token_rows(u, hf, l), token_rows(u, hf, l + 1)], axis=1)
             for u in range(pps) for hf in range(halves)], axis=0).astype(BF16)
        acc = acc + _dot(x_l, w_ref[l * HEAD_DIM:(l + 2) * HEAD_DIM, :])
    yk_ref[...] = acc[:, :HEAD_DIM]
    yv_ref[...] = acc[:, HEAD_DIM:]
    for g in range(G_NSA):
        kc_ref[g] = yk_ref[pl.ds(g, nblk, stride=NSA_CH), :]
        vc_ref[g] = yv_ref[pl.ds(G_NSA + g, nblk, stride=NSA_CH), :]


def _cmp_decode(l, pt_flat, cache_nsa2d, w_kv, dec_b, page):
    npg = pt_flat.shape[0] // dec_b
    pps = min(CMP_PAGES_PER_STEP, npg)
    assert npg % pps == 0
    halves = page // CMP_BLOCK
    nblk = pps * halves
    body = functools.partial(_cmp_dec_body, pps=pps, page=page)

    def page_spec(u):
        return pl.BlockSpec((None, None, page * NSA_CH, HEAD_DIM),
                            lambda b, c, pt: (l, pt[b * npg + c * pps + u], 0, 0))

    out_spec = pl.BlockSpec((None, G_NSA, nblk, HEAD_DIM), lambda b, c, pt: (b, 0, c, 0))
    shape = jax.ShapeDtypeStruct((dec_b, G_NSA, npg * halves, HEAD_DIM), F32)
    grid_spec = pltpu.PrefetchScalarGridSpec(
        num_scalar_prefetch=1,
        grid=(dec_b, npg // pps),
        in_specs=[page_spec(u) for u in range(pps)]
        + [pl.BlockSpec(w_kv.shape, lambda b, c, pt: (0, 0))],
        out_specs=[out_spec, out_spec],
        scratch_shapes=[pltpu.VMEM((nblk * NSA_CH, HEAD_DIM), F32)] * 2,
    )
    return pl.pallas_call(
        body,
        grid_spec=grid_spec,
        out_shape=[shape, shape],
        compiler_params=_cparams(2),
        name="cmp_decode",
    )(pt_flat, *([cache_nsa2d] * pps), w_kv)


def _nsa_sel_body(q_ref, kcr_ref, vcr_ref, new_ref, wk0_ref, wv0_ref, bk_ref, bv_ref, gk_ref,
                  sbc_ref, kw_ref, vw_ref, sbw_ref, idx_ref, oc_ref, ow_ref,
                  *, nb0, nbp, past_len, w_buf, wp):
    nb = nb0 + 1
    new = new_ref[...].astype(BF16)
    new_k = _rms(_dot(new, wk0_ref[...]) + bk_ref[...], gk_ref[...])
    new_v = _dot(new, wv0_ref[...]) + bv_ref[...]
    row8 = lax.broadcasted_iota(I32, (8, HEAD_DIM), 0)
    n1 = lax.broadcasted_iota(I32, (1, nbp), 1)
    nq = lax.broadcasted_iota(I32, (ROWS_S, nbp), 1)
    pos_q = past_len
    pad = jnp.zeros((nbp - nb0 - 8, HEAD_DIM), F32)
    for g in range(G_NSA):
        q = q_ref[g]
        kc = jnp.concatenate(
            [_rms(kcr_ref[g] + bk_ref[...], gk_ref[...]),
             jnp.where(row8 == 0, jnp.broadcast_to(new_k[g:g + 1], (8, HEAD_DIM)), 0.0), pad],
            axis=0).astype(BF16)
        vc = jnp.concatenate(
            [vcr_ref[g] + bv_ref[...],
             jnp.where(row8 == 0, jnp.broadcast_to(new_v[G_NSA + g:G_NSA + g + 1], (8, HEAD_DIM)),
                       0.0), pad], axis=0).astype(BF16)
        s_c = _dot_nt(q, kc) * SCALE + sbc_ref[g]
        valid_c = (CMP_BLOCK * nq + (CMP_BLOCK - 1) <= pos_q) & (nq < nb)
        p_c = _masked_softmax(s_c, valid_c)
        oc_ref[g] = _dot(p_c.astype(BF16), vc)
        imp = p_c[0:1]
        for hh in range(1, HPG):
            imp = imp + p_c[hh:hh + 1]

        cur = pos_q // CMP_BLOCK
        forced = (n1 == 0) | (n1 >= cur - (N_LOCAL - 1))
        score = jnp.where(n1 > cur, -jnp.inf, jnp.where(forced, FORCE_SCORE, imp))
        sq = jnp.broadcast_to(score, (LANE, nbp))
        col = jnp.concatenate([sq[:, k * LANE:(k + 1) * LANE].T for k in range(nbp // LANE)],
                              axis=0)
        j_i = lax.broadcasted_iota(I32, (nbp, LANE), 0)
        ranks = []
        for k in range(nbp // LANE):
            mine = jnp.broadcast_to(score[:, k * LANE:(k + 1) * LANE], (nbp, LANE))
            n_i = k * LANE + lax.broadcasted_iota(I32, (nbp, LANE), 1)
            ahead = (col > mine) | ((col == mine) & (j_i < n_i))
            ranks.append(jnp.sum(ahead.astype(F32), axis=0, keepdims=True))
        rank = jnp.concatenate(ranks, axis=1)
        slot = lax.broadcasted_iota(I32, (TOP_N, nbp), 0).astype(F32)
        hit = jnp.broadcast_to(rank, (TOP_N, nbp)) == slot
        ids = jnp.sum(jnp.where(hit, lax.broadcasted_iota(I32, (TOP_N, nbp), 1).astype(F32), 0.0),
                      axis=1, keepdims=True)
        idx_ref[g] = jnp.broadcast_to(ids, (TOP_N, LANE)).astype(I32)

        jw = lax.broadcasted_iota(I32, (ROWS_S, wp), 1)
        dist_w = w_buf - jw
        valid_w = (dist_w >= 0) & (dist_w <= WINDOW) & (past_len - w_buf + jw >= 0)
        s_w = _dot_nt(q, kw_ref[g]) * SCALE + sbw_ref[g]
        p_w = _masked_softmax(s_w, valid_w)
        ow_ref[g] = _dot(p_w.astype(BF16), vw_ref[g])


def _nsa_select(q_nsa, kcr, vcr, new_c, wk0, wv0, bk, bv, gk0, sbc, kw, vw, sbw, past_len, w_buf):
    dec_b, _, nb0, _ = kcr.shape
    nbp = sbc.shape[-1]
    wp = sbw.shape[-1]
    assert nb0 + 1 >= TOP_N and nbp >= nb0 + 8 and nb0 % 8 == 0
    body = functools.partial(_nsa_sel_body, nb0=nb0, nbp=nbp, past_len=past_len, w_buf=w_buf, wp=wp)

    def per_b(shape):
        nd = len(shape)
        return pl.BlockSpec((None,) + tuple(shape), lambda b: (b,) + (0,) * nd)

    def whole(arr):
        nd = arr.ndim
        return pl.BlockSpec(arr.shape, lambda b: (0,) * nd)

    vec = (G_NSA, ROWS_S, HEAD_DIM)
    return pl.pallas_call(
        body,
        grid=(dec_b,),
        in_specs=[per_b(vec), per_b((G_NSA, nb0, HEAD_DIM)), per_b((G_NSA, nb0, HEAD_DIM)),
                  per_b((8, HEAD_DIM)), whole(wk0), whole(wv0), whole(bk), whole(bv), whole(gk0),
                  whole(sbc), per_b((G_NSA, wp, HEAD_DIM)), per_b((G_NSA, wp, HEAD_DIM)), whole(sbw)],
        out_specs=[per_b((G_NSA, TOP_N, LANE)), per_b(vec), per_b(vec)],
        out_shape=[jax.ShapeDtypeStruct((dec_b, G_NSA, TOP_N, LANE), I32),
                   jax.ShapeDtypeStruct((dec_b,) + vec, F32),
                   jax.ShapeDtypeStruct((dec_b,) + vec, F32)],
        compiler_params=_cparams(1),
        name="nsa_select",
    )(q_nsa, kcr, vcr, new_c, wk0, wv0, bk, bv, gk0, sbc, kw, vw, sbw)


GATHER_BLOCKS = 8


def _nsa_gather_body(pt_ref, idx_ref, rb_ref, q_ref, *refs, nb0, past_len):
    n_pages = G_NSA * GATHER_BLOCKS
    page_refs = refs[:n_pages]
    new_ref, gs_ref, oc_ref, ow_ref, o_ref, m_ref, l_ref, a_ref = refs[n_pages:]
    b = pl.program_id(0)
    k = pl.program_id(1)
    pos_q = past_len
    keys = GATHER_BLOCKS * CMP_BLOCK
    row = lax.broadcasted_iota(I32, (CMP_BLOCK, HEAD_DIM), 0)
    l1 = lax.broadcasted_iota(I32, (1, CMP_BLOCK), 1)
    hrow = lax.broadcasted_iota(I32, (ROWS_S, keys), 0)
    for g in range(G_NSA):
        new_k = jnp.broadcast_to(new_ref[g:g + 1, :], (CMP_BLOCK, HEAD_DIM))
        new_v = jnp.broadcast_to(new_ref[G_NSA + g:G_NSA + g + 1, :], (CMP_BLOCK, HEAD_DIM))
        ks, vs, dist = [], [], []
        for kk in range(GATHER_BLOCKS):
            blk = idx_ref[(b * G_NSA + g) * TOP_N + k * GATHER_BLOCKS + kk]
            is_new = blk >= nb0
            page_ref = page_refs[g * GATHER_BLOCKS + kk]
            k_blk = page_ref[pl.ds(2 * G_NSA + g, CMP_BLOCK, stride=NSA_CH), :]
            v_blk = page_ref[pl.ds(3 * G_NSA + g, CMP_BLOCK, stride=NSA_CH), :]
            ks.append(jnp.where(is_new, jnp.where(row == 0, new_k, 0.0), k_blk).astype(BF16))
            vs.append(jnp.where(is_new, jnp.where(row == 0, new_v, 0.0), v_blk).astype(BF16))
            dist.append(pos_q - (blk * CMP_BLOCK + l1))
        ks = jnp.concatenate(ks, axis=0)
        vs = jnp.concatenate(vs, axis=0)
        dist = jnp.concatenate(dist, axis=1)
        bucket = jnp.broadcast_to(_t5_bucket(dist), (ROWS_S, keys))
        bias = jnp.zeros((ROWS_S, keys), F32)
        for hh in range(HPG):
            for bk in range(N_BUCKETS):
                bias = jnp.where((hrow == hh) & (bucket == bk), rb_ref[bk, g * HPG + hh], bias)
        s = _dot_nt(q_ref[g], ks) * SCALE + bias
        logit = jnp.where(jnp.broadcast_to(dist >= 0, (ROWS_S, keys)), s, NEG_INF)
        fl = _Flash(m_ref.at[g], l_ref.at[g], a_ref.at[g])

        @pl.when(k == 0)
        def _():
            fl.first(logit, vs)

        @pl.when(k > 0)
        def _():
            fl.update(logit, vs)

        @pl.when(k == pl.num_programs(1) - 1)
        def _():
            o_ref[g] = gs_ref[g, 0] * oc_ref[g] + gs_ref[g, 1] * fl.result() + gs_ref[g, 2] * ow_ref[g]


def _nsa_gather(l, pt_flat, idx_flat, rel_bias, q_nsa, cache_nsa2d, new_s, gs, oc, ow, nb0,
                past_len, page):
    dec_b = q_nsa.shape[0]
    npg = pt_flat.shape[0] // dec_b
    halves = page // CMP_BLOCK
    body = functools.partial(_nsa_gather_body, nb0=nb0, past_len=past_len)

    def page_spec(g, kk):
        def imap(b, k, pt, idx):
            blk = jnp.minimum(idx[(b * G_NSA + g) * TOP_N + k * GATHER_BLOCKS + kk], nb0 - 1)
            return (l, pt[b * npg + blk // halves], blk % halves, 0)
        return pl.BlockSpec((None, None, CMP_BLOCK * NSA_CH, HEAD_DIM), imap)

    def per_b(shape):
        nd = len(shape)
        return pl.BlockSpec((None,) + tuple(shape), lambda b, k, pt, idx: (b,) + (0,) * nd)

    vec = (G_NSA, ROWS_S, HEAD_DIM)
    assert TOP_N % GATHER_BLOCKS == 0
    n_pages = G_NSA * GATHER_BLOCKS
    grid_spec = pltpu.PrefetchScalarGridSpec(
        num_scalar_prefetch=2,
        grid=(dec_b, TOP_N // GATHER_BLOCKS),
        in_specs=[pl.BlockSpec(memory_space=pltpu.SMEM), per_b(vec)]
        + [page_spec(g, kk) for g in range(G_NSA) for kk in range(GATHER_BLOCKS)]
        + [per_b((8, HEAD_DIM)), per_b((G_NSA, 3, ROWS_S, HEAD_DIM)), per_b(vec), per_b(vec)],
        out_specs=per_b(vec),
        scratch_shapes=[pltpu.VMEM((G_NSA, ROWS_S, 1), F32), pltpu.VMEM((G_NSA, ROWS_S, 1), F32),
                        pltpu.VMEM((G_NSA, ROWS_S, HEAD_DIM), F32)],
    )
    return pl.pallas_call(
        body,
        grid_spec=grid_spec,
        out_shape=jax.ShapeDtypeStruct((dec_b,) + vec, F32),
        compiler_params=_cparams(2),
        name="nsa_gather",
    )(pt_flat, idx_flat, rel_bias, q_nsa, *([cache_nsa2d] * n_pages), new_s, gs, oc, ow)


def _pad_rows(x, rows):
    return jnp.pad(x, ((0, rows - x.shape[0]),) + ((0, 0),) * (x.ndim - 1))


def _sample_inproj(xs, wt, p, dec_b):
    rows = xs.shape[0]
    zb, sb_rows, nsa_rows, win_rows, gates = _inproj(
        xs, p[0], wt["w_a"], wt["w_b"], wt["w_gate"], wt["gain"], rows, rows)
    zb8 = zb[:dec_b]
    q_sb = zb8[:, QS_BLK * LANE:KSB_BLK * LANE].reshape(dec_b, H_SB, HEAD_DIM)
    q_sb = jnp.pad(q_sb, ((0, 0), (0, ROWS_S - H_SB), (0, 0)))
    return q_sb, (zb8, sb_rows[:dec_b], nsa_rows[:dec_b], win_rows[:dec_b], gates)


def _sample_layer(l, xs, projected, o_sb, wt, p, sb_tabs, rel_bias, pt_flat, cache_nsa2d, win_l,
                  w_cmp0, dec_b, past_len, page):
    na, bk, bv, gk0, go, nf = p
    sbc, sbw = sb_tabs
    w_buf = win_l.shape[1]
    rows = xs.shape[0]
    zb8, sb_rows, nsa_rows, win_rows, gates = projected
    o_sb = o_sb.reshape(dec_b, D_SB)

    q_nsa = zb8[:, :D_NSA].reshape(dec_b, G_NSA, HPG, HEAD_DIM)
    q_nsa = jnp.pad(q_nsa, ((0, 0), (0, 0), (0, ROWS_S - HPG), (0, 0)))
    kcr, vcr = _cmp_decode(l, pt_flat, cache_nsa2d, wt["w_kv"], dec_b, page)
    new_c = jnp.pad(nsa_rows[:, :2 * D_KV].reshape(dec_b, 2 * G_NSA, HEAD_DIM),
                    ((0, 0), (0, 8 - 2 * G_NSA), (0, 0)))
    new_s = jnp.pad(nsa_rows[:, 2 * D_KV:].reshape(dec_b, 2 * G_NSA, HEAD_DIM),
                    ((0, 0), (0, 8 - 2 * G_NSA), (0, 0)))
    new_w = win_rows.reshape(dec_b, 1, 2, G_NSA, HEAD_DIM)
    win_all = jnp.concatenate([win_l, new_w], axis=1)
    wp = sbw.shape[-1]
    win_t = jnp.transpose(win_all, (2, 0, 3, 1, 4))
    win_t = jnp.pad(win_t, ((0, 0), (0, 0), (0, 0), (0, wp - w_buf - 1), (0, 0))).astype(BF16)
    idx, oc, ow = _nsa_select(q_nsa, kcr, vcr, new_c, w_cmp0[0], w_cmp0[1], bk, bv, gk0, sbc,
                              win_t[0], win_t[1], sbw, past_len, w_buf)
    gs = gates[:dec_b].reshape(dec_b, G_NSA, LANE)[:, :, :3 * HPG].reshape(dec_b, G_NSA, 3, HPG)
    gs = jnp.pad(gs, ((0, 0), (0, 0), (0, 0), (0, ROWS_S - HPG)))
    gs = jnp.broadcast_to(gs[..., None], gs.shape + (HEAD_DIM,))
    o_nsa = _nsa_gather(l, pt_flat, idx[:, :, :, 0].reshape(-1), rel_bias, q_nsa, cache_nsa2d,
                        new_s, gs, oc, ow, kcr.shape[2], past_len, page)
    o_nsa = o_nsa[:, :, :HPG].reshape(dec_b, D_NSA)

    x1 = _outproj(_pad_rows(o_nsa, rows), _pad_rows(o_sb, rows), go, wt["w_out"], xs)
    x2 = _ffn(x1, nf, wt["w_gate_ffn"], wt["w_up"], wt["w_down"])
    new_win = win_all[:, 1:]
    return x2, sb_rows, nsa_rows, new_win


def kernel(x_prompt, x_sample, cache_sb_kv, cache_nsa_kv, state_win_kv, page_table, rel_bias,
           norm_attn, w_in, g_q, g_k, w_cmp_k, b_cmp_k, w_cmp_v, b_cmp_v, g_out, w_out,
           norm_ffn, w_gate, w_up, w_down):
    batch, seq, _ = x_prompt.shape
    dec_b, dec_t, _ = x_sample.shape
    assert dec_t == 1
    depth, n_pool, page = cache_sb_kv.shape[:3]
    npg = page_table.shape[1]
    past_len = npg * page
    w_buf = state_win_kv.shape[2]
    nb_s = past_len // CMP_BLOCK + 1
    nbp_s = -(-(nb_s + 7) // LANE) * LANE
    wp_s = -(-(w_buf + 1) // LANE) * LANE

    toep, bias_c, sc, sw = _bias_tables(rel_bias, seq, past_len, w_buf, nbp_s, wp_s)
    sbc = jnp.pad(sc.reshape(G_NSA, HPG, nbp_s), ((0, 0), (0, ROWS_S - HPG), (0, 0)))
    sbw = jnp.pad(sw.reshape(G_NSA, HPG, wp_s), ((0, 0), (0, ROWS_S - HPG), (0, 0)))
    pt_flat = page_table.reshape(-1).astype(I32)
    cache_nsa2d = cache_nsa_kv.reshape(depth, n_pool, page * NSA_CH, HEAD_DIM)

    xp = x_prompt.reshape(batch * seq, D_MODEL)
    xs = _pad_rows(x_sample.reshape(dec_b, D_MODEL), ROWS_S)
    sb_p, sb_s, nsa_p, nsa_s, win_p, win_s = [], [], [], [], [], []
    w_keep = min(WINDOW, seq)
    for l in range(depth):
        wt = _prep_weights(l, w_in, g_q, g_k, w_cmp_k, w_cmp_v, w_out, w_gate, w_up, w_down)
        p = (norm_attn[l][None], b_cmp_k[l][None], b_cmp_v[l][None], g_k[l, 0][None],
             g_out[l][None], norm_ffn[l][None])
        q_sb, projected = _sample_inproj(xs, wt, p, dec_b)
        xp, sb_rows, nsa_rows, win_rows, o_sb = _prompt_layer(
            xp, wt, p, (toep, bias_c), rel_bias, batch, seq, (l, pt_flat, q_sb, cache_sb_kv))
        sb_p.append(sb_rows.reshape(batch, seq, 2, H_SB, HEAD_DIM))
        nsa_p.append(nsa_rows.reshape(batch, seq, 4, G_NSA, HEAD_DIM))
        win_p.append(win_rows.reshape(batch, w_keep, 2, G_NSA, HEAD_DIM))

        w_cmp0 = (w_cmp_k[l, 0].astype(BF16), w_cmp_v[l, 0].astype(BF16))
        xs, sb_rows, nsa_rows, new_win = _sample_layer(
            l, xs, projected, o_sb, wt, p, (sbc, sbw), rel_bias, pt_flat, cache_nsa2d,
            state_win_kv[l], w_cmp0, dec_b, past_len, page)
        sb_s.append(sb_rows.reshape(dec_b, 1, 2, H_SB, HEAD_DIM))
        nsa_s.append(nsa_rows.reshape(dec_b, 1, 4, G_NSA, HEAD_DIM))
        win_s.append(new_win)
    return (xp.reshape(batch, seq, D_MODEL), xs[:dec_b].reshape(dec_b, 1, D_MODEL),
            jnp.stack(sb_p), jnp.stack(sb_s), jnp.stack(nsa_p), jnp.stack(nsa_s),
            jnp.stack(win_p), jnp.stack(win_s))
```

```python
import functools
import math

import jax
import jax.numpy as jnp
from jax import lax
from jax.experimental import pallas as pl
from jax.experimental.pallas import tpu as pltpu

F32 = jnp.float32
BF16 = jnp.bfloat16
I32 = jnp.int32

LANE = 128
HEAD_DIM = 128
G_NSA = 2
HPG = 4
H_NSA = G_NSA * HPG
H_SB = 8
D_NSA = H_NSA * HEAD_DIM
D_SB = H_SB * HEAD_DIM
D_MODEL = D_NSA + D_SB
D_KV = G_NSA * HEAD_DIM
CMP_BLOCK = 64
TOP_N = 16
N_LOCAL = 2
WINDOW = 512
N_BUCKETS = 32
MAX_DISTANCE = 128
EPS = 1e-6
FORCE_SCORE = 1e4
NEG_INF = -1e30
SCALE = HEAD_DIM ** -0.5

COL_TILE = 512
N_MAIN = D_NSA + 6 * D_KV + 3 * D_SB
QS_BLK = (D_NSA + 6 * D_KV) // LANE
KSB_BLK = QS_BLK + D_SB // LANE
VSB_BLK = KSB_BLK + D_SB // LANE
KS_BLK = (D_NSA + 2 * D_KV) // LANE
VS_BLK = KS_BLK + G_NSA
KW_BLK = VS_BLK + G_NSA
VW_BLK = KW_BLK + G_NSA
VMEM_LIMIT = 56 * 1024 * 1024

SB_TQ = 256
SB_TK = 256
SB_HB = 8
NSA_TQ = 256
SB_PAGES_PER_STEP = 16
CMP_PAGES_PER_STEP = 16


VMEM_LIMIT_FUSED = 62 * 1024 * 1024


def _cparams(n_axes, vmem_limit=VMEM_LIMIT):
    return pltpu.CompilerParams(dimension_semantics=("arbitrary",) * n_axes,
                                vmem_limit_bytes=vmem_limit)


def _dot(a, b):
    return jnp.dot(a, b, preferred_element_type=F32)


def _dot_nt(a, b):
    return lax.dot_general(a, b, (((1,), (1,)), ((), ())), preferred_element_type=F32)


def _split_bf16(x):
    hi = x.astype(BF16)
    lo = (x - hi.astype(F32)).astype(BF16)
    return hi, lo


def _dot_split(x, u):
    hi, lo = _split_bf16(x)
    return _dot(hi, u) + _dot(lo, u)


def _rms(y, gain):
    ms = jnp.mean(y * y, axis=-1, keepdims=True)
    return y * lax.rsqrt(ms + EPS) * gain


def _neg_softplus(z):
    return -(jnp.maximum(z, 0.0) + jnp.log(1.0 + jnp.exp(-jnp.abs(z))))


def _t5_bucket(dist):
    n = jnp.maximum(dist, 0)
    max_exact = N_BUCKETS // 2
    nf = jnp.maximum(n, 1).astype(F32)
    large = max_exact + (jnp.log(nf / max_exact) / math.log(MAX_DISTANCE / max_exact)
                         * (N_BUCKETS - max_exact)).astype(I32)
    large = jnp.minimum(large, N_BUCKETS - 1)
    return jnp.where(n < max_exact, n, large)


def _pick_tile(m, candidates):
    for t in candidates:
        if m % t == 0:
            return t
    raise ValueError(f"no tile for {m}")


def _bias_body(rb_ref, toep_ref, bc_ref, sc_ref, sw_ref, *, seq, past_len, w_buf):
    h = pl.program_id(0)
    tq = NSA_TQ

    def table(dist):
        bk = _t5_bucket(dist)
        out = jnp.zeros(dist.shape, F32)
        for b in range(N_BUCKETS):
            out = jnp.where(bk == b, rb_ref[b, h], out)
        return out

    k = lax.broadcasted_iota(I32, (tq, tq), 0)
    q = lax.broadcasted_iota(I32, (tq, tq), 1)
    toep_ref[0] = table(tq + q - k)
    toep_ref[1] = table(q - k)
    n = lax.broadcasted_iota(I32, (LANE, tq), 0)
    qc = lax.broadcasted_iota(I32, (LANE, tq), 1)
    for i in range(seq // tq):
        bc_ref[i] = table(i * tq + qc - CMP_BLOCK * n - (CMP_BLOCK - 1))
    n1 = lax.broadcasted_iota(I32, sc_ref.shape, 1)
    sc_ref[...] = table(past_len - CMP_BLOCK * n1 - (CMP_BLOCK - 1))
    j1 = lax.broadcasted_iota(I32, sw_ref.shape, 1)
    sw_ref[...] = table(w_buf - j1)


def _bias_tables(rel_bias, seq, past_len, w_buf, nbp_s, wp_s):
    body = functools.partial(_bias_body, seq=seq, past_len=past_len, w_buf=w_buf)
    nq = seq // NSA_TQ
    return pl.pallas_call(
        body,
        grid=(H_NSA,),
        in_specs=[pl.BlockSpec(memory_space=pltpu.SMEM)],
        out_specs=[
            pl.BlockSpec((2, NSA_TQ, NSA_TQ), lambda h: (0, 0, h)),
            pl.BlockSpec((nq, LANE, NSA_TQ), lambda h: (0, 0, h)),
            pl.BlockSpec((None, 1, nbp_s), lambda h: (h, 0, 0)),
            pl.BlockSpec((None, 1, wp_s), lambda h: (h, 0, 0)),
        ],
        out_shape=[
            jax.ShapeDtypeStruct((2, NSA_TQ, H_NSA * NSA_TQ), F32),
            jax.ShapeDtypeStruct((nq, LANE, H_NSA * NSA_TQ), F32),
            jax.ShapeDtypeStruct((H_NSA, 1, nbp_s), F32),
            jax.ShapeDtypeStruct((H_NSA, 1, wp_s), F32),
        ],
        compiler_params=_cparams(1),
        name="bias_tables",
    )(rel_bias)


Q_TILES = D_NSA // COL_TILE
NSA_TILE0 = Q_TILES
WIN_TILE = NSA_TILE0 + 4 * D_KV // COL_TILE
SB_TILE0 = WIN_TILE + 2 * D_KV // COL_TILE + D_SB // COL_TILE


def _inproj_body(x_ref, na_ref, wa_ref, wb_ref, wg_ref, gain_ref, zb_ref, sb_ref, nsa_ref, win_ref,
                 gate_ref, h_ref, *, w_keep):
    j = pl.program_id(1)
    tm = x_ref.shape[0]
    half = COL_TILE // 2

    @pl.when(j == 0)
    def _():
        h = _rms(x_ref[...], na_ref[...]).astype(BF16)
        h_ref[...] = h
        gate_ref[...] = jax.nn.sigmoid(_dot(h, wg_ref[...]))

    gain = gain_ref[...]

    def emit(w_ref, norm_chunks, f32_ref=None, row0=0):
        h = h_ref[...]
        accs = [_dot(h, w_ref[:, :half]), _dot(h, w_ref[:, half:])]
        for c in range(COL_TILE // LANE):
            sl = slice(c * LANE, (c + 1) * LANE)
            y = accs[c * LANE // half][:, (c * LANE) % half:(c * LANE) % half + LANE]
            if c in norm_chunks:
                y = _rms(y, gain[:, sl])
            zb_ref[:, sl] = y.astype(BF16)
            if f32_ref is not None:
                f32_ref[:, sl] = y[row0:, :]

    @pl.when(j < Q_TILES)
    def _():
        emit(wa_ref, (0, 1, 2, 3))

    @pl.when(j == NSA_TILE0)
    def _():
        emit(wa_ref, (), nsa_ref)

    @pl.when(j == NSA_TILE0 + 1)
    def _():
        emit(wa_ref, (0, 1), nsa_ref)

    @pl.when(j == WIN_TILE)
    def _():
        emit(wa_ref, (0, 1), win_ref, tm - w_keep)

    @pl.when((j > WIN_TILE) & (j < SB_TILE0))
    def _():
        emit(wb_ref, ())

    @pl.when(j >= SB_TILE0)
    def _():
        emit(wb_ref, (), sb_ref)


def _inproj(x2d, na, w_a, w_b, w_gate, gain_cols, seq, w_keep):
    m = x2d.shape[0]
    tm = _pick_tile(seq, (1024, 512, 256, 128, 16))
    assert w_keep <= tm
    tpb = seq // tm
    n_sb = 2 * D_SB // COL_TILE
    n_nsa = 4 * D_KV // COL_TILE
    n_a = w_a.shape[1] // COL_TILE
    n_b = w_b.shape[1] // COL_TILE
    assert n_a == WIN_TILE + 1 and n_a + n_b == N_MAIN // COL_TILE
    body = functools.partial(_inproj_body, w_keep=w_keep)
    return pl.pallas_call(
        body,
        grid=(m // tm, n_a + n_b),
        in_specs=[
            pl.BlockSpec((tm, D_MODEL), lambda i, j: (i, 0)),
            pl.BlockSpec((1, D_MODEL), lambda i, j: (0, 0)),
            pl.BlockSpec((D_MODEL, COL_TILE), lambda i, j: (0, jnp.minimum(j, n_a - 1))),
            pl.BlockSpec((D_MODEL, COL_TILE), lambda i, j: (0, jnp.maximum(j - n_a, 0))),
            pl.BlockSpec((D_MODEL, 2 * LANE), lambda i, j: (0, 0)),
            pl.BlockSpec((1, COL_TILE), lambda i, j: (0, j)),
        ],
        out_specs=[
            pl.BlockSpec((tm, COL_TILE), lambda i, j: (i, j)),
            pl.BlockSpec((tm, COL_TILE), lambda i, j: (i, jnp.clip(j - SB_TILE0, 0, n_sb - 1))),
            pl.BlockSpec((tm, COL_TILE), lambda i, j: (i, jnp.clip(j - NSA_TILE0, 0, n_nsa - 1))),
            pl.BlockSpec((w_keep, COL_TILE), lambda i, j: (i // tpb, 0)),
            pl.BlockSpec((tm, 2 * LANE), lambda i, j: (i, 0)),
        ],
        out_shape=[
            jax.ShapeDtypeStruct((m, N_MAIN), BF16),
            jax.ShapeDtypeStruct((m, 2 * D_SB), F32),
            jax.ShapeDtypeStruct((m, 4 * D_KV), F32),
            jax.ShapeDtypeStruct((m // seq * w_keep, 2 * D_KV), F32),
            jax.ShapeDtypeStruct((m, 2 * LANE), F32),
        ],
        scratch_shapes=[pltpu.VMEM((tm, D_MODEL), BF16)],
        compiler_params=_cparams(2),
        name="inproj",
    )(x2d, na, w_a, w_b, w_gate, gain_cols)


def _outproj_body(on_ref, os_ref, go_ref, w_ref, x_ref, y_ref, h_ref):
    j = pl.program_id(1)

    @pl.when(j == 0)
    def _():
        go = go_ref[...]
        h_ref[:, :D_NSA] = _rms(on_ref[...], go[:, :D_NSA]).astype(BF16)
        h_ref[:, D_NSA:] = _rms(os_ref[...], go[:, D_NSA:]).astype(BF16)

    y_ref[...] = x_ref[...] + _dot(h_ref[...], w_ref[...])


def _outproj(o_nsa, o_sb, g_out, w_out, x2d):
    m = x2d.shape[0]
    tm = _pick_tile(m, (1024, 512, 256, 128, 16))
    return pl.pallas_call(
        _outproj_body,
        grid=(m // tm, D_MODEL // COL_TILE),
        in_specs=[
            pl.BlockSpec((tm, D_NSA), lambda i, j: (i, 0)),
            pl.BlockSpec((tm, D_SB), lambda i, j: (i, 0)),
            pl.BlockSpec((1, D_MODEL), lambda i, j: (0, 0)),
            pl.BlockSpec((D_MODEL, COL_TILE), lambda i, j: (0, j)),
            pl.BlockSpec((tm, COL_TILE), lambda i, j: (i, j)),
        ],
        out_specs=pl.BlockSpec((tm, COL_TILE), lambda i, j: (i, j)),
        out_shape=jax.ShapeDtypeStruct((m, D_MODEL), F32),
        scratch_shapes=[pltpu.VMEM((tm, D_MODEL), BF16)],
        compiler_params=_cparams(2),
        name="outproj",
    )(o_nsa, o_sb, g_out, w_out, x2d)


def _ffn_body(*refs, sb):
    if sb is None:
        x_ref, nf_ref, wg_ref, wu_ref, wd_ref, y_ref, h_ref = refs
    else:
        pps = sb["pps"]
        x_ref, nf_ref, wg_ref, wu_ref, wd_ref, q_ref = refs[1:7]
        page_refs = refs[7:7 + pps]
        up_ref, upp_ref, y_ref, o_ref, h_ref, z_ref, sacc_ref = refs[7 + pps:]
    i = pl.program_id(0)
    j = pl.program_id(1)

    @pl.when(j == 0)
    def _():
        x = x_ref[...]
        h_ref[...] = _rms(x, nf_ref[...]).astype(BF16)
        y_ref[...] = x

    h = h_ref[...]
    act = jax.nn.silu(_dot(h, wg_ref[...])) * _dot(h, wu_ref[...])
    y_ref[...] += _dot(act.astype(BF16), wd_ref[...])

    if sb is not None:
        nc = sb["n_chunks"]
        s = i * pl.num_programs(1) + j

        @pl.when(s < sb["n_steps"])
        def _():
            _sb_dec_step((s // nc) % 2, s % nc, nc - 1, q_ref, page_refs, up_ref, upp_ref, o_ref,
                         z_ref, sacc_ref, sb["page"])


def _ffn(x2d, norm_ffn, w_gate, w_up, w_down, sb_args=None):
    m = x2d.shape[0]
    d_ff = w_gate.shape[1]
    if sb_args is None:
        tm = _pick_tile(m, (512, 256, 128, 16))
        tf = _pick_tile(d_ff, (512, 256))
        x_mode = None
    else:
        tm = _pick_tile(m, (1024, 512, 256, 128))
        tf = _pick_tile(d_ff, (256,))
        x_mode = pl.Buffered(1)
    nf = d_ff // tf
    in_specs = [
        pl.BlockSpec((tm, D_MODEL), lambda i, j, *_: (i, 0), pipeline_mode=x_mode),
        pl.BlockSpec((1, D_MODEL), lambda i, j, *_: (0, 0)),
        pl.BlockSpec((D_MODEL, tf), lambda i, j, *_: (0, j)),
        pl.BlockSpec((D_MODEL, tf), lambda i, j, *_: (0, j)),
        pl.BlockSpec((tf, D_MODEL), lambda i, j, *_: (j, 0)),
    ]
    y_spec = pl.BlockSpec((tm, D_MODEL), lambda i, j, *_: (i, 0))
    y_shape = jax.ShapeDtypeStruct((m, D_MODEL), F32)
    h_scratch = pltpu.VMEM((tm, D_MODEL), BF16)
    if sb_args is None:
        return pl.pallas_call(
            functools.partial(_ffn_body, sb=None),
            grid=(m // tm, nf),
            in_specs=in_specs,
            out_specs=y_spec,
            out_shape=y_shape,
            scratch_shapes=[h_scratch],
            compiler_params=_cparams(2),
            name="ffn",
        )(x2d, norm_ffn, w_gate, w_up, w_down)

    l, pt_flat, q_sb, cache_sb = sb_args
    dec_b = q_sb.shape[0]
    page = cache_sb.shape[2]
    npg = pt_flat.shape[0] // dec_b
    pps = SB_PAGES_PER_STEP
    nc = npg // pps
    n_steps = dec_b * 2 * nc
    assert n_steps <= (m // tm) * nf
    width = page * H_SB
    up, upp = _sb_dec_consts(npg, page)

    def pos(i, j):
        s = jnp.minimum(i * nf + j, n_steps - 1)
        return s // (2 * nc), (s // nc) % 2, s % nc

    def page_spec(u):
        def imap(i, j, pt):
            b, ph, c = pos(i, j)
            return (l, pt[b * npg + c * pps + u], 0, ph, 0, 0)
        return pl.BlockSpec((None, None, page, None, H_SB, HEAD_DIM), imap)

    once = pl.Buffered(1)
    grid_spec = pltpu.PrefetchScalarGridSpec(
        num_scalar_prefetch=1,
        grid=(m // tm, nf),
        in_specs=in_specs
        + [pl.BlockSpec((None, ROWS_S, HEAD_DIM), lambda i, j, pt: (pos(i, j)[0], 0, 0))]
        + [page_spec(u) for u in range(pps)]
        + [pl.BlockSpec((width, 2 * width), lambda i, j, pt: (0, 0), pipeline_mode=once),
           pl.BlockSpec((npg, npg), lambda i, j, pt: (0, 0), pipeline_mode=once)],
        out_specs=[y_spec,
                   pl.BlockSpec((None, H_SB, HEAD_DIM), lambda i, j, pt: (pos(i, j)[0], 0, 0))],
        scratch_shapes=[h_scratch, pltpu.VMEM((npg, width), F32),
                        pltpu.VMEM((ROWS_S, HEAD_DIM), F32)],
    )
    sb = dict(pps=pps, page=page, n_chunks=nc, n_steps=n_steps)
    return pl.pallas_call(
        functools.partial(_ffn_body, sb=sb),
        grid_spec=grid_spec,
        out_shape=[y_shape, jax.ShapeDtypeStruct((dec_b, H_SB, HEAD_DIM), F32)],
        compiler_params=_cparams(2, VMEM_LIMIT_FUSED),
        name="ffn_sb_decode",
    )(pt_flat, x2d, norm_ffn, w_gate, w_up, w_down, q_sb, *([cache_sb] * pps), up, upp)


SB_SUB = 4


def _sb_body(*refs):
    n_sub = SB_HB // SB_SUB
    q_refs, k_refs, v_refs = refs[:n_sub], refs[n_sub:2 * n_sub], refs[2 * n_sub:3 * n_sub]
    u_ref, o_ref, acc_ref, run_ref = refs[3 * n_sub:]
    i = pl.program_id(2)
    row = lax.broadcasted_iota(I32, (SB_TQ, SB_TK), 0)
    col = lax.broadcasted_iota(I32, (SB_TQ, SB_TK), 1)
    strict = col < row

    def tile(kb, diagonal):
        off = pl.multiple_of(kb * SB_TK, SB_TK)
        heads = range(SB_HB)
        sls = [slice(h % SB_SUB * HEAD_DIM, (h % SB_SUB + 1) * HEAD_DIM) for h in heads]
        q_ref = [q_refs[h // SB_SUB] for h in heads]
        k_ref = [k_refs[h // SB_SUB] for h in heads]
        v_ref = [v_refs[h // SB_SUB] for h in heads]
        z = [_dot_nt(q_ref[h][:, sls[h]], k_ref[h][pl.ds(off, SB_TK), sls[h]]) * SCALE
             for h in heads]
        log_1m = [_neg_softplus(zh) for zh in z]
        if diagonal:
            log_1m = [jnp.where(strict, lh, 0.0) for lh in log_1m]
        parts = [_split_bf16(lh) for lh in log_1m]
        u = u_ref[...]
        later = [_dot(hi, u) + _dot(lo, u) for hi, lo in parts]
        total = [jnp.sum(lh, axis=-1, keepdims=True) for lh in log_1m]
        if diagonal:
            a = [jnp.where(strict, jnp.exp(z[h] + log_1m[h] + later[h]), 0.0) for h in heads]
        else:
            run = [run_ref[h] for h in heads]
            a = [jnp.exp(z[h] + log_1m[h] + later[h] + run[h]) for h in heads]
        pv = [_dot(a[h].astype(BF16), v_ref[h][pl.ds(off, SB_TK), sls[h]]) for h in heads]
        for h in heads:
            if diagonal:
                acc_ref[h] = pv[h]
                run_ref[h] = total[h]
            else:
                acc_ref[h] += pv[h]
                run_ref[h] = run[h] + total[h]

    tile(i, True)

    def step(it, carry):
        tile(i - 1 - it, False)
        return carry

    lax.fori_loop(0, i, step, 0)
    for h in range(SB_HB):
        o_ref[:, h * HEAD_DIM:(h + 1) * HEAD_DIM] = acc_ref[h]


def _sb_prompt(zb, batch, seq):
    nq = seq // SB_TQ
    width = SB_HB * HEAD_DIM
    sub_w = SB_SUB * HEAD_DIM
    n_sub = SB_HB // SB_SUB
    assert QS_BLK % SB_SUB == 0 and KSB_BLK % SB_SUB == 0 and VSB_BLK % SB_SUB == 0
    r = lax.broadcasted_iota(I32, (SB_TK, SB_TK), 0)
    c = lax.broadcasted_iota(I32, (SB_TK, SB_TK), 1)
    u = (r > c).astype(BF16)

    def q_spec(n):
        return pl.BlockSpec((SB_TQ, sub_w),
                            lambda b, h, i: (b * nq + i, QS_BLK // SB_SUB + h * n_sub + n))

    def kv_spec(blk, n):
        return pl.BlockSpec((seq, sub_w), lambda b, h, i: (b, blk // SB_SUB + h * n_sub + n))

    subs = range(n_sub)
    return pl.pallas_call(
        _sb_body,
        grid=(batch, H_SB // SB_HB, nq),
        in_specs=[q_spec(n) for n in subs] + [kv_spec(KSB_BLK, n) for n in subs]
        + [kv_spec(VSB_BLK, n) for n in subs]
        + [pl.BlockSpec((SB_TK, SB_TK), lambda b, h, i: (0, 0))],
        out_specs=pl.BlockSpec((SB_TQ, width), lambda b, h, i: (b * nq + i, h)),
        out_shape=jax.ShapeDtypeStruct((batch * seq, D_SB), F32),
        scratch_shapes=[pltpu.VMEM((SB_HB, SB_TQ, HEAD_DIM), F32), pltpu.VMEM((SB_HB, SB_TQ, 1), F32)],
        compiler_params=_cparams(3),
        name="sb_prompt",
    )(*([zb] * (3 * n_sub)), u)


def _cmp_prompt_body(x0_ref, x1_ref, x2_ref, x3_ref, wk_ref, wv_ref, bk_ref, bv_ref, gk_ref,
                     kc_ref, vct_ref, *, nb):
    pad = jnp.zeros((LANE - nb, HEAD_DIM), F32)
    for c, x_ref in enumerate((x0_ref, x1_ref, x2_ref, x3_ref)):
        rows = jnp.concatenate(
            [x_ref[pl.ds(l, nb, stride=CMP_BLOCK), :] for l in range(CMP_BLOCK)], axis=1)
        rows = rows.astype(BF16)
        if c < G_NSA:
            y = _rms(_dot(rows, wk_ref[...]) + bk_ref[...], gk_ref[...])
            kc_ref[c] = jnp.concatenate([y, pad], axis=0).astype(BF16)
        else:
            y = _dot(rows, wv_ref[...]) + bv_ref[...]
            vct_ref[c - G_NSA] = jnp.concatenate([y, pad], axis=0).T.astype(BF16)


def _cmp_prompt(nsa_rows, batch, seq, wk, wv, bk, bv, gk0):
    nb = seq // CMP_BLOCK
    assert nb % 8 == 0 and nb < LANE
    body = functools.partial(_cmp_prompt_body, nb=nb)
    shape = jax.ShapeDtypeStruct((batch, G_NSA, LANE, HEAD_DIM), BF16)

    def x_spec(c):
        return pl.BlockSpec((seq, HEAD_DIM), lambda b: (b, c))

    return pl.pallas_call(
        body,
        grid=(batch,),
        in_specs=[
            x_spec(0), x_spec(1), x_spec(2), x_spec(3),
            pl.BlockSpec(wk.shape, lambda b: (0, 0)),
            pl.BlockSpec(wv.shape, lambda b: (0, 0)),
            pl.BlockSpec((1, HEAD_DIM), lambda b: (0, 0)),
            pl.BlockSpec((1, HEAD_DIM), lambda b: (0, 0)),
            pl.BlockSpec((1, HEAD_DIM), lambda b: (0, 0)),
        ],
        out_specs=[pl.BlockSpec((None, G_NSA, LANE, HEAD_DIM), lambda b: (b, 0, 0, 0))] * 2,
        out_shape=[shape, shape],
        compiler_params=_cparams(1),
        name="cmp_prompt",
    )(nsa_rows, nsa_rows, nsa_rows, nsa_rows, wk, wv, bk, bv, gk0)


def _masked_softmax(s, valid):
    logit = jnp.where(valid, s, NEG_INF)
    e = jnp.exp(logit - jnp.max(logit, axis=-1, keepdims=True))
    return jnp.where(valid, e / jnp.sum(e, axis=-1, keepdims=True), 0.0)


class _Flash:
    def __init__(self, m_ref, l_ref, a_ref):
        self.m_ref, self.l_ref, self.a_ref = m_ref, l_ref, a_ref

    def first(self, logit, v):
        m = jnp.max(logit, axis=-1, keepdims=True)
        p = jnp.exp(logit - m)
        self.m_ref[...] = m
        self.l_ref[...] = jnp.sum(p, axis=-1, keepdims=True)
        self.a_ref[...] = _dot(p.astype(BF16), v)

    def update(self, logit, v):
        m_old = self.m_ref[...]
        m = jnp.maximum(m_old, jnp.max(logit, axis=-1, keepdims=True))
        alpha = jnp.exp(m_old - m)
        p = jnp.exp(logit - m)
        self.m_ref[...] = m
        self.l_ref[...] = alpha * self.l_ref[...] + jnp.sum(p, axis=-1, keepdims=True)
        self.a_ref[...] = alpha * self.a_ref[...] + _dot(p.astype(BF16), v)

    def result(self):
        return self.a_ref[...] / self.l_ref[...]


class _FlashT:
    def __init__(self, m_ref, l_ref, a_ref):
        self.m_ref, self.l_ref, self.a_ref = m_ref, l_ref, a_ref

    @staticmethod
    def step(states, logits, v_ts, first):
        ms = [jnp.max(lg, axis=0, keepdims=True) for lg in logits]
        if not first:
            olds = [st.m_ref[...] for st in states]
            ms = [jnp.maximum(o, m) for o, m in zip(olds, ms)]
            alphas = [jnp.exp(o - m) for o, m in zip(olds, ms)]
        ps = [jnp.exp(lg - m) for lg, m in zip(logits, ms)]
        sums = [jnp.sum(p, axis=0, keepdims=True) for p in ps]
        pvs = [_dot(v_t, p.astype(BF16)) for v_t, p in zip(v_ts, ps)]
        for n, st in enumerate(states):
            st.m_ref[...] = ms[n]
            if first:
                st.l_ref[...] = sums[n]
                st.a_ref[...] = pvs[n]
            else:
                st.l_ref[...] = alphas[n] * st.l_ref[...] + sums[n]
                st.a_ref[...] = alphas[n] * st.a_ref[...] + pvs[n]

    def update(self, logit, v_t):
        _FlashT.step([self], [logit], [v_t], False)

    def result(self):
        return self.a_ref[...] / self.l_ref[...]


def _nsa_body(pt_ref, rb_ref, q_ref, kc_ref, vct_ref, ks_ref, vs_ref, kw_ref, vw_ref, g_ref,
              toep_ref, bc_ref, et_ref, *rest, nb, seq, cmp):
    page_refs = rest[:cmp["pps"]]
    (wkv_ref, o_ref, kcr_ref, vcr_ref, vst_ref, vwt_ref, m_s, l_s, a_s, m_w, l_w, a_w, yk_ref,
     yv_ref) = rest[cmp["pps"]:]
    g = pl.program_id(1)
    i = pl.program_id(2)
    tq = NSA_TQ
    sub = tq // LANE
    width = HPG * tq
    q0 = i * tq
    step = (pl.program_id(0) * pl.num_programs(1) + g) * pl.num_programs(2) + i

    @pl.when(step < cmp["n_steps"])
    def _():
        _cmp_dec_step(page_refs, wkv_ref, kcr_ref, vcr_ref, yk_ref, yv_ref, cmp["page"])

    @pl.when(i == 0)
    def _():
        for c in range(seq // LANE):
            rows = slice(c * LANE, (c + 1) * LANE)
            cols = slice((c % sub) * LANE, (c % sub + 1) * LANE)
            vst_ref[c // sub, :, cols] = vs_ref[rows, :].astype(F32).T.astype(BF16)
            vwt_ref[c // sub, :, cols] = vw_ref[rows, :].astype(F32).T.astype(BF16)

    qs = jnp.concatenate([q_ref[:, hh * LANE:(hh + 1) * LANE] for hh in range(HPG)], axis=0)

    def q_off(rows):
        return lax.broadcasted_iota(I32, (rows, width), 1) % tq

    n_r = lax.broadcasted_iota(I32, (LANE, width), 0)
    s_c = _dot_nt(kc_ref[...], qs) * SCALE + bc_ref[...]
    valid_c = (CMP_BLOCK * n_r + (CMP_BLOCK - 1) <= q0 + q_off(LANE)) & (n_r < nb)
    logit = jnp.where(valid_c, s_c, NEG_INF)
    e = jnp.exp(logit - jnp.max(logit, axis=0, keepdims=True))
    p_c = jnp.where(valid_c, e / jnp.sum(e, axis=0, keepdims=True), 0.0)
    oc_t = _dot(vct_ref[...], p_c.astype(BF16))
    imp_t = p_c[:, 0:tq]
    for hh in range(1, HPG):
        imp_t = imp_t + p_c[:, hh * tq:(hh + 1) * tq]

    nbr = -(-nb // 8) * 8
    n_i = lax.broadcasted_iota(I32, (nbr, tq), 0)
    cur = (q0 + lax.broadcasted_iota(I32, (nbr, tq), 1)) // CMP_BLOCK
    forced = (n_i == 0) | (n_i >= cur - (N_LOCAL - 1))
    score = jnp.where(n_i > cur, -jnp.inf, jnp.where(forced, FORCE_SCORE, imp_t[0:nbr]))
    rank = jnp.zeros((nbr, tq), I32)
    for j in range(nb):
        sj = jnp.broadcast_to(score[j:j + 1, :], (nbr, tq))
        ahead = (sj > score) | ((sj == score) & (n_i > j))
        rank = rank + ahead.astype(I32)
    sel_t = ((rank < min(TOP_N, nb)) & (score > -jnp.inf)).astype(F32)
    if nbr < LANE:
        sel_t = jnp.concatenate([sel_t, jnp.zeros((LANE - nbr, tq), F32)], axis=0)
    sel_t = sel_t.astype(BF16)

    k_r = lax.broadcasted_iota(I32, (tq, width), 0)
    q_c = q_off(tq)
    causal = k_r <= q_c
    bias_far = jnp.concatenate(
        [jnp.full((1, tq), rb_ref[N_BUCKETS - 1, g * HPG + hh], F32) for hh in range(HPG)], axis=1)

    def chunk(ref, c):
        return ref[pl.ds(pl.multiple_of(c * tq, tq), tq), :]

    def scores(ref, c):
        return _dot_nt(chunk(ref, c), qs) * SCALE

    def selected(c):
        hit = _dot(chunk(et_ref, c), sel_t)
        return jnp.concatenate([hit] * HPG, axis=1) > 0.5

    fs = _FlashT(m_s, l_s, a_s)
    fw = _FlashT(m_w, l_w, a_w)

    def both(c, bias, sel_extra, win_mask, first):
        sel_mask = selected(c) if sel_extra is None else selected(c) & sel_extra
        lg_s = jnp.where(sel_mask, scores(ks_ref, c) + bias, NEG_INF)
        lg_w = scores(kw_ref, c) + bias
        if win_mask is not None:
            lg_w = jnp.where(win_mask, lg_w, NEG_INF)
        _FlashT.step([fs, fw], [lg_s, lg_w], [vst_ref[c], vwt_ref[c]], first)

    both(i, toep_ref[1], causal, causal, True)

    @pl.when(i >= 1)
    def _():
        both(i - 1, toep_ref[0], None, None, False)

    @pl.when(i >= 2)
    def _():
        both(i - 2, bias_far, None, k_r >= q_c, False)

    def far_step(c, carry):
        fs.update(jnp.where(selected(c), scores(ks_ref, c) + bias_far, NEG_INF), vst_ref[c])
        return carry

    lax.fori_loop(0, jnp.maximum(i - 2, 0), far_step, 0)

    g_t = jnp.concatenate([g_ref[t * LANE:(t + 1) * LANE, :].T for t in range(sub)], axis=1)

    def gate(branch):
        return jnp.concatenate(
            [g_t[branch * HPG + hh:branch * HPG + hh + 1, :] for hh in range(HPG)], axis=1)

    o_t = gate(0) * oc_t + gate(1) * fs.result() + gate(2) * fw.result()
    for hh in range(HPG):
        for t in range(sub):
            lanes = slice(hh * tq + t * LANE, hh * tq + (t + 1) * LANE)
            o_ref[t * LANE:(t + 1) * LANE, hh * LANE:(hh + 1) * LANE] = o_t[:, lanes].T


def _nsa_prompt(rel_bias, zb, kc, vct, gates, toep, bias_c, batch, seq, cmp_args):
    tq = NSA_TQ
    assert WINDOW == 2 * tq and tq % LANE == 0 and seq % tq == 0
    nq = seq // tq
    nb = seq // CMP_BLOCK
    assert nb <= LANE
    width = HPG * tq
    k = lax.broadcasted_iota(I32, (seq, LANE), 0)
    n = lax.broadcasted_iota(I32, (seq, LANE), 1)
    expand_t = (k // CMP_BLOCK == n).astype(BF16)

    l, pt_flat, cache_nsa2d, w_kv, dec_b, page = cmp_args
    npg = pt_flat.shape[0] // dec_b
    pps = min(CMP_PAGES_PER_STEP, npg)
    assert npg % pps == 0
    nchunk = npg // pps
    n_cmp = dec_b * nchunk
    assert n_cmp <= batch * G_NSA * nq
    halves = page // CMP_BLOCK
    nblk = pps * halves
    body = functools.partial(_nsa_body, nb=nb, seq=seq,
                             cmp=dict(pps=pps, page=page, n_steps=n_cmp))

    def kv_spec(blk):
        return pl.BlockSpec((seq, HEAD_DIM), lambda b, g, i, pt: (b, blk + g))

    def cmp_pos(b, g, i):
        s = jnp.minimum((b * G_NSA + g) * nq + i, n_cmp - 1)
        return s // nchunk, s % nchunk

    def page_spec(u):
        def imap(b, g, i, pt):
            row, c = cmp_pos(b, g, i)
            return (l, pt[row * npg + c * pps + u], 0, 0)
        return pl.BlockSpec((None, None, page * NSA_CH, HEAD_DIM), imap)

    def raw_imap(b, g, i, pt):
        row, c = cmp_pos(b, g, i)
        return (row, 0, c, 0)

    raw_spec = pl.BlockSpec((None, G_NSA, nblk, HEAD_DIM), raw_imap)
    raw_shape = jax.ShapeDtypeStruct((dec_b, G_NSA, npg * halves, HEAD_DIM), F32)
    stat = pltpu.VMEM((1, width), F32)
    accum = pltpu.VMEM((HEAD_DIM, width), F32)
    v_t = pltpu.VMEM((nq, HEAD_DIM, tq), BF16)
    y_raw = pltpu.VMEM((nblk * NSA_CH, HEAD_DIM), F32)
    grid_spec = pltpu.PrefetchScalarGridSpec(
        num_scalar_prefetch=1,
        grid=(batch, G_NSA, nq),
        in_specs=[
            pl.BlockSpec(memory_space=pltpu.SMEM),
            pl.BlockSpec((tq, HPG * HEAD_DIM), lambda b, g, i, pt: (b * nq + i, g)),
            pl.BlockSpec((None, None, LANE, HEAD_DIM), lambda b, g, i, pt: (b, g, 0, 0)),
            pl.BlockSpec((None, None, HEAD_DIM, LANE), lambda b, g, i, pt: (b, g, 0, 0)),
            kv_spec(KS_BLK), kv_spec(VS_BLK), kv_spec(KW_BLK), kv_spec(VW_BLK),
            pl.BlockSpec((tq, LANE), lambda b, g, i, pt: (b * nq + i, g)),
            pl.BlockSpec((2, tq, width), lambda b, g, i, pt: (0, 0, g)),
            pl.BlockSpec((None, LANE, width), lambda b, g, i, pt: (i, 0, g)),
            pl.BlockSpec((seq, LANE), lambda b, g, i, pt: (0, 0)),
        ]
        + [page_spec(u) for u in range(pps)]
        + [pl.BlockSpec(w_kv.shape, lambda b, g, i, pt: (0, 0), pipeline_mode=pl.Buffered(1))],
        out_specs=[pl.BlockSpec((tq, HPG * HEAD_DIM), lambda b, g, i, pt: (b * nq + i, g)),
                   raw_spec, raw_spec],
        scratch_shapes=[v_t, v_t, stat, stat, accum, stat, stat, accum, y_raw, y_raw],
    )
    return pl.pallas_call(
        body,
        grid_spec=grid_spec,
        out_shape=[jax.ShapeDtypeStruct((batch * seq, D_NSA), F32), raw_shape, raw_shape],
        compiler_params=_cparams(3),
        name="nsa_prompt_cmp_decode",
    )(pt_flat, rel_bias, zb, kc, vct, zb, zb, zb, zb, gates, toep, bias_c, expand_t,
      *([cache_nsa2d] * pps), w_kv)


def _cast_body(w_ref, o_ref):
    o_ref[...] = w_ref[...].astype(BF16)


def _cast_layer_bf16(w, l, cols=None):
    _, rows, n = w.shape
    cols = n if cols is None else cols
    assert cols % LANE == 0
    tr = _pick_tile(rows, (256, 128))
    return pl.pallas_call(
        _cast_body,
        grid=(rows // tr,),
        in_specs=[pl.BlockSpec((None, tr, cols), lambda i: (l, i, 0))],
        out_specs=pl.BlockSpec((tr, cols), lambda i: (i, 0)),
        out_shape=jax.ShapeDtypeStruct((rows, cols), BF16),
        compiler_params=_cparams(1),
        name="cast_bf16",
    )(w)


def _prep_weights(l, w_in, g_q, g_k, w_cmp_k, w_cmp_v, w_out, w_gate, w_up, w_down):
    n_gate = 3 * H_NSA
    g0 = D_NSA + 6 * D_KV
    wi = w_in[l]
    w_a = wi[:, :g0].astype(BF16)
    w_b = wi[:, g0 + n_gate:].astype(BF16)
    wg = wi[:, g0:g0 + n_gate].reshape(D_MODEL, 3, G_NSA, HPG)
    wg = jnp.transpose(wg, (0, 2, 1, 3)).reshape(D_MODEL, G_NSA, 3 * HPG)
    wg = jnp.pad(wg, ((0, 0), (0, 0), (0, LANE - 3 * HPG))).reshape(D_MODEL, G_NSA * LANE)
    ones = jnp.ones((HEAD_DIM,), F32)
    gain = jnp.concatenate(
        [jnp.tile(g_q[l], H_NSA), jnp.tile(ones, 2 * G_NSA),
         jnp.tile(g_k[l, 1], G_NSA), jnp.tile(ones, G_NSA),
         jnp.tile(g_k[l, 2], G_NSA), jnp.tile(ones, G_NSA),
         jnp.tile(ones, 3 * H_SB)])[None, :]
    wk = w_cmp_k[l].reshape(CMP_BLOCK * HEAD_DIM, HEAD_DIM).astype(BF16)
    wv = w_cmp_v[l].reshape(CMP_BLOCK * HEAD_DIM, HEAD_DIM).astype(BF16)
    return dict(
        w_a=w_a, w_b=w_b, w_gate=wg.astype(BF16), gain=gain, wk=wk, wv=wv,
        w_kv=jnp.concatenate([wk, wv], axis=1),
        w_out=_cast_layer_bf16(w_out, l), w_gate_ffn=_cast_layer_bf16(w_gate, l),
        w_up=_cast_layer_bf16(w_up, l), w_down=_cast_layer_bf16(w_down, l))


def _prompt_layer(x2d, wt, p, tables, rel_bias, batch, seq, sb_args, cmp_args):
    na, bk, bv, gk0, go, nf = p
    zb, sb_rows, nsa_rows, win_rows, gates = _inproj(
        x2d, na, wt["w_a"], wt["w_b"], wt["w_gate"], wt["gain"], seq, min(WINDOW, seq))
    kc, vct = _cmp_prompt(nsa_rows, batch, seq, wt["wk"], wt["wv"], bk, bv, gk0)
    o_nsa, kcr, vcr = _nsa_prompt(rel_bias, zb, kc, vct, gates, tables[0], tables[1], batch, seq,
                                  cmp_args)
    o_sb = _sb_prompt(zb, batch, seq)
    x1 = _outproj(o_nsa, o_sb, go, wt["w_out"], x2d)
    x2, o_sb_sample = _ffn(x1, nf, wt["w_gate_ffn"], wt["w_up"], wt["w_down"], sb_args)
    return x2, sb_rows, nsa_rows, win_rows, (o_sb_sample, kcr, vcr)


ROWS_S = 16


def _sb_dec_step(ph, c, last_c, q_ref, page_refs, up_ref, upp_ref, o_ref, z_ref, acc_ref, page):
    pps = len(page_refs)
    width = page * H_SB
    r = lax.broadcasted_iota(I32, (ROWS_S, width), 0)
    lane = lax.broadcasted_iota(I32, (ROWS_S, width), 1)
    own_head = (lane % H_SB) == r

    @pl.when(ph == 0)
    def _():
        q = q_ref[...]
        for u in range(pps):
            ks = page_refs[u][...].reshape(width, HEAD_DIM).astype(BF16)
            zt = _dot_nt(q, ks)
            z_ref[pl.ds(c * pps + u, 1), :] = jnp.sum(jnp.where(own_head, zt, 0.0), axis=0,
                                                      keepdims=True)

    @pl.when((ph == 1) & (c == 0))
    def _():
        z = z_ref[...] * SCALE
        log_1m = _neg_softplus(z)
        cs = _dot_split(log_1m, up_ref[...])
        hi, lo = _split_bf16(cs[:, width:])
        later_pages = _dot(upp_ref[...], hi) + _dot(upp_ref[...], lo)
        z_ref[...] = jnp.exp(z + log_1m + cs[:, :width] + later_pages)
        acc_ref[...] = jnp.zeros_like(acc_ref)

    @pl.when(ph == 1)
    def _():
        for u in range(pps):
            vs = page_refs[u][...].reshape(width, HEAD_DIM).astype(BF16)
            a = jnp.broadcast_to(z_ref[pl.ds(c * pps + u, 1), :], (ROWS_S, width))
            acc_ref[...] += _dot(jnp.where(own_head, a, 0.0).astype(BF16), vs)

    @pl.when((ph == 1) & (c == last_c))
    def _():
        o_ref[...] = acc_ref[0:H_SB, :]


def _sb_dec_consts(npg, page):
    assert npg % SB_PAGES_PER_STEP == 0 and npg % 8 == 0
    width = page * H_SB
    r = lax.broadcasted_iota(I32, (width, 2 * width), 0)
    c = lax.broadcasted_iota(I32, (width, 2 * width), 1)
    same_head = (r % H_SB) == (c % H_SB)
    up = (same_head & ((c >= width) | (r // H_SB > c // H_SB))).astype(BF16)
    pr = lax.broadcasted_iota(I32, (npg, npg), 0)
    pc = lax.broadcasted_iota(I32, (npg, npg), 1)
    upp = (pc > pr).astype(BF16)
    return up, upp


NSA_CH = 4 * G_NSA


def _cmp_dec_step(page_refs, w_ref, kc_ref, vc_ref, yk_ref, yv_ref, page):
    pps = len(page_refs)
    halves = page // CMP_BLOCK
    nblk = pps * halves
    acc = jnp.zeros((nblk * NSA_CH, 2 * HEAD_DIM), F32)

    def token_rows(u, hf, l):
        return page_refs[u][pl.ds((hf * CMP_BLOCK + l) * NSA_CH, NSA_CH), :]

    for l in range(0, CMP_BLOCK, 2):
        x_l = jnp.concatenate(
            [jnp.concatenate([token_rows(u, hf, l), token_rows(u, hf, l + 1)], axis=1)
             for u in range(pps) for hf in range(halves)], axis=0).astype(BF16)
        acc = acc + _dot(x_l, w_ref[l * HEAD_DIM:(l + 2) * HEAD_DIM, :])
    yk_ref[...] = acc[:, :HEAD_DIM]
    yv_ref[...] = acc[:, HEAD_DIM:]
    for g in range(G_NSA):
        kc_ref[g] = yk_ref[pl.ds(g, nblk, stride=NSA_CH), :]
        vc_ref[g] = yv_ref[pl.ds(G_NSA + g, nblk, stride=NSA_CH), :]


def _nsa_sel_body(q_ref, kcr_ref, vcr_ref, new_ref, wk0_ref, wv0_ref, bk_ref, bv_ref, gk_ref,
                  sbc_ref, kw_ref, vw_ref, sbw_ref, idx_ref, oc_ref, ow_ref,
                  *, nb0, nbp, past_len, w_buf, wp):
    nb = nb0 + 1
    new = new_ref[...].astype(BF16)
    new_k = _rms(_dot(new, wk0_ref[...]) + bk_ref[...], gk_ref[...])
    new_v = _dot(new, wv0_ref[...]) + bv_ref[...]
    row8 = lax.broadcasted_iota(I32, (8, HEAD_DIM), 0)
    n1 = lax.broadcasted_iota(I32, (1, nbp), 1)
    nq = lax.broadcasted_iota(I32, (ROWS_S, nbp), 1)
    pos_q = past_len
    pad = jnp.zeros((nbp - nb0 - 8, HEAD_DIM), F32)
    for g in range(G_NSA):
        q = q_ref[g]
        kc = jnp.concatenate(
            [_rms(kcr_ref[g] + bk_ref[...], gk_ref[...]),
             jnp.where(row8 == 0, jnp.broadcast_to(new_k[g:g + 1], (8, HEAD_DIM)), 0.0), pad],
            axis=0).astype(BF16)
        vc = jnp.concatenate(
            [vcr_ref[g] + bv_ref[...],
             jnp.where(row8 == 0, jnp.broadcast_to(new_v[G_NSA + g:G_NSA + g + 1], (8, HEAD_DIM)),
                       0.0), pad], axis=0).astype(BF16)
        s_c = _dot_nt(q, kc) * SCALE + sbc_ref[g]
        valid_c = (CMP_BLOCK * nq + (CMP_BLOCK - 1) <= pos_q) & (nq < nb)
        p_c = _masked_softmax(s_c, valid_c)
        oc_ref[g] = _dot(p_c.astype(BF16), vc)
        imp = p_c[0:1]
        for hh in range(1, HPG):
            imp = imp + p_c[hh:hh + 1]

        cur = pos_q // CMP_BLOCK
        forced = (n1 == 0) | (n1 >= cur - (N_LOCAL - 1))
        score = jnp.where(n1 > cur, -jnp.inf, jnp.where(forced, FORCE_SCORE, imp))
        sq = jnp.broadcast_to(score, (LANE, nbp))
        col = jnp.concatenate([sq[:, k * LANE:(k + 1) * LANE].T for k in range(nbp // LANE)],
                              axis=0)
        j_i = lax.broadcasted_iota(I32, (nbp, LANE), 0)
        ranks = []
        for k in range(nbp // LANE):
            mine = jnp.broadcast_to(score[:, k * LANE:(k + 1) * LANE], (nbp, LANE))
            n_i = k * LANE + lax.broadcasted_iota(I32, (nbp, LANE), 1)
            ahead = (col > mine) | ((col == mine) & (j_i < n_i))
            ranks.append(jnp.sum(ahead.astype(F32), axis=0, keepdims=True))
        rank = jnp.concatenate(ranks, axis=1)
        slot = lax.broadcasted_iota(I32, (TOP_N, nbp), 0).astype(F32)
        hit = jnp.broadcast_to(rank, (TOP_N, nbp)) == slot
        ids = jnp.sum(jnp.where(hit, lax.broadcasted_iota(I32, (TOP_N, nbp), 1).astype(F32), 0.0),
                      axis=1, keepdims=True)
        idx_ref[g] = jnp.broadcast_to(ids, (TOP_N, LANE)).astype(I32)

        jw = lax.broadcasted_iota(I32, (ROWS_S, wp), 1)
        dist_w = w_buf - jw
        valid_w = (dist_w >= 0) & (dist_w <= WINDOW) & (past_len - w_buf + jw >= 0)
        s_w = _dot_nt(q, kw_ref[g]) * SCALE + sbw_ref[g]
        p_w = _masked_softmax(s_w, valid_w)
        ow_ref[g] = _dot(p_w.astype(BF16), vw_ref[g])


def _nsa_select(q_nsa, kcr, vcr, new_c, wk0, wv0, bk, bv, gk0, sbc, kw, vw, sbw, past_len, w_buf):
    dec_b, _, nb0, _ = kcr.shape
    nbp = sbc.shape[-1]
    wp = sbw.shape[-1]
    assert nb0 + 1 >= TOP_N and nbp >= nb0 + 8 and nb0 % 8 == 0
    body = functools.partial(_nsa_sel_body, nb0=nb0, nbp=nbp, past_len=past_len, w_buf=w_buf, wp=wp)

    def per_b(shape):
        nd = len(shape)
        return pl.BlockSpec((None,) + tuple(shape), lambda b: (b,) + (0,) * nd)

    def whole(arr):
        nd = arr.ndim
        return pl.BlockSpec(arr.shape, lambda b: (0,) * nd)

    vec = (G_NSA, ROWS_S, HEAD_DIM)
    return pl.pallas_call(
        body,
        grid=(dec_b,),
        in_specs=[per_b(vec), per_b((G_NSA, nb0, HEAD_DIM)), per_b((G_NSA, nb0, HEAD_DIM)),
                  per_b((8, HEAD_DIM)), whole(wk0), whole(wv0), whole(bk), whole(bv), whole(gk0),
                  whole(sbc), per_b((G_NSA, wp, HEAD_DIM)), per_b((G_NSA, wp, HEAD_DIM)), whole(sbw)],
        out_specs=[per_b((G_NSA, TOP_N, LANE)), per_b(vec), per_b(vec)],
        out_shape=[jax.ShapeDtypeStruct((dec_b, G_NSA, TOP_N, LANE), I32),
                   jax.ShapeDtypeStruct((dec_b,) + vec, F32),
                   jax.ShapeDtypeStruct((dec_b,) + vec, F32)],
        compiler_params=_cparams(1),
        name="nsa_select",
    )(q_nsa, kcr, vcr, new_c, wk0, wv0, bk, bv, gk0, sbc, kw, vw, sbw)


GATHER_BLOCKS = 8


def _nsa_gather_body(pt_ref, idx_ref, rb_ref, q_ref, *refs, nb0, past_len):
    n_pages = G_NSA * GATHER_BLOCKS
    page_refs = refs[:n_pages]
    new_ref, gs_ref, oc_ref, ow_ref, o_ref, m_ref, l_ref, a_ref = refs[n_pages:]
    b = pl.program_id(0)
    k = pl.program_id(1)
    pos_q = past_len
    keys = GATHER_BLOCKS * CMP_BLOCK
    row = lax.broadcasted_iota(I32, (CMP_BLOCK, HEAD_DIM), 0)
    l1 = lax.broadcasted_iota(I32, (1, CMP_BLOCK), 1)
    hrow = lax.broadcasted_iota(I32, (ROWS_S, keys), 0)
    for g in range(G_NSA):
        new_k = jnp.broadcast_to(new_ref[g:g + 1, :], (CMP_BLOCK, HEAD_DIM))
        new_v = jnp.broadcast_to(new_ref[G_NSA + g:G_NSA + g + 1, :], (CMP_BLOCK, HEAD_DIM))
        ks, vs, dist = [], [], []
        for kk in range(GATHER_BLOCKS):
            blk = idx_ref[(b * G_NSA + g) * TOP_N + k * GATHER_BLOCKS + kk]
            is_new = blk >= nb0
            page_ref = page_refs[g * GATHER_BLOCKS + kk]
            k_blk = page_ref[pl.ds(2 * G_NSA + g, CMP_BLOCK, stride=NSA_CH), :]
            v_blk = page_ref[pl.ds(3 * G_NSA + g, CMP_BLOCK, stride=NSA_CH), :]
            ks.append(jnp.where(is_new, jnp.where(row == 0, new_k, 0.0), k_blk).astype(BF16))
            vs.append(jnp.where(is_new, jnp.where(row == 0, new_v, 0.0), v_blk).astype(BF16))
            dist.append(pos_q - (blk * CMP_BLOCK + l1))
        ks = jnp.concatenate(ks, axis=0)
        vs = jnp.concatenate(vs, axis=0)
        dist = jnp.concatenate(dist, axis=1)
        bucket = jnp.broadcast_to(_t5_bucket(dist), (ROWS_S, keys))
        bias = jnp.zeros((ROWS_S, keys), F32)
        for hh in range(HPG):
            for bk in range(N_BUCKETS):
                bias = jnp.where((hrow == hh) & (bucket == bk), rb_ref[bk, g * HPG + hh], bias)
        s = _dot_nt(q_ref[g], ks) * SCALE + bias
        logit = jnp.where(jnp.broadcast_to(dist >= 0, (ROWS_S, keys)), s, NEG_INF)
        fl = _Flash(m_ref.at[g], l_ref.at[g], a_ref.at[g])

        @pl.when(k == 0)
        def _():
            fl.first(logit, vs)

        @pl.when(k > 0)
        def _():
            fl.update(logit, vs)

        @pl.when(k == pl.num_programs(1) - 1)
        def _():
            o_ref[g] = gs_ref[g, 0] * oc_ref[g] + gs_ref[g, 1] * fl.result() + gs_ref[g, 2] * ow_ref[g]


def _nsa_gather(l, pt_flat, idx_flat, rel_bias, q_nsa, cache_nsa2d, new_s, gs, oc, ow, nb0,
                past_len, page):
    dec_b = q_nsa.shape[0]
    npg = pt_flat.shape[0] // dec_b
    halves = page // CMP_BLOCK
    body = functools.partial(_nsa_gather_body, nb0=nb0, past_len=past_len)

    def page_spec(g, kk):
        def imap(b, k, pt, idx):
            blk = jnp.minimum(idx[(b * G_NSA + g) * TOP_N + k * GATHER_BLOCKS + kk], nb0 - 1)
            return (l, pt[b * npg + blk // halves], blk % halves, 0)
        return pl.BlockSpec((None, None, CMP_BLOCK * NSA_CH, HEAD_DIM), imap)

    def per_b(shape):
        nd = len(shape)
        return pl.BlockSpec((None,) + tuple(shape), lambda b, k, pt, idx: (b,) + (0,) * nd)

    vec = (G_NSA, ROWS_S, HEAD_DIM)
    assert TOP_N % GATHER_BLOCKS == 0
    n_pages = G_NSA * GATHER_BLOCKS
    grid_spec = pltpu.PrefetchScalarGridSpec(
        num_scalar_prefetch=2,
        grid=(dec_b, TOP_N // GATHER_BLOCKS),
        in_specs=[pl.BlockSpec(memory_space=pltpu.SMEM), per_b(vec)]
        + [page_spec(g, kk) for g in range(G_NSA) for kk in range(GATHER_BLOCKS)]
        + [per_b((8, HEAD_DIM)), per_b((G_NSA, 3, ROWS_S, HEAD_DIM)), per_b(vec), per_b(vec)],
        out_specs=per_b(vec),
        scratch_shapes=[pltpu.VMEM((G_NSA, ROWS_S, 1), F32), pltpu.VMEM((G_NSA, ROWS_S, 1), F32),
                        pltpu.VMEM((G_NSA, ROWS_S, HEAD_DIM), F32)],
    )
    return pl.pallas_call(
        body,
        grid_spec=grid_spec,
        out_shape=jax.ShapeDtypeStruct((dec_b,) + vec, F32),
        compiler_params=_cparams(2),
        name="nsa_gather",
    )(pt_flat, idx_flat, rel_bias, q_nsa, *([cache_nsa2d] * n_pages), new_s, gs, oc, ow)


def _pad_rows(x, rows):
    return jnp.pad(x, ((0, rows - x.shape[0]),) + ((0, 0),) * (x.ndim - 1))


def _sample_inproj(xs, wt, p, dec_b):
    rows = xs.shape[0]
    zb, sb_rows, nsa_rows, win_rows, gates = _inproj(
        xs, p[0], wt["w_a"], wt["w_b"], wt["w_gate"], wt["gain"], rows, rows)
    zb8 = zb[:dec_b]
    q_sb = zb8[:, QS_BLK * LANE:KSB_BLK * LANE].reshape(dec_b, H_SB, HEAD_DIM)
    q_sb = jnp.pad(q_sb, ((0, 0), (0, ROWS_S - H_SB), (0, 0)))
    return q_sb, (zb8, sb_rows[:dec_b], nsa_rows[:dec_b], win_rows[:dec_b], gates)


def _sample_layer(l, xs, projected, walked, wt, p, sb_tabs, rel_bias, pt_flat, cache_nsa2d, win_l,
                  w_cmp0, dec_b, past_len, page):
    na, bk, bv, gk0, go, nf = p
    sbc, sbw = sb_tabs
    w_buf = win_l.shape[1]
    rows = xs.shape[0]
    zb8, sb_rows, nsa_rows, win_rows, gates = projected
    o_sb, kcr, vcr = walked
    o_sb = o_sb.reshape(dec_b, D_SB)

    q_nsa = zb8[:, :D_NSA].reshape(dec_b, G_NSA, HPG, HEAD_DIM)
    q_nsa = jnp.pad(q_nsa, ((0, 0), (0, 0), (0, ROWS_S - HPG), (0, 0)))
    new_c = jnp.pad(nsa_rows[:, :2 * D_KV].reshape(dec_b, 2 * G_NSA, HEAD_DIM),
                    ((0, 0), (0, 8 - 2 * G_NSA), (0, 0)))
    new_s = jnp.pad(nsa_rows[:, 2 * D_KV:].reshape(dec_b, 2 * G_NSA, HEAD_DIM),
                    ((0, 0), (0, 8 - 2 * G_NSA), (0, 0)))
    new_w = win_rows.reshape(dec_b, 1, 2, G_NSA, HEAD_DIM)
    win_all = jnp.concatenate([win_l, new_w], axis=1)
    wp = sbw.shape[-1]
    win_t = jnp.transpose(win_all, (2, 0, 3, 1, 4))
    win_t = jnp.pad(win_t, ((0, 0), (0, 0), (0, 0), (0, wp - w_buf - 1), (0, 0))).astype(BF16)
    idx, oc, ow = _nsa_select(q_nsa, kcr, vcr, new_c, w_cmp0[0], w_cmp0[1], bk, bv, gk0, sbc,
                              win_t[0], win_t[1], sbw, past_len, w_buf)
    gs = gates[:dec_b].reshape(dec_b, G_NSA, LANE)[:, :, :3 * HPG].reshape(dec_b, G_NSA, 3, HPG)
    gs = jnp.pad(gs, ((0, 0), (0, 0), (0, 0), (0, ROWS_S - HPG)))
    gs = jnp.broadcast_to(gs[..., None], gs.shape + (HEAD_DIM,))
    o_nsa = _nsa_gather(l, pt_flat, idx[:, :, :, 0].reshape(-1), rel_bias, q_nsa, cache_nsa2d,
                        new_s, gs, oc, ow, kcr.shape[2], past_len, page)
    o_nsa = o_nsa[:, :, :HPG].reshape(dec_b, D_NSA)

    x1 = _outproj(_pad_rows(o_nsa, rows), _pad_rows(o_sb, rows), go, wt["w_out"], xs)
    x2 = _ffn(x1, nf, wt["w_gate_ffn"], wt["w_up"], wt["w_down"])
    new_win = win_all[:, 1:]
    return x2, sb_rows, nsa_rows, new_win


def kernel(x_prompt, x_sample, cache_sb_kv, cache_nsa_kv, state_win_kv, page_table, rel_bias,
           norm_attn, w_in, g_q, g_k, w_cmp_k, b_cmp_k, w_cmp_v, b_cmp_v, g_out, w_out,
           norm_ffn, w_gate, w_up, w_down):
    batch, seq, _ = x_prompt.shape
    dec_b, dec_t, _ = x_sample.shape
    assert dec_t == 1
    depth, n_pool, page = cache_sb_kv.shape[:3]
    npg = page_table.shape[1]
    past_len = npg * page
    w_buf = state_win_kv.shape[2]
    nb_s = past_len // CMP_BLOCK + 1
    nbp_s = -(-(nb_s + 7) // LANE) * LANE
    wp_s = -(-(w_buf + 1) // LANE) * LANE

    toep, bias_c, sc, sw = _bias_tables(rel_bias, seq, past_len, w_buf, nbp_s, wp_s)
    sbc = jnp.pad(sc.reshape(G_NSA, HPG, nbp_s), ((0, 0), (0, ROWS_S - HPG), (0, 0)))
    sbw = jnp.pad(sw.reshape(G_NSA, HPG, wp_s), ((0, 0), (0, ROWS_S - HPG), (0, 0)))
    pt_flat = page_table.reshape(-1).astype(I32)
    cache_nsa2d = cache_nsa_kv.reshape(depth, n_pool, page * NSA_CH, HEAD_DIM)

    xp = x_prompt.reshape(batch * seq, D_MODEL)
    xs = _pad_rows(x_sample.reshape(dec_b, D_MODEL), ROWS_S)
    sb_p, sb_s, nsa_p, nsa_s, win_p, win_s = [], [], [], [], [], []
    w_keep = min(WINDOW, seq)
    for l in range(depth):
        wt = _prep_weights(l, w_in, g_q, g_k, w_cmp_k, w_cmp_v, w_out, w_gate, w_up, w_down)
        p = (norm_attn[l][None], b_cmp_k[l][None], b_cmp_v[l][None], g_k[l, 0][None],
             g_out[l][None], norm_ffn[l][None])
        q_sb, projected = _sample_inproj(xs, wt, p, dec_b)
        xp, sb_rows, nsa_rows, win_rows, walked = _prompt_layer(
            xp, wt, p, (toep, bias_c), rel_bias, batch, seq, (l, pt_flat, q_sb, cache_sb_kv),
            (l, pt_flat, cache_nsa2d, wt["w_kv"], dec_b, page))
        sb_p.append(sb_rows.reshape(batch, seq, 2, H_SB, HEAD_DIM))
        nsa_p.append(nsa_rows.reshape(batch, seq, 4, G_NSA, HEAD_DIM))
        win_p.append(win_rows.reshape(batch, w_keep, 2, G_NSA, HEAD_DIM))

        w_cmp0 = (w_cmp_k[l, 0].astype(BF16), w_cmp_v[l, 0].astype(BF16))
        xs, sb_rows, nsa_rows, new_win = _sample_layer(
            l, xs, projected, walked, wt, p, (sbc, sbw), rel_bias, pt_flat, cache_nsa2d,
            state_win_kv[l], w_cmp0, dec_b, past_len, page)
        sb_s.append(sb_rows.reshape(dec_b, 1, 2, H_SB, HEAD_DIM))
        nsa_s.append(nsa_rows.reshape(dec_b, 1, 4, G_NSA, HEAD_DIM))
        win_s.append(new_win)
    return (xp.reshape(batch, seq, D_MODEL), xs[:dec_b].reshape(dec_b, 1, D_MODEL),
            jnp.stack(sb_p), jnp.stack(sb_s), jnp.stack(nsa_p), jnp.stack(nsa_s),
            jnp.stack(win_p), jnp.stack(win_s))
```

```python
import functools
import math

import jax
import jax.numpy as jnp
from jax import lax
from jax.experimental import pallas as pl
from jax.experimental.pallas import tpu as pltpu

F32 = jnp.float32
BF16 = jnp.bfloat16
I32 = jnp.int32

LANE = 128
HEAD_DIM = 128
G_NSA = 2
HPG = 4
H_NSA = G_NSA * HPG
H_SB = 8
D_NSA = H_NSA * HEAD_DIM
D_SB = H_SB * HEAD_DIM
D_MODEL = D_NSA + D_SB
D_KV = G_NSA * HEAD_DIM
CMP_BLOCK = 64
TOP_N = 16
N_LOCAL = 2
WINDOW = 512
N_BUCKETS = 32
MAX_DISTANCE = 128
EPS = 1e-6
FORCE_SCORE = 1e4
NEG_INF = -1e30
SCALE = HEAD_DIM ** -0.5

COL_TILE = 512
N_MAIN = D_NSA + 6 * D_KV + 3 * D_SB
QS_BLK = (D_NSA + 6 * D_KV) // LANE
KSB_BLK = QS_BLK + D_SB // LANE
VSB_BLK = KSB_BLK + D_SB // LANE
KS_BLK = (D_NSA + 2 * D_KV) // LANE
VS_BLK = KS_BLK + G_NSA
KW_BLK = VS_BLK + G_NSA
VW_BLK = KW_BLK + G_NSA
VMEM_LIMIT = 56 * 1024 * 1024

SB_TQ = 256
SB_TK = 256
SB_HB = 8
NSA_TQ = 256
SB_PAGES_PER_STEP = 16
CMP_PAGES_PER_STEP = 16


VMEM_LIMIT_FUSED = 62 * 1024 * 1024


def _cparams(n_axes, vmem_limit=VMEM_LIMIT):
    return pltpu.CompilerParams(dimension_semantics=("arbitrary",) * n_axes,
                                vmem_limit_bytes=vmem_limit)


def _dot(a, b):
    return jnp.dot(a, b, preferred_element_type=F32)


def _dot_nt(a, b):
    return lax.dot_general(a, b, (((1,), (1,)), ((), ())), preferred_element_type=F32)


def _split_bf16(x):
    hi = x.astype(BF16)
    lo = (x - hi.astype(F32)).astype(BF16)
    return hi, lo


def _dot_split(x, u):
    hi, lo = _split_bf16(x)
    return _dot(hi, u) + _dot(lo, u)


def _rms(y, gain):
    ms = jnp.mean(y * y, axis=-1, keepdims=True)
    return y * lax.rsqrt(ms + EPS) * gain


def _neg_softplus(z):
    return -(jnp.maximum(z, 0.0) + jnp.log(1.0 + jnp.exp(-jnp.abs(z))))


def _t5_bucket(dist):
    n = jnp.maximum(dist, 0)
    max_exact = N_BUCKETS // 2
    nf = jnp.maximum(n, 1).astype(F32)
    large = max_exact + (jnp.log(nf / max_exact) / math.log(MAX_DISTANCE / max_exact)
                         * (N_BUCKETS - max_exact)).astype(I32)
    large = jnp.minimum(large, N_BUCKETS - 1)
    return jnp.where(n < max_exact, n, large)


def _pick_tile(m, candidates):
    for t in candidates:
        if m % t == 0:
            return t
    raise ValueError(f"no tile for {m}")


def _bias_body(rb_ref, toep_ref, bc_ref, sc_ref, sw_ref, *, seq, past_len, w_buf):
    h = pl.program_id(0)
    tq = NSA_TQ

    def table(dist):
        bk = _t5_bucket(dist)
        out = jnp.zeros(dist.shape, F32)
        for b in range(N_BUCKETS):
            out = jnp.where(bk == b, rb_ref[b, h], out)
        return out

    k = lax.broadcasted_iota(I32, (tq, tq), 0)
    q = lax.broadcasted_iota(I32, (tq, tq), 1)
    toep_ref[0] = table(tq + q - k)
    toep_ref[1] = table(q - k)
    n = lax.broadcasted_iota(I32, (LANE, tq), 0)
    qc = lax.broadcasted_iota(I32, (LANE, tq), 1)
    for i in range(seq // tq):
        bc_ref[i] = table(i * tq + qc - CMP_BLOCK * n - (CMP_BLOCK - 1))
    n1 = lax.broadcasted_iota(I32, sc_ref.shape, 1)
    sc_ref[...] = table(past_len - CMP_BLOCK * n1 - (CMP_BLOCK - 1))
    j1 = lax.broadcasted_iota(I32, sw_ref.shape, 1)
    sw_ref[...] = table(w_buf - j1)


def _bias_tables(rel_bias, seq, past_len, w_buf, nbp_s, wp_s):
    body = functools.partial(_bias_body, seq=seq, past_len=past_len, w_buf=w_buf)
    nq = seq // NSA_TQ
    return pl.pallas_call(
        body,
        grid=(H_NSA,),
        in_specs=[pl.BlockSpec(memory_space=pltpu.SMEM)],
        out_specs=[
            pl.BlockSpec((2, NSA_TQ, NSA_TQ), lambda h: (0, 0, h)),
            pl.BlockSpec((nq, LANE, NSA_TQ), lambda h: (0, 0, h)),
            pl.BlockSpec((None, 1, nbp_s), lambda h: (h, 0, 0)),
            pl.BlockSpec((None, 1, wp_s), lambda h: (h, 0, 0)),
        ],
        out_shape=[
            jax.ShapeDtypeStruct((2, NSA_TQ, H_NSA * NSA_TQ), F32),
            jax.ShapeDtypeStruct((nq, LANE, H_NSA * NSA_TQ), F32),
            jax.ShapeDtypeStruct((H_NSA, 1, nbp_s), F32),
            jax.ShapeDtypeStruct((H_NSA, 1, wp_s), F32),
        ],
        compiler_params=_cparams(1),
        name="bias_tables",
    )(rel_bias)


Q_TILES = D_NSA // COL_TILE
NSA_TILE0 = Q_TILES
WIN_TILE = NSA_TILE0 + 4 * D_KV // COL_TILE
SB_TILE0 = WIN_TILE + 2 * D_KV // COL_TILE + D_SB // COL_TILE


def _inproj_body(x_ref, na_ref, wa_ref, wb_ref, wg_ref, gain_ref, zb_ref, sb_ref, nsa_ref, win_ref,
                 gate_ref, h_ref, *, w_keep):
    j = pl.program_id(1)
    tm = x_ref.shape[0]
    half = COL_TILE // 2

    @pl.when(j == 0)
    def _():
        h = _rms(x_ref[...], na_ref[...]).astype(BF16)
        h_ref[...] = h
        gate_ref[...] = jax.nn.sigmoid(_dot(h, wg_ref[...]))

    gain = gain_ref[...]
    chunks = COL_TILE // LANE

    def emit(w_ref, norm_chunks, f32_ref=None, first=0, per_row=1, row0=0):
        h = h_ref[...]
        accs = [_dot(h, w_ref[:, :half]), _dot(h, w_ref[:, half:])]
        for c in range(chunks):
            sl = slice(c * LANE, (c + 1) * LANE)
            y = accs[c * LANE // half][:, (c * LANE) % half:(c * LANE) % half + LANE]
            if c in norm_chunks:
                y = _rms(y, gain[:, sl])
            zb_ref[:, sl] = y.astype(BF16)
            if f32_ref is not None:
                f32_ref[pl.ds(first + c, tm - row0, stride=per_row), :] = y[row0:, :]

    @pl.when(j < Q_TILES)
    def _():
        emit(wa_ref, (0, 1, 2, 3))

    @pl.when(j == NSA_TILE0)
    def _():
        emit(wa_ref, (), nsa_ref, 0, 2 * chunks)

    @pl.when(j == NSA_TILE0 + 1)
    def _():
        emit(wa_ref, (0, 1), nsa_ref, chunks, 2 * chunks)

    @pl.when(j == WIN_TILE)
    def _():
        emit(wa_ref, (0, 1), win_ref, 0, chunks, tm - w_keep)

    @pl.when((j > WIN_TILE) & (j < SB_TILE0))
    def _():
        emit(wb_ref, ())

    @pl.when(j >= SB_TILE0)
    def _():
        emit(wb_ref, (), sb_ref, (j - SB_TILE0) * chunks, 2 * H_SB)


def _inproj(x2d, na, w_a, w_b, w_gate, gain_cols, seq, w_keep):
    m = x2d.shape[0]
    tm = _pick_tile(seq, (1024, 512, 256, 128, 16))
    assert w_keep <= tm
    tpb = seq // tm
    sb_per, nsa_per, win_per = 2 * H_SB, 4 * G_NSA, 2 * G_NSA
    n_a = w_a.shape[1] // COL_TILE
    n_b = w_b.shape[1] // COL_TILE
    assert n_a == WIN_TILE + 1 and n_a + n_b == N_MAIN // COL_TILE
    body = functools.partial(_inproj_body, w_keep=w_keep)
    return pl.pallas_call(
        body,
        grid=(m // tm, n_a + n_b),
        in_specs=[
            pl.BlockSpec((tm, D_MODEL), lambda i, j: (i, 0), pipeline_mode=pl.Buffered(1)),
            pl.BlockSpec((1, D_MODEL), lambda i, j: (0, 0)),
            pl.BlockSpec((D_MODEL, COL_TILE), lambda i, j: (0, jnp.minimum(j, n_a - 1))),
            pl.BlockSpec((D_MODEL, COL_TILE), lambda i, j: (0, jnp.maximum(j - n_a, 0))),
            pl.BlockSpec((D_MODEL, 2 * LANE), lambda i, j: (0, 0)),
            pl.BlockSpec((1, COL_TILE), lambda i, j: (0, j)),
        ],
        out_specs=[
            pl.BlockSpec((tm, COL_TILE), lambda i, j: (i, j)),
            pl.BlockSpec((tm * sb_per, LANE), lambda i, j: (i, 0)),
            pl.BlockSpec((tm * nsa_per, LANE), lambda i, j: (i, 0)),
            pl.BlockSpec((w_keep * win_per, LANE), lambda i, j: (i // tpb, 0)),
            pl.BlockSpec((tm, 2 * LANE), lambda i, j: (i, 0)),
        ],
        out_shape=[
            jax.ShapeDtypeStruct((m, N_MAIN), BF16),
            jax.ShapeDtypeStruct((m * sb_per, LANE), F32),
            jax.ShapeDtypeStruct((m * nsa_per, LANE), F32),
            jax.ShapeDtypeStruct((m // seq * w_keep * win_per, LANE), F32),
            jax.ShapeDtypeStruct((m, 2 * LANE), F32),
        ],
        scratch_shapes=[pltpu.VMEM((tm, D_MODEL), BF16)],
        compiler_params=_cparams(2),
        name="inproj",
    )(x2d, na, w_a, w_b, w_gate, gain_cols)


def _outproj_body(on_ref, os_ref, go_ref, w_ref, x_ref, y_ref, h_ref):
    j = pl.program_id(1)

    @pl.when(j == 0)
    def _():
        go = go_ref[...]
        h_ref[:, :D_NSA] = _rms(on_ref[...], go[:, :D_NSA]).astype(BF16)
        h_ref[:, D_NSA:] = _rms(os_ref[...], go[:, D_NSA:]).astype(BF16)

    y_ref[...] = x_ref[...] + _dot(h_ref[...], w_ref[...])


def _outproj(o_nsa, o_sb, g_out, w_out, x2d):
    m = x2d.shape[0]
    tm = _pick_tile(m, (1024, 512, 256, 128, 16))
    return pl.pallas_call(
        _outproj_body,
        grid=(m // tm, D_MODEL // COL_TILE),
        in_specs=[
            pl.BlockSpec((tm, D_NSA), lambda i, j: (i, 0)),
            pl.BlockSpec((tm, D_SB), lambda i, j: (i, 0)),
            pl.BlockSpec((1, D_MODEL), lambda i, j: (0, 0)),
            pl.BlockSpec((D_MODEL, COL_TILE), lambda i, j: (0, j)),
            pl.BlockSpec((tm, COL_TILE), lambda i, j: (i, j)),
        ],
        out_specs=pl.BlockSpec((tm, COL_TILE), lambda i, j: (i, j)),
        out_shape=jax.ShapeDtypeStruct((m, D_MODEL), F32),
        scratch_shapes=[pltpu.VMEM((tm, D_MODEL), BF16)],
        compiler_params=_cparams(2),
        name="outproj",
    )(o_nsa, o_sb, g_out, w_out, x2d)


def _ffn_body(*refs, sb):
    if sb is None:
        x_ref, nf_ref, wg_ref, wu_ref, wd_ref, y_ref, h_ref = refs
    else:
        pps = sb["pps"]
        x_ref, nf_ref, wg_ref, wu_ref, wd_ref, q_ref = refs[1:7]
        page_refs = refs[7:7 + pps]
        up_ref, upp_ref, y_ref, o_ref, h_ref, z_ref, sacc_ref = refs[7 + pps:]
    i = pl.program_id(0)
    j = pl.program_id(1)

    @pl.when(j == 0)
    def _():
        x = x_ref[...]
        h_ref[...] = _rms(x, nf_ref[...]).astype(BF16)
        y_ref[...] = x

    h = h_ref[...]
    act = jax.nn.silu(_dot(h, wg_ref[...])) * _dot(h, wu_ref[...])
    y_ref[...] += _dot(act.astype(BF16), wd_ref[...])

    if sb is not None:
        nc = sb["n_chunks"]
        s = i * pl.num_programs(1) + j

        @pl.when(s < sb["n_steps"])
        def _():
            _sb_dec_step((s // nc) % 2, s % nc, nc - 1, q_ref, page_refs, up_ref, upp_ref, o_ref,
                         z_ref, sacc_ref, sb["page"])


def _ffn(x2d, norm_ffn, w_gate, w_up, w_down, sb_args=None):
    m = x2d.shape[0]
    d_ff = w_gate.shape[1]
    if sb_args is None:
        tm = _pick_tile(m, (512, 256, 128, 16))
        tf = _pick_tile(d_ff, (512, 256))
        x_mode = None
    else:
        tm = _pick_tile(m, (1024, 512, 256, 128))
        tf = _pick_tile(d_ff, (256,))
        x_mode = pl.Buffered(1)
    nf = d_ff // tf
    in_specs = [
        pl.BlockSpec((tm, D_MODEL), lambda i, j, *_: (i, 0), pipeline_mode=x_mode),
        pl.BlockSpec((1, D_MODEL), lambda i, j, *_: (0, 0)),
        pl.BlockSpec((D_MODEL, tf), lambda i, j, *_: (0, j)),
        pl.BlockSpec((D_MODEL, tf), lambda i, j, *_: (0, j)),
        pl.BlockSpec((tf, D_MODEL), lambda i, j, *_: (j, 0)),
    ]
    y_spec = pl.BlockSpec((tm, D_MODEL), lambda i, j, *_: (i, 0))
    y_shape = jax.ShapeDtypeStruct((m, D_MODEL), F32)
    h_scratch = pltpu.VMEM((tm, D_MODEL), BF16)
    if sb_args is None:
        return pl.pallas_call(
            functools.partial(_ffn_body, sb=None),
            grid=(m // tm, nf),
            in_specs=in_specs,
            out_specs=y_spec,
            out_shape=y_shape,
            scratch_shapes=[h_scratch],
            compiler_params=_cparams(2),
            name="ffn",
        )(x2d, norm_ffn, w_gate, w_up, w_down)

    l, pt_flat, q_sb, cache_sb = sb_args
    dec_b = q_sb.shape[0]
    page = cache_sb.shape[2]
    npg = pt_flat.shape[0] // dec_b
    pps = SB_PAGES_PER_STEP
    nc = npg // pps
    n_steps = dec_b * 2 * nc
    assert n_steps <= (m // tm) * nf
    width = page * H_SB
    up, upp = _sb_dec_consts(npg, page)

    def pos(i, j):
        s = jnp.minimum(i * nf + j, n_steps - 1)
        return s // (2 * nc), (s // nc) % 2, s % nc

    def page_spec(u):
        def imap(i, j, pt):
            b, ph, c = pos(i, j)
            return (l, pt[b * npg + c * pps + u], 0, ph, 0, 0)
        return pl.BlockSpec((None, None, page, None, H_SB, HEAD_DIM), imap)

    once = pl.Buffered(1)
    grid_spec = pltpu.PrefetchScalarGridSpec(
        num_scalar_prefetch=1,
        grid=(m // tm, nf),
        in_specs=in_specs
        + [pl.BlockSpec((None, ROWS_S, HEAD_DIM), lambda i, j, pt: (pos(i, j)[0], 0, 0))]
        + [page_spec(u) for u in range(pps)]
        + [pl.BlockSpec((width, 2 * width), lambda i, j, pt: (0, 0), pipeline_mode=once),
           pl.BlockSpec((npg, npg), lambda i, j, pt: (0, 0), pipeline_mode=once)],
        out_specs=[y_spec,
                   pl.BlockSpec((None, H_SB, HEAD_DIM), lambda i, j, pt: (pos(i, j)[0], 0, 0))],
        scratch_shapes=[h_scratch, pltpu.VMEM((npg, width), F32),
                        pltpu.VMEM((ROWS_S, HEAD_DIM), F32)],
    )
    sb = dict(pps=pps, page=page, n_chunks=nc, n_steps=n_steps)
    return pl.pallas_call(
        functools.partial(_ffn_body, sb=sb),
        grid_spec=grid_spec,
        out_shape=[y_shape, jax.ShapeDtypeStruct((dec_b, H_SB, HEAD_DIM), F32)],
        compiler_params=_cparams(2, VMEM_LIMIT_FUSED),
        name="ffn_sb_decode",
    )(pt_flat, x2d, norm_ffn, w_gate, w_up, w_down, q_sb, *([cache_sb] * pps), up, upp)


SB_SUB = 4


def _sb_body(*refs):
    n_sub = SB_HB // SB_SUB
    q_refs, k_refs, v_refs = refs[:n_sub], refs[n_sub:2 * n_sub], refs[2 * n_sub:3 * n_sub]
    u_ref, o_ref, acc_ref, run_ref = refs[3 * n_sub:]
    i = pl.program_id(2)
    row = lax.broadcasted_iota(I32, (SB_TQ, SB_TK), 0)
    col = lax.broadcasted_iota(I32, (SB_TQ, SB_TK), 1)
    strict = col < row

    def tile(kb, diagonal):
        off = pl.multiple_of(kb * SB_TK, SB_TK)
        heads = range(SB_HB)
        sls = [slice(h % SB_SUB * HEAD_DIM, (h % SB_SUB + 1) * HEAD_DIM) for h in heads]
        q_ref = [q_refs[h // SB_SUB] for h in heads]
        k_ref = [k_refs[h // SB_SUB] for h in heads]
        v_ref = [v_refs[h // SB_SUB] for h in heads]
        z = [_dot_nt(q_ref[h][:, sls[h]], k_ref[h][pl.ds(off, SB_TK), sls[h]]) * SCALE
             for h in heads]
        log_1m = [_neg_softplus(zh) for zh in z]
        if diagonal:
            log_1m = [jnp.where(strict, lh, 0.0) for lh in log_1m]
        parts = [_split_bf16(lh) for lh in log_1m]
        u = u_ref[...]
        later = [_dot(hi, u) + _dot(lo, u) for hi, lo in parts]
        total = [jnp.sum(lh, axis=-1, keepdims=True) for lh in log_1m]
        if diagonal:
            a = [jnp.where(strict, jnp.exp(z[h] + log_1m[h] + later[h]), 0.0) for h in heads]
        else:
            run = [run_ref[h] for h in heads]
            a = [jnp.exp(z[h] + log_1m[h] + later[h] + run[h]) for h in heads]
        pv = [_dot(a[h].astype(BF16), v_ref[h][pl.ds(off, SB_TK), sls[h]]) for h in heads]
        for h in heads:
            if diagonal:
                acc_ref[h] = pv[h]
                run_ref[h] = total[h]
            else:
                acc_ref[h] += pv[h]
                run_ref[h] = run[h] + total[h]

    tile(i, True)

    def step(it, carry):
        tile(i - 1 - it, False)
        return carry

    lax.fori_loop(0, i, step, 0)
    for h in range(SB_HB):
        o_ref[:, h * HEAD_DIM:(h + 1) * HEAD_DIM] = acc_ref[h]


def _sb_prompt(zb, batch, seq):
    nq = seq // SB_TQ
    width = SB_HB * HEAD_DIM
    sub_w = SB_SUB * HEAD_DIM
    n_sub = SB_HB // SB_SUB
    assert QS_BLK % SB_SUB == 0 and KSB_BLK % SB_SUB == 0 and VSB_BLK % SB_SUB == 0
    r = lax.broadcasted_iota(I32, (SB_TK, SB_TK), 0)
    c = lax.broadcasted_iota(I32, (SB_TK, SB_TK), 1)
    u = (r > c).astype(BF16)

    def q_spec(n):
        return pl.BlockSpec((SB_TQ, sub_w),
                            lambda b, h, i: (b * nq + i, QS_BLK // SB_SUB + h * n_sub + n))

    def kv_spec(blk, n):
        return pl.BlockSpec((seq, sub_w), lambda b, h, i: (b, blk // SB_SUB + h * n_sub + n))

    subs = range(n_sub)
    return pl.pallas_call(
        _sb_body,
        grid=(batch, H_SB // SB_HB, nq),
        in_specs=[q_spec(n) for n in subs] + [kv_spec(KSB_BLK, n) for n in subs]
        + [kv_spec(VSB_BLK, n) for n in subs]
        + [pl.BlockSpec((SB_TK, SB_TK), lambda b, h, i: (0, 0))],
        out_specs=pl.BlockSpec((SB_TQ, width), lambda b, h, i: (b * nq + i, h)),
        out_shape=jax.ShapeDtypeStruct((batch * seq, D_SB), F32),
        scratch_shapes=[pltpu.VMEM((SB_HB, SB_TQ, HEAD_DIM), F32), pltpu.VMEM((SB_HB, SB_TQ, 1), F32)],
        compiler_params=_cparams(3),
        name="sb_prompt",
    )(*([zb] * (3 * n_sub)), u)


def _cmp_prompt_body(x_ref, wk_ref, wv_ref, bk_ref, bv_ref, gk_ref, kc_ref, vct_ref, *, nb):
    pad = jnp.zeros((LANE - nb, HEAD_DIM), F32)
    per = 4 * G_NSA
    for c in range(2 * G_NSA):
        rows = jnp.concatenate(
            [x_ref[pl.ds(l * per + c, nb, stride=CMP_BLOCK * per), :] for l in range(CMP_BLOCK)],
            axis=1)
        rows = rows.astype(BF16)
        if c < G_NSA:
            y = _rms(_dot(rows, wk_ref[...]) + bk_ref[...], gk_ref[...])
            kc_ref[c] = jnp.concatenate([y, pad], axis=0).astype(BF16)
        else:
            y = _dot(rows, wv_ref[...]) + bv_ref[...]
            vct_ref[c - G_NSA] = jnp.concatenate([y, pad], axis=0).T.astype(BF16)


def _cmp_prompt(nsa_rows, batch, seq, wk, wv, bk, bv, gk0):
    nb = seq // CMP_BLOCK
    assert nb % 8 == 0 and nb < LANE
    body = functools.partial(_cmp_prompt_body, nb=nb)
    shape = jax.ShapeDtypeStruct((batch, G_NSA, LANE, HEAD_DIM), BF16)

    return pl.pallas_call(
        body,
        grid=(batch,),
        in_specs=[
            pl.BlockSpec((seq * 4 * G_NSA, HEAD_DIM), lambda b: (b, 0)),
            pl.BlockSpec(wk.shape, lambda b: (0, 0)),
            pl.BlockSpec(wv.shape, lambda b: (0, 0)),
            pl.BlockSpec((1, HEAD_DIM), lambda b: (0, 0)),
            pl.BlockSpec((1, HEAD_DIM), lambda b: (0, 0)),
            pl.BlockSpec((1, HEAD_DIM), lambda b: (0, 0)),
        ],
        out_specs=[pl.BlockSpec((None, G_NSA, LANE, HEAD_DIM), lambda b: (b, 0, 0, 0))] * 2,
        out_shape=[shape, shape],
        compiler_params=_cparams(1),
        name="cmp_prompt",
    )(nsa_rows, wk, wv, bk, bv, gk0)


def _masked_softmax(s, valid):
    logit = jnp.where(valid, s, NEG_INF)
    e = jnp.exp(logit - jnp.max(logit, axis=-1, keepdims=True))
    return jnp.where(valid, e / jnp.sum(e, axis=-1, keepdims=True), 0.0)


class _Flash:
    def __init__(self, m_ref, l_ref, a_ref):
        self.m_ref, self.l_ref, self.a_ref = m_ref, l_ref, a_ref

    def first(self, logit, v):
        m = jnp.max(logit, axis=-1, keepdims=True)
        p = jnp.exp(logit - m)
        self.m_ref[...] = m
        self.l_ref[...] = jnp.sum(p, axis=-1, keepdims=True)
        self.a_ref[...] = _dot(p.astype(BF16), v)

    def update(self, logit, v):
        m_old = self.m_ref[...]
        m = jnp.maximum(m_old, jnp.max(logit, axis=-1, keepdims=True))
        alpha = jnp.exp(m_old - m)
        p = jnp.exp(logit - m)
        self.m_ref[...] = m
        self.l_ref[...] = alpha * self.l_ref[...] + jnp.sum(p, axis=-1, keepdims=True)
        self.a_ref[...] = alpha * self.a_ref[...] + _dot(p.astype(BF16), v)

    def result(self):
        return self.a_ref[...] / self.l_ref[...]


class _FlashT:
    def __init__(self, m_ref, l_ref, a_ref):
        self.m_ref, self.l_ref, self.a_ref = m_ref, l_ref, a_ref

    @staticmethod
    def step(states, logits, v_ts, first):
        ms = [jnp.max(lg, axis=0, keepdims=True) for lg in logits]
        if not first:
            olds = [st.m_ref[...] for st in states]
            ms = [jnp.maximum(o, m) for o, m in zip(olds, ms)]
            alphas = [jnp.exp(o - m) for o, m in zip(olds, ms)]
        ps = [jnp.exp(lg - m) for lg, m in zip(logits, ms)]
        sums = [jnp.sum(p, axis=0, keepdims=True) for p in ps]
        pvs = [_dot(v_t, p.astype(BF16)) for v_t, p in zip(v_ts, ps)]
        for n, st in enumerate(states):
            st.m_ref[...] = ms[n]
            if first:
                st.l_ref[...] = sums[n]
                st.a_ref[...] = pvs[n]
            else:
                st.l_ref[...] = alphas[n] * st.l_ref[...] + sums[n]
                st.a_ref[...] = alphas[n] * st.a_ref[...] + pvs[n]

    def update(self, logit, v_t):
        _FlashT.step([self], [logit], [v_t], False)

    def result(self):
        return self.a_ref[...] / self.l_ref[...]


def _nsa_body(pt_ref, rb_ref, q_ref, kc_ref, vct_ref, ks_ref, vs_ref, kw_ref, vw_ref, g_ref,
              toep_ref, bc_ref, et_ref, *rest, nb, seq, cmp):
    page_refs = rest[:cmp["pps"]]
    (wkv_ref, o_ref, kcr_ref, vcr_ref, vst_ref, vwt_ref, m_s, l_s, a_s, m_w, l_w, a_w, yk_ref,
     yv_ref) = rest[cmp["pps"]:]
    g = pl.program_id(1)
    i = pl.program_id(2)
    tq = NSA_TQ
    sub = tq // LANE
    width = HPG * tq
    q0 = i * tq
    step = (pl.program_id(0) * pl.num_programs(1) + g) * pl.num_programs(2) + i

    @pl.when(step < cmp["n_steps"])
    def _():
        _cmp_dec_step(page_refs, wkv_ref, kcr_ref, vcr_ref, yk_ref, yv_ref, cmp["page"])

    @pl.when(i == 0)
    def _():
        for c in range(seq // LANE):
            rows = slice(c * LANE, (c + 1) * LANE)
            cols = slice((c % sub) * LANE, (c % sub + 1) * LANE)
            vst_ref[c // sub, :, cols] = vs_ref[rows, :].astype(F32).T.astype(BF16)
            vwt_ref[c // sub, :, cols] = vw_ref[rows, :].astype(F32).T.astype(BF16)

    qs = jnp.concatenate([q_ref[:, hh * LANE:(hh + 1) * LANE] for hh in range(HPG)], axis=0)

    def q_off(rows):
        return lax.broadcasted_iota(I32, (rows, width), 1) % tq

    n_r = lax.broadcasted_iota(I32, (LANE, width), 0)
    s_c = _dot_nt(kc_ref[...], qs) * SCALE + bc_ref[...]
    valid_c = (CMP_BLOCK * n_r + (CMP_BLOCK - 1) <= q0 + q_off(LANE)) & (n_r < nb)
    logit = jnp.where(valid_c, s_c, NEG_INF)
    e = jnp.exp(logit - jnp.max(logit, axis=0, keepdims=True))
    p_c = jnp.where(valid_c, e / jnp.sum(e, axis=0, keepdims=True), 0.0)
    oc_t = _dot(vct_ref[...], p_c.astype(BF16))
    imp_t = p_c[:, 0:tq]
    for hh in range(1, HPG):
        imp_t = imp_t + p_c[:, hh * tq:(hh + 1) * tq]

    nbr = -(-nb // 8) * 8
    n_i = lax.broadcasted_iota(I32, (nbr, tq), 0)
    cur = (q0 + lax.broadcasted_iota(I32, (nbr, tq), 1)) // CMP_BLOCK
    forced = (n_i == 0) | (n_i >= cur - (N_LOCAL - 1))
    score = jnp.where(n_i > cur, -jnp.inf, jnp.where(forced, FORCE_SCORE, imp_t[0:nbr]))
    rank = jnp.zeros((nbr, tq), I32)
    for j in range(nb):
        sj = jnp.broadcast_to(score[j:j + 1, :], (nbr, tq))
        ahead = (sj > score) | ((sj == score) & (n_i > j))
        rank = rank + ahead.astype(I32)
    sel_t = ((rank < min(TOP_N, nb)) & (score > -jnp.inf)).astype(F32)
    if nbr < LANE:
        sel_t = jnp.concatenate([sel_t, jnp.zeros((LANE - nbr, tq), F32)], axis=0)
    sel_t = sel_t.astype(BF16)

    k_r = lax.broadcasted_iota(I32, (tq, width), 0)
    q_c = q_off(tq)
    causal = k_r <= q_c
    bias_far = jnp.concatenate(
        [jnp.full((1, tq), rb_ref[N_BUCKETS - 1, g * HPG + hh], F32) for hh in range(HPG)], axis=1)

    def chunk(ref, c):
        return ref[pl.ds(pl.multiple_of(c * tq, tq), tq), :]

    def scores(ref, c):
        return _dot_nt(chunk(ref, c), qs) * SCALE

    def selected(c):
        hit = _dot(chunk(et_ref, c), sel_t)
        return jnp.concatenate([hit] * HPG, axis=1) > 0.5

    fs = _FlashT(m_s, l_s, a_s)
    fw = _FlashT(m_w, l_w, a_w)

    def both(c, bias, sel_extra, win_mask, first):
        sel_mask = selected(c) if sel_extra is None else selected(c) & sel_extra
        lg_s = jnp.where(sel_mask, scores(ks_ref, c) + bias, NEG_INF)
        lg_w = scores(kw_ref, c) + bias
        if win_mask is not None:
            lg_w = jnp.where(win_mask, lg_w, NEG_INF)
        _FlashT.step([fs, fw], [lg_s, lg_w], [vst_ref[c], vwt_ref[c]], first)

    both(i, toep_ref[1], causal, causal, True)

    @pl.when(i >= 1)
    def _():
        both(i - 1, toep_ref[0], None, None, False)

    @pl.when(i >= 2)
    def _():
        both(i - 2, bias_far, None, k_r >= q_c, False)

    def far_step(c, carry):
        fs.update(jnp.where(selected(c), scores(ks_ref, c) + bias_far, NEG_INF), vst_ref[c])
        return carry

    lax.fori_loop(0, jnp.maximum(i - 2, 0), far_step, 0)

    g_t = jnp.concatenate([g_ref[t * LANE:(t + 1) * LANE, :].T for t in range(sub)], axis=1)

    def gate(branch):
        return jnp.concatenate(
            [g_t[branch * HPG + hh:branch * HPG + hh + 1, :] for hh in range(HPG)], axis=1)

    o_t = gate(0) * oc_t + gate(1) * fs.result() + gate(2) * fw.result()
    for hh in range(HPG):
        for t in range(sub):
            lanes = slice(hh * tq + t * LANE, hh * tq + (t + 1) * LANE)
            o_ref[t * LANE:(t + 1) * LANE, hh * LANE:(hh + 1) * LANE] = o_t[:, lanes].T


def _nsa_prompt(rel_bias, zb, kc, vct, gates, toep, bias_c, batch, seq, cmp_args):
    tq = NSA_TQ
    assert WINDOW == 2 * tq and tq % LANE == 0 and seq % tq == 0
    nq = seq // tq
    nb = seq // CMP_BLOCK
    assert nb <= LANE
    width = HPG * tq
    k = lax.broadcasted_iota(I32, (seq, LANE), 0)
    n = lax.broadcasted_iota(I32, (seq, LANE), 1)
    expand_t = (k // CMP_BLOCK == n).astype(BF16)

    l, pt_flat, cache_nsa2d, w_kv, dec_b, page = cmp_args
    npg = pt_flat.shape[0] // dec_b
    pps = min(CMP_PAGES_PER_STEP, npg)
    assert npg % pps == 0
    nchunk = npg // pps
    n_cmp = dec_b * nchunk
    assert n_cmp <= batch * G_NSA * nq
    halves = page // CMP_BLOCK
    nblk = pps * halves
    body = functools.partial(_nsa_body, nb=nb, seq=seq,
                             cmp=dict(pps=pps, page=page, n_steps=n_cmp))

    def kv_spec(blk):
        return pl.BlockSpec((seq, HEAD_DIM), lambda b, g, i, pt: (b, blk + g))

    def cmp_pos(b, g, i):
        s = jnp.minimum((b * G_NSA + g) * nq + i, n_cmp - 1)
        return s // nchunk, s % nchunk

    def page_spec(u):
        def imap(b, g, i, pt):
            row, c = cmp_pos(b, g, i)
            return (l, pt[row * npg + c * pps + u], 0, 0)
        return pl.BlockSpec((None, None, page * NSA_CH, HEAD_DIM), imap)

    def raw_imap(b, g, i, pt):
        row, c = cmp_pos(b, g, i)
        return (row, 0, c, 0)

    raw_spec = pl.BlockSpec((None, G_NSA, nblk, HEAD_DIM), raw_imap)
    raw_shape = jax.ShapeDtypeStruct((dec_b, G_NSA, npg * halves, HEAD_DIM), F32)
    stat = pltpu.VMEM((1, width), F32)
    accum = pltpu.VMEM((HEAD_DIM, width), F32)
    v_t = pltpu.VMEM((nq, HEAD_DIM, tq), BF16)
    y_raw = pltpu.VMEM((nblk * NSA_CH, HEAD_DIM), F32)
    grid_spec = pltpu.PrefetchScalarGridSpec(
        num_scalar_prefetch=1,
        grid=(batch, G_NSA, nq),
        in_specs=[
            pl.BlockSpec(memory_space=pltpu.SMEM),
            pl.BlockSpec((tq, HPG * HEAD_DIM), lambda b, g, i, pt: (b * nq + i, g)),
            pl.BlockSpec((None, None, LANE, HEAD_DIM), lambda b, g, i, pt: (b, g, 0, 0)),
            pl.BlockSpec((None, None, HEAD_DIM, LANE), lambda b, g, i, pt: (b, g, 0, 0)),
            kv_spec(KS_BLK), kv_spec(VS_BLK), kv_spec(KW_BLK), kv_spec(VW_BLK),
            pl.BlockSpec((tq, LANE), lambda b, g, i, pt: (b * nq + i, g)),
            pl.BlockSpec((2, tq, width), lambda b, g, i, pt: (0, 0, g)),
            pl.BlockSpec((None, LANE, width), lambda b, g, i, pt: (i, 0, g)),
            pl.BlockSpec((seq, LANE), lambda b, g, i, pt: (0, 0)),
        ]
        + [page_spec(u) for u in range(pps)]
        + [pl.BlockSpec(w_kv.shape, lambda b, g, i, pt: (0, 0), pipeline_mode=pl.Buffered(1))],
        out_specs=[pl.BlockSpec((tq, HPG * HEAD_DIM), lambda b, g, i, pt: (b * nq + i, g)),
                   raw_spec, raw_spec],
        scratch_shapes=[v_t, v_t, stat, stat, accum, stat, stat, accum, y_raw, y_raw],
    )
    return pl.pallas_call(
        body,
        grid_spec=grid_spec,
        out_shape=[jax.ShapeDtypeStruct((batch * seq, D_NSA), F32), raw_shape, raw_shape],
        compiler_params=_cparams(3),
        name="nsa_prompt_cmp_decode",
    )(pt_flat, rel_bias, zb, kc, vct, zb, zb, zb, zb, gates, toep, bias_c, expand_t,
      *([cache_nsa2d] * pps), w_kv)


def _cast_body(w_ref, o_ref):
    o_ref[...] = w_ref[...].astype(BF16)


def _cast_layer_bf16(w, l, cols=None):
    _, rows, n = w.shape
    cols = n if cols is None else cols
    assert cols % LANE == 0
    tr = _pick_tile(rows, (256, 128))
    return pl.pallas_call(
        _cast_body,
        grid=(rows // tr,),
        in_specs=[pl.BlockSpec((None, tr, cols), lambda i: (l, i, 0))],
        out_specs=pl.BlockSpec((tr, cols), lambda i: (i, 0)),
        out_shape=jax.ShapeDtypeStruct((rows, cols), BF16),
        compiler_params=_cparams(1),
        name="cast_bf16",
    )(w)


def _prep_weights(l, w_in, g_q, g_k, w_cmp_k, w_cmp_v, w_out, w_gate, w_up, w_down):
    n_gate = 3 * H_NSA
    g0 = D_NSA + 6 * D_KV
    wi = w_in[l]
    w_a = wi[:, :g0].astype(BF16)
    w_b = wi[:, g0 + n_gate:].astype(BF16)
    wg = wi[:, g0:g0 + n_gate].reshape(D_MODEL, 3, G_NSA, HPG)
    wg = jnp.transpose(wg, (0, 2, 1, 3)).reshape(D_MODEL, G_NSA, 3 * HPG)
    wg = jnp.pad(wg, ((0, 0), (0, 0), (0, LANE - 3 * HPG))).reshape(D_MODEL, G_NSA * LANE)
    ones = jnp.ones((HEAD_DIM,), F32)
    gain = jnp.concatenate(
        [jnp.tile(g_q[l], H_NSA), jnp.tile(ones, 2 * G_NSA),
         jnp.tile(g_k[l, 1], G_NSA), jnp.tile(ones, G_NSA),
         jnp.tile(g_k[l, 2], G_NSA), jnp.tile(ones, G_NSA),
         jnp.tile(ones, 3 * H_SB)])[None, :]
    wk = w_cmp_k[l].reshape(CMP_BLOCK * HEAD_DIM, HEAD_DIM).astype(BF16)
    wv = w_cmp_v[l].reshape(CMP_BLOCK * HEAD_DIM, HEAD_DIM).astype(BF16)
    return dict(
        w_a=w_a, w_b=w_b, w_gate=wg.astype(BF16), gain=gain, wk=wk, wv=wv,
        w_kv=jnp.concatenate([wk, wv], axis=1),
        w_out=_cast_layer_bf16(w_out, l), w_gate_ffn=_cast_layer_bf16(w_gate, l),
        w_up=_cast_layer_bf16(w_up, l), w_down=_cast_layer_bf16(w_down, l))


def _prompt_layer(x2d, wt, p, tables, rel_bias, batch, seq, sb_args, cmp_args):
    na, bk, bv, gk0, go, nf = p
    zb, sb_rows, nsa_rows, win_rows, gates = _inproj(
        x2d, na, wt["w_a"], wt["w_b"], wt["w_gate"], wt["gain"], seq, min(WINDOW, seq))
    kc, vct = _cmp_prompt(nsa_rows, batch, seq, wt["wk"], wt["wv"], bk, bv, gk0)
    o_nsa, kcr, vcr = _nsa_prompt(rel_bias, zb, kc, vct, gates, tables[0], tables[1], batch, seq,
                                  cmp_args)
    o_sb = _sb_prompt(zb, batch, seq)
    x1 = _outproj(o_nsa, o_sb, go, wt["w_out"], x2d)
    x2, o_sb_sample = _ffn(x1, nf, wt["w_gate_ffn"], wt["w_up"], wt["w_down"], sb_args)
    return x2, sb_rows, nsa_rows, win_rows, (o_sb_sample, kcr, vcr)


ROWS_S = 16


def _sb_dec_step(ph, c, last_c, q_ref, page_refs, up_ref, upp_ref, o_ref, z_ref, acc_ref, page):
    pps = len(page_refs)
    width = page * H_SB
    r = lax.broadcasted_iota(I32, (ROWS_S, width), 0)
    lane = lax.broadcasted_iota(I32, (ROWS_S, width), 1)
    own_head = (lane % H_SB) == r

    @pl.when(ph == 0)
    def _():
        q = q_ref[...]
        for u in range(pps):
            ks = page_refs[u][...].reshape(width, HEAD_DIM).astype(BF16)
            zt = _dot_nt(q, ks)
            z_ref[pl.ds(c * pps + u, 1), :] = jnp.sum(jnp.where(own_head, zt, 0.0), axis=0,
                                                      keepdims=True)

    @pl.when((ph == 1) & (c == 0))
    def _():
        z = z_ref[...] * SCALE
        log_1m = _neg_softplus(z)
        cs = _dot_split(log_1m, up_ref[...])
        hi, lo = _split_bf16(cs[:, width:])
        later_pages = _dot(upp_ref[...], hi) + _dot(upp_ref[...], lo)
        z_ref[...] = jnp.exp(z + log_1m + cs[:, :width] + later_pages)
        acc_ref[...] = jnp.zeros_like(acc_ref)

    @pl.when(ph == 1)
    def _():
        for u in range(pps):
            vs = page_refs[u][...].reshape(width, HEAD_DIM).astype(BF16)
            a = jnp.broadcast_to(z_ref[pl.ds(c * pps + u, 1), :], (ROWS_S, width))
            acc_ref[...] += _dot(jnp.where(own_head, a, 0.0).astype(BF16), vs)

    @pl.when((ph == 1) & (c == last_c))
    def _():
        o_ref[...] = acc_ref[0:H_SB, :]


def _sb_dec_consts(npg, page):
    assert npg % SB_PAGES_PER_STEP == 0 and npg % 8 == 0
    width = page * H_SB
    r = lax.broadcasted_iota(I32, (width, 2 * width), 0)
    c = lax.broadcasted_iota(I32, (width, 2 * width), 1)
    same_head = (r % H_SB) == (c % H_SB)
    up = (same_head & ((c >= width) | (r // H_SB > c // H_SB))).astype(BF16)
    pr = lax.broadcasted_iota(I32, (npg, npg), 0)
    pc = lax.broadcasted_iota(I32, (npg, npg), 1)
    upp = (pc > pr).astype(BF16)
    return up, upp


NSA_CH = 4 * G_NSA


def _cmp_dec_step(page_refs, w_ref, kc_ref, vc_ref, yk_ref, yv_ref, page):
    pps = len(page_refs)
    halves = page // CMP_BLOCK
    nblk = pps * halves
    acc = jnp.zeros((nblk * NSA_CH, 2 * HEAD_DIM), F32)

    def token_rows(u, hf, l):
        return page_refs[u][pl.ds((hf * CMP_BLOCK + l) * NSA_CH, NSA_CH), :]

    for l in range(0, CMP_BLOCK, 2):
        x_l = jnp.concatenate(
            [jnp.concatenate([token_rows(u, hf, l), token_rows(u, hf, l + 1)], axis=1)
             for u in range(pps) for hf in range(halves)], axis=0).astype(BF16)
        acc = acc + _dot(x_l, w_ref[l * HEAD_DIM:(l + 2) * HEAD_DIM, :])
    yk_ref[...] = acc[:, :HEAD_DIM]
    yv_ref[...] = acc[:, HEAD_DIM:]
    for g in range(G_NSA):
        kc_ref[g] = yk_ref[pl.ds(g, nblk, stride=NSA_CH), :]
        vc_ref[g] = yv_ref[pl.ds(G_NSA + g, nblk, stride=NSA_CH), :]


def _nsa_sel_body(q_ref, kcr_ref, vcr_ref, new_ref, wk0_ref, wv0_ref, bk_ref, bv_ref, gk_ref,
                  sbc_ref, kw_ref, vw_ref, sbw_ref, idx_ref, oc_ref, ow_ref,
                  *, nb0, nbp, past_len, w_buf, wp):
    nb = nb0 + 1
    new = new_ref[...].astype(BF16)
    new_k = _rms(_dot(new, wk0_ref[...]) + bk_ref[...], gk_ref[...])
    new_v = _dot(new, wv0_ref[...]) + bv_ref[...]
    row8 = lax.broadcasted_iota(I32, (8, HEAD_DIM), 0)
    n1 = lax.broadcasted_iota(I32, (1, nbp), 1)
    nq = lax.broadcasted_iota(I32, (ROWS_S, nbp), 1)
    pos_q = past_len
    pad = jnp.zeros((nbp - nb0 - 8, HEAD_DIM), F32)
    for g in range(G_NSA):
        q = q_ref[g]
        kc = jnp.concatenate(
            [_rms(kcr_ref[g] + bk_ref[...], gk_ref[...]),
             jnp.where(row8 == 0, jnp.broadcast_to(new_k[g:g + 1], (8, HEAD_DIM)), 0.0), pad],
            axis=0).astype(BF16)
        vc = jnp.concatenate(
            [vcr_ref[g] + bv_ref[...],
             jnp.where(row8 == 0, jnp.broadcast_to(new_v[G_NSA + g:G_NSA + g + 1], (8, HEAD_DIM)),
                       0.0), pad], axis=0).astype(BF16)
        s_c = _dot_nt(q, kc) * SCALE + sbc_ref[g]
        valid_c = (CMP_BLOCK * nq + (CMP_BLOCK - 1) <= pos_q) & (nq < nb)
        p_c = _masked_softmax(s_c, valid_c)
        oc_ref[g] = _dot(p_c.astype(BF16), vc)
        imp = p_c[0:1]
        for hh in range(1, HPG):
            imp = imp + p_c[hh:hh + 1]

        cur = pos_q // CMP_BLOCK
        forced = (n1 == 0) | (n1 >= cur - (N_LOCAL - 1))
        score = jnp.where(n1 > cur, -jnp.inf, jnp.where(forced, FORCE_SCORE, imp))
        sq = jnp.broadcast_to(score, (LANE, nbp))
        col = jnp.concatenate([sq[:, k * LANE:(k + 1) * LANE].T for k in range(nbp // LANE)],
                              axis=0)
        j_i = lax.broadcasted_iota(I32, (nbp, LANE), 0)
        ranks = []
        for k in range(nbp // LANE):
            mine = jnp.broadcast_to(score[:, k * LANE:(k + 1) * LANE], (nbp, LANE))
            n_i = k * LANE + lax.broadcasted_iota(I32, (nbp, LANE), 1)
            ahead = (col > mine) | ((col == mine) & (j_i < n_i))
            ranks.append(jnp.sum(ahead.astype(F32), axis=0, keepdims=True))
        rank = jnp.concatenate(ranks, axis=1)
        slot = lax.broadcasted_iota(I32, (TOP_N, nbp), 0).astype(F32)
        hit = jnp.broadcast_to(rank, (TOP_N, nbp)) == slot
        ids = jnp.sum(jnp.where(hit, lax.broadcasted_iota(I32, (TOP_N, nbp), 1).astype(F32), 0.0),
                      axis=1, keepdims=True)
        idx_ref[g] = jnp.broadcast_to(ids, (TOP_N, LANE)).astype(I32)

        jw = lax.broadcasted_iota(I32, (ROWS_S, wp), 1)
        dist_w = w_buf - jw
        valid_w = (dist_w >= 0) & (dist_w <= WINDOW) & (past_len - w_buf + jw >= 0)
        s_w = _dot_nt(q, kw_ref[g]) * SCALE + sbw_ref[g]
        p_w = _masked_softmax(s_w, valid_w)
        ow_ref[g] = _dot(p_w.astype(BF16), vw_ref[g])


def _nsa_select(q_nsa, kcr, vcr, new_c, wk0, wv0, bk, bv, gk0, sbc, kw, vw, sbw, past_len, w_buf):
    dec_b, _, nb0, _ = kcr.shape
    nbp = sbc.shape[-1]
    wp = sbw.shape[-1]
    assert nb0 + 1 >= TOP_N and nbp >= nb0 + 8 and nb0 % 8 == 0
    body = functools.partial(_nsa_sel_body, nb0=nb0, nbp=nbp, past_len=past_len, w_buf=w_buf, wp=wp)

    def per_b(shape):
        nd = len(shape)
        return pl.BlockSpec((None,) + tuple(shape), lambda b: (b,) + (0,) * nd)

    def whole(arr):
        nd = arr.ndim
        return pl.BlockSpec(arr.shape, lambda b: (0,) * nd)

    vec = (G_NSA, ROWS_S, HEAD_DIM)
    return pl.pallas_call(
        body,
        grid=(dec_b,),
        in_specs=[per_b(vec), per_b((G_NSA, nb0, HEAD_DIM)), per_b((G_NSA, nb0, HEAD_DIM)),
                  per_b((8, HEAD_DIM)), whole(wk0), whole(wv0), whole(bk), whole(bv), whole(gk0),
                  whole(sbc), per_b((G_NSA, wp, HEAD_DIM)), per_b((G_NSA, wp, HEAD_DIM)), whole(sbw)],
        out_specs=[per_b((G_NSA, TOP_N, LANE)), per_b(vec), per_b(vec)],
        out_shape=[jax.ShapeDtypeStruct((dec_b, G_NSA, TOP_N, LANE), I32),
                   jax.ShapeDtypeStruct((dec_b,) + vec, F32),
                   jax.ShapeDtypeStruct((dec_b,) + vec, F32)],
        compiler_params=_cparams(1),
        name="nsa_select",
    )(q_nsa, kcr, vcr, new_c, wk0, wv0, bk, bv, gk0, sbc, kw, vw, sbw)


GATHER_BLOCKS = 8


def _nsa_gather_body(pt_ref, idx_ref, rb_ref, q_ref, *refs, nb0, past_len):
    n_pages = G_NSA * GATHER_BLOCKS
    page_refs = refs[:n_pages]
    new_ref, gs_ref, oc_ref, ow_ref, o_ref, m_ref, l_ref, a_ref = refs[n_pages:]
    b = pl.program_id(0)
    k = pl.program_id(1)
    pos_q = past_len
    keys = GATHER_BLOCKS * CMP_BLOCK
    row = lax.broadcasted_iota(I32, (CMP_BLOCK, HEAD_DIM), 0)
    l1 = lax.broadcasted_iota(I32, (1, CMP_BLOCK), 1)
    hrow = lax.broadcasted_iota(I32, (ROWS_S, keys), 0)
    for g in range(G_NSA):
        new_k = jnp.broadcast_to(new_ref[g:g + 1, :], (CMP_BLOCK, HEAD_DIM))
        new_v = jnp.broadcast_to(new_ref[G_NSA + g:G_NSA + g + 1, :], (CMP_BLOCK, HEAD_DIM))
        ks, vs, dist = [], [], []
        for kk in range(GATHER_BLOCKS):
            blk = idx_ref[(b * G_NSA + g) * TOP_N + k * GATHER_BLOCKS + kk]
            is_new = blk >= nb0
            page_ref = page_refs[g * GATHER_BLOCKS + kk]
            k_blk = page_ref[pl.ds(2 * G_NSA + g, CMP_BLOCK, stride=NSA_CH), :]
            v_blk = page_ref[pl.ds(3 * G_NSA + g, CMP_BLOCK, stride=NSA_CH), :]
            ks.append(jnp.where(is_new, jnp.where(row == 0, new_k, 0.0), k_blk).astype(BF16))
            vs.append(jnp.where(is_new, jnp.where(row == 0, new_v, 0.0), v_blk).astype(BF16))
            dist.append(pos_q - (blk * CMP_BLOCK + l1))
        ks = jnp.concatenate(ks, axis=0)
        vs = jnp.concatenate(vs, axis=0)
        dist = jnp.concatenate(dist, axis=1)
        bucket = jnp.broadcast_to(_t5_bucket(dist), (ROWS_S, keys))
        bias = jnp.zeros((ROWS_S, keys), F32)
        for hh in range(HPG):
            for bk in range(N_BUCKETS):
                bias = jnp.where((hrow == hh) & (bucket == bk), rb_ref[bk, g * HPG + hh], bias)
        s = _dot_nt(q_ref[g], ks) * SCALE + bias
        logit = jnp.where(jnp.broadcast_to(dist >= 0, (ROWS_S, keys)), s, NEG_INF)
        fl = _Flash(m_ref.at[g], l_ref.at[g], a_ref.at[g])

        @pl.when(k == 0)
        def _():
            fl.first(logit, vs)

        @pl.when(k > 0)
        def _():
            fl.update(logit, vs)

        @pl.when(k == pl.num_programs(1) - 1)
        def _():
            o_ref[g] = gs_ref[g, 0] * oc_ref[g] + gs_ref[g, 1] * fl.result() + gs_ref[g, 2] * ow_ref[g]


def _nsa_gather(l, pt_flat, idx_flat, rel_bias, q_nsa, cache_nsa2d, new_s, gs, oc, ow, nb0,
                past_len, page):
    dec_b = q_nsa.shape[0]
    npg = pt_flat.shape[0] // dec_b
    halves = page // CMP_BLOCK
    body = functools.partial(_nsa_gather_body, nb0=nb0, past_len=past_len)

    def page_spec(g, kk):
        def imap(b, k, pt, idx):
            blk = jnp.minimum(idx[(b * G_NSA + g) * TOP_N + k * GATHER_BLOCKS + kk], nb0 - 1)
            return (l, pt[b * npg + blk // halves], blk % halves, 0)
        return pl.BlockSpec((None, None, CMP_BLOCK * NSA_CH, HEAD_DIM), imap)

    def per_b(shape):
        nd = len(shape)
        return pl.BlockSpec((None,) + tuple(shape), lambda b, k, pt, idx: (b,) + (0,) * nd)

    vec = (G_NSA, ROWS_S, HEAD_DIM)
    assert TOP_N % GATHER_BLOCKS == 0
    n_pages = G_NSA * GATHER_BLOCKS
    grid_spec = pltpu.PrefetchScalarGridSpec(
        num_scalar_prefetch=2,
        grid=(dec_b, TOP_N // GATHER_BLOCKS),
        in_specs=[pl.BlockSpec(memory_space=pltpu.SMEM), per_b(vec)]
        + [page_spec(g, kk) for g in range(G_NSA) for kk in range(GATHER_BLOCKS)]
        + [per_b((8, HEAD_DIM)), per_b((G_NSA, 3, ROWS_S, HEAD_DIM)), per_b(vec), per_b(vec)],
        out_specs=per_b(vec),
        scratch_shapes=[pltpu.VMEM((G_NSA, ROWS_S, 1), F32), pltpu.VMEM((G_NSA, ROWS_S, 1), F32),
                        pltpu.VMEM((G_NSA, ROWS_S, HEAD_DIM), F32)],
    )
    return pl.pallas_call(
        body,
        grid_spec=grid_spec,
        out_shape=jax.ShapeDtypeStruct((dec_b,) + vec, F32),
        compiler_params=_cparams(2),
        name="nsa_gather",
    )(pt_flat, idx_flat, rel_bias, q_nsa, *([cache_nsa2d] * n_pages), new_s, gs, oc, ow)


def _pad_rows(x, rows):
    return jnp.pad(x, ((0, rows - x.shape[0]),) + ((0, 0),) * (x.ndim - 1))


def _sample_inproj(xs, wt, p, dec_b):
    rows = xs.shape[0]
    zb, sb_rows, nsa_rows, win_rows, gates = _inproj(
        xs, p[0], wt["w_a"], wt["w_b"], wt["w_gate"], wt["gain"], rows, rows)
    zb8 = zb[:dec_b]
    q_sb = zb8[:, QS_BLK * LANE:KSB_BLK * LANE].reshape(dec_b, H_SB, HEAD_DIM)
    q_sb = jnp.pad(q_sb, ((0, 0), (0, ROWS_S - H_SB), (0, 0)))
    keep = lambda a: a[:dec_b * (a.shape[0] // rows)]
    return q_sb, (zb8, keep(sb_rows), keep(nsa_rows), keep(win_rows), gates)


def _sample_layer(l, xs, projected, walked, wt, p, sb_tabs, rel_bias, pt_flat, cache_nsa2d, win_l,
                  w_cmp0, dec_b, past_len, page):
    na, bk, bv, gk0, go, nf = p
    sbc, sbw = sb_tabs
    w_buf = win_l.shape[1]
    rows = xs.shape[0]
    zb8, sb_rows, nsa_rows, win_rows, gates = projected
    o_sb, kcr, vcr = walked
    o_sb = o_sb.reshape(dec_b, D_SB)

    q_nsa = zb8[:, :D_NSA].reshape(dec_b, G_NSA, HPG, HEAD_DIM)
    q_nsa = jnp.pad(q_nsa, ((0, 0), (0, 0), (0, ROWS_S - HPG), (0, 0)))
    nsa_new = nsa_rows.reshape(dec_b, 4 * G_NSA, HEAD_DIM)
    new_c = jnp.pad(nsa_new[:, :2 * G_NSA], ((0, 0), (0, 8 - 2 * G_NSA), (0, 0)))
    new_s = jnp.pad(nsa_new[:, 2 * G_NSA:], ((0, 0), (0, 8 - 2 * G_NSA), (0, 0)))
    new_w = win_rows.reshape(dec_b, 1, 2, G_NSA, HEAD_DIM)
    win_all = jnp.concatenate([win_l, new_w], axis=1)
    wp = sbw.shape[-1]
    win_t = jnp.transpose(win_all, (2, 0, 3, 1, 4))
    win_t = jnp.pad(win_t, ((0, 0), (0, 0), (0, 0), (0, wp - w_buf - 1), (0, 0))).astype(BF16)
    idx, oc, ow = _nsa_select(q_nsa, kcr, vcr, new_c, w_cmp0[0], w_cmp0[1], bk, bv, gk0, sbc,
                              win_t[0], win_t[1], sbw, past_len, w_buf)
    gs = gates[:dec_b].reshape(dec_b, G_NSA, LANE)[:, :, :3 * HPG].reshape(dec_b, G_NSA, 3, HPG)
    gs = jnp.pad(gs, ((0, 0), (0, 0), (0, 0), (0, ROWS_S - HPG)))
    gs = jnp.broadcast_to(gs[..., None], gs.shape + (HEAD_DIM,))
    o_nsa = _nsa_gather(l, pt_flat, idx[:, :, :, 0].reshape(-1), rel_bias, q_nsa, cache_nsa2d,
                        new_s, gs, oc, ow, kcr.shape[2], past_len, page)
    o_nsa = o_nsa[:, :, :HPG].reshape(dec_b, D_NSA)

    x1 = _outproj(_pad_rows(o_nsa, rows), _pad_rows(o_sb, rows), go, wt["w_out"], xs)
    x2 = _ffn(x1, nf, wt["w_gate_ffn"], wt["w_up"], wt["w_down"])
    new_win = win_all[:, 1:]
    return x2, sb_rows, nsa_rows, new_win


def kernel(x_prompt, x_sample, cache_sb_kv, cache_nsa_kv, state_win_kv, page_table, rel_bias,
           norm_attn, w_in, g_q, g_k, w_cmp_k, b_cmp_k, w_cmp_v, b_cmp_v, g_out, w_out,
           norm_ffn, w_gate, w_up, w_down):
    batch, seq, _ = x_prompt.shape
    dec_b, dec_t, _ = x_sample.shape
    assert dec_t == 1
    depth, n_pool, page = cache_sb_kv.shape[:3]
    npg = page_table.shape[1]
    past_len = npg * page
    w_buf = state_win_kv.shape[2]
    nb_s = past_len // CMP_BLOCK + 1
    nbp_s = -(-(nb_s + 7) // LANE) * LANE
    wp_s = -(-(w_buf + 1) // LANE) * LANE

    toep, bias_c, sc, sw = _bias_tables(rel_bias, seq, past_len, w_buf, nbp_s, wp_s)
    sbc = jnp.pad(sc.reshape(G_NSA, HPG, nbp_s), ((0, 0), (0, ROWS_S - HPG), (0, 0)))
    sbw = jnp.pad(sw.reshape(G_NSA, HPG, wp_s), ((0, 0), (0, ROWS_S - HPG), (0, 0)))
    pt_flat = page_table.reshape(-1).astype(I32)
    cache_nsa2d = cache_nsa_kv.reshape(depth, n_pool, page * NSA_CH, HEAD_DIM)

    xp = x_prompt.reshape(batch * seq, D_MODEL)
    xs = _pad_rows(x_sample.reshape(dec_b, D_MODEL), ROWS_S)
    sb_p, sb_s, nsa_p, nsa_s, win_p, win_s = [], [], [], [], [], []
    w_keep = min(WINDOW, seq)
    for l in range(depth):
        wt = _prep_weights(l, w_in, g_q, g_k, w_cmp_k, w_cmp_v, w_out, w_gate, w_up, w_down)
        p = (norm_attn[l][None], b_cmp_k[l][None], b_cmp_v[l][None], g_k[l, 0][None],
             g_out[l][None], norm_ffn[l][None])
        q_sb, projected = _sample_inproj(xs, wt, p, dec_b)
        xp, sb_rows, nsa_rows, win_rows, walked = _prompt_layer(
            xp, wt, p, (toep, bias_c), rel_bias, batch, seq, (l, pt_flat, q_sb, cache_sb_kv),
            (l, pt_flat, cache_nsa2d, wt["w_kv"], dec_b, page))
        sb_p.append(sb_rows.reshape(batch, seq, 2, H_SB, HEAD_DIM))
        nsa_p.append(nsa_rows.reshape(batch, seq, 4, G_NSA, HEAD_DIM))
        win_p.append(win_rows.reshape(batch, w_keep, 2, G_NSA, HEAD_DIM))

        w_cmp0 = (w_cmp_k[l, 0].astype(BF16), w_cmp_v[l, 0].astype(BF16))
        xs, sb_rows, nsa_rows, new_win = _sample_layer(
            l, xs, projected, walked, wt, p, (sbc, sbw), rel_bias, pt_flat, cache_nsa2d,
            state_win_kv[l], w_cmp0, dec_b, past_len, page)
        sb_s.append(sb_rows.reshape(dec_b, 1, 2, H_SB, HEAD_DIM))
        nsa_s.append(nsa_rows.reshape(dec_b, 1, 4, G_NSA, HEAD_DIM))
        win_s.append(new_win)
    return (xp.reshape(batch, seq, D_MODEL), xs[:dec_b].reshape(dec_b, 1, D_MODEL),
            jnp.stack(sb_p), jnp.stack(sb_s), jnp.stack(nsa_p), jnp.stack(nsa_s),
            jnp.stack(win_p), jnp.stack(win_s))
```

```python
import functools
import math

import jax
import jax.numpy as jnp
from jax import lax
from jax.experimental import pallas as pl
from jax.experimental.pallas import tpu as pltpu

F32 = jnp.float32
BF16 = jnp.bfloat16
I32 = jnp.int32

LANE = 128
HEAD_DIM = 128
G_NSA = 2
HPG = 4
H_NSA = G_NSA * HPG
H_SB = 8
D_NSA = H_NSA * HEAD_DIM
D_SB = H_SB * HEAD_DIM
D_MODEL = D_NSA + D_SB
D_KV = G_NSA * HEAD_DIM
CMP_BLOCK = 64
TOP_N = 16
N_LOCAL = 2
WINDOW = 512
N_BUCKETS = 32
MAX_DISTANCE = 128
EPS = 1e-6
FORCE_SCORE = 1e4
NEG_INF = -1e30
SCALE = HEAD_DIM ** -0.5

COL_TILE = 512
N_MAIN = D_NSA + 6 * D_KV + 3 * D_SB
QS_BLK = (D_NSA + 6 * D_KV) // LANE
KSB_BLK = QS_BLK + D_SB // LANE
VSB_BLK = KSB_BLK + D_SB // LANE
KS_BLK = (D_NSA + 2 * D_KV) // LANE
VS_BLK = KS_BLK + G_NSA
KW_BLK = VS_BLK + G_NSA
VW_BLK = KW_BLK + G_NSA
VMEM_LIMIT = 56 * 1024 * 1024

SB_TQ = 256
SB_TK = 256
SB_HB = 8
NSA_TQ = 256
SB_PAGES_PER_STEP = 16
CMP_PAGES_PER_STEP = 16


VMEM_LIMIT_FUSED = 62 * 1024 * 1024


def _cparams(n_axes, vmem_limit=VMEM_LIMIT):
    return pltpu.CompilerParams(dimension_semantics=("arbitrary",) * n_axes,
                                vmem_limit_bytes=vmem_limit)


def _dot(a, b):
    return jnp.dot(a, b, preferred_element_type=F32)


def _dot_nt(a, b):
    return lax.dot_general(a, b, (((1,), (1,)), ((), ())), preferred_element_type=F32)


def _split_bf16(x):
    hi = x.astype(BF16)
    lo = (x - hi.astype(F32)).astype(BF16)
    return hi, lo


def _dot_split(x, u):
    hi, lo = _split_bf16(x)
    return _dot(hi, u) + _dot(lo, u)


def _rms(y, gain):
    ms = jnp.mean(y * y, axis=-1, keepdims=True)
    return y * lax.rsqrt(ms + EPS) * gain


def _neg_softplus(z):
    return -(jnp.maximum(z, 0.0) + jnp.log(1.0 + jnp.exp(-jnp.abs(z))))


def _t5_bucket(dist):
    n = jnp.maximum(dist, 0)
    max_exact = N_BUCKETS // 2
    nf = jnp.maximum(n, 1).astype(F32)
    large = max_exact + (jnp.log(nf / max_exact) / math.log(MAX_DISTANCE / max_exact)
                         * (N_BUCKETS - max_exact)).astype(I32)
    large = jnp.minimum(large, N_BUCKETS - 1)
    return jnp.where(n < max_exact, n, large)


def _pick_tile(m, candidates):
    for t in candidates:
        if m % t == 0:
            return t
    raise ValueError(f"no tile for {m}")


def _bias_body(rb_ref, toep_ref, bc_ref, sc_ref, sw_ref, *, seq, past_len, w_buf):
    h = pl.program_id(0)
    tq = NSA_TQ

    def table(dist):
        bk = _t5_bucket(dist)
        out = jnp.zeros(dist.shape, F32)
        for b in range(N_BUCKETS):
            out = jnp.where(bk == b, rb_ref[b, h], out)
        return out

    k = lax.broadcasted_iota(I32, (tq, tq), 0)
    q = lax.broadcasted_iota(I32, (tq, tq), 1)
    toep_ref[0] = table(tq + q - k)
    toep_ref[1] = table(q - k)
    nbr = -(-(seq // CMP_BLOCK) // 8) * 8
    n = lax.broadcasted_iota(I32, (nbr, tq), 0)
    qc = lax.broadcasted_iota(I32, (nbr, tq), 1)
    for i in range(seq // tq):
        bc_ref[i, 0:nbr, :] = table(i * tq + qc - CMP_BLOCK * n - (CMP_BLOCK - 1))
        if nbr < LANE:
            bc_ref[i, nbr:, :] = jnp.zeros((LANE - nbr, tq), F32)
    n1 = lax.broadcasted_iota(I32, sc_ref.shape, 1)
    sc_ref[...] = table(past_len - CMP_BLOCK * n1 - (CMP_BLOCK - 1))
    j1 = lax.broadcasted_iota(I32, sw_ref.shape, 1)
    sw_ref[...] = table(w_buf - j1)


def _bias_tables(rel_bias, seq, past_len, w_buf, nbp_s, wp_s):
    body = functools.partial(_bias_body, seq=seq, past_len=past_len, w_buf=w_buf)
    nq = seq // NSA_TQ
    return pl.pallas_call(
        body,
        grid=(H_NSA,),
        in_specs=[pl.BlockSpec(memory_space=pltpu.SMEM)],
        out_specs=[
            pl.BlockSpec((2, NSA_TQ, NSA_TQ), lambda h: (0, 0, h)),
            pl.BlockSpec((nq, LANE, NSA_TQ), lambda h: (0, 0, h)),
            pl.BlockSpec((None, 1, nbp_s), lambda h: (h, 0, 0)),
            pl.BlockSpec((None, 1, wp_s), lambda h: (h, 0, 0)),
        ],
        out_shape=[
            jax.ShapeDtypeStruct((2, NSA_TQ, H_NSA * NSA_TQ), F32),
            jax.ShapeDtypeStruct((nq, LANE, H_NSA * NSA_TQ), F32),
            jax.ShapeDtypeStruct((H_NSA, 1, nbp_s), F32),
            jax.ShapeDtypeStruct((H_NSA, 1, wp_s), F32),
        ],
        compiler_params=_cparams(1),
        name="bias_tables",
    )(rel_bias)


Q_TILES = D_NSA // COL_TILE
NSA_TILE0 = Q_TILES
WIN_TILE = NSA_TILE0 + 4 * D_KV // COL_TILE
SB_TILE0 = WIN_TILE + 2 * D_KV // COL_TILE + D_SB // COL_TILE


def _inproj_body(x_ref, na_ref, wa_ref, wb_ref, wg_ref, gain_ref, zb_ref, sb_ref, nsa_ref, win_ref,
                 gate_ref, h_ref, *, w_keep):
    j = pl.program_id(1)
    tm = x_ref.shape[0]
    half = COL_TILE // 2

    @pl.when(j == 0)
    def _():
        h = _rms(x_ref[...], na_ref[...]).astype(BF16)
        h_ref[...] = h
        gate_ref[...] = jax.nn.sigmoid(_dot(h, wg_ref[...]))

    gain = gain_ref[...]
    chunks = COL_TILE // LANE

    def emit(w_ref, norm_chunks, f32_ref=None, first=0, per_row=1, row0=0):
        h = h_ref[...]
        accs = [_dot(h, w_ref[:, :half]), _dot(h, w_ref[:, half:])]
        for c in range(chunks):
            sl = slice(c * LANE, (c + 1) * LANE)
            y = accs[c * LANE // half][:, (c * LANE) % half:(c * LANE) % half + LANE]
            if c in norm_chunks:
                y = _rms(y, gain[:, sl])
            zb_ref[:, sl] = y.astype(BF16)
            if f32_ref is not None:
                f32_ref[pl.ds(first + c, tm - row0, stride=per_row), :] = y[row0:, :]

    @pl.when(j < Q_TILES)
    def _():
        emit(wa_ref, (0, 1, 2, 3))

    @pl.when(j == NSA_TILE0)
    def _():
        emit(wa_ref, (), nsa_ref, 0, 2 * chunks)

    @pl.when(j == NSA_TILE0 + 1)
    def _():
        emit(wa_ref, (0, 1), nsa_ref, chunks, 2 * chunks)

    @pl.when(j == WIN_TILE)
    def _():
        emit(wa_ref, (0, 1), win_ref, 0, chunks, tm - w_keep)

    @pl.when((j > WIN_TILE) & (j < SB_TILE0))
    def _():
        emit(wb_ref, ())

    @pl.when(j >= SB_TILE0)
    def _():
        emit(wb_ref, (), sb_ref, (j - SB_TILE0) * chunks, 2 * H_SB)


def _inproj(x2d, na, w_a, w_b, w_gate, gain_cols, seq, w_keep):
    m = x2d.shape[0]
    tm = _pick_tile(seq, (1024, 512, 256, 128, 16))
    assert w_keep <= tm
    tpb = seq // tm
    sb_per, nsa_per, win_per = 2 * H_SB, 4 * G_NSA, 2 * G_NSA
    n_a = w_a.shape[1] // COL_TILE
    n_b = w_b.shape[1] // COL_TILE
    assert n_a == WIN_TILE + 1 and n_a + n_b == N_MAIN // COL_TILE
    body = functools.partial(_inproj_body, w_keep=w_keep)
    return pl.pallas_call(
        body,
        grid=(m // tm, n_a + n_b),
        in_specs=[
            pl.BlockSpec((tm, D_MODEL), lambda i, j: (i, 0)),
            pl.BlockSpec((1, D_MODEL), lambda i, j: (0, 0)),
            pl.BlockSpec((D_MODEL, COL_TILE), lambda i, j: (0, jnp.minimum(j, n_a - 1))),
            pl.BlockSpec((D_MODEL, COL_TILE), lambda i, j: (0, jnp.maximum(j - n_a, 0))),
            pl.BlockSpec((D_MODEL, 2 * LANE), lambda i, j: (0, 0)),
            pl.BlockSpec((1, COL_TILE), lambda i, j: (0, j)),
        ],
        out_specs=[
            pl.BlockSpec((tm, COL_TILE), lambda i, j: (i, j)),
            pl.BlockSpec((tm * sb_per, LANE), lambda i, j: (i, 0), pipeline_mode=pl.Buffered(1)),
            pl.BlockSpec((tm * nsa_per, LANE), lambda i, j: (i, 0), pipeline_mode=pl.Buffered(1)),
            pl.BlockSpec((w_keep * win_per, LANE), lambda i, j: (i // tpb, 0)),
            pl.BlockSpec((tm, 2 * LANE), lambda i, j: (i, 0)),
        ],
        out_shape=[
            jax.ShapeDtypeStruct((m, N_MAIN), BF16),
            jax.ShapeDtypeStruct((m * sb_per, LANE), F32),
            jax.ShapeDtypeStruct((m * nsa_per, LANE), F32),
            jax.ShapeDtypeStruct((m // seq * w_keep * win_per, LANE), F32),
            jax.ShapeDtypeStruct((m, 2 * LANE), F32),
        ],
        scratch_shapes=[pltpu.VMEM((tm, D_MODEL), BF16)],
        compiler_params=_cparams(2),
        name="inproj",
    )(x2d, na, w_a, w_b, w_gate, gain_cols)


def _outproj_body(on_ref, os_ref, go_ref, w_ref, x_ref, y_ref, h_ref):
    j = pl.program_id(1)

    @pl.when(j == 0)
    def _():
        go = go_ref[...]
        h_ref[:, :D_NSA] = _rms(on_ref[...], go[:, :D_NSA]).astype(BF16)
        h_ref[:, D_NSA:] = _rms(os_ref[...], go[:, D_NSA:]).astype(BF16)

    y_ref[...] = x_ref[...] + _dot(h_ref[...], w_ref[...])


def _outproj(o_nsa, o_sb, g_out, w_out, x2d):
    m = x2d.shape[0]
    tm = _pick_tile(m, (512, 256, 128, 16))
    tn = D_MODEL
    return pl.pallas_call(
        _outproj_body,
        grid=(m // tm, D_MODEL // tn),
        in_specs=[
            pl.BlockSpec((tm, D_NSA), lambda i, j: (i, 0)),
            pl.BlockSpec((tm, D_SB), lambda i, j: (i, 0)),
            pl.BlockSpec((1, D_MODEL), lambda i, j: (0, 0)),
            pl.BlockSpec((D_MODEL, tn), lambda i, j: (0, j), pipeline_mode=pl.Buffered(1)),
            pl.BlockSpec((tm, tn), lambda i, j: (i, j)),
        ],
        out_specs=pl.BlockSpec((tm, tn), lambda i, j: (i, j)),
        out_shape=jax.ShapeDtypeStruct((m, D_MODEL), F32),
        scratch_shapes=[pltpu.VMEM((tm, D_MODEL), BF16)],
        compiler_params=_cparams(2),
        name="outproj",
    )(o_nsa, o_sb, g_out, w_out, x2d)


def _ffn_body(*refs, sb):
    if sb is None:
        x_ref, nf_ref, wg_ref, wu_ref, wd_ref, y_ref, h_ref = refs
    else:
        pps = sb["pps"]
        x_ref, nf_ref, wg_ref, wu_ref, wd_ref, q_ref = refs[1:7]
        page_refs = refs[7:7 + pps]
        up_ref, upp_ref, y_ref, o_ref, h_ref, z_ref, sacc_ref = refs[7 + pps:]
    i = pl.program_id(0)
    j = pl.program_id(1)

    @pl.when(j == 0)
    def _():
        x = x_ref[...]
        h_ref[...] = _rms(x, nf_ref[...]).astype(BF16)
        y_ref[...] = x

    h = h_ref[...]
    act = jax.nn.silu(_dot(h, wg_ref[...])) * _dot(h, wu_ref[...])
    y_ref[...] += _dot(act.astype(BF16), wd_ref[...])

    if sb is not None:
        nc = sb["n_chunks"]
        s = i * pl.num_programs(1) + j

        @pl.when(s < sb["n_steps"])
        def _():
            _sb_dec_step((s // nc) % 2, s % nc, nc - 1, q_ref, page_refs, up_ref, upp_ref, o_ref,
                         z_ref, sacc_ref, sb["page"])


def _ffn(x2d, norm_ffn, w_gate, w_up, w_down, sb_args=None):
    m = x2d.shape[0]
    d_ff = w_gate.shape[1]
    if sb_args is None:
        tm = _pick_tile(m, (512, 256, 128, 16))
        tf = _pick_tile(d_ff, (512, 256))
        x_mode = None
    else:
        tm = _pick_tile(m, (1024, 512, 256, 128))
        tf = _pick_tile(d_ff, (256,))
        x_mode = pl.Buffered(1)
    nf = d_ff // tf
    in_specs = [
        pl.BlockSpec((tm, D_MODEL), lambda i, j, *_: (i, 0), pipeline_mode=x_mode),
        pl.BlockSpec((1, D_MODEL), lambda i, j, *_: (0, 0)),
        pl.BlockSpec((D_MODEL, tf), lambda i, j, *_: (0, j)),
        pl.BlockSpec((D_MODEL, tf), lambda i, j, *_: (0, j)),
        pl.BlockSpec((tf, D_MODEL), lambda i, j, *_: (j, 0)),
    ]
    y_spec = pl.BlockSpec((tm, D_MODEL), lambda i, j, *_: (i, 0))
    y_shape = jax.ShapeDtypeStruct((m, D_MODEL), F32)
    h_scratch = pltpu.VMEM((tm, D_MODEL), BF16)
    if sb_args is None:
        return pl.pallas_call(
            functools.partial(_ffn_body, sb=None),
            grid=(m // tm, nf),
            in_specs=in_specs,
            out_specs=y_spec,
            out_shape=y_shape,
            scratch_shapes=[h_scratch],
            compiler_params=_cparams(2),
            name="ffn",
        )(x2d, norm_ffn, w_gate, w_up, w_down)

    l, pt_flat, q_sb, cache_sb = sb_args
    dec_b = q_sb.shape[0]
    page = cache_sb.shape[2]
    npg = pt_flat.shape[0] // dec_b
    pps = SB_PAGES_PER_STEP
    nc = npg // pps
    n_steps = dec_b * 2 * nc
    assert n_steps <= (m // tm) * nf
    width = page * H_SB
    up, upp = _sb_dec_consts(npg, page)

    def pos(i, j):
        s = jnp.minimum(i * nf + j, n_steps - 1)
        return s // (2 * nc), (s // nc) % 2, s % nc

    def page_spec(u):
        def imap(i, j, pt):
            b, ph, c = pos(i, j)
            return (l, pt[b * npg + c * pps + u], 0, ph, 0, 0)
        return pl.BlockSpec((None, None, page, None, H_SB, HEAD_DIM), imap)

    once = pl.Buffered(1)
    grid_spec = pltpu.PrefetchScalarGridSpec(
        num_scalar_prefetch=1,
        grid=(m // tm, nf),
        in_specs=in_specs
        + [pl.BlockSpec((None, ROWS_S, HEAD_DIM), lambda i, j, pt: (pos(i, j)[0], 0, 0))]
        + [page_spec(u) for u in range(pps)]
        + [pl.BlockSpec((width, 2 * width), lambda i, j, pt: (0, 0), pipeline_mode=once),
           pl.BlockSpec((npg, npg), lambda i, j, pt: (0, 0), pipeline_mode=once)],
        out_specs=[y_spec,
                   pl.BlockSpec((None, H_SB, HEAD_DIM), lambda i, j, pt: (pos(i, j)[0], 0, 0))],
        scratch_shapes=[h_scratch, pltpu.VMEM((npg, width), F32),
                        pltpu.VMEM((ROWS_S, HEAD_DIM), F32)],
    )
    sb = dict(pps=pps, page=page, n_chunks=nc, n_steps=n_steps)
    return pl.pallas_call(
        functools.partial(_ffn_body, sb=sb),
        grid_spec=grid_spec,
        out_shape=[y_shape, jax.ShapeDtypeStruct((dec_b, H_SB, HEAD_DIM), F32)],
        compiler_params=_cparams(2, VMEM_LIMIT_FUSED),
        name="ffn_sb_decode",
    )(pt_flat, x2d, norm_ffn, w_gate, w_up, w_down, q_sb, *([cache_sb] * pps), up, upp)


SB_SUB = 4


def _sb_body(*refs):
    n_sub = SB_HB // SB_SUB
    q_refs, k_refs, v_refs = refs[:n_sub], refs[n_sub:2 * n_sub], refs[2 * n_sub:3 * n_sub]
    u_ref, o_ref, acc_ref, run_ref = refs[3 * n_sub:]
    i = pl.program_id(2)
    row = lax.broadcasted_iota(I32, (SB_TQ, SB_TK), 0)
    col = lax.broadcasted_iota(I32, (SB_TQ, SB_TK), 1)
    strict = col < row

    def tile(kb, diagonal):
        off = pl.multiple_of(kb * SB_TK, SB_TK)
        heads = range(SB_HB)
        sls = [slice(h % SB_SUB * HEAD_DIM, (h % SB_SUB + 1) * HEAD_DIM) for h in heads]
        q_ref = [q_refs[h // SB_SUB] for h in heads]
        k_ref = [k_refs[h // SB_SUB] for h in heads]
        v_ref = [v_refs[h // SB_SUB] for h in heads]
        z = [_dot_nt(q_ref[h][:, sls[h]], k_ref[h][pl.ds(off, SB_TK), sls[h]]) * SCALE
             for h in heads]
        log_1m = [_neg_softplus(zh) for zh in z]
        if diagonal:
            log_1m = [jnp.where(strict, lh, 0.0) for lh in log_1m]
        parts = [_split_bf16(lh) for lh in log_1m]
        u = u_ref[...]
        later = [_dot(hi, u) + _dot(lo, u) for hi, lo in parts]
        total = [jnp.sum(lh, axis=-1, keepdims=True) for lh in log_1m]
        if diagonal:
            a = [jnp.where(strict, jnp.exp(z[h] + log_1m[h] + later[h]), 0.0) for h in heads]
        else:
            run = [run_ref[h] for h in heads]
            a = [jnp.exp(z[h] + log_1m[h] + later[h] + run[h]) for h in heads]
        pv = [_dot(a[h].astype(BF16), v_ref[h][pl.ds(off, SB_TK), sls[h]]) for h in heads]
        for h in heads:
            if diagonal:
                acc_ref[h] = pv[h]
                run_ref[h] = total[h]
            else:
                acc_ref[h] += pv[h]
                run_ref[h] = run[h] + total[h]

    tile(i, True)

    def step(it, carry):
        tile(i - 1 - it, False)
        return carry

    lax.fori_loop(0, i, step, 0)
    for h in range(SB_HB):
        o_ref[:, h * HEAD_DIM:(h + 1) * HEAD_DIM] = acc_ref[h]


def _sb_prompt(zb, batch, seq):
    nq = seq // SB_TQ
    width = SB_HB * HEAD_DIM
    sub_w = SB_SUB * HEAD_DIM
    n_sub = SB_HB // SB_SUB
    assert QS_BLK % SB_SUB == 0 and KSB_BLK % SB_SUB == 0 and VSB_BLK % SB_SUB == 0
    r = lax.broadcasted_iota(I32, (SB_TK, SB_TK), 0)
    c = lax.broadcasted_iota(I32, (SB_TK, SB_TK), 1)
    u = (r > c).astype(BF16)

    def q_spec(n):
        return pl.BlockSpec((SB_TQ, sub_w),
                            lambda b, h, i: (b * nq + i, QS_BLK // SB_SUB + h * n_sub + n))

    def kv_spec(blk, n):
        return pl.BlockSpec((seq, sub_w), lambda b, h, i: (b, blk // SB_SUB + h * n_sub + n))

    subs = range(n_sub)
    return pl.pallas_call(
        _sb_body,
        grid=(batch, H_SB // SB_HB, nq),
        in_specs=[q_spec(n) for n in subs] + [kv_spec(KSB_BLK, n) for n in subs]
        + [kv_spec(VSB_BLK, n) for n in subs]
        + [pl.BlockSpec((SB_TK, SB_TK), lambda b, h, i: (0, 0))],
        out_specs=pl.BlockSpec((SB_TQ, width), lambda b, h, i: (b * nq + i, h)),
        out_shape=jax.ShapeDtypeStruct((batch * seq, D_SB), F32),
        scratch_shapes=[pltpu.VMEM((SB_HB, SB_TQ, HEAD_DIM), F32), pltpu.VMEM((SB_HB, SB_TQ, 1), F32)],
        compiler_params=_cparams(3),
        name="sb_prompt",
    )(*([zb] * (3 * n_sub)), u)


def _cmp_prompt_body(x_ref, wk_ref, wv_ref, bk_ref, bv_ref, gk_ref, kc_ref, vct_ref, *, nb):
    pad = jnp.zeros((LANE - nb, HEAD_DIM), F32)
    per = 4 * G_NSA
    for c in range(2 * G_NSA):
        rows = jnp.concatenate(
            [x_ref[pl.ds(l * per + c, nb, stride=CMP_BLOCK * per), :] for l in range(CMP_BLOCK)],
            axis=1)
        rows = rows.astype(BF16)
        if c < G_NSA:
            y = _rms(_dot(rows, wk_ref[...]) + bk_ref[...], gk_ref[...])
            kc_ref[c] = jnp.concatenate([y, pad], axis=0).astype(BF16)
        else:
            y = _dot(rows, wv_ref[...]) + bv_ref[...]
            vct_ref[c - G_NSA] = jnp.concatenate([y, pad], axis=0).T.astype(BF16)


def _cmp_prompt(nsa_rows, batch, seq, wk, wv, bk, bv, gk0):
    nb = seq // CMP_BLOCK
    assert nb % 8 == 0 and nb < LANE
    body = functools.partial(_cmp_prompt_body, nb=nb)
    shape = jax.ShapeDtypeStruct((batch, G_NSA, LANE, HEAD_DIM), BF16)

    return pl.pallas_call(
        body,
        grid=(batch,),
        in_specs=[
            pl.BlockSpec((seq * 4 * G_NSA, HEAD_DIM), lambda b: (b, 0)),
            pl.BlockSpec(wk.shape, lambda b: (0, 0)),
            pl.BlockSpec(wv.shape, lambda b: (0, 0)),
            pl.BlockSpec((1, HEAD_DIM), lambda b: (0, 0)),
            pl.BlockSpec((1, HEAD_DIM), lambda b: (0, 0)),
            pl.BlockSpec((1, HEAD_DIM), lambda b: (0, 0)),
        ],
        out_specs=[pl.BlockSpec((None, G_NSA, LANE, HEAD_DIM), lambda b: (b, 0, 0, 0))] * 2,
        out_shape=[shape, shape],
        compiler_params=_cparams(1),
        name="cmp_prompt",
    )(nsa_rows, wk, wv, bk, bv, gk0)


def _masked_softmax(s, valid):
    logit = jnp.where(valid, s, NEG_INF)
    e = jnp.exp(logit - jnp.max(logit, axis=-1, keepdims=True))
    return jnp.where(valid, e / jnp.sum(e, axis=-1, keepdims=True), 0.0)


class _Flash:
    def __init__(self, m_ref, l_ref, a_ref):
        self.m_ref, self.l_ref, self.a_ref = m_ref, l_ref, a_ref

    def first(self, logit, v):
        m = jnp.max(logit, axis=-1, keepdims=True)
        p = jnp.exp(logit - m)
        self.m_ref[...] = m
        self.l_ref[...] = jnp.sum(p, axis=-1, keepdims=True)
        self.a_ref[...] = _dot(p.astype(BF16), v)

    def update(self, logit, v):
        m_old = self.m_ref[...]
        m = jnp.maximum(m_old, jnp.max(logit, axis=-1, keepdims=True))
        alpha = jnp.exp(m_old - m)
        p = jnp.exp(logit - m)
        self.m_ref[...] = m
        self.l_ref[...] = alpha * self.l_ref[...] + jnp.sum(p, axis=-1, keepdims=True)
        self.a_ref[...] = alpha * self.a_ref[...] + _dot(p.astype(BF16), v)

    def result(self):
        return self.a_ref[...] / self.l_ref[...]


class _FlashT:
    def __init__(self, m_ref, l_ref, a_ref):
        self.m_ref, self.l_ref, self.a_ref = m_ref, l_ref, a_ref

    @staticmethod
    def step(states, logits, v_ts, first):
        ms = [jnp.max(lg, axis=0, keepdims=True) for lg in logits]
        if not first:
            olds = [st.m_ref[...] for st in states]
            ms = [jnp.maximum(o, m) for o, m in zip(olds, ms)]
            alphas = [jnp.exp(o - m) for o, m in zip(olds, ms)]
        ps = [jnp.exp(lg - m) for lg, m in zip(logits, ms)]
        sums = [jnp.sum(p, axis=0, keepdims=True) for p in ps]
        pvs = [_dot(v_t, p.astype(BF16)) for v_t, p in zip(v_ts, ps)]
        for n, st in enumerate(states):
            st.m_ref[...] = ms[n]
            if first:
                st.l_ref[...] = sums[n]
                st.a_ref[...] = pvs[n]
            else:
                st.l_ref[...] = alphas[n] * st.l_ref[...] + sums[n]
                st.a_ref[...] = alphas[n] * st.a_ref[...] + pvs[n]

    def update(self, logit, v_t):
        _FlashT.step([self], [logit], [v_t], False)

    def result(self):
        return self.a_ref[...] / self.l_ref[...]


def _nsa_body(pt_ref, rb_ref, q_ref, kc_ref, vct_ref, ks_ref, vs_ref, kw_ref, vw_ref, g_ref,
              toep_ref, bc_ref, et_ref, *rest, nb, seq, cmp):
    page_refs = rest[:cmp["pps"]]
    (wkv_ref, o_ref, kcr_ref, vcr_ref, vst_ref, vwt_ref, m_s, l_s, a_s, m_w, l_w, a_w, yk_ref,
     yv_ref) = rest[cmp["pps"]:]
    g = pl.program_id(1)
    i = pl.program_id(2)
    tq = NSA_TQ
    sub = tq // LANE
    width = HPG * tq
    q0 = i * tq
    step = (pl.program_id(0) * pl.num_programs(1) + g) * pl.num_programs(2) + i

    @pl.when(step < cmp["n_steps"])
    def _():
        _cmp_dec_step(page_refs, wkv_ref, kcr_ref, vcr_ref, yk_ref, yv_ref, cmp["page"])

    @pl.when(i == 0)
    def _():
        for c in range(seq // LANE):
            rows = slice(c * LANE, (c + 1) * LANE)
            cols = slice((c % sub) * LANE, (c % sub + 1) * LANE)
            vst_ref[c // sub, :, cols] = vs_ref[rows, :].astype(F32).T.astype(BF16)
            vwt_ref[c // sub, :, cols] = vw_ref[rows, :].astype(F32).T.astype(BF16)

    qs = jnp.concatenate([q_ref[:, hh * LANE:(hh + 1) * LANE] for hh in range(HPG)], axis=0)

    def q_off(rows):
        return lax.broadcasted_iota(I32, (rows, width), 1) % tq

    n_r = lax.broadcasted_iota(I32, (LANE, width), 0)
    s_c = _dot_nt(kc_ref[...], qs) * SCALE + bc_ref[...]
    valid_c = (CMP_BLOCK * n_r + (CMP_BLOCK - 1) <= q0 + q_off(LANE)) & (n_r < nb)
    logit = jnp.where(valid_c, s_c, NEG_INF)
    e = jnp.exp(logit - jnp.max(logit, axis=0, keepdims=True))
    p_c = jnp.where(valid_c, e / jnp.sum(e, axis=0, keepdims=True), 0.0)
    oc_t = _dot(vct_ref[...], p_c.astype(BF16))
    imp_t = p_c[:, 0:tq]
    for hh in range(1, HPG):
        imp_t = imp_t + p_c[:, hh * tq:(hh + 1) * tq]

    nbr = -(-nb // 8) * 8
    n_i = lax.broadcasted_iota(I32, (nbr, tq), 0)
    cur = (q0 + lax.broadcasted_iota(I32, (nbr, tq), 1)) // CMP_BLOCK
    forced = (n_i == 0) | (n_i >= cur - (N_LOCAL - 1))
    score = jnp.where(n_i > cur, -jnp.inf, jnp.where(forced, FORCE_SCORE, imp_t[0:nbr]))
    rank = jnp.zeros((nbr, tq), I32)
    for j in range(nb):
        sj = jnp.broadcast_to(score[j:j + 1, :], (nbr, tq))
        ahead = (sj > score) | ((sj == score) & (n_i > j))
        rank = rank + ahead.astype(I32)
    sel_t = ((rank < min(TOP_N, nb)) & (score > -jnp.inf)).astype(F32)
    if nbr < LANE:
        sel_t = jnp.concatenate([sel_t, jnp.zeros((LANE - nbr, tq), F32)], axis=0)
    sel_t = sel_t.astype(BF16)

    k_r = lax.broadcasted_iota(I32, (tq, width), 0)
    q_c = q_off(tq)
    causal = k_r <= q_c
    bias_far = jnp.concatenate(
        [jnp.full((1, tq), rb_ref[N_BUCKETS - 1, g * HPG + hh], F32) for hh in range(HPG)], axis=1)

    def chunk(ref, c):
        return ref[pl.ds(pl.multiple_of(c * tq, tq), tq), :]

    def scores(ref, c):
        return _dot_nt(chunk(ref, c), qs) * SCALE

    def selected(c):
        hit = _dot(chunk(et_ref, c), sel_t)
        return jnp.concatenate([hit] * HPG, axis=1) > 0.5

    fs = _FlashT(m_s, l_s, a_s)
    fw = _FlashT(m_w, l_w, a_w)

    def both(c, bias, sel_extra, win_mask, first):
        sel_mask = selected(c) if sel_extra is None else selected(c) & sel_extra
        lg_s = jnp.where(sel_mask, scores(ks_ref, c) + bias, NEG_INF)
        lg_w = scores(kw_ref, c) + bias
        if win_mask is not None:
            lg_w = jnp.where(win_mask, lg_w, NEG_INF)
        _FlashT.step([fs, fw], [lg_s, lg_w], [vst_ref[c], vwt_ref[c]], first)

    both(i, toep_ref[1], causal, causal, True)

    @pl.when(i >= 1)
    def _():
        both(i - 1, toep_ref[0], None, None, False)

    @pl.when(i >= 2)
    def _():
        both(i - 2, bias_far, None, k_r >= q_c, False)

    def far_step(c, carry):
        fs.update(jnp.where(selected(c), scores(ks_ref, c) + bias_far, NEG_INF), vst_ref[c])
        return carry

    lax.fori_loop(0, jnp.maximum(i - 2, 0), far_step, 0)

    g_t = jnp.concatenate([g_ref[t * LANE:(t + 1) * LANE, :].T for t in range(sub)], axis=1)

    def gate(branch):
        return jnp.concatenate(
            [g_t[branch * HPG + hh:branch * HPG + hh + 1, :] for hh in range(HPG)], axis=1)

    o_t = gate(0) * oc_t + gate(1) * fs.result() + gate(2) * fw.result()
    for hh in range(HPG):
        for t in range(sub):
            lanes = slice(hh * tq + t * LANE, hh * tq + (t + 1) * LANE)
            o_ref[t * LANE:(t + 1) * LANE, hh * LANE:(hh + 1) * LANE] = o_t[:, lanes].T


def _nsa_prompt(rel_bias, zb, kc, vct, gates, toep, bias_c, batch, seq, cmp_args):
    tq = NSA_TQ
    assert WINDOW == 2 * tq and tq % LANE == 0 and seq % tq == 0
    nq = seq // tq
    nb = seq // CMP_BLOCK
    assert nb <= LANE
    width = HPG * tq
    k = lax.broadcasted_iota(I32, (seq, LANE), 0)
    n = lax.broadcasted_iota(I32, (seq, LANE), 1)
    expand_t = (k // CMP_BLOCK == n).astype(BF16)

    l, pt_flat, cache_nsa2d, w_kv, dec_b, page = cmp_args
    npg = pt_flat.shape[0] // dec_b
    pps = min(CMP_PAGES_PER_STEP, npg)
    assert npg % pps == 0
    nchunk = npg // pps
    n_cmp = dec_b * nchunk
    assert n_cmp <= batch * G_NSA * nq
    halves = page // CMP_BLOCK
    nblk = pps * halves
    body = functools.partial(_nsa_body, nb=nb, seq=seq,
                             cmp=dict(pps=pps, page=page, n_steps=n_cmp))

    def kv_spec(blk):
        return pl.BlockSpec((seq, HEAD_DIM), lambda b, g, i, pt: (b, blk + g))

    def cmp_pos(b, g, i):
        s = jnp.minimum((b * G_NSA + g) * nq + i, n_cmp - 1)
        return s // nchunk, s % nchunk

    def page_spec(u):
        def imap(b, g, i, pt):
            row, c = cmp_pos(b, g, i)
            return (l, pt[row * npg + c * pps + u], 0, 0)
        return pl.BlockSpec((None, None, page * NSA_CH, HEAD_DIM), imap)

    def raw_imap(b, g, i, pt):
        row, c = cmp_pos(b, g, i)
        return (row, 0, c, 0)

    raw_spec = pl.BlockSpec((None, G_NSA, nblk, HEAD_DIM), raw_imap)
    raw_shape = jax.ShapeDtypeStruct((dec_b, G_NSA, npg * halves, HEAD_DIM), F32)
    stat = pltpu.VMEM((1, width), F32)
    accum = pltpu.VMEM((HEAD_DIM, width), F32)
    v_t = pltpu.VMEM((nq, HEAD_DIM, tq), BF16)
    y_raw = pltpu.VMEM((nblk * NSA_CH, HEAD_DIM), F32)
    grid_spec = pltpu.PrefetchScalarGridSpec(
        num_scalar_prefetch=1,
        grid=(batch, G_NSA, nq),
        in_specs=[
            pl.BlockSpec(memory_space=pltpu.SMEM),
            pl.BlockSpec((tq, HPG * HEAD_DIM), lambda b, g, i, pt: (b * nq + i, g)),
            pl.BlockSpec((None, None, LANE, HEAD_DIM), lambda b, g, i, pt: (b, g, 0, 0)),
            pl.BlockSpec((None, None, HEAD_DIM, LANE), lambda b, g, i, pt: (b, g, 0, 0)),
            kv_spec(KS_BLK), kv_spec(VS_BLK), kv_spec(KW_BLK), kv_spec(VW_BLK),
            pl.BlockSpec((tq, LANE), lambda b, g, i, pt: (b * nq + i, g)),
            pl.BlockSpec((2, tq, width), lambda b, g, i, pt: (0, 0, g)),
            pl.BlockSpec((None, LANE, width), lambda b, g, i, pt: (i, 0, g)),
            pl.BlockSpec((seq, LANE), lambda b, g, i, pt: (0, 0)),
        ]
        + [page_spec(u) for u in range(pps)]
        + [pl.BlockSpec(w_kv.shape, lambda b, g, i, pt: (0, 0), pipeline_mode=pl.Buffered(1))],
        out_specs=[pl.BlockSpec((tq, HPG * HEAD_DIM), lambda b, g, i, pt: (b * nq + i, g)),
                   raw_spec, raw_spec],
        scratch_shapes=[v_t, v_t, stat, stat, accum, stat, stat, accum, y_raw, y_raw],
    )
    return pl.pallas_call(
        body,
        grid_spec=grid_spec,
        out_shape=[jax.ShapeDtypeStruct((batch * seq, D_NSA), F32), raw_shape, raw_shape],
        compiler_params=_cparams(3),
        name="nsa_prompt_cmp_decode",
    )(pt_flat, rel_bias, zb, kc, vct, zb, zb, zb, zb, gates, toep, bias_c, expand_t,
      *([cache_nsa2d] * pps), w_kv)


def _cast_body(w_ref, o_ref):
    o_ref[...] = w_ref[...].astype(BF16)


def _cast_layer_bf16(w, l, cols=None):
    _, rows, n = w.shape
    cols = n if cols is None else cols
    assert cols % LANE == 0
    tr = _pick_tile(rows, (256, 128))
    return pl.pallas_call(
        _cast_body,
        grid=(rows // tr,),
        in_specs=[pl.BlockSpec((None, tr, cols), lambda i: (l, i, 0))],
        out_specs=pl.BlockSpec((tr, cols), lambda i: (i, 0)),
        out_shape=jax.ShapeDtypeStruct((rows, cols), BF16),
        compiler_params=_cparams(1),
        name="cast_bf16",
    )(w)


def _prep_weights(l, w_in, g_q, g_k, w_cmp_k, w_cmp_v, w_out, w_gate, w_up, w_down):
    n_gate = 3 * H_NSA
    g0 = D_NSA + 6 * D_KV
    wi = w_in[l]
    w_a = wi[:, :g0].astype(BF16)
    w_b = wi[:, g0 + n_gate:].astype(BF16)
    wg = wi[:, g0:g0 + n_gate].reshape(D_MODEL, 3, G_NSA, HPG)
    wg = jnp.transpose(wg, (0, 2, 1, 3)).reshape(D_MODEL, G_NSA, 3 * HPG)
    wg = jnp.pad(wg, ((0, 0), (0, 0), (0, LANE - 3 * HPG))).reshape(D_MODEL, G_NSA * LANE)
    ones = jnp.ones((HEAD_DIM,), F32)
    gain = jnp.concatenate(
        [jnp.tile(g_q[l], H_NSA), jnp.tile(ones, 2 * G_NSA),
         jnp.tile(g_k[l, 1], G_NSA), jnp.tile(ones, G_NSA),
         jnp.tile(g_k[l, 2], G_NSA), jnp.tile(ones, G_NSA),
         jnp.tile(ones, 3 * H_SB)])[None, :]
    wk = w_cmp_k[l].reshape(CMP_BLOCK * HEAD_DIM, HEAD_DIM).astype(BF16)
    wv = w_cmp_v[l].reshape(CMP_BLOCK * HEAD_DIM, HEAD_DIM).astype(BF16)
    return dict(
        w_a=w_a, w_b=w_b, w_gate=wg.astype(BF16), gain=gain, wk=wk, wv=wv,
        w_kv=jnp.concatenate([wk, wv], axis=1),
        w_out=_cast_layer_bf16(w_out, l), w_gate_ffn=_cast_layer_bf16(w_gate, l),
        w_up=_cast_layer_bf16(w_up, l), w_down=_cast_layer_bf16(w_down, l))


def _prompt_layer(x2d, wt, p, tables, rel_bias, batch, seq, sb_args, cmp_args):
    na, bk, bv, gk0, go, nf = p
    zb, sb_rows, nsa_rows, win_rows, gates = _inproj(
        x2d, na, wt["w_a"], wt["w_b"], wt["w_gate"], wt["gain"], seq, min(WINDOW, seq))
    kc, vct = _cmp_prompt(nsa_rows, batch, seq, wt["wk"], wt["wv"], bk, bv, gk0)
    o_nsa, kcr, vcr = _nsa_prompt(rel_bias, zb, kc, vct, gates, tables[0], tables[1], batch, seq,
                                  cmp_args)
    o_sb = _sb_prompt(zb, batch, seq)
    x1 = _outproj(o_nsa, o_sb, go, wt["w_out"], x2d)
    x2, o_sb_sample = _ffn(x1, nf, wt["w_gate_ffn"], wt["w_up"], wt["w_down"], sb_args)
    return x2, sb_rows, nsa_rows, win_rows, (o_sb_sample, kcr, vcr)


ROWS_S = 16


def _sb_dec_step(ph, c, last_c, q_ref, page_refs, up_ref, upp_ref, o_ref, z_ref, acc_ref, page):
    pps = len(page_refs)
    width = page * H_SB
    r = lax.broadcasted_iota(I32, (ROWS_S, width), 0)
    lane = lax.broadcasted_iota(I32, (ROWS_S, width), 1)
    own_head = (lane % H_SB) == r

    @pl.when(ph == 0)
    def _():
        q = q_ref[...]
        for u in range(pps):
            ks = page_refs[u][...].reshape(width, HEAD_DIM).astype(BF16)
            zt = _dot_nt(q, ks)
            z_ref[pl.ds(c * pps + u, 1), :] = jnp.sum(jnp.where(own_head, zt, 0.0), axis=0,
                                                      keepdims=True)

    @pl.when((ph == 1) & (c == 0))
    def _():
        z = z_ref[...] * SCALE
        log_1m = _neg_softplus(z)
        cs = _dot_split(log_1m, up_ref[...])
        hi, lo = _split_bf16(cs[:, width:])
        later_pages = _dot(upp_ref[...], hi) + _dot(upp_ref[...], lo)
        z_ref[...] = jnp.exp(z + log_1m + cs[:, :width] + later_pages)
        acc_ref[...] = jnp.zeros_like(acc_ref)

    @pl.when(ph == 1)
    def _():
        for u in range(pps):
            vs = page_refs[u][...].reshape(width, HEAD_DIM).astype(BF16)
            a = jnp.broadcast_to(z_ref[pl.ds(c * pps + u, 1), :], (ROWS_S, width))
            acc_ref[...] += _dot(jnp.where(own_head, a, 0.0).astype(BF16), vs)

    @pl.when((ph == 1) & (c == last_c))
    def _():
        o_ref[...] = acc_ref[0:H_SB, :]


def _sb_dec_consts(npg, page):
    assert npg % SB_PAGES_PER_STEP == 0 and npg % 8 == 0
    width = page * H_SB
    r = lax.broadcasted_iota(I32, (width, 2 * width), 0)
    c = lax.broadcasted_iota(I32, (width, 2 * width), 1)
    same_head = (r % H_SB) == (c % H_SB)
    up = (same_head & ((c >= width) | (r // H_SB > c // H_SB))).astype(BF16)
    pr = lax.broadcasted_iota(I32, (npg, npg), 0)
    pc = lax.broadcasted_iota(I32, (npg, npg), 1)
    upp = (pc > pr).astype(BF16)
    return up, upp


NSA_CH = 4 * G_NSA


def _cmp_dec_step(page_refs, w_ref, kc_ref, vc_ref, yk_ref, yv_ref, page):
    pps = len(page_refs)
    halves = page // CMP_BLOCK
    nblk = pps * halves
    acc = jnp.zeros((nblk * NSA_CH, 2 * HEAD_DIM), F32)

    def token_rows(u, hf, l):
        return page_refs[u][pl.ds((hf * CMP_BLOCK + l) * NSA_CH, NSA_CH), :]

    for l in range(0, CMP_BLOCK, 2):
        x_l = jnp.concatenate(
            [jnp.concatenate([token_rows(u, hf, l), token_rows(u, hf, l + 1)], axis=1)
             for u in range(pps) for hf in range(halves)], axis=0).astype(BF16)
        acc = acc + _dot(x_l, w_ref[l * HEAD_DIM:(l + 2) * HEAD_DIM, :])
    yk_ref[...] = acc[:, :HEAD_DIM]
    yv_ref[...] = acc[:, HEAD_DIM:]
    for g in range(G_NSA):
        kc_ref[g] = yk_ref[pl.ds(g, nblk, stride=NSA_CH), :]
        vc_ref[g] = yv_ref[pl.ds(G_NSA + g, nblk, stride=NSA_CH), :]


def _nsa_sel_body(q_ref, kcr_ref, vcr_ref, new_ref, wk0_ref, wv0_ref, bk_ref, bv_ref, gk_ref,
                  sbc_ref, kw_ref, vw_ref, sbw_ref, idx_ref, oc_ref, ow_ref,
                  *, nb0, nbp, past_len, w_buf, wp):
    nb = nb0 + 1
    new = new_ref[...].astype(BF16)
    new_k = _rms(_dot(new, wk0_ref[...]) + bk_ref[...], gk_ref[...])
    new_v = _dot(new, wv0_ref[...]) + bv_ref[...]
    row8 = lax.broadcasted_iota(I32, (8, HEAD_DIM), 0)
    n1 = lax.broadcasted_iota(I32, (1, nbp), 1)
    nq = lax.broadcasted_iota(I32, (ROWS_S, nbp), 1)
    pos_q = past_len
    pad = jnp.zeros((nbp - nb0 - 8, HEAD_DIM), F32)
    for g in range(G_NSA):
        q = q_ref[g]
        kc = jnp.concatenate(
            [_rms(kcr_ref[g] + bk_ref[...], gk_ref[...]),
             jnp.where(row8 == 0, jnp.broadcast_to(new_k[g:g + 1], (8, HEAD_DIM)), 0.0), pad],
            axis=0).astype(BF16)
        vc = jnp.concatenate(
            [vcr_ref[g] + bv_ref[...],
             jnp.where(row8 == 0, jnp.broadcast_to(new_v[G_NSA + g:G_NSA + g + 1], (8, HEAD_DIM)),
                       0.0), pad], axis=0).astype(BF16)
        s_c = _dot_nt(q, kc) * SCALE + sbc_ref[g]
        valid_c = (CMP_BLOCK * nq + (CMP_BLOCK - 1) <= pos_q) & (nq < nb)
        p_c = _masked_softmax(s_c, valid_c)
        oc_ref[g] = _dot(p_c.astype(BF16), vc)
        imp = p_c[0:1]
        for hh in range(1, HPG):
            imp = imp + p_c[hh:hh + 1]

        cur = pos_q // CMP_BLOCK
        forced = (n1 == 0) | (n1 >= cur - (N_LOCAL - 1))
        score = jnp.where(n1 > cur, -jnp.inf, jnp.where(forced, FORCE_SCORE, imp))
        sq = jnp.broadcast_to(score, (LANE, nbp))
        col = jnp.concatenate([sq[:, k * LANE:(k + 1) * LANE].T for k in range(nbp // LANE)],
                              axis=0)
        j_i = lax.broadcasted_iota(I32, (nbp, LANE), 0)
        ranks = []
        for k in range(nbp // LANE):
            mine = jnp.broadcast_to(score[:, k * LANE:(k + 1) * LANE], (nbp, LANE))
            n_i = k * LANE + lax.broadcasted_iota(I32, (nbp, LANE), 1)
            ahead = (col > mine) | ((col == mine) & (j_i < n_i))
            ranks.append(jnp.sum(ahead.astype(F32), axis=0, keepdims=True))
        rank = jnp.concatenate(ranks, axis=1)
        slot = lax.broadcasted_iota(I32, (TOP_N, nbp), 0).astype(F32)
        hit = jnp.broadcast_to(rank, (TOP_N, nbp)) == slot
        ids = jnp.sum(jnp.where(hit, lax.broadcasted_iota(I32, (TOP_N, nbp), 1).astype(F32), 0.0),
                      axis=1, keepdims=True)
        idx_ref[g] = jnp.broadcast_to(ids, (TOP_N, LANE)).astype(I32)

        jw = lax.broadcasted_iota(I32, (ROWS_S, wp), 1)
        dist_w = w_buf - jw
        valid_w = (dist_w >= 0) & (dist_w <= WINDOW) & (past_len - w_buf + jw >= 0)
        s_w = _dot_nt(q, kw_ref[g]) * SCALE + sbw_ref[g]
        p_w = _masked_softmax(s_w, valid_w)
        ow_ref[g] = _dot(p_w.astype(BF16), vw_ref[g])


def _nsa_select(q_nsa, kcr, vcr, new_c, wk0, wv0, bk, bv, gk0, sbc, kw, vw, sbw, past_len, w_buf):
    dec_b, _, nb0, _ = kcr.shape
    nbp = sbc.shape[-1]
    wp = sbw.shape[-1]
    assert nb0 + 1 >= TOP_N and nbp >= nb0 + 8 and nb0 % 8 == 0
    body = functools.partial(_nsa_sel_body, nb0=nb0, nbp=nbp, past_len=past_len, w_buf=w_buf, wp=wp)

    def per_b(shape):
        nd = len(shape)
        return pl.BlockSpec((None,) + tuple(shape), lambda b: (b,) + (0,) * nd)

    def whole(arr):
        nd = arr.ndim
        return pl.BlockSpec(arr.shape, lambda b: (0,) * nd)

    vec = (G_NSA, ROWS_S, HEAD_DIM)
    return pl.pallas_call(
        body,
        grid=(dec_b,),
        in_specs=[per_b(vec), per_b((G_NSA, nb0, HEAD_DIM)), per_b((G_NSA, nb0, HEAD_DIM)),
                  per_b((8, HEAD_DIM)), whole(wk0), whole(wv0), whole(bk), whole(bv), whole(gk0),
                  whole(sbc), per_b((G_NSA, wp, HEAD_DIM)), per_b((G_NSA, wp, HEAD_DIM)), whole(sbw)],
        out_specs=[per_b((G_NSA, TOP_N, LANE)), per_b(vec), per_b(vec)],
        out_shape=[jax.ShapeDtypeStruct((dec_b, G_NSA, TOP_N, LANE), I32),
                   jax.ShapeDtypeStruct((dec_b,) + vec, F32),
                   jax.ShapeDtypeStruct((dec_b,) + vec, F32)],
        compiler_params=_cparams(1),
        name="nsa_select",
    )(q_nsa, kcr, vcr, new_c, wk0, wv0, bk, bv, gk0, sbc, kw, vw, sbw)


GATHER_BLOCKS = 8


def _nsa_gather_body(pt_ref, idx_ref, rb_ref, q_ref, *refs, nb0, past_len):
    n_pages = G_NSA * GATHER_BLOCKS
    page_refs = refs[:n_pages]
    new_ref, gs_ref, oc_ref, ow_ref, o_ref, m_ref, l_ref, a_ref = refs[n_pages:]
    b = pl.program_id(0)
    k = pl.program_id(1)
    pos_q = past_len
    keys = GATHER_BLOCKS * CMP_BLOCK
    row = lax.broadcasted_iota(I32, (CMP_BLOCK, HEAD_DIM), 0)
    l1 = lax.broadcasted_iota(I32, (1, CMP_BLOCK), 1)
    hrow = lax.broadcasted_iota(I32, (ROWS_S, keys), 0)
    for g in range(G_NSA):
        new_k = jnp.broadcast_to(new_ref[g:g + 1, :], (CMP_BLOCK, HEAD_DIM))
        new_v = jnp.broadcast_to(new_ref[G_NSA + g:G_NSA + g + 1, :], (CMP_BLOCK, HEAD_DIM))
        ks, vs, dist = [], [], []
        for kk in range(GATHER_BLOCKS):
            blk = idx_ref[(b * G_NSA + g) * TOP_N + k * GATHER_BLOCKS + kk]
            is_new = blk >= nb0
            page_ref = page_refs[g * GATHER_BLOCKS + kk]
            k_blk = page_ref[pl.ds(2 * G_NSA + g, CMP_BLOCK, stride=NSA_CH), :]
            v_blk = page_ref[pl.ds(3 * G_NSA + g, CMP_BLOCK, stride=NSA_CH), :]
            ks.append(jnp.where(is_new, jnp.where(row == 0, new_k, 0.0), k_blk).astype(BF16))
            vs.append(jnp.where(is_new, jnp.where(row == 0, new_v, 0.0), v_blk).astype(BF16))
            dist.append(pos_q - (blk * CMP_BLOCK + l1))
        ks = jnp.concatenate(ks, axis=0)
        vs = jnp.concatenate(vs, axis=0)
        dist = jnp.concatenate(dist, axis=1)
        bucket = jnp.broadcast_to(_t5_bucket(dist), (ROWS_S, keys))
        bias = jnp.zeros((ROWS_S, keys), F32)
        for hh in range(HPG):
            for bk in range(N_BUCKETS):
                bias = jnp.where((hrow == hh) & (bucket == bk), rb_ref[bk, g * HPG + hh], bias)
        s = _dot_nt(q_ref[g], ks) * SCALE + bias
        logit = jnp.where(jnp.broadcast_to(dist >= 0, (ROWS_S, keys)), s, NEG_INF)
        fl = _Flash(m_ref.at[g], l_ref.at[g], a_ref.at[g])

        @pl.when(k == 0)
        def _():
            fl.first(logit, vs)

        @pl.when(k > 0)
        def _():
            fl.update(logit, vs)

        @pl.when(k == pl.num_programs(1) - 1)
        def _():
            o_ref[g] = gs_ref[g, 0] * oc_ref[g] + gs_ref[g, 1] * fl.result() + gs_ref[g, 2] * ow_ref[g]


def _nsa_gather(l, pt_flat, idx_flat, rel_bias, q_nsa, cache_nsa2d, new_s, gs, oc, ow, nb0,
                past_len, page):
    dec_b = q_nsa.shape[0]
    npg = pt_flat.shape[0] // dec_b
    halves = page // CMP_BLOCK
    body = functools.partial(_nsa_gather_body, nb0=nb0, past_len=past_len)

    def page_spec(g, kk):
        def imap(b, k, pt, idx):
            blk = jnp.minimum(idx[(b * G_NSA + g) * TOP_N + k * GATHER_BLOCKS + kk], nb0 - 1)
            return (l, pt[b * npg + blk // halves], blk % halves, 0)
        return pl.BlockSpec((None, None, CMP_BLOCK * NSA_CH, HEAD_DIM), imap)

    def per_b(shape):
        nd = len(shape)
        return pl.BlockSpec((None,) + tuple(shape), lambda b, k, pt, idx: (b,) + (0,) * nd)

    vec = (G_NSA, ROWS_S, HEAD_DIM)
    assert TOP_N % GATHER_BLOCKS == 0
    n_pages = G_NSA * GATHER_BLOCKS
    grid_spec = pltpu.PrefetchScalarGridSpec(
        num_scalar_prefetch=2,
        grid=(dec_b, TOP_N // GATHER_BLOCKS),
        in_specs=[pl.BlockSpec(memory_space=pltpu.SMEM), per_b(vec)]
        + [page_spec(g, kk) for g in range(G_NSA) for kk in range(GATHER_BLOCKS)]
        + [per_b((8, HEAD_DIM)), per_b((G_NSA, 3, ROWS_S, HEAD_DIM)), per_b(vec), per_b(vec)],
        out_specs=per_b(vec),
        scratch_shapes=[pltpu.VMEM((G_NSA, ROWS_S, 1), F32), pltpu.VMEM((G_NSA, ROWS_S, 1), F32),
                        pltpu.VMEM((G_NSA, ROWS_S, HEAD_DIM), F32)],
    )
    return pl.pallas_call(
        body,
        grid_spec=grid_spec,
        out_shape=jax.ShapeDtypeStruct((dec_b,) + vec, F32),
        compiler_params=_cparams(2),
        name="nsa_gather",
    )(pt_flat, idx_flat, rel_bias, q_nsa, *([cache_nsa2d] * n_pages), new_s, gs, oc, ow)


def _pad_rows(x, rows):
    return jnp.pad(x, ((0, rows - x.shape[0]),) + ((0, 0),) * (x.ndim - 1))


def _sample_inproj(xs, wt, p, dec_b):
    rows = xs.shape[0]
    zb, sb_rows, nsa_rows, win_rows, gates = _inproj(
        xs, p[0], wt["w_a"], wt["w_b"], wt["w_gate"], wt["gain"], rows, rows)
    zb8 = zb[:dec_b]
    q_sb = zb8[:, QS_BLK * LANE:KSB_BLK * LANE].reshape(dec_b, H_SB, HEAD_DIM)
    q_sb = jnp.pad(q_sb, ((0, 0), (0, ROWS_S - H_SB), (0, 0)))
    keep = lambda a: a[:dec_b * (a.shape[0] // rows)]
    return q_sb, (zb8, keep(sb_rows), keep(nsa_rows), keep(win_rows), gates)


def _sample_layer(l, xs, projected, walked, wt, p, sb_tabs, rel_bias, pt_flat, cache_nsa2d, win_l,
                  w_cmp0, dec_b, past_len, page):
    na, bk, bv, gk0, go, nf = p
    sbc, sbw = sb_tabs
    w_buf = win_l.shape[1]
    rows = xs.shape[0]
    zb8, sb_rows, nsa_rows, win_rows, gates = projected
    o_sb, kcr, vcr = walked
    o_sb = o_sb.reshape(dec_b, D_SB)

    q_nsa = zb8[:, :D_NSA].reshape(dec_b, G_NSA, HPG, HEAD_DIM)
    q_nsa = jnp.pad(q_nsa, ((0, 0), (0, 0), (0, ROWS_S - HPG), (0, 0)))
    nsa_new = nsa_rows.reshape(dec_b, 4 * G_NSA, HEAD_DIM)
    new_c = jnp.pad(nsa_new[:, :2 * G_NSA], ((0, 0), (0, 8 - 2 * G_NSA), (0, 0)))
    new_s = jnp.pad(nsa_new[:, 2 * G_NSA:], ((0, 0), (0, 8 - 2 * G_NSA), (0, 0)))
    new_w = win_rows.reshape(dec_b, 1, 2, G_NSA, HEAD_DIM)
    win_all = jnp.concatenate([win_l, new_w], axis=1)
    wp = sbw.shape[-1]
    win_t = jnp.transpose(win_all, (2, 0, 3, 1, 4))
    win_t = jnp.pad(win_t, ((0, 0), (0, 0), (0, 0), (0, wp - w_buf - 1), (0, 0))).astype(BF16)
    idx, oc, ow = _nsa_select(q_nsa, kcr, vcr, new_c, w_cmp0[0], w_cmp0[1], bk, bv, gk0, sbc,
                              win_t[0], win_t[1], sbw, past_len, w_buf)
    gs = gates[:dec_b].reshape(dec_b, G_NSA, LANE)[:, :, :3 * HPG].reshape(dec_b, G_NSA, 3, HPG)
    gs = jnp.pad(gs, ((0, 0), (0, 0), (0, 0), (0, ROWS_S - HPG)))
    gs = jnp.broadcast_to(gs[..., None], gs.shape + (HEAD_DIM,))
    o_nsa = _nsa_gather(l, pt_flat, idx[:, :, :, 0].reshape(-1), rel_bias, q_nsa, cache_nsa2d,
                        new_s, gs, oc, ow, kcr.shape[2], past_len, page)
    o_nsa = o_nsa[:, :, :HPG].reshape(dec_b, D_NSA)

    x1 = _outproj(_pad_rows(o_nsa, rows), _pad_rows(o_sb, rows), go, wt["w_out"], xs)
    x2 = _ffn(x1, nf, wt["w_gate_ffn"], wt["w_up"], wt["w_down"])
    new_win = win_all[:, 1:]
    return x2, sb_rows, nsa_rows, new_win


def kernel(x_prompt, x_sample, cache_sb_kv, cache_nsa_kv, state_win_kv, page_table, rel_bias,
           norm_attn, w_in, g_q, g_k, w_cmp_k, b_cmp_k, w_cmp_v, b_cmp_v, g_out, w_out,
           norm_ffn, w_gate, w_up, w_down):
    batch, seq, _ = x_prompt.shape
    dec_b, dec_t, _ = x_sample.shape
    assert dec_t == 1
    depth, n_pool, page = cache_sb_kv.shape[:3]
    npg = page_table.shape[1]
    past_len = npg * page
    w_buf = state_win_kv.shape[2]
    nb_s = past_len // CMP_BLOCK + 1
    nbp_s = -(-(nb_s + 7) // LANE) * LANE
    wp_s = -(-(w_buf + 1) // LANE) * LANE

    toep, bias_c, sc, sw = _bias_tables(rel_bias, seq, past_len, w_buf, nbp_s, wp_s)
    sbc = jnp.pad(sc.reshape(G_NSA, HPG, nbp_s), ((0, 0), (0, ROWS_S - HPG), (0, 0)))
    sbw = jnp.pad(sw.reshape(G_NSA, HPG, wp_s), ((0, 0), (0, ROWS_S - HPG), (0, 0)))
    pt_flat = page_table.reshape(-1).astype(I32)
    cache_nsa2d = cache_nsa_kv.reshape(depth, n_pool, page * NSA_CH, HEAD_DIM)

    xp = x_prompt.reshape(batch * seq, D_MODEL)
    xs = _pad_rows(x_sample.reshape(dec_b, D_MODEL), ROWS_S)
    sb_p, sb_s, nsa_p, nsa_s, win_p, win_s = [], [], [], [], [], []
    w_keep = min(WINDOW, seq)
    for l in range(depth):
        wt = _prep_weights(l, w_in, g_q, g_k, w_cmp_k, w_cmp_v, w_out, w_gate, w_up, w_down)
        p = (norm_attn[l][None], b_cmp_k[l][None], b_cmp_v[l][None], g_k[l, 0][None],
             g_out[l][None], norm_ffn[l][None])
        q_sb, projected = _sample_inproj(xs, wt, p, dec_b)
        xp, sb_rows, nsa_rows, win_rows, walked = _prompt_layer(
            xp, wt, p, (toep, bias_c), rel_bias, batch, seq, (l, pt_flat, q_sb, cache_sb_kv),
            (l, pt_flat, cache_nsa2d, wt["w_kv"], dec_b, page))
        sb_p.append(sb_rows.reshape(batch, seq, 2, H_SB, HEAD_DIM))
        nsa_p.append(nsa_rows.reshape(batch, seq, 4, G_NSA, HEAD_DIM))
        win_p.append(win_rows.reshape(batch, w_keep, 2, G_NSA, HEAD_DIM))

        w_cmp0 = (w_cmp_k[l, 0].astype(BF16), w_cmp_v[l, 0].astype(BF16))
        xs, sb_rows, nsa_rows, new_win = _sample_layer(
            l, xs, projected, walked, wt, p, (sbc, sbw), rel_bias, pt_flat, cache_nsa2d,
            state_win_kv[l], w_cmp0, dec_b, past_len, page)
        sb_s.append(sb_rows.reshape(dec_b, 1, 2, H_SB, HEAD_DIM))
        nsa_s.append(nsa_rows.reshape(dec_b, 1, 4, G_NSA, HEAD_DIM))
        win_s.append(new_win)
    return (xp.reshape(batch, seq, D_MODEL), xs[:dec_b].reshape(dec_b, 1, D_MODEL),
            jnp.stack(sb_p), jnp.stack(sb_s), jnp.stack(nsa_p), jnp.stack(nsa_s),
            jnp.stack(win_p), jnp.stack(win_s))
```

```python
import functools
import math

import jax
import jax.numpy as jnp
from jax import lax
from jax.experimental import pallas as pl
from jax.experimental.pallas import tpu as pltpu

F32 = jnp.float32
BF16 = jnp.bfloat16
I32 = jnp.int32

LANE = 128
HEAD_DIM = 128
G_NSA = 2
HPG = 4
H_NSA = G_NSA * HPG
H_SB = 8
D_NSA = H_NSA * HEAD_DIM
D_SB = H_SB * HEAD_DIM
D_MODEL = D_NSA + D_SB
D_KV = G_NSA * HEAD_DIM
CMP_BLOCK = 64
TOP_N = 16
N_LOCAL = 2
WINDOW = 512
N_BUCKETS = 32
MAX_DISTANCE = 128
EPS = 1e-6
FORCE_SCORE = 1e4
NEG_INF = -1e30
SCALE = HEAD_DIM ** -0.5

COL_TILE = 512
N_MAIN = D_NSA + 6 * D_KV + 3 * D_SB
QS_BLK = (D_NSA + 6 * D_KV) // LANE
KSB_BLK = QS_BLK + D_SB // LANE
VSB_BLK = KSB_BLK + D_SB // LANE
KS_BLK = (D_NSA + 2 * D_KV) // LANE
VS_BLK = KS_BLK + G_NSA
KW_BLK = VS_BLK + G_NSA
VW_BLK = KW_BLK + G_NSA
VMEM_LIMIT = 56 * 1024 * 1024

SB_TQ = 256
SB_TK = 256
SB_HB = 8
NSA_TQ = 256
SB_PAGES_PER_STEP = 16
CMP_PAGES_PER_STEP = 16


VMEM_LIMIT_FUSED = 62 * 1024 * 1024


def _cparams(n_axes, vmem_limit=VMEM_LIMIT):
    return pltpu.CompilerParams(dimension_semantics=("arbitrary",) * n_axes,
                                vmem_limit_bytes=vmem_limit)


def _dot(a, b):
    return jnp.dot(a, b, preferred_element_type=F32)


def _dot_nt(a, b):
    return lax.dot_general(a, b, (((1,), (1,)), ((), ())), preferred_element_type=F32)


def _split_bf16(x):
    hi = x.astype(BF16)
    lo = (x - hi.astype(F32)).astype(BF16)
    return hi, lo


def _dot_split(x, u):
    hi, lo = _split_bf16(x)
    return _dot(hi, u) + _dot(lo, u)


def _rms(y, gain):
    ms = jnp.mean(y * y, axis=-1, keepdims=True)
    return y * lax.rsqrt(ms + EPS) * gain


def _neg_softplus(z):
    return -(jnp.maximum(z, 0.0) + jnp.log(1.0 + jnp.exp(-jnp.abs(z))))


def _t5_bucket(dist):
    n = jnp.maximum(dist, 0)
    max_exact = N_BUCKETS // 2
    nf = jnp.maximum(n, 1).astype(F32)
    large = max_exact + (jnp.log(nf / max_exact) / math.log(MAX_DISTANCE / max_exact)
                         * (N_BUCKETS - max_exact)).astype(I32)
    large = jnp.minimum(large, N_BUCKETS - 1)
    return jnp.where(n < max_exact, n, large)


def _div(x, n):
    if n & (n - 1) == 0:
        return lax.shift_right_logical(x, n.bit_length() - 1)
    return x // n


def _mod(x, n):
    if n & (n - 1) == 0:
        return x & (n - 1)
    return x % n


def _pick_tile(m, candidates):
    for t in candidates:
        if m % t == 0:
            return t
    raise ValueError(f"no tile for {m}")


def _bias_body(rb_ref, toep_ref, bc_ref, sc_ref, sw_ref, *, seq, past_len, w_buf):
    h = pl.program_id(0)
    tq = NSA_TQ

    def table(dist):
        bk = _t5_bucket(dist)
        out = jnp.zeros(dist.shape, F32)
        for b in range(N_BUCKETS):
            out = jnp.where(bk == b, rb_ref[b, h], out)
        return out

    k = lax.broadcasted_iota(I32, (tq, tq), 0)
    q = lax.broadcasted_iota(I32, (tq, tq), 1)
    toep_ref[0] = table(tq + q - k)
    toep_ref[1] = table(q - k)
    nbr = -(-(seq // CMP_BLOCK) // 8) * 8
    n = lax.broadcasted_iota(I32, (nbr, tq), 0)
    qc = lax.broadcasted_iota(I32, (nbr, tq), 1)
    for i in range(seq // tq):
        bc_ref[i, 0:nbr, :] = table(i * tq + qc - CMP_BLOCK * n - (CMP_BLOCK - 1))
        if nbr < LANE:
            bc_ref[i, nbr:, :] = jnp.zeros((LANE - nbr, tq), F32)
    n1 = lax.broadcasted_iota(I32, sc_ref.shape, 1)
    sc_ref[...] = table(past_len - CMP_BLOCK * n1 - (CMP_BLOCK - 1))
    j1 = lax.broadcasted_iota(I32, sw_ref.shape, 1)
    sw_ref[...] = table(w_buf - j1)


def _bias_tables(rel_bias, seq, past_len, w_buf, nbp_s, wp_s):
    body = functools.partial(_bias_body, seq=seq, past_len=past_len, w_buf=w_buf)
    nq = seq // NSA_TQ
    return pl.pallas_call(
        body,
        grid=(H_NSA,),
        in_specs=[pl.BlockSpec(memory_space=pltpu.SMEM)],
        out_specs=[
            pl.BlockSpec((2, NSA_TQ, NSA_TQ), lambda h: (0, 0, h)),
            pl.BlockSpec((nq, LANE, NSA_TQ), lambda h: (0, 0, h)),
            pl.BlockSpec((None, 1, nbp_s), lambda h: (h, 0, 0)),
            pl.BlockSpec((None, 1, wp_s), lambda h: (h, 0, 0)),
        ],
        out_shape=[
            jax.ShapeDtypeStruct((2, NSA_TQ, H_NSA * NSA_TQ), F32),
            jax.ShapeDtypeStruct((nq, LANE, H_NSA * NSA_TQ), F32),
            jax.ShapeDtypeStruct((H_NSA, 1, nbp_s), F32),
            jax.ShapeDtypeStruct((H_NSA, 1, wp_s), F32),
        ],
        compiler_params=_cparams(1),
        name="bias_tables",
    )(rel_bias)


Q_TILES = D_NSA // COL_TILE
NSA_TILE0 = Q_TILES
WIN_TILE = NSA_TILE0 + 4 * D_KV // COL_TILE
SB_TILE0 = WIN_TILE + 2 * D_KV // COL_TILE + D_SB // COL_TILE


def _inproj_body(x_ref, na_ref, wa_ref, wb_ref, wg_ref, gain_ref, zb_ref, sb_ref, nsa_ref, win_ref,
                 gate_ref, h_ref, *, w_keep):
    j = pl.program_id(1)
    tm = x_ref.shape[0]
    half = COL_TILE // 2

    @pl.when(j == 0)
    def _():
        h = _rms(x_ref[...], na_ref[...]).astype(BF16)
        h_ref[...] = h
        gate_ref[...] = jax.nn.sigmoid(_dot(h, wg_ref[...]))

    gain = gain_ref[...]
    chunks = COL_TILE // LANE

    def emit(w_ref, norm_chunks, f32_ref=None, first=0, per_row=1, row0=0):
        h = h_ref[...]
        accs = [_dot(h, w_ref[:, :half]), _dot(h, w_ref[:, half:])]
        for c in range(chunks):
            sl = slice(c * LANE, (c + 1) * LANE)
            y = accs[c * LANE // half][:, (c * LANE) % half:(c * LANE) % half + LANE]
            if c in norm_chunks:
                y = _rms(y, gain[:, sl])
            zb_ref[:, sl] = y.astype(BF16)
            if f32_ref is not None:
                f32_ref[pl.ds(first + c, tm - row0, stride=per_row), :] = y[row0:, :]

    @pl.when(j < Q_TILES)
    def _():
        emit(wa_ref, (0, 1, 2, 3))

    @pl.when(j == NSA_TILE0)
    def _():
        emit(wa_ref, (), nsa_ref, 0, 2 * chunks)

    @pl.when(j == NSA_TILE0 + 1)
    def _():
        emit(wa_ref, (0, 1), nsa_ref, chunks, 2 * chunks)

    @pl.when(j == WIN_TILE)
    def _():
        emit(wa_ref, (0, 1), win_ref, 0, chunks, tm - w_keep)

    @pl.when((j > WIN_TILE) & (j < SB_TILE0))
    def _():
        emit(wb_ref, ())

    @pl.when(j >= SB_TILE0)
    def _():
        emit(wb_ref, (), sb_ref, (j - SB_TILE0) * chunks, 2 * H_SB)


def _inproj(x2d, na, w_a, w_b, w_gate, gain_cols, seq, w_keep):
    m = x2d.shape[0]
    tm = _pick_tile(seq, (1024, 512, 256, 128, 16))
    assert w_keep <= tm
    tpb = seq // tm
    sb_per, nsa_per, win_per = 2 * H_SB, 4 * G_NSA, 2 * G_NSA
    n_a = w_a.shape[1] // COL_TILE
    n_b = w_b.shape[1] // COL_TILE
    assert n_a == WIN_TILE + 1 and n_a + n_b == N_MAIN // COL_TILE
    body = functools.partial(_inproj_body, w_keep=w_keep)
    return pl.pallas_call(
        body,
        grid=(m // tm, n_a + n_b),
        in_specs=[
            pl.BlockSpec((tm, D_MODEL), lambda i, j: (i, 0)),
            pl.BlockSpec((1, D_MODEL), lambda i, j: (0, 0)),
            pl.BlockSpec((D_MODEL, COL_TILE), lambda i, j: (0, jnp.minimum(j, n_a - 1))),
            pl.BlockSpec((D_MODEL, COL_TILE), lambda i, j: (0, jnp.maximum(j - n_a, 0))),
            pl.BlockSpec((D_MODEL, 2 * LANE), lambda i, j: (0, 0)),
            pl.BlockSpec((1, COL_TILE), lambda i, j: (0, j)),
        ],
        out_specs=[
            pl.BlockSpec((tm, COL_TILE), lambda i, j: (i, j)),
            pl.BlockSpec((tm * sb_per, LANE), lambda i, j: (i, 0), pipeline_mode=pl.Buffered(1)),
            pl.BlockSpec((tm * nsa_per, LANE), lambda i, j: (i, 0), pipeline_mode=pl.Buffered(1)),
            pl.BlockSpec((w_keep * win_per, LANE), lambda i, j: (_div(i, tpb), 0)),
            pl.BlockSpec((tm, 2 * LANE), lambda i, j: (i, 0)),
        ],
        out_shape=[
            jax.ShapeDtypeStruct((m, N_MAIN), BF16),
            jax.ShapeDtypeStruct((m * sb_per, LANE), F32),
            jax.ShapeDtypeStruct((m * nsa_per, LANE), F32),
            jax.ShapeDtypeStruct((m // seq * w_keep * win_per, LANE), F32),
            jax.ShapeDtypeStruct((m, 2 * LANE), F32),
        ],
        scratch_shapes=[pltpu.VMEM((tm, D_MODEL), BF16)],
        compiler_params=_cparams(2),
        name="inproj",
    )(x2d, na, w_a, w_b, w_gate, gain_cols)


def _outproj_body(on_ref, os_ref, go_ref, w_ref, x_ref, y_ref, h_ref):
    j = pl.program_id(1)

    @pl.when(j == 0)
    def _():
        go = go_ref[...]
        h_ref[:, :D_NSA] = _rms(on_ref[...], go[:, :D_NSA]).astype(BF16)
        h_ref[:, D_NSA:] = _rms(os_ref[...], go[:, D_NSA:]).astype(BF16)

    y_ref[...] = x_ref[...] + _dot(h_ref[...], w_ref[...])


def _outproj(o_nsa, o_sb, g_out, w_out, x2d):
    m = x2d.shape[0]
    tm = _pick_tile(m, (512, 256, 128, 16))
    tn = D_MODEL
    return pl.pallas_call(
        _outproj_body,
        grid=(m // tm, D_MODEL // tn),
        in_specs=[
            pl.BlockSpec((tm, D_NSA), lambda i, j: (i, 0)),
            pl.BlockSpec((tm, D_SB), lambda i, j: (i, 0)),
            pl.BlockSpec((1, D_MODEL), lambda i, j: (0, 0)),
            pl.BlockSpec((D_MODEL, tn), lambda i, j: (0, j), pipeline_mode=pl.Buffered(1)),
            pl.BlockSpec((tm, tn), lambda i, j: (i, j)),
        ],
        out_specs=pl.BlockSpec((tm, tn), lambda i, j: (i, j)),
        out_shape=jax.ShapeDtypeStruct((m, D_MODEL), F32),
        scratch_shapes=[pltpu.VMEM((tm, D_MODEL), BF16)],
        compiler_params=_cparams(2),
        name="outproj",
    )(o_nsa, o_sb, g_out, w_out, x2d)


def _ffn_body(*refs, sb):
    if sb is None:
        x_ref, nf_ref, wg_ref, wu_ref, wd_ref, y_ref, h_ref = refs
    else:
        pps = sb["pps"]
        x_ref, nf_ref, wg_ref, wu_ref, wd_ref, q_ref = refs[1:7]
        page_refs = refs[7:7 + pps]
        up_ref, upp_ref, y_ref, o_ref, h_ref, z_ref, sacc_ref = refs[7 + pps:]
    i = pl.program_id(0)
    j = pl.program_id(1)

    @pl.when(j == 0)
    def _():
        x = x_ref[...]
        h_ref[...] = _rms(x, nf_ref[...]).astype(BF16)
        y_ref[...] = x

    h = h_ref[...]
    act = jax.nn.silu(_dot(h, wg_ref[...])) * _dot(h, wu_ref[...])
    y_ref[...] += _dot(act.astype(BF16), wd_ref[...])

    if sb is not None:
        nc = sb["n_chunks"]
        s = i * pl.num_programs(1) + j

        @pl.when(s < sb["n_steps"])
        def _():
            _sb_dec_step(_mod(_div(s, nc), 2), _mod(s, nc), nc - 1, q_ref, page_refs, up_ref, upp_ref, o_ref,
                         z_ref, sacc_ref, sb["page"])


def _ffn(x2d, norm_ffn, w_gate, w_up, w_down, sb_args=None):
    m = x2d.shape[0]
    d_ff = w_gate.shape[1]
    if sb_args is None:
        tm = _pick_tile(m, (512, 256, 128, 16))
        tf = _pick_tile(d_ff, (512, 256))
        x_mode = None
    else:
        tm = _pick_tile(m, (1024, 512, 256, 128))
        tf = _pick_tile(d_ff, (256,))
        x_mode = pl.Buffered(1)
    nf = d_ff // tf
    in_specs = [
        pl.BlockSpec((tm, D_MODEL), lambda i, j, *_: (i, 0), pipeline_mode=x_mode),
        pl.BlockSpec((1, D_MODEL), lambda i, j, *_: (0, 0)),
        pl.BlockSpec((D_MODEL, tf), lambda i, j, *_: (0, j)),
        pl.BlockSpec((D_MODEL, tf), lambda i, j, *_: (0, j)),
        pl.BlockSpec((tf, D_MODEL), lambda i, j, *_: (j, 0)),
    ]
    y_spec = pl.BlockSpec((tm, D_MODEL), lambda i, j, *_: (i, 0))
    y_shape = jax.ShapeDtypeStruct((m, D_MODEL), F32)
    h_scratch = pltpu.VMEM((tm, D_MODEL), BF16)
    if sb_args is None:
        return pl.pallas_call(
            functools.partial(_ffn_body, sb=None),
            grid=(m // tm, nf),
            in_specs=in_specs,
            out_specs=y_spec,
            out_shape=y_shape,
            scratch_shapes=[h_scratch],
            compiler_params=_cparams(2),
            name="ffn",
        )(x2d, norm_ffn, w_gate, w_up, w_down)

    l, pt_flat, q_sb, cache_sb = sb_args
    dec_b = q_sb.shape[0]
    page = cache_sb.shape[2]
    npg = pt_flat.shape[0] // dec_b
    pps = SB_PAGES_PER_STEP
    nc = npg // pps
    n_steps = dec_b * 2 * nc
    assert n_steps <= (m // tm) * nf
    width = page * H_SB
    up, upp = _sb_dec_consts(npg, page)

    def pos(i, j):
        s = jnp.minimum(i * nf + j, n_steps - 1)
        return _div(s, 2 * nc), _mod(_div(s, nc), 2), _mod(s, nc)

    def page_spec(u):
        def imap(i, j, pt):
            b, ph, c = pos(i, j)
            return (l, pt[b * npg + c * pps + u], 0, ph, 0, 0)
        return pl.BlockSpec((None, None, page, None, H_SB, HEAD_DIM), imap)

    once = pl.Buffered(1)
    grid_spec = pltpu.PrefetchScalarGridSpec(
        num_scalar_prefetch=1,
        grid=(m // tm, nf),
        in_specs=in_specs
        + [pl.BlockSpec((None, ROWS_S, HEAD_DIM), lambda i, j, pt: (pos(i, j)[0], 0, 0))]
        + [page_spec(u) for u in range(pps)]
        + [pl.BlockSpec((width, 2 * width), lambda i, j, pt: (0, 0), pipeline_mode=once),
           pl.BlockSpec((npg, npg), lambda i, j, pt: (0, 0), pipeline_mode=once)],
        out_specs=[y_spec,
                   pl.BlockSpec((None, H_SB, HEAD_DIM), lambda i, j, pt: (pos(i, j)[0], 0, 0))],
        scratch_shapes=[h_scratch, pltpu.VMEM((npg, width), F32),
                        pltpu.VMEM((ROWS_S, HEAD_DIM), F32)],
    )
    sb = dict(pps=pps, page=page, n_chunks=nc, n_steps=n_steps)
    return pl.pallas_call(
        functools.partial(_ffn_body, sb=sb),
        grid_spec=grid_spec,
        out_shape=[y_shape, jax.ShapeDtypeStruct((dec_b, H_SB, HEAD_DIM), F32)],
        compiler_params=_cparams(2, VMEM_LIMIT_FUSED),
        name="ffn_sb_decode",
    )(pt_flat, x2d, norm_ffn, w_gate, w_up, w_down, q_sb, *([cache_sb] * pps), up, upp)


SB_SUB = 4


def _sb_body(*refs, n_cast, cast_steps):
    n_sub = SB_HB // SB_SUB
    q_refs, k_refs, v_refs = refs[:n_sub], refs[n_sub:2 * n_sub], refs[2 * n_sub:3 * n_sub]
    u_ref = refs[3 * n_sub]
    w_refs = refs[3 * n_sub + 1:3 * n_sub + 1 + n_cast]
    o_ref = refs[3 * n_sub + 1 + n_cast]
    wb_refs = refs[3 * n_sub + 2 + n_cast:3 * n_sub + 2 + 2 * n_cast]
    acc_ref, run_ref = refs[3 * n_sub + 2 + 2 * n_cast:]
    i = pl.program_id(2)
    pos = (pl.program_id(0) * pl.num_programs(1) + pl.program_id(1)) * pl.num_programs(2) + i

    @pl.when(pos < cast_steps)
    def _():
        for w_ref, wb_ref in zip(w_refs, wb_refs):
            wb_ref[...] = w_ref[...].astype(BF16)

    row = lax.broadcasted_iota(I32, (SB_TQ, SB_TK), 0)
    col = lax.broadcasted_iota(I32, (SB_TQ, SB_TK), 1)
    strict = col < row

    def tile(kb, diagonal):
        off = pl.multiple_of(kb * SB_TK, SB_TK)
        heads = range(SB_HB)
        sls = [slice(h % SB_SUB * HEAD_DIM, (h % SB_SUB + 1) * HEAD_DIM) for h in heads]
        q_ref = [q_refs[h // SB_SUB] for h in heads]
        k_ref = [k_refs[h // SB_SUB] for h in heads]
        v_ref = [v_refs[h // SB_SUB] for h in heads]
        z = [_dot_nt(q_ref[h][:, sls[h]], k_ref[h][pl.ds(off, SB_TK), sls[h]]) * SCALE
             for h in heads]
        log_1m = [_neg_softplus(zh) for zh in z]
        if diagonal:
            log_1m = [jnp.where(strict, lh, 0.0) for lh in log_1m]
        parts = [_split_bf16(lh) for lh in log_1m]
        u = u_ref[...]
        later = [_dot(hi, u) + _dot(lo, u) for hi, lo in parts]
        total = [jnp.sum(lh, axis=-1, keepdims=True) for lh in log_1m]
        if diagonal:
            a = [jnp.where(strict, jnp.exp(z[h] + log_1m[h] + later[h]), 0.0) for h in heads]
        else:
            run = [run_ref[h] for h in heads]
            a = [jnp.exp(z[h] + log_1m[h] + later[h] + run[h]) for h in heads]
        pv = [_dot(a[h].astype(BF16), v_ref[h][pl.ds(off, SB_TK), sls[h]]) for h in heads]
        for h in heads:
            if diagonal:
                acc_ref[h] = pv[h]
                run_ref[h] = total[h]
            else:
                acc_ref[h] += pv[h]
                run_ref[h] = run[h] + total[h]

    tile(i, True)

    def step(it, carry):
        tile(i - 1 - it, False)
        return carry

    lax.fori_loop(0, i, step, 0)
    for h in range(SB_HB):
        o_ref[:, h * HEAD_DIM:(h + 1) * HEAD_DIM] = acc_ref[h]


def _sb_prompt(zb, batch, seq, l, weights):
    nq = seq // SB_TQ
    n_grid = batch * (H_SB // SB_HB) * nq
    cast_steps = 1
    while (cast_steps * 2 <= n_grid
           and all(w.shape[1] % (cast_steps * 2 * 16) == 0 for w in weights)):
        cast_steps *= 2

    def w_spec(w, stacked):
        rows = w.shape[1] // cast_steps
        imap = lambda b, h, i: (jnp.minimum((b * (H_SB // SB_HB) + h) * nq + i, cast_steps - 1), 0)
        if stacked:
            return pl.BlockSpec((None, rows, w.shape[2]), lambda b, h, i: (l,) + imap(b, h, i))
        return pl.BlockSpec((rows, w.shape[2]), imap)

    width = SB_HB * HEAD_DIM
    sub_w = SB_SUB * HEAD_DIM
    n_sub = SB_HB // SB_SUB
    assert QS_BLK % SB_SUB == 0 and KSB_BLK % SB_SUB == 0 and VSB_BLK % SB_SUB == 0
    r = lax.broadcasted_iota(I32, (SB_TK, SB_TK), 0)
    c = lax.broadcasted_iota(I32, (SB_TK, SB_TK), 1)
    u = (r > c).astype(BF16)

    def q_spec(n):
        return pl.BlockSpec((SB_TQ, sub_w),
                            lambda b, h, i: (b * nq + i, QS_BLK // SB_SUB + h * n_sub + n))

    def kv_spec(blk, n):
        return pl.BlockSpec((seq, sub_w), lambda b, h, i: (b, blk // SB_SUB + h * n_sub + n))

    subs = range(n_sub)
    outs = pl.pallas_call(
        functools.partial(_sb_body, n_cast=len(weights), cast_steps=cast_steps),
        grid=(batch, H_SB // SB_HB, nq),
        in_specs=[q_spec(n) for n in subs] + [kv_spec(KSB_BLK, n) for n in subs]
        + [kv_spec(VSB_BLK, n) for n in subs]
        + [pl.BlockSpec((SB_TK, SB_TK), lambda b, h, i: (0, 0))]
        + [w_spec(w, True) for w in weights],
        out_specs=[pl.BlockSpec((SB_TQ, width), lambda b, h, i: (b * nq + i, h))]
        + [w_spec(w, False) for w in weights],
        out_shape=[jax.ShapeDtypeStruct((batch * seq, D_SB), F32)]
        + [jax.ShapeDtypeStruct(w.shape[1:], BF16) for w in weights],
        scratch_shapes=[pltpu.VMEM((SB_HB, SB_TQ, HEAD_DIM), F32), pltpu.VMEM((SB_HB, SB_TQ, 1), F32)],
        compiler_params=_cparams(3),
        name="sb_prompt_cast",
    )(*([zb] * (3 * n_sub)), u, *weights)
    return outs[0], outs[1:]


def _cmp_prompt_body(x_ref, wk_ref, wv_ref, bk_ref, bv_ref, gk_ref, kc_ref, vct_ref, *, nb):
    pad = jnp.zeros((LANE - nb, HEAD_DIM), F32)
    per = 4 * G_NSA
    for c in range(2 * G_NSA):
        rows = jnp.concatenate(
            [x_ref[pl.ds(l * per + c, nb, stride=CMP_BLOCK * per), :] for l in range(CMP_BLOCK)],
            axis=1)
        rows = rows.astype(BF16)
        if c < G_NSA:
            y = _rms(_dot(rows, wk_ref[...]) + bk_ref[...], gk_ref[...])
            kc_ref[c] = jnp.concatenate([y, pad], axis=0).astype(BF16)
        else:
            y = _dot(rows, wv_ref[...]) + bv_ref[...]
            vct_ref[c - G_NSA] = jnp.concatenate([y, pad], axis=0).T.astype(BF16)


def _cmp_prompt(nsa_rows, batch, seq, wk, wv, bk, bv, gk0):
    nb = seq // CMP_BLOCK
    assert nb % 8 == 0 and nb < LANE
    body = functools.partial(_cmp_prompt_body, nb=nb)
    shape = jax.ShapeDtypeStruct((batch, G_NSA, LANE, HEAD_DIM), BF16)

    return pl.pallas_call(
        body,
        grid=(batch,),
        in_specs=[
            pl.BlockSpec((seq * 4 * G_NSA, HEAD_DIM), lambda b: (b, 0)),
            pl.BlockSpec(wk.shape, lambda b: (0, 0)),
            pl.BlockSpec(wv.shape, lambda b: (0, 0)),
            pl.BlockSpec((1, HEAD_DIM), lambda b: (0, 0)),
            pl.BlockSpec((1, HEAD_DIM), lambda b: (0, 0)),
            pl.BlockSpec((1, HEAD_DIM), lambda b: (0, 0)),
        ],
        out_specs=[pl.BlockSpec((None, G_NSA, LANE, HEAD_DIM), lambda b: (b, 0, 0, 0))] * 2,
        out_shape=[shape, shape],
        compiler_params=_cparams(1),
        name="cmp_prompt",
    )(nsa_rows, wk, wv, bk, bv, gk0)


def _masked_softmax(s, valid):
    logit = jnp.where(valid, s, NEG_INF)
    e = jnp.exp(logit - jnp.max(logit, axis=-1, keepdims=True))
    return jnp.where(valid, e / jnp.sum(e, axis=-1, keepdims=True), 0.0)


class _Flash:
    def __init__(self, m_ref, l_ref, a_ref):
        self.m_ref, self.l_ref, self.a_ref = m_ref, l_ref, a_ref

    def first(self, logit, v):
        m = jnp.max(logit, axis=-1, keepdims=True)
        p = jnp.exp(logit - m)
        self.m_ref[...] = m
        self.l_ref[...] = jnp.sum(p, axis=-1, keepdims=True)
        self.a_ref[...] = _dot(p.astype(BF16), v)

    def update(self, logit, v):
        m_old = self.m_ref[...]
        m = jnp.maximum(m_old, jnp.max(logit, axis=-1, keepdims=True))
        alpha = jnp.exp(m_old - m)
        p = jnp.exp(logit - m)
        self.m_ref[...] = m
        self.l_ref[...] = alpha * self.l_ref[...] + jnp.sum(p, axis=-1, keepdims=True)
        self.a_ref[...] = alpha * self.a_ref[...] + _dot(p.astype(BF16), v)

    def result(self):
        return self.a_ref[...] / self.l_ref[...]


class _FlashT:
    def __init__(self, m_ref, l_ref, a_ref):
        self.m_ref, self.l_ref, self.a_ref = m_ref, l_ref, a_ref

    @staticmethod
    def step(states, logits, v_ts, first):
        ms = [jnp.max(lg, axis=0, keepdims=True) for lg in logits]
        if not first:
            olds = [st.m_ref[...] for st in states]
            ms = [jnp.maximum(o, m) for o, m in zip(olds, ms)]
            alphas = [jnp.exp(o - m) for o, m in zip(olds, ms)]
        ps = [jnp.exp(lg - m) for lg, m in zip(logits, ms)]
        sums = [jnp.sum(p, axis=0, keepdims=True) for p in ps]
        pvs = [_dot(v_t, p.astype(BF16)) for v_t, p in zip(v_ts, ps)]
        for n, st in enumerate(states):
            st.m_ref[...] = ms[n]
            if first:
                st.l_ref[...] = sums[n]
                st.a_ref[...] = pvs[n]
            else:
                st.l_ref[...] = alphas[n] * st.l_ref[...] + sums[n]
                st.a_ref[...] = alphas[n] * st.a_ref[...] + pvs[n]

    def update(self, logit, v_t):
        _FlashT.step([self], [logit], [v_t], False)

    def result(self):
        return self.a_ref[...] / self.l_ref[...]


def _nsa_body(pt_ref, rb_ref, q_ref, kc_ref, vct_ref, ks_ref, vs_ref, kw_ref, vw_ref, g_ref,
              toep_ref, bc_ref, et_ref, *rest, nb, seq, cmp):
    page_refs = rest[:cmp["pps"]]
    (wkv_ref, o_ref, kcr_ref, vcr_ref, vst_ref, vwt_ref, m_s, l_s, a_s, m_w, l_w, a_w, yk_ref,
     yv_ref) = rest[cmp["pps"]:]
    g = pl.program_id(1)
    i = pl.program_id(2)
    tq = NSA_TQ
    sub = tq // LANE
    width = HPG * tq
    q0 = i * tq
    step = (pl.program_id(0) * pl.num_programs(1) + g) * pl.num_programs(2) + i

    @pl.when(step < cmp["n_steps"])
    def _():
        _cmp_dec_step(page_refs, wkv_ref, kcr_ref, vcr_ref, yk_ref, yv_ref, cmp["page"])

    @pl.when(i == 0)
    def _():
        for c in range(seq // LANE):
            rows = slice(c * LANE, (c + 1) * LANE)
            cols = slice((c % sub) * LANE, (c % sub + 1) * LANE)
            vst_ref[c // sub, :, cols] = vs_ref[rows, :].astype(F32).T.astype(BF16)
            vwt_ref[c // sub, :, cols] = vw_ref[rows, :].astype(F32).T.astype(BF16)

    qs = jnp.concatenate([q_ref[:, hh * LANE:(hh + 1) * LANE] for hh in range(HPG)], axis=0)

    def q_off(rows):
        return lax.broadcasted_iota(I32, (rows, width), 1) % tq

    n_r = lax.broadcasted_iota(I32, (LANE, width), 0)
    s_c = _dot_nt(kc_ref[...], qs) * SCALE + bc_ref[...]
    valid_c = (CMP_BLOCK * n_r + (CMP_BLOCK - 1) <= q0 + q_off(LANE)) & (n_r < nb)
    logit = jnp.where(valid_c, s_c, NEG_INF)
    e = jnp.exp(logit - jnp.max(logit, axis=0, keepdims=True))
    p_c = jnp.where(valid_c, e / jnp.sum(e, axis=0, keepdims=True), 0.0)
    oc_t = _dot(vct_ref[...], p_c.astype(BF16))
    imp_t = p_c[:, 0:tq]
    for hh in range(1, HPG):
        imp_t = imp_t + p_c[:, hh * tq:(hh + 1) * tq]

    nbr = -(-nb // 8) * 8
    n_i = lax.broadcasted_iota(I32, (nbr, tq), 0)
    cur = (q0 + lax.broadcasted_iota(I32, (nbr, tq), 1)) // CMP_BLOCK
    forced = (n_i == 0) | (n_i >= cur - (N_LOCAL - 1))
    score = jnp.where(n_i > cur, -jnp.inf, jnp.where(forced, FORCE_SCORE, imp_t[0:nbr]))
    rank = jnp.zeros((nbr, tq), I32)
    for j in range(nb):
        sj = jnp.broadcast_to(score[j:j + 1, :], (nbr, tq))
        ahead = (sj > score) | ((sj == score) & (n_i > j))
        rank = rank + ahead.astype(I32)
    sel_t = ((rank < min(TOP_N, nb)) & (score > -jnp.inf)).astype(F32)
    if nbr < LANE:
        sel_t = jnp.concatenate([sel_t, jnp.zeros((LANE - nbr, tq), F32)], axis=0)
    sel_t = sel_t.astype(BF16)

    k_r = lax.broadcasted_iota(I32, (tq, width), 0)
    q_c = q_off(tq)
    causal = k_r <= q_c
    bias_far = jnp.concatenate(
        [jnp.full((1, tq), rb_ref[N_BUCKETS - 1, g * HPG + hh], F32) for hh in range(HPG)], axis=1)

    def chunk(ref, c):
        return ref[pl.ds(pl.multiple_of(c * tq, tq), tq), :]

    def scores(ref, c):
        return _dot_nt(chunk(ref, c), qs) * SCALE

    def selected(c):
        hit = _dot(chunk(et_ref, c), sel_t)
        return jnp.concatenate([hit] * HPG, axis=1) > 0.5

    fs = _FlashT(m_s, l_s, a_s)
    fw = _FlashT(m_w, l_w, a_w)

    def both(c, bias, sel_extra, win_mask, first):
        sel_mask = selected(c) if sel_extra is None else selected(c) & sel_extra
        lg_s = jnp.where(sel_mask, scores(ks_ref, c) + bias, NEG_INF)
        lg_w = scores(kw_ref, c) + bias
        if win_mask is not None:
            lg_w = jnp.where(win_mask, lg_w, NEG_INF)
        _FlashT.step([fs, fw], [lg_s, lg_w], [vst_ref[c], vwt_ref[c]], first)

    both(i, toep_ref[1], causal, causal, True)

    @pl.when(i >= 1)
    def _():
        both(i - 1, toep_ref[0], None, None, False)

    @pl.when(i >= 2)
    def _():
        both(i - 2, bias_far, None, k_r >= q_c, False)

    def far_step(c, carry):
        fs.update(jnp.where(selected(c), scores(ks_ref, c) + bias_far, NEG_INF), vst_ref[c])
        return carry

    lax.fori_loop(0, jnp.maximum(i - 2, 0), far_step, 0)

    g_t = jnp.concatenate([g_ref[t * LANE:(t + 1) * LANE, :].T for t in range(sub)], axis=1)

    def gate(branch):
        return jnp.concatenate(
            [g_t[branch * HPG + hh:branch * HPG + hh + 1, :] for hh in range(HPG)], axis=1)

    o_t = gate(0) * oc_t + gate(1) * fs.result() + gate(2) * fw.result()
    for hh in range(HPG):
        for t in range(sub):
            lanes = slice(hh * tq + t * LANE, hh * tq + (t + 1) * LANE)
            o_ref[t * LANE:(t + 1) * LANE, hh * LANE:(hh + 1) * LANE] = o_t[:, lanes].T


def _nsa_prompt(rel_bias, zb, kc, vct, gates, toep, bias_c, batch, seq, cmp_args):
    tq = NSA_TQ
    assert WINDOW == 2 * tq and tq % LANE == 0 and seq % tq == 0
    nq = seq // tq
    nb = seq // CMP_BLOCK
    assert nb <= LANE
    width = HPG * tq
    k = lax.broadcasted_iota(I32, (seq, LANE), 0)
    n = lax.broadcasted_iota(I32, (seq, LANE), 1)
    expand_t = (k // CMP_BLOCK == n).astype(BF16)

    l, pt_flat, cache_nsa2d, w_kv, dec_b, page = cmp_args
    npg = pt_flat.shape[0] // dec_b
    pps = min(CMP_PAGES_PER_STEP, npg)
    assert npg % pps == 0
    nchunk = npg // pps
    n_cmp = dec_b * nchunk
    assert n_cmp <= batch * G_NSA * nq
    halves = page // CMP_BLOCK
    nblk = pps * halves
    body = functools.partial(_nsa_body, nb=nb, seq=seq,
                             cmp=dict(pps=pps, page=page, n_steps=n_cmp))

    def kv_spec(blk):
        return pl.BlockSpec((seq, HEAD_DIM), lambda b, g, i, pt: (b, blk + g))

    def cmp_pos(b, g, i):
        s = jnp.minimum((b * G_NSA + g) * nq + i, n_cmp - 1)
        return _div(s, nchunk), _mod(s, nchunk)

    def page_spec(u):
        def imap(b, g, i, pt):
            row, c = cmp_pos(b, g, i)
            return (l, pt[row * npg + c * pps + u], 0, 0)
        return pl.BlockSpec((None, None, page * NSA_CH, HEAD_DIM), imap)

    def raw_imap(b, g, i, pt):
        row, c = cmp_pos(b, g, i)
        return (row, 0, c, 0)

    raw_spec = pl.BlockSpec((None, G_NSA, nblk, HEAD_DIM), raw_imap)
    raw_shape = jax.ShapeDtypeStruct((dec_b, G_NSA, npg * halves, HEAD_DIM), F32)
    stat = pltpu.VMEM((1, width), F32)
    accum = pltpu.VMEM((HEAD_DIM, width), F32)
    v_t = pltpu.VMEM((nq, HEAD_DIM, tq), BF16)
    y_raw = pltpu.VMEM((nblk * NSA_CH, HEAD_DIM), F32)
    grid_spec = pltpu.PrefetchScalarGridSpec(
        num_scalar_prefetch=1,
        grid=(batch, G_NSA, nq),
        in_specs=[
            pl.BlockSpec(memory_space=pltpu.SMEM),
            pl.BlockSpec((tq, HPG * HEAD_DIM), lambda b, g, i, pt: (b * nq + i, g)),
            pl.BlockSpec((None, None, LANE, HEAD_DIM), lambda b, g, i, pt: (b, g, 0, 0)),
            pl.BlockSpec((None, None, HEAD_DIM, LANE), lambda b, g, i, pt: (b, g, 0, 0)),
            kv_spec(KS_BLK), kv_spec(VS_BLK), kv_spec(KW_BLK), kv_spec(VW_BLK),
            pl.BlockSpec((tq, LANE), lambda b, g, i, pt: (b * nq + i, g)),
            pl.BlockSpec((2, tq, width), lambda b, g, i, pt: (0, 0, g)),
            pl.BlockSpec((None, LANE, width), lambda b, g, i, pt: (i, 0, g)),
            pl.BlockSpec((seq, LANE), lambda b, g, i, pt: (0, 0)),
        ]
        + [page_spec(u) for u in range(pps)]
        + [pl.BlockSpec(w_kv.shape, lambda b, g, i, pt: (0, 0), pipeline_mode=pl.Buffered(1))],
        out_specs=[pl.BlockSpec((tq, HPG * HEAD_DIM), lambda b, g, i, pt: (b * nq + i, g)),
                   raw_spec, raw_spec],
        scratch_shapes=[v_t, v_t, stat, stat, accum, stat, stat, accum, y_raw, y_raw],
    )
    return pl.pallas_call(
        body,
        grid_spec=grid_spec,
        out_shape=[jax.ShapeDtypeStruct((batch * seq, D_NSA), F32), raw_shape, raw_shape],
        compiler_params=_cparams(3),
        name="nsa_prompt_cmp_decode",
    )(pt_flat, rel_bias, zb, kc, vct, zb, zb, zb, zb, gates, toep, bias_c, expand_t,
      *([cache_nsa2d] * pps), w_kv)


def _prep_weights(l, w_in, g_q, g_k, w_cmp_k, w_cmp_v):
    n_gate = 3 * H_NSA
    g0 = D_NSA + 6 * D_KV
    wi = w_in[l]
    w_a = wi[:, :g0].astype(BF16)
    w_b = wi[:, g0 + n_gate:].astype(BF16)
    wg = wi[:, g0:g0 + n_gate].reshape(D_MODEL, 3, G_NSA, HPG)
    wg = jnp.transpose(wg, (0, 2, 1, 3)).reshape(D_MODEL, G_NSA, 3 * HPG)
    wg = jnp.pad(wg, ((0, 0), (0, 0), (0, LANE - 3 * HPG))).reshape(D_MODEL, G_NSA * LANE)
    ones = jnp.ones((HEAD_DIM,), F32)
    gain = jnp.concatenate(
        [jnp.tile(g_q[l], H_NSA), jnp.tile(ones, 2 * G_NSA),
         jnp.tile(g_k[l, 1], G_NSA), jnp.tile(ones, G_NSA),
         jnp.tile(g_k[l, 2], G_NSA), jnp.tile(ones, G_NSA),
         jnp.tile(ones, 3 * H_SB)])[None, :]
    wk = w_cmp_k[l].reshape(CMP_BLOCK * HEAD_DIM, HEAD_DIM).astype(BF16)
    wv = w_cmp_v[l].reshape(CMP_BLOCK * HEAD_DIM, HEAD_DIM).astype(BF16)
    return dict(
        w_a=w_a, w_b=w_b, w_gate=wg.astype(BF16), gain=gain, wk=wk, wv=wv,
        w_kv=jnp.concatenate([wk, wv], axis=1))


def _prompt_layer(x2d, l, wt, dense_w, p, tables, rel_bias, batch, seq, sb_args, cmp_args):
    na, bk, bv, gk0, go, nf = p
    zb, sb_rows, nsa_rows, win_rows, gates = _inproj(
        x2d, na, wt["w_a"], wt["w_b"], wt["w_gate"], wt["gain"], seq, min(WINDOW, seq))
    kc, vct = _cmp_prompt(nsa_rows, batch, seq, wt["wk"], wt["wv"], bk, bv, gk0)
    o_nsa, kcr, vcr = _nsa_prompt(rel_bias, zb, kc, vct, gates, tables[0], tables[1], batch, seq,
                                  cmp_args)
    o_sb, dense_b = _sb_prompt(zb, batch, seq, l, dense_w)
    wt["w_out"], wt["w_gate_ffn"], wt["w_up"], wt["w_down"] = dense_b
    x1 = _outproj(o_nsa, o_sb, go, wt["w_out"], x2d)
    x2, o_sb_sample = _ffn(x1, nf, wt["w_gate_ffn"], wt["w_up"], wt["w_down"], sb_args)
    return x2, sb_rows, nsa_rows, win_rows, (o_sb_sample, kcr, vcr)


ROWS_S = 16


def _sb_dec_step(ph, c, last_c, q_ref, page_refs, up_ref, upp_ref, o_ref, z_ref, acc_ref, page):
    pps = len(page_refs)
    width = page * H_SB
    r = lax.broadcasted_iota(I32, (ROWS_S, width), 0)
    lane = lax.broadcasted_iota(I32, (ROWS_S, width), 1)
    own_head = (lane % H_SB) == r

    @pl.when(ph == 0)
    def _():
        q = q_ref[...]
        for u in range(pps):
            ks = page_refs[u][...].reshape(width, HEAD_DIM).astype(BF16)
            zt = _dot_nt(q, ks)
            z_ref[pl.ds(c * pps + u, 1), :] = jnp.sum(jnp.where(own_head, zt, 0.0), axis=0,
                                                      keepdims=True)

    @pl.when((ph == 1) & (c == 0))
    def _():
        z = z_ref[...] * SCALE
        log_1m = _neg_softplus(z)
        cs = _dot_split(log_1m, up_ref[...])
        hi, lo = _split_bf16(cs[:, width:])
        later_pages = _dot(upp_ref[...], hi) + _dot(upp_ref[...], lo)
        z_ref[...] = jnp.exp(z + log_1m + cs[:, :width] + later_pages)
        acc_ref[...] = jnp.zeros_like(acc_ref)

    @pl.when(ph == 1)
    def _():
        for u in range(pps):
            vs = page_refs[u][...].reshape(width, HEAD_DIM).astype(BF16)
            a = jnp.broadcast_to(z_ref[pl.ds(c * pps + u, 1), :], (ROWS_S, width))
            acc_ref[...] += _dot(jnp.where(own_head, a, 0.0).astype(BF16), vs)

    @pl.when((ph == 1) & (c == last_c))
    def _():
        o_ref[...] = acc_ref[0:H_SB, :]


def _sb_dec_consts(npg, page):
    assert npg % SB_PAGES_PER_STEP == 0 and npg % 8 == 0
    width = page * H_SB
    r = lax.broadcasted_iota(I32, (width, 2 * width), 0)
    c = lax.broadcasted_iota(I32, (width, 2 * width), 1)
    same_head = (r % H_SB) == (c % H_SB)
    up = (same_head & ((c >= width) | (r // H_SB > c // H_SB))).astype(BF16)
    pr = lax.broadcasted_iota(I32, (npg, npg), 0)
    pc = lax.broadcasted_iota(I32, (npg, npg), 1)
    upp = (pc > pr).astype(BF16)
    return up, upp


NSA_CH = 4 * G_NSA


def _cmp_dec_step(page_refs, w_ref, kc_ref, vc_ref, yk_ref, yv_ref, page):
    pps = len(page_refs)
    halves = page // CMP_BLOCK
    nblk = pps * halves
    acc = jnp.zeros((nblk * NSA_CH, 2 * HEAD_DIM), F32)

    def token_rows(u, hf, l):
        return page_refs[u][pl.ds((hf * CMP_BLOCK + l) * NSA_CH, NSA_CH), :]

    for l in range(0, CMP_BLOCK, 2):
        x_l = jnp.concatenate(
            [jnp.concatenate([token_rows(u, hf, l), token_rows(u, hf, l + 1)], axis=1)
             for u in range(pps) for hf in range(halves)], axis=0).astype(BF16)
        acc = acc + _dot(x_l, w_ref[l * HEAD_DIM:(l + 2) * HEAD_DIM, :])
    yk_ref[...] = acc[:, :HEAD_DIM]
    yv_ref[...] = acc[:, HEAD_DIM:]
    for g in range(G_NSA):
        kc_ref[g] = yk_ref[pl.ds(g, nblk, stride=NSA_CH), :]
        vc_ref[g] = yv_ref[pl.ds(G_NSA + g, nblk, stride=NSA_CH), :]


def _nsa_sel_body(q_ref, kcr_ref, vcr_ref, new_ref, wk0_ref, wv0_ref, bk_ref, bv_ref, gk_ref,
                  sbc_ref, kw_ref, vw_ref, sbw_ref, idx_ref, oc_ref, ow_ref,
                  *, nb0, nbp, past_len, w_buf, wp):
    nb = nb0 + 1
    new = new_ref[...].astype(BF16)
    new_k = _rms(_dot(new, wk0_ref[...]) + bk_ref[...], gk_ref[...])
    new_v = _dot(new, wv0_ref[...]) + bv_ref[...]
    row8 = lax.broadcasted_iota(I32, (8, HEAD_DIM), 0)
    n1 = lax.broadcasted_iota(I32, (1, nbp), 1)
    nq = lax.broadcasted_iota(I32, (ROWS_S, nbp), 1)
    pos_q = past_len
    pad = jnp.zeros((nbp - nb0 - 8, HEAD_DIM), F32)
    for g in range(G_NSA):
        q = q_ref[g]
        kc = jnp.concatenate(
            [_rms(kcr_ref[g] + bk_ref[...], gk_ref[...]),
             jnp.where(row8 == 0, jnp.broadcast_to(new_k[g:g + 1], (8, HEAD_DIM)), 0.0), pad],
            axis=0).astype(BF16)
        vc = jnp.concatenate(
            [vcr_ref[g] + bv_ref[...],
             jnp.where(row8 == 0, jnp.broadcast_to(new_v[G_NSA + g:G_NSA + g + 1], (8, HEAD_DIM)),
                       0.0), pad], axis=0).astype(BF16)
        s_c = _dot_nt(q, kc) * SCALE + sbc_ref[g]
        valid_c = (CMP_BLOCK * nq + (CMP_BLOCK - 1) <= pos_q) & (nq < nb)
        p_c = _masked_softmax(s_c, valid_c)
        oc_ref[g] = _dot(p_c.astype(BF16), vc)
        imp = p_c[0:1]
        for hh in range(1, HPG):
            imp = imp + p_c[hh:hh + 1]

        cur = pos_q // CMP_BLOCK
        forced = (n1 == 0) | (n1 >= cur - (N_LOCAL - 1))
        score = jnp.where(n1 > cur, -jnp.inf, jnp.where(forced, FORCE_SCORE, imp))
        sq = jnp.broadcast_to(score, (LANE, nbp))
        col = jnp.concatenate([sq[:, k * LANE:(k + 1) * LANE].T for k in range(nbp // LANE)],
                              axis=0)
        j_i = lax.broadcasted_iota(I32, (nbp, LANE), 0)
        ranks = []
        for k in range(nbp // LANE):
            mine = jnp.broadcast_to(score[:, k * LANE:(k + 1) * LANE], (nbp, LANE))
            n_i = k * LANE + lax.broadcasted_iota(I32, (nbp, LANE), 1)
            ahead = (col > mine) | ((col == mine) & (j_i < n_i))
            ranks.append(jnp.sum(ahead.astype(F32), axis=0, keepdims=True))
        rank = jnp.concatenate(ranks, axis=1)
        slot = lax.broadcasted_iota(I32, (TOP_N, nbp), 0).astype(F32)
        hit = jnp.broadcast_to(rank, (TOP_N, nbp)) == slot
        ids = jnp.sum(jnp.where(hit, lax.broadcasted_iota(I32, (TOP_N, nbp), 1).astype(F32), 0.0),
                      axis=1, keepdims=True)
        idx_ref[g] = jnp.broadcast_to(ids, (TOP_N, LANE)).astype(I32)

        jw = lax.broadcasted_iota(I32, (ROWS_S, wp), 1)
        dist_w = w_buf - jw
        valid_w = (dist_w >= 0) & (dist_w <= WINDOW) & (past_len - w_buf + jw >= 0)
        s_w = _dot_nt(q, kw_ref[g]) * SCALE + sbw_ref[g]
        p_w = _masked_softmax(s_w, valid_w)
        ow_ref[g] = _dot(p_w.astype(BF16), vw_ref[g])


def _nsa_select(q_nsa, kcr, vcr, new_c, wk0, wv0, bk, bv, gk0, sbc, kw, vw, sbw, past_len, w_buf):
    dec_b, _, nb0, _ = kcr.shape
    nbp = sbc.shape[-1]
    wp = sbw.shape[-1]
    assert nb0 + 1 >= TOP_N and nbp >= nb0 + 8 and nb0 % 8 == 0
    body = functools.partial(_nsa_sel_body, nb0=nb0, nbp=nbp, past_len=past_len, w_buf=w_buf, wp=wp)

    def per_b(shape):
        nd = len(shape)
        return pl.BlockSpec((None,) + tuple(shape), lambda b: (b,) + (0,) * nd)

    def whole(arr):
        nd = arr.ndim
        return pl.BlockSpec(arr.shape, lambda b: (0,) * nd)

    vec = (G_NSA, ROWS_S, HEAD_DIM)
    return pl.pallas_call(
        body,
        grid=(dec_b,),
        in_specs=[per_b(vec), per_b((G_NSA, nb0, HEAD_DIM)), per_b((G_NSA, nb0, HEAD_DIM)),
                  per_b((8, HEAD_DIM)), whole(wk0), whole(wv0), whole(bk), whole(bv), whole(gk0),
                  whole(sbc), per_b((G_NSA, wp, HEAD_DIM)), per_b((G_NSA, wp, HEAD_DIM)), whole(sbw)],
        out_specs=[per_b((G_NSA, TOP_N, LANE)), per_b(vec), per_b(vec)],
        out_shape=[jax.ShapeDtypeStruct((dec_b, G_NSA, TOP_N, LANE), I32),
                   jax.ShapeDtypeStruct((dec_b,) + vec, F32),
                   jax.ShapeDtypeStruct((dec_b,) + vec, F32)],
        compiler_params=_cparams(1),
        name="nsa_select",
    )(q_nsa, kcr, vcr, new_c, wk0, wv0, bk, bv, gk0, sbc, kw, vw, sbw)


GATHER_BLOCKS = 8


def _nsa_gather_body(pt_ref, idx_ref, rb_ref, q_ref, *refs, nb0, past_len):
    n_pages = G_NSA * GATHER_BLOCKS
    page_refs = refs[:n_pages]
    new_ref, gs_ref, oc_ref, ow_ref, o_ref, m_ref, l_ref, a_ref = refs[n_pages:]
    b = pl.program_id(0)
    k = pl.program_id(1)
    pos_q = past_len
    keys = GATHER_BLOCKS * CMP_BLOCK
    row = lax.broadcasted_iota(I32, (CMP_BLOCK, HEAD_DIM), 0)
    l1 = lax.broadcasted_iota(I32, (1, CMP_BLOCK), 1)
    hrow = lax.broadcasted_iota(I32, (ROWS_S, keys), 0)
    for g in range(G_NSA):
        new_k = jnp.broadcast_to(new_ref[g:g + 1, :], (CMP_BLOCK, HEAD_DIM))
        new_v = jnp.broadcast_to(new_ref[G_NSA + g:G_NSA + g + 1, :], (CMP_BLOCK, HEAD_DIM))
        ks, vs, dist = [], [], []
        for kk in range(GATHER_BLOCKS):
            blk = idx_ref[(b * G_NSA + g) * TOP_N + k * GATHER_BLOCKS + kk]
            is_new = blk >= nb0
            page_ref = page_refs[g * GATHER_BLOCKS + kk]
            k_blk = page_ref[pl.ds(2 * G_NSA + g, CMP_BLOCK, stride=NSA_CH), :]
            v_blk = page_ref[pl.ds(3 * G_NSA + g, CMP_BLOCK, stride=NSA_CH), :]
            ks.append(jnp.where(is_new, jnp.where(row == 0, new_k, 0.0), k_blk).astype(BF16))
            vs.append(jnp.where(is_new, jnp.where(row == 0, new_v, 0.0), v_blk).astype(BF16))
            dist.append(pos_q - (blk * CMP_BLOCK + l1))
        ks = jnp.concatenate(ks, axis=0)
        vs = jnp.concatenate(vs, axis=0)
        dist = jnp.concatenate(dist, axis=1)
        bucket = jnp.broadcast_to(_t5_bucket(dist), (ROWS_S, keys))
        bias = jnp.zeros((ROWS_S, keys), F32)
        for hh in range(HPG):
            for bk in range(N_BUCKETS):
                bias = jnp.where((hrow == hh) & (bucket == bk), rb_ref[bk, g * HPG + hh], bias)
        s = _dot_nt(q_ref[g], ks) * SCALE + bias
        logit = jnp.where(jnp.broadcast_to(dist >= 0, (ROWS_S, keys)), s, NEG_INF)
        fl = _Flash(m_ref.at[g], l_ref.at[g], a_ref.at[g])

        @pl.when(k == 0)
        def _():
            fl.first(logit, vs)

        @pl.when(k > 0)
        def _():
            fl.update(logit, vs)

        @pl.when(k == pl.num_programs(1) - 1)
        def _():
            o_ref[g] = gs_ref[g, 0] * oc_ref[g] + gs_ref[g, 1] * fl.result() + gs_ref[g, 2] * ow_ref[g]


def _nsa_gather(l, pt_flat, idx_flat, rel_bias, q_nsa, cache_nsa2d, new_s, gs, oc, ow, nb0,
                past_len, page):
    dec_b = q_nsa.shape[0]
    npg = pt_flat.shape[0] // dec_b
    halves = page // CMP_BLOCK
    body = functools.partial(_nsa_gather_body, nb0=nb0, past_len=past_len)

    def page_spec(g, kk):
        def imap(b, k, pt, idx):
            blk = jnp.minimum(idx[(b * G_NSA + g) * TOP_N + k * GATHER_BLOCKS + kk], nb0 - 1)
            return (l, pt[b * npg + _div(blk, halves)], _mod(blk, halves), 0)
        return pl.BlockSpec((None, None, CMP_BLOCK * NSA_CH, HEAD_DIM), imap)

    def per_b(shape):
        nd = len(shape)
        return pl.BlockSpec((None,) + tuple(shape), lambda b, k, pt, idx: (b,) + (0,) * nd)

    vec = (G_NSA, ROWS_S, HEAD_DIM)
    assert TOP_N % GATHER_BLOCKS == 0
    n_pages = G_NSA * GATHER_BLOCKS
    grid_spec = pltpu.PrefetchScalarGridSpec(
        num_scalar_prefetch=2,
        grid=(dec_b, TOP_N // GATHER_BLOCKS),
        in_specs=[pl.BlockSpec(memory_space=pltpu.SMEM), per_b(vec)]
        + [page_spec(g, kk) for g in range(G_NSA) for kk in range(GATHER_BLOCKS)]
        + [per_b((8, HEAD_DIM)), per_b((G_NSA, 3, ROWS_S, HEAD_DIM)), per_b(vec), per_b(vec)],
        out_specs=per_b(vec),
        scratch_shapes=[pltpu.VMEM((G_NSA, ROWS_S, 1), F32), pltpu.VMEM((G_NSA, ROWS_S, 1), F32),
                        pltpu.VMEM((G_NSA, ROWS_S, HEAD_DIM), F32)],
    )
    return pl.pallas_call(
        body,
        grid_spec=grid_spec,
        out_shape=jax.ShapeDtypeStruct((dec_b,) + vec, F32),
        compiler_params=_cparams(2),
        name="nsa_gather",
    )(pt_flat, idx_flat, rel_bias, q_nsa, *([cache_nsa2d] * n_pages), new_s, gs, oc, ow)


def _pad_rows(x, rows):
    return jnp.pad(x, ((0, rows - x.shape[0]),) + ((0, 0),) * (x.ndim - 1))


def _sample_inproj(xs, wt, p, dec_b):
    rows = xs.shape[0]
    zb, sb_rows, nsa_rows, win_rows, gates = _inproj(
        xs, p[0], wt["w_a"], wt["w_b"], wt["w_gate"], wt["gain"], rows, rows)
    zb8 = zb[:dec_b]
    q_sb = zb8[:, QS_BLK * LANE:KSB_BLK * LANE].reshape(dec_b, H_SB, HEAD_DIM)
    q_sb = jnp.pad(q_sb, ((0, 0), (0, ROWS_S - H_SB), (0, 0)))
    keep = lambda a: a[:dec_b * (a.shape[0] // rows)]
    return q_sb, (zb8, keep(sb_rows), keep(nsa_rows), keep(win_rows), gates)


def _sample_layer(l, xs, projected, walked, wt, p, sb_tabs, rel_bias, pt_flat, cache_nsa2d, win_l,
                  w_cmp0, dec_b, past_len, page):
    na, bk, bv, gk0, go, nf = p
    sbc, sbw = sb_tabs
    w_buf = win_l.shape[1]
    rows = xs.shape[0]
    zb8, sb_rows, nsa_rows, win_rows, gates = projected
    o_sb, kcr, vcr = walked
    o_sb = o_sb.reshape(dec_b, D_SB)

    q_nsa = zb8[:, :D_NSA].reshape(dec_b, G_NSA, HPG, HEAD_DIM)
    q_nsa = jnp.pad(q_nsa, ((0, 0), (0, 0), (0, ROWS_S - HPG), (0, 0)))
    nsa_new = nsa_rows.reshape(dec_b, 4 * G_NSA, HEAD_DIM)
    new_c = jnp.pad(nsa_new[:, :2 * G_NSA], ((0, 0), (0, 8 - 2 * G_NSA), (0, 0)))
    new_s = jnp.pad(nsa_new[:, 2 * G_NSA:], ((0, 0), (0, 8 - 2 * G_NSA), (0, 0)))
    new_w = win_rows.reshape(dec_b, 1, 2, G_NSA, HEAD_DIM)
    win_all = jnp.concatenate([win_l, new_w], axis=1)
    wp = sbw.shape[-1]
    win_t = jnp.transpose(win_all, (2, 0, 3, 1, 4))
    win_t = jnp.pad(win_t, ((0, 0), (0, 0), (0, 0), (0, wp - w_buf - 1), (0, 0))).astype(BF16)
    idx, oc, ow = _nsa_select(q_nsa, kcr, vcr, new_c, w_cmp0[0], w_cmp0[1], bk, bv, gk0, sbc,
                              win_t[0], win_t[1], sbw, past_len, w_buf)
    gs = gates[:dec_b].reshape(dec_b, G_NSA, LANE)[:, :, :3 * HPG].reshape(dec_b, G_NSA, 3, HPG)
    gs = jnp.pad(gs, ((0, 0), (0, 0), (0, 0), (0, ROWS_S - HPG)))
    gs = jnp.broadcast_to(gs[..., None], gs.shape + (HEAD_DIM,))
    o_nsa = _nsa_gather(l, pt_flat, idx[:, :, :, 0].reshape(-1), rel_bias, q_nsa, cache_nsa2d,
                        new_s, gs, oc, ow, kcr.shape[2], past_len, page)
    o_nsa = o_nsa[:, :, :HPG].reshape(dec_b, D_NSA)

    x1 = _outproj(_pad_rows(o_nsa, rows), _pad_rows(o_sb, rows), go, wt["w_out"], xs)
    x2 = _ffn(x1, nf, wt["w_gate_ffn"], wt["w_up"], wt["w_down"])
    new_win = win_all[:, 1:]
    return x2, sb_rows, nsa_rows, new_win


def kernel(x_prompt, x_sample, cache_sb_kv, cache_nsa_kv, state_win_kv, page_table, rel_bias,
           norm_attn, w_in, g_q, g_k, w_cmp_k, b_cmp_k, w_cmp_v, b_cmp_v, g_out, w_out,
           norm_ffn, w_gate, w_up, w_down):
    batch, seq, _ = x_prompt.shape
    dec_b, dec_t, _ = x_sample.shape
    assert dec_t == 1
    depth, n_pool, page = cache_sb_kv.shape[:3]
    npg = page_table.shape[1]
    past_len = npg * page
    w_buf = state_win_kv.shape[2]
    nb_s = past_len // CMP_BLOCK + 1
    nbp_s = -(-(nb_s + 7) // LANE) * LANE
    wp_s = -(-(w_buf + 1) // LANE) * LANE

    toep, bias_c, sc, sw = _bias_tables(rel_bias, seq, past_len, w_buf, nbp_s, wp_s)
    sbc = jnp.pad(sc.reshape(G_NSA, HPG, nbp_s), ((0, 0), (0, ROWS_S - HPG), (0, 0)))
    sbw = jnp.pad(sw.reshape(G_NSA, HPG, wp_s), ((0, 0), (0, ROWS_S - HPG), (0, 0)))
    pt_flat = page_table.reshape(-1).astype(I32)
    cache_nsa2d = cache_nsa_kv.reshape(depth, n_pool, page * NSA_CH, HEAD_DIM)

    xp = x_prompt.reshape(batch * seq, D_MODEL)
    xs = _pad_rows(x_sample.reshape(dec_b, D_MODEL), ROWS_S)
    sb_p, sb_s, nsa_p, nsa_s, win_p, win_s = [], [], [], [], [], []
    w_keep = min(WINDOW, seq)
    for l in range(depth):
        wt = _prep_weights(l, w_in, g_q, g_k, w_cmp_k, w_cmp_v)
        p = (norm_attn[l][None], b_cmp_k[l][None], b_cmp_v[l][None], g_k[l, 0][None],
             g_out[l][None], norm_ffn[l][None])
        q_sb, projected = _sample_inproj(xs, wt, p, dec_b)
        xp, sb_rows, nsa_rows, win_rows, walked = _prompt_layer(
            xp, l, wt, (w_out, w_gate, w_up, w_down), p, (toep, bias_c), rel_bias, batch, seq,
            (l, pt_flat, q_sb, cache_sb_kv), (l, pt_flat, cache_nsa2d, wt["w_kv"], dec_b, page))
        sb_p.append(sb_rows.reshape(batch, seq, 2, H_SB, HEAD_DIM))
        nsa_p.append(nsa_rows.reshape(batch, seq, 4, G_NSA, HEAD_DIM))
        win_p.append(win_rows.reshape(batch, w_keep, 2, G_NSA, HEAD_DIM))

        w_cmp0 = (w_cmp_k[l, 0].astype(BF16), w_cmp_v[l, 0].astype(BF16))
        xs, sb_rows, nsa_rows, new_win = _sample_layer(
            l, xs, projected, walked, wt, p, (sbc, sbw), rel_bias, pt_flat, cache_nsa2d,
            state_win_kv[l], w_cmp0, dec_b, past_len, page)
        sb_s.append(sb_rows.reshape(dec_b, 1, 2, H_SB, HEAD_DIM))
        nsa_s.append(nsa_rows.reshape(dec_b, 1, 4, G_NSA, HEAD_DIM))
        win_s.append(new_win)
    return (xp.reshape(batch, seq, D_MODEL), xs[:dec_b].reshape(dec_b, 1, D_MODEL),
            jnp.stack(sb_p), jnp.stack(sb_s), jnp.stack(nsa_p), jnp.stack(nsa_s),
            jnp.stack(win_p), jnp.stack(win_s))
```

```python
import functools
import math

import jax
import jax.numpy as jnp
from jax import lax
from jax.experimental import pallas as pl
from jax.experimental.pallas import tpu as pltpu

F32 = jnp.float32
BF16 = jnp.bfloat16
I32 = jnp.int32

LANE = 128
HEAD_DIM = 128
G_NSA = 2
HPG = 4
H_NSA = G_NSA * HPG
H_SB = 8
D_NSA = H_NSA * HEAD_DIM
D_SB = H_SB * HEAD_DIM
D_MODEL = D_NSA + D_SB
D_KV = G_NSA * HEAD_DIM
CMP_BLOCK = 64
TOP_N = 16
N_LOCAL = 2
WINDOW = 512
N_BUCKETS = 32
MAX_DISTANCE = 128
EPS = 1e-6
FORCE_SCORE = 1e4
NEG_INF = -1e30
SCALE = HEAD_DIM ** -0.5

COL_TILE = 512
N_MAIN = D_NSA + 6 * D_KV + 3 * D_SB
QS_BLK = (D_NSA + 6 * D_KV) // LANE
KSB_BLK = QS_BLK + D_SB // LANE
VSB_BLK = KSB_BLK + D_SB // LANE
KS_BLK = (D_NSA + 2 * D_KV) // LANE
VS_BLK = KS_BLK + G_NSA
KW_BLK = VS_BLK + G_NSA
VW_BLK = KW_BLK + G_NSA
VMEM_LIMIT = 56 * 1024 * 1024

SB_TQ = 256
SB_TK = 256
SB_HB = 8
NSA_TQ = 256
SB_PAGES_PER_STEP = 16
CMP_PAGES_PER_STEP = 16


VMEM_LIMIT_FUSED = 62 * 1024 * 1024


def _cparams(n_axes, vmem_limit=VMEM_LIMIT):
    return pltpu.CompilerParams(dimension_semantics=("arbitrary",) * n_axes,
                                vmem_limit_bytes=vmem_limit)


def _dot(a, b):
    return jnp.dot(a, b, preferred_element_type=F32)


def _dot_nt(a, b):
    return lax.dot_general(a, b, (((1,), (1,)), ((), ())), preferred_element_type=F32)


def _split_bf16(x):
    hi = x.astype(BF16)
    lo = (x - hi.astype(F32)).astype(BF16)
    return hi, lo


def _dot_split(x, u):
    hi, lo = _split_bf16(x)
    return _dot(hi, u) + _dot(lo, u)


def _rms(y, gain):
    ms = jnp.mean(y * y, axis=-1, keepdims=True)
    return y * lax.rsqrt(ms + EPS) * gain


def _neg_softplus(z):
    return -(jnp.maximum(z, 0.0) + jnp.log(1.0 + jnp.exp(-jnp.abs(z))))


def _t5_bucket(dist):
    n = jnp.maximum(dist, 0)
    max_exact = N_BUCKETS // 2
    nf = jnp.maximum(n, 1).astype(F32)
    large = max_exact + (jnp.log(nf / max_exact) / math.log(MAX_DISTANCE / max_exact)
                         * (N_BUCKETS - max_exact)).astype(I32)
    large = jnp.minimum(large, N_BUCKETS - 1)
    return jnp.where(n < max_exact, n, large)


def _div(x, n):
    if n & (n - 1) == 0:
        return lax.shift_right_logical(x, n.bit_length() - 1)
    return x // n


def _mod(x, n):
    if n & (n - 1) == 0:
        return x & (n - 1)
    return x % n


def _pick_tile(m, candidates):
    for t in candidates:
        if m % t == 0:
            return t
    raise ValueError(f"no tile for {m}")


def _bias_body(rb_ref, toep_ref, bc_ref, sc_ref, sw_ref, *, seq, past_len, w_buf):
    h = pl.program_id(0)
    tq = NSA_TQ

    def table(dist):
        bk = _t5_bucket(dist)
        out = jnp.zeros(dist.shape, F32)
        for b in range(N_BUCKETS):
            out = jnp.where(bk == b, rb_ref[b, h], out)
        return out

    k = lax.broadcasted_iota(I32, (tq, tq), 0)
    q = lax.broadcasted_iota(I32, (tq, tq), 1)
    toep_ref[0] = table(tq + q - k)
    toep_ref[1] = table(q - k)
    nbr = -(-(seq // CMP_BLOCK) // 8) * 8
    n = lax.broadcasted_iota(I32, (nbr, tq), 0)
    qc = lax.broadcasted_iota(I32, (nbr, tq), 1)
    for i in range(seq // tq):
        bc_ref[i, 0:nbr, :] = table(i * tq + qc - CMP_BLOCK * n - (CMP_BLOCK - 1))
        if nbr < LANE:
            bc_ref[i, nbr:, :] = jnp.zeros((LANE - nbr, tq), F32)
    n1 = lax.broadcasted_iota(I32, sc_ref.shape, 1)
    sc_ref[...] = table(past_len - CMP_BLOCK * n1 - (CMP_BLOCK - 1))
    j1 = lax.broadcasted_iota(I32, sw_ref.shape, 1)
    sw_ref[...] = table(w_buf - j1)


def _bias_tables(rel_bias, seq, past_len, w_buf, nbp_s, wp_s):
    body = functools.partial(_bias_body, seq=seq, past_len=past_len, w_buf=w_buf)
    nq = seq // NSA_TQ
    return pl.pallas_call(
        body,
        grid=(H_NSA,),
        in_specs=[pl.BlockSpec(memory_space=pltpu.SMEM)],
        out_specs=[
            pl.BlockSpec((2, NSA_TQ, NSA_TQ), lambda h: (0, 0, h)),
            pl.BlockSpec((nq, LANE, NSA_TQ), lambda h: (0, 0, h)),
            pl.BlockSpec((None, 1, nbp_s), lambda h: (h, 0, 0)),
            pl.BlockSpec((None, 1, wp_s), lambda h: (h, 0, 0)),
        ],
        out_shape=[
            jax.ShapeDtypeStruct((2, NSA_TQ, H_NSA * NSA_TQ), F32),
            jax.ShapeDtypeStruct((nq, LANE, H_NSA * NSA_TQ), F32),
            jax.ShapeDtypeStruct((H_NSA, 1, nbp_s), F32),
            jax.ShapeDtypeStruct((H_NSA, 1, wp_s), F32),
        ],
        compiler_params=_cparams(1),
        name="bias_tables",
    )(rel_bias)


Q_TILES = D_NSA // COL_TILE
NSA_TILE0 = Q_TILES
WIN_TILE = NSA_TILE0 + 4 * D_KV // COL_TILE
SB_TILE0 = WIN_TILE + 2 * D_KV // COL_TILE + D_SB // COL_TILE


def _inproj_body(x_ref, na_ref, wa_ref, wb_ref, wg_ref, gain_ref, *rest, w_keep):
    zb_ref, sb_ref, nsa_ref, win_ref, gate_ref, h_ref = rest[-6:]
    j = pl.program_id(1)
    tm = x_ref.shape[0]
    half = COL_TILE // 2

    @pl.when(j == 0)
    def _():
        h = _rms(x_ref[...], na_ref[...]).astype(BF16)
        h_ref[...] = h
        gate_ref[...] = jax.nn.sigmoid(_dot(h, wg_ref[...]))

    gain = gain_ref[...]
    chunks = COL_TILE // LANE

    def emit(w_ref, norm_chunks, f32_ref=None, first=0, per_row=1, row0=0):
        h = h_ref[...]
        accs = [_dot(h, w_ref[:, :half]), _dot(h, w_ref[:, half:])]
        for c in range(chunks):
            sl = slice(c * LANE, (c + 1) * LANE)
            y = accs[c * LANE // half][:, (c * LANE) % half:(c * LANE) % half + LANE]
            if c in norm_chunks:
                y = _rms(y, gain[:, sl])
            zb_ref[:, sl] = y.astype(BF16)
            if f32_ref is not None:
                f32_ref[pl.ds(first + c, tm - row0, stride=per_row), :] = y[row0:, :]

    @pl.when(j < Q_TILES)
    def _():
        emit(wa_ref, (0, 1, 2, 3))

    @pl.when(j == NSA_TILE0)
    def _():
        emit(wa_ref, (), nsa_ref, 0, 2 * chunks)

    @pl.when(j == NSA_TILE0 + 1)
    def _():
        emit(wa_ref, (0, 1), nsa_ref, chunks, 2 * chunks)

    @pl.when(j == WIN_TILE)
    def _():
        emit(wa_ref, (0, 1), win_ref, 0, chunks, tm - w_keep)

    @pl.when((j > WIN_TILE) & (j < SB_TILE0))
    def _():
        emit(wb_ref, ())

    @pl.when(j >= SB_TILE0)
    def _():
        emit(wb_ref, (), sb_ref, (j - SB_TILE0) * chunks, 2 * H_SB)


def _inproj(x2d, na, w_a, w_b, w_gate, gain_cols, seq, w_keep, stacked=None):
    m = x2d.shape[0]
    tm = _pick_tile(seq, (1024, 512, 256, 128, 16))
    assert w_keep <= tm
    tpb = seq // tm
    sb_per, nsa_per, win_per = 2 * H_SB, 4 * G_NSA, 2 * G_NSA
    n_a = w_a.shape[1] // COL_TILE
    n_b = w_b.shape[1] // COL_TILE
    assert n_a == WIN_TILE + 1 and n_a + n_b == N_MAIN // COL_TILE
    body = functools.partial(_inproj_body, w_keep=w_keep)
    layer, depth, bufs = (0, 1, ()) if stacked is None else stacked
    row0, win0 = layer * (m // tm), layer * (m // seq)
    n_in = 6
    return pl.pallas_call(
        body,
        grid=(m // tm, n_a + n_b),
        in_specs=[
            pl.BlockSpec((tm, D_MODEL), lambda i, j: (i, 0)),
            pl.BlockSpec((1, D_MODEL), lambda i, j: (0, 0)),
            pl.BlockSpec((D_MODEL, COL_TILE), lambda i, j: (0, jnp.minimum(j, n_a - 1))),
            pl.BlockSpec((D_MODEL, COL_TILE), lambda i, j: (0, jnp.maximum(j - n_a, 0))),
            pl.BlockSpec((D_MODEL, 2 * LANE), lambda i, j: (0, 0)),
            pl.BlockSpec((1, COL_TILE), lambda i, j: (0, j)),
        ] + [pl.BlockSpec(memory_space=pl.ANY)] * len(bufs),
        out_specs=[
            pl.BlockSpec((tm, COL_TILE), lambda i, j: (i, j)),
            pl.BlockSpec((tm * sb_per, LANE), lambda i, j: (row0 + i, 0),
                         pipeline_mode=pl.Buffered(1)),
            pl.BlockSpec((tm * nsa_per, LANE), lambda i, j: (row0 + i, 0),
                         pipeline_mode=pl.Buffered(1)),
            pl.BlockSpec((w_keep * win_per, LANE), lambda i, j: (win0 + _div(i, tpb), 0)),
            pl.BlockSpec((tm, 2 * LANE), lambda i, j: (i, 0)),
        ],
        out_shape=[
            jax.ShapeDtypeStruct((m, N_MAIN), BF16),
            jax.ShapeDtypeStruct((depth * m * sb_per, LANE), F32),
            jax.ShapeDtypeStruct((depth * m * nsa_per, LANE), F32),
            jax.ShapeDtypeStruct((depth * (m // seq) * w_keep * win_per, LANE), F32),
            jax.ShapeDtypeStruct((m, 2 * LANE), F32),
        ],
        input_output_aliases={n_in + k: 1 + k for k in range(len(bufs))},
        scratch_shapes=[pltpu.VMEM((tm, D_MODEL), BF16)],
        compiler_params=_cparams(2),
        name="inproj",
    )(x2d, na, w_a, w_b, w_gate, gain_cols, *bufs)


def _outproj_body(on_ref, os_ref, go_ref, w_ref, x_ref, y_ref, h_ref):
    j = pl.program_id(1)

    @pl.when(j == 0)
    def _():
        go = go_ref[...]
        h_ref[:, :D_NSA] = _rms(on_ref[...], go[:, :D_NSA]).astype(BF16)
        h_ref[:, D_NSA:] = _rms(os_ref[...], go[:, D_NSA:]).astype(BF16)

    y_ref[...] = x_ref[...] + _dot(h_ref[...], w_ref[...])


def _outproj(o_nsa, o_sb, g_out, w_out, x2d):
    m = x2d.shape[0]
    tm = _pick_tile(m, (512, 256, 128, 16))
    tn = D_MODEL
    return pl.pallas_call(
        _outproj_body,
        grid=(m // tm, D_MODEL // tn),
        in_specs=[
            pl.BlockSpec((tm, D_NSA), lambda i, j: (i, 0)),
            pl.BlockSpec((tm, D_SB), lambda i, j: (i, 0)),
            pl.BlockSpec((1, D_MODEL), lambda i, j: (0, 0)),
            pl.BlockSpec((D_MODEL, tn), lambda i, j: (0, j), pipeline_mode=pl.Buffered(1)),
            pl.BlockSpec((tm, tn), lambda i, j: (i, j)),
        ],
        out_specs=pl.BlockSpec((tm, tn), lambda i, j: (i, j)),
        out_shape=jax.ShapeDtypeStruct((m, D_MODEL), F32),
        scratch_shapes=[pltpu.VMEM((tm, D_MODEL), BF16)],
        compiler_params=_cparams(2),
        name="outproj",
    )(o_nsa, o_sb, g_out, w_out, x2d)


def _ffn_body(*refs, sb):
    if sb is None:
        x_ref, nf_ref, wg_ref, wu_ref, wd_ref, y_ref, h_ref = refs
    else:
        pps = sb["pps"]
        x_ref, nf_ref, wg_ref, wu_ref, wd_ref, q_ref = refs[1:7]
        page_refs = refs[7:7 + pps]
        up_ref, upp_ref, y_ref, o_ref, h_ref, z_ref, sacc_ref = refs[7 + pps:]
    i = pl.program_id(0)
    j = pl.program_id(1)

    @pl.when(j == 0)
    def _():
        x = x_ref[...]
        h_ref[...] = _rms(x, nf_ref[...]).astype(BF16)
        y_ref[...] = x

    h = h_ref[...]
    act = jax.nn.silu(_dot(h, wg_ref[...])) * _dot(h, wu_ref[...])
    y_ref[...] += _dot(act.astype(BF16), wd_ref[...])

    if sb is not None:
        nc = sb["n_chunks"]
        s = i * pl.num_programs(1) + j

        @pl.when(s < sb["n_steps"])
        def _():
            _sb_dec_step(_mod(_div(s, nc), 2), _mod(s, nc), nc - 1, q_ref, page_refs, up_ref, upp_ref, o_ref,
                         z_ref, sacc_ref, sb["page"])


def _ffn(x2d, norm_ffn, w_gate, w_up, w_down, sb_args=None):
    m = x2d.shape[0]
    d_ff = w_gate.shape[1]
    if sb_args is None:
        tm = _pick_tile(m, (512, 256, 128, 16))
        tf = _pick_tile(d_ff, (512, 256))
        x_mode = None
    else:
        tm = _pick_tile(m, (1024, 512, 256, 128))
        tf = _pick_tile(d_ff, (256,))
        x_mode = pl.Buffered(1)
    nf = d_ff // tf
    in_specs = [
        pl.BlockSpec((tm, D_MODEL), lambda i, j, *_: (i, 0), pipeline_mode=x_mode),
        pl.BlockSpec((1, D_MODEL), lambda i, j, *_: (0, 0)),
        pl.BlockSpec((D_MODEL, tf), lambda i, j, *_: (0, j)),
        pl.BlockSpec((D_MODEL, tf), lambda i, j, *_: (0, j)),
        pl.BlockSpec((tf, D_MODEL), lambda i, j, *_: (j, 0)),
    ]
    y_spec = pl.BlockSpec((tm, D_MODEL), lambda i, j, *_: (i, 0))
    y_shape = jax.ShapeDtypeStruct((m, D_MODEL), F32)
    h_scratch = pltpu.VMEM((tm, D_MODEL), BF16)
    if sb_args is None:
        return pl.pallas_call(
            functools.partial(_ffn_body, sb=None),
            grid=(m // tm, nf),
            in_specs=in_specs,
            out_specs=y_spec,
            out_shape=y_shape,
            scratch_shapes=[h_scratch],
            compiler_params=_cparams(2),
            name="ffn",
        )(x2d, norm_ffn, w_gate, w_up, w_down)

    l, pt_flat, q_sb, cache_sb = sb_args
    dec_b = q_sb.shape[0]
    page = cache_sb.shape[2]
    npg = pt_flat.shape[0] // dec_b
    pps = SB_PAGES_PER_STEP
    nc = npg // pps
    n_steps = dec_b * 2 * nc
    assert n_steps <= (m // tm) * nf
    width = page * H_SB
    up, upp = _sb_dec_consts(npg, page)

    def pos(i, j):
        s = jnp.minimum(i * nf + j, n_steps - 1)
        return _div(s, 2 * nc), _mod(_div(s, nc), 2), _mod(s, nc)

    def page_spec(u):
        def imap(i, j, pt):
            b, ph, c = pos(i, j)
            return (l, pt[b * npg + c * pps + u], 0, ph, 0, 0)
        return pl.BlockSpec((None, None, page, None, H_SB, HEAD_DIM), imap)

    once = pl.Buffered(1)
    grid_spec = pltpu.PrefetchScalarGridSpec(
        num_scalar_prefetch=1,
        grid=(m // tm, nf),
        in_specs=in_specs
        + [pl.BlockSpec((None, ROWS_S, HEAD_DIM), lambda i, j, pt: (pos(i, j)[0], 0, 0))]
        + [page_spec(u) for u in range(pps)]
        + [pl.BlockSpec((width, 2 * width), lambda i, j, pt: (0, 0), pipeline_mode=once),
           pl.BlockSpec((npg, npg), lambda i, j, pt: (0, 0), pipeline_mode=once)],
        out_specs=[y_spec,
                   pl.BlockSpec((None, H_SB, HEAD_DIM), lambda i, j, pt: (pos(i, j)[0], 0, 0))],
        scratch_shapes=[h_scratch, pltpu.VMEM((npg, width), F32),
                        pltpu.VMEM((ROWS_S, HEAD_DIM), F32)],
    )
    sb = dict(pps=pps, page=page, n_chunks=nc, n_steps=n_steps)
    return pl.pallas_call(
        functools.partial(_ffn_body, sb=sb),
        grid_spec=grid_spec,
        out_shape=[y_shape, jax.ShapeDtypeStruct((dec_b, H_SB, HEAD_DIM), F32)],
        compiler_params=_cparams(2, VMEM_LIMIT_FUSED),
        name="ffn_sb_decode",
    )(pt_flat, x2d, norm_ffn, w_gate, w_up, w_down, q_sb, *([cache_sb] * pps), up, upp)


SB_SUB = 4


def _sb_body(*refs, n_cast, cast_steps):
    n_sub = SB_HB // SB_SUB
    q_refs, k_refs, v_refs = refs[:n_sub], refs[n_sub:2 * n_sub], refs[2 * n_sub:3 * n_sub]
    u_ref = refs[3 * n_sub]
    w_refs = refs[3 * n_sub + 1:3 * n_sub + 1 + n_cast]
    o_ref = refs[3 * n_sub + 1 + n_cast]
    wb_refs = refs[3 * n_sub + 2 + n_cast:3 * n_sub + 2 + 2 * n_cast]
    acc_ref, run_ref = refs[3 * n_sub + 2 + 2 * n_cast:]
    i = pl.program_id(2)
    pos = (pl.program_id(0) * pl.num_programs(1) + pl.program_id(1)) * pl.num_programs(2) + i

    @pl.when(pos < cast_steps)
    def _():
        for w_ref, wb_ref in zip(w_refs, wb_refs):
            wb_ref[...] = w_ref[...].astype(BF16)

    row = lax.broadcasted_iota(I32, (SB_TQ, SB_TK), 0)
    col = lax.broadcasted_iota(I32, (SB_TQ, SB_TK), 1)
    strict = col < row

    def tile(kb, diagonal):
        off = pl.multiple_of(kb * SB_TK, SB_TK)
        heads = range(SB_HB)
        sls = [slice(h % SB_SUB * HEAD_DIM, (h % SB_SUB + 1) * HEAD_DIM) for h in heads]
        q_ref = [q_refs[h // SB_SUB] for h in heads]
        k_ref = [k_refs[h // SB_SUB] for h in heads]
        v_ref = [v_refs[h // SB_SUB] for h in heads]
        z = [_dot_nt(q_ref[h][:, sls[h]], k_ref[h][pl.ds(off, SB_TK), sls[h]]) * SCALE
             for h in heads]
        log_1m = [_neg_softplus(zh) for zh in z]
        if diagonal:
            log_1m = [jnp.where(strict, lh, 0.0) for lh in log_1m]
        parts = [_split_bf16(lh) for lh in log_1m]
        u = u_ref[...]
        later = [_dot(hi, u) + _dot(lo, u) for hi, lo in parts]
        total = [jnp.sum(lh, axis=-1, keepdims=True) for lh in log_1m]
        if diagonal:
            a = [jnp.where(strict, jnp.exp(z[h] + log_1m[h] + later[h]), 0.0) for h in heads]
        else:
            run = [run_ref[h] for h in heads]
            a = [jnp.exp(z[h] + log_1m[h] + later[h] + run[h]) for h in heads]
        pv = [_dot(a[h].astype(BF16), v_ref[h][pl.ds(off, SB_TK), sls[h]]) for h in heads]
        for h in heads:
            if diagonal:
                acc_ref[h] = pv[h]
                run_ref[h] = total[h]
            else:
                acc_ref[h] += pv[h]
                run_ref[h] = run[h] + total[h]

    tile(i, True)

    def step(it, carry):
        tile(i - 1 - it, False)
        return carry

    lax.fori_loop(0, i, step, 0)
    for h in range(SB_HB):
        o_ref[:, h * HEAD_DIM:(h + 1) * HEAD_DIM] = acc_ref[h]


def _sb_prompt(zb, batch, seq, l, weights):
    nq = seq // SB_TQ
    n_grid = batch * (H_SB // SB_HB) * nq
    cast_steps = 1
    while (cast_steps * 2 <= n_grid
           and all(w.shape[1] % (cast_steps * 2 * 16) == 0 for w in weights)):
        cast_steps *= 2

    def w_spec(w, stacked):
        rows = w.shape[1] // cast_steps
        imap = lambda b, h, i: (jnp.minimum((b * (H_SB // SB_HB) + h) * nq + i, cast_steps - 1), 0)
        if stacked:
            return pl.BlockSpec((None, rows, w.shape[2]), lambda b, h, i: (l,) + imap(b, h, i))
        return pl.BlockSpec((rows, w.shape[2]), imap)

    width = SB_HB * HEAD_DIM
    sub_w = SB_SUB * HEAD_DIM
    n_sub = SB_HB // SB_SUB
    assert QS_BLK % SB_SUB == 0 and KSB_BLK % SB_SUB == 0 and VSB_BLK % SB_SUB == 0
    r = lax.broadcasted_iota(I32, (SB_TK, SB_TK), 0)
    c = lax.broadcasted_iota(I32, (SB_TK, SB_TK), 1)
    u = (r > c).astype(BF16)

    def q_spec(n):
        return pl.BlockSpec((SB_TQ, sub_w),
                            lambda b, h, i: (b * nq + i, QS_BLK // SB_SUB + h * n_sub + n))

    def kv_spec(blk, n):
        return pl.BlockSpec((seq, sub_w), lambda b, h, i: (b, blk // SB_SUB + h * n_sub + n))

    subs = range(n_sub)
    outs = pl.pallas_call(
        functools.partial(_sb_body, n_cast=len(weights), cast_steps=cast_steps),
        grid=(batch, H_SB // SB_HB, nq),
        in_specs=[q_spec(n) for n in subs] + [kv_spec(KSB_BLK, n) for n in subs]
        + [kv_spec(VSB_BLK, n) for n in subs]
        + [pl.BlockSpec((SB_TK, SB_TK), lambda b, h, i: (0, 0))]
        + [w_spec(w, True) for w in weights],
        out_specs=[pl.BlockSpec((SB_TQ, width), lambda b, h, i: (b * nq + i, h))]
        + [w_spec(w, False) for w in weights],
        out_shape=[jax.ShapeDtypeStruct((batch * seq, D_SB), F32)]
        + [jax.ShapeDtypeStruct(w.shape[1:], BF16) for w in weights],
        scratch_shapes=[pltpu.VMEM((SB_HB, SB_TQ, HEAD_DIM), F32), pltpu.VMEM((SB_HB, SB_TQ, 1), F32)],
        compiler_params=_cparams(3),
        name="sb_prompt_cast",
    )(*([zb] * (3 * n_sub)), u, *weights)
    return outs[0], outs[1:]


def _cmp_prompt_body(x_ref, wk_ref, wv_ref, bk_ref, bv_ref, gk_ref, kc_ref, vct_ref, *, nb):
    pad = jnp.zeros((LANE - nb, HEAD_DIM), F32)
    per = 4 * G_NSA
    for c in range(2 * G_NSA):
        rows = jnp.concatenate(
            [x_ref[pl.ds(l * per + c, nb, stride=CMP_BLOCK * per), :] for l in range(CMP_BLOCK)],
            axis=1)
        rows = rows.astype(BF16)
        if c < G_NSA:
            y = _rms(_dot(rows, wk_ref[...]) + bk_ref[...], gk_ref[...])
            kc_ref[c] = jnp.concatenate([y, pad], axis=0).astype(BF16)
        else:
            y = _dot(rows, wv_ref[...]) + bv_ref[...]
            vct_ref[c - G_NSA] = jnp.concatenate([y, pad], axis=0).T.astype(BF16)


def _cmp_prompt(nsa_rows, layer, batch, seq, wk, wv, bk, bv, gk0):
    nb = seq // CMP_BLOCK
    assert nb % 8 == 0 and nb < LANE
    body = functools.partial(_cmp_prompt_body, nb=nb)
    shape = jax.ShapeDtypeStruct((batch, G_NSA, LANE, HEAD_DIM), BF16)

    return pl.pallas_call(
        body,
        grid=(batch,),
        in_specs=[
            pl.BlockSpec((seq * 4 * G_NSA, HEAD_DIM), lambda b: (layer * batch + b, 0)),
            pl.BlockSpec(wk.shape, lambda b: (0, 0)),
            pl.BlockSpec(wv.shape, lambda b: (0, 0)),
            pl.BlockSpec((1, HEAD_DIM), lambda b: (0, 0)),
            pl.BlockSpec((1, HEAD_DIM), lambda b: (0, 0)),
            pl.BlockSpec((1, HEAD_DIM), lambda b: (0, 0)),
        ],
        out_specs=[pl.BlockSpec((None, G_NSA, LANE, HEAD_DIM), lambda b: (b, 0, 0, 0))] * 2,
        out_shape=[shape, shape],
        compiler_params=_cparams(1),
        name="cmp_prompt",
    )(nsa_rows, wk, wv, bk, bv, gk0)


def _masked_softmax(s, valid):
    logit = jnp.where(valid, s, NEG_INF)
    e = jnp.exp(logit - jnp.max(logit, axis=-1, keepdims=True))
    return jnp.where(valid, e / jnp.sum(e, axis=-1, keepdims=True), 0.0)


class _Flash:
    def __init__(self, m_ref, l_ref, a_ref):
        self.m_ref, self.l_ref, self.a_ref = m_ref, l_ref, a_ref

    def first(self, logit, v):
        m = jnp.max(logit, axis=-1, keepdims=True)
        p = jnp.exp(logit - m)
        self.m_ref[...] = m
        self.l_ref[...] = jnp.sum(p, axis=-1, keepdims=True)
        self.a_ref[...] = _dot(p.astype(BF16), v)

    def update(self, logit, v):
        m_old = self.m_ref[...]
        m = jnp.maximum(m_old, jnp.max(logit, axis=-1, keepdims=True))
        alpha = jnp.exp(m_old - m)
        p = jnp.exp(logit - m)
        self.m_ref[...] = m
        self.l_ref[...] = alpha * self.l_ref[...] + jnp.sum(p, axis=-1, keepdims=True)
        self.a_ref[...] = alpha * self.a_ref[...] + _dot(p.astype(BF16), v)

    def result(self):
        return self.a_ref[...] / self.l_ref[...]


class _FlashT:
    def __init__(self, m_ref, l_ref, a_ref):
        self.m_ref, self.l_ref, self.a_ref = m_ref, l_ref, a_ref

    @staticmethod
    def step(states, logits, v_ts, first):
        ms = [jnp.max(lg, axis=0, keepdims=True) for lg in logits]
        if not first:
            olds = [st.m_ref[...] for st in states]
            ms = [jnp.maximum(o, m) for o, m in zip(olds, ms)]
            alphas = [jnp.exp(o - m) for o, m in zip(olds, ms)]
        ps = [jnp.exp(lg - m) for lg, m in zip(logits, ms)]
        sums = [jnp.sum(p, axis=0, keepdims=True) for p in ps]
        pvs = [_dot(v_t, p.astype(BF16)) for v_t, p in zip(v_ts, ps)]
        for n, st in enumerate(states):
            st.m_ref[...] = ms[n]
            if first:
                st.l_ref[...] = sums[n]
                st.a_ref[...] = pvs[n]
            else:
                st.l_ref[...] = alphas[n] * st.l_ref[...] + sums[n]
                st.a_ref[...] = alphas[n] * st.a_ref[...] + pvs[n]

    def update(self, logit, v_t):
        _FlashT.step([self], [logit], [v_t], False)

    def result(self):
        return self.a_ref[...] / self.l_ref[...]


def _nsa_body(pt_ref, rb_ref, q_ref, kc_ref, vct_ref, ks_ref, vs_ref, kw_ref, vw_ref, g_ref,
              toep_ref, bc_ref, et_ref, *rest, nb, seq, cmp):
    page_refs = rest[:cmp["pps"]]
    (wkv_ref, o_ref, kcr_ref, vcr_ref, vst_ref, vwt_ref, m_s, l_s, a_s, m_w, l_w, a_w, yk_ref,
     yv_ref) = rest[cmp["pps"]:]
    g = pl.program_id(1)
    i = pl.program_id(2)
    tq = NSA_TQ
    sub = tq // LANE
    width = HPG * tq
    q0 = i * tq
    step = (pl.program_id(0) * pl.num_programs(1) + g) * pl.num_programs(2) + i

    @pl.when(step < cmp["n_steps"])
    def _():
        _cmp_dec_step(page_refs, wkv_ref, kcr_ref, vcr_ref, yk_ref, yv_ref, cmp["page"])

    @pl.when(i == 0)
    def _():
        for c in range(seq // LANE):
            rows = slice(c * LANE, (c + 1) * LANE)
            cols = slice((c % sub) * LANE, (c % sub + 1) * LANE)
            vst_ref[c // sub, :, cols] = vs_ref[rows, :].astype(F32).T.astype(BF16)
            vwt_ref[c // sub, :, cols] = vw_ref[rows, :].astype(F32).T.astype(BF16)

    qs = jnp.concatenate([q_ref[:, hh * LANE:(hh + 1) * LANE] for hh in range(HPG)], axis=0)

    def q_off(rows):
        return lax.broadcasted_iota(I32, (rows, width), 1) % tq

    n_r = lax.broadcasted_iota(I32, (LANE, width), 0)
    s_c = _dot_nt(kc_ref[...], qs) * SCALE + bc_ref[...]
    valid_c = (CMP_BLOCK * n_r + (CMP_BLOCK - 1) <= q0 + q_off(LANE)) & (n_r < nb)
    logit = jnp.where(valid_c, s_c, NEG_INF)
    e = jnp.exp(logit - jnp.max(logit, axis=0, keepdims=True))
    p_c = jnp.where(valid_c, e / jnp.sum(e, axis=0, keepdims=True), 0.0)
    oc_t = _dot(vct_ref[...], p_c.astype(BF16))
    imp_t = p_c[:, 0:tq]
    for hh in range(1, HPG):
        imp_t = imp_t + p_c[:, hh * tq:(hh + 1) * tq]

    nbr = -(-nb // 8) * 8
    n_i = lax.broadcasted_iota(I32, (nbr, tq), 0)
    cur = (q0 + lax.broadcasted_iota(I32, (nbr, tq), 1)) // CMP_BLOCK
    forced = (n_i == 0) | (n_i >= cur - (N_LOCAL - 1))
    score = jnp.where(n_i > cur, -jnp.inf, jnp.where(forced, FORCE_SCORE, imp_t[0:nbr]))
    rank = jnp.zeros((nbr, tq), I32)
    for j in range(nb):
        sj = jnp.broadcast_to(score[j:j + 1, :], (nbr, tq))
        ahead = (sj > score) | ((sj == score) & (n_i > j))
        rank = rank + ahead.astype(I32)
    sel_t = ((rank < min(TOP_N, nb)) & (score > -jnp.inf)).astype(F32)
    if nbr < LANE:
        sel_t = jnp.concatenate([sel_t, jnp.zeros((LANE - nbr, tq), F32)], axis=0)
    sel_t = sel_t.astype(BF16)

    k_r = lax.broadcasted_iota(I32, (tq, width), 0)
    q_c = q_off(tq)
    causal = k_r <= q_c
    bias_far = jnp.concatenate(
        [jnp.full((1, tq), rb_ref[N_BUCKETS - 1, g * HPG + hh], F32) for hh in range(HPG)], axis=1)

    def chunk(ref, c):
        return ref[pl.ds(pl.multiple_of(c * tq, tq), tq), :]

    def scores(ref, c):
        return _dot_nt(chunk(ref, c), qs) * SCALE

    def selected(c):
        hit = _dot(chunk(et_ref, c), sel_t)
        return jnp.concatenate([hit] * HPG, axis=1) > 0.5

    fs = _FlashT(m_s, l_s, a_s)
    fw = _FlashT(m_w, l_w, a_w)

    def both(c, bias, sel_extra, win_mask, first):
        sel_mask = selected(c) if sel_extra is None else selected(c) & sel_extra
        lg_s = jnp.where(sel_mask, scores(ks_ref, c) + bias, NEG_INF)
        lg_w = scores(kw_ref, c) + bias
        if win_mask is not None:
            lg_w = jnp.where(win_mask, lg_w, NEG_INF)
        _FlashT.step([fs, fw], [lg_s, lg_w], [vst_ref[c], vwt_ref[c]], first)

    both(i, toep_ref[1], causal, causal, True)

    @pl.when(i >= 1)
    def _():
        both(i - 1, toep_ref[0], None, None, False)

    @pl.when(i >= 2)
    def _():
        both(i - 2, bias_far, None, k_r >= q_c, False)

    def far_step(c, carry):
        fs.update(jnp.where(selected(c), scores(ks_ref, c) + bias_far, NEG_INF), vst_ref[c])
        return carry

    lax.fori_loop(0, jnp.maximum(i - 2, 0), far_step, 0)

    g_t = jnp.concatenate([g_ref[t * LANE:(t + 1) * LANE, :].T for t in range(sub)], axis=1)

    def gate(branch):
        return jnp.concatenate(
            [g_t[branch * HPG + hh:branch * HPG + hh + 1, :] for hh in range(HPG)], axis=1)

    o_t = gate(0) * oc_t + gate(1) * fs.result() + gate(2) * fw.result()
    for hh in range(HPG):
        for t in range(sub):
            lanes = slice(hh * tq + t * LANE, hh * tq + (t + 1) * LANE)
            o_ref[t * LANE:(t + 1) * LANE, hh * LANE:(hh + 1) * LANE] = o_t[:, lanes].T


def _nsa_prompt(rel_bias, zb, kc, vct, gates, toep, bias_c, batch, seq, cmp_args):
    tq = NSA_TQ
    assert WINDOW == 2 * tq and tq % LANE == 0 and seq % tq == 0
    nq = seq // tq
    nb = seq // CMP_BLOCK
    assert nb <= LANE
    width = HPG * tq
    k = lax.broadcasted_iota(I32, (seq, LANE), 0)
    n = lax.broadcasted_iota(I32, (seq, LANE), 1)
    expand_t = (k // CMP_BLOCK == n).astype(BF16)

    l, pt_flat, cache_nsa2d, w_kv, dec_b, page = cmp_args
    npg = pt_flat.shape[0] // dec_b
    pps = min(CMP_PAGES_PER_STEP, npg)
    assert npg % pps == 0
    nchunk = npg // pps
    n_cmp = dec_b * nchunk
    assert n_cmp <= batch * G_NSA * nq
    halves = page // CMP_BLOCK
    nblk = pps * halves
    body = functools.partial(_nsa_body, nb=nb, seq=seq,
                             cmp=dict(pps=pps, page=page, n_steps=n_cmp))

    def kv_spec(blk):
        return pl.BlockSpec((seq, HEAD_DIM), lambda b, g, i, pt: (b, blk + g))

    def cmp_pos(b, g, i):
        s = jnp.minimum((b * G_NSA + g) * nq + i, n_cmp - 1)
        return _div(s, nchunk), _mod(s, nchunk)

    def page_spec(u):
        def imap(b, g, i, pt):
            row, c = cmp_pos(b, g, i)
            return (l, pt[row * npg + c * pps + u], 0, 0)
        return pl.BlockSpec((None, None, page * NSA_CH, HEAD_DIM), imap)

    def raw_imap(b, g, i, pt):
        row, c = cmp_pos(b, g, i)
        return (row, 0, c, 0)

    raw_spec = pl.BlockSpec((None, G_NSA, nblk, HEAD_DIM), raw_imap)
    raw_shape = jax.ShapeDtypeStruct((dec_b, G_NSA, npg * halves, HEAD_DIM), F32)
    stat = pltpu.VMEM((1, width), F32)
    accum = pltpu.VMEM((HEAD_DIM, width), F32)
    v_t = pltpu.VMEM((nq, HEAD_DIM, tq), BF16)
    y_raw = pltpu.VMEM((nblk * NSA_CH, HEAD_DIM), F32)
    grid_spec = pltpu.PrefetchScalarGridSpec(
        num_scalar_prefetch=1,
        grid=(batch, G_NSA, nq),
        in_specs=[
            pl.BlockSpec(memory_space=pltpu.SMEM),
            pl.BlockSpec((tq, HPG * HEAD_DIM), lambda b, g, i, pt: (b * nq + i, g)),
            pl.BlockSpec((None, None, LANE, HEAD_DIM), lambda b, g, i, pt: (b, g, 0, 0)),
            pl.BlockSpec((None, None, HEAD_DIM, LANE), lambda b, g, i, pt: (b, g, 0, 0)),
            kv_spec(KS_BLK), kv_spec(VS_BLK), kv_spec(KW_BLK), kv_spec(VW_BLK),
            pl.BlockSpec((tq, LANE), lambda b, g, i, pt: (b * nq + i, g)),
            pl.BlockSpec((2, tq, width), lambda b, g, i, pt: (0, 0, g)),
            pl.BlockSpec((None, LANE, width), lambda b, g, i, pt: (i, 0, g)),
            pl.BlockSpec((seq, LANE), lambda b, g, i, pt: (0, 0)),
        ]
        + [page_spec(u) for u in range(pps)]
        + [pl.BlockSpec(w_kv.shape, lambda b, g, i, pt: (0, 0), pipeline_mode=pl.Buffered(1))],
        out_specs=[pl.BlockSpec((tq, HPG * HEAD_DIM), lambda b, g, i, pt: (b * nq + i, g)),
                   raw_spec, raw_spec],
        scratch_shapes=[v_t, v_t, stat, stat, accum, stat, stat, accum, y_raw, y_raw],
    )
    return pl.pallas_call(
        body,
        grid_spec=grid_spec,
        out_shape=[jax.ShapeDtypeStruct((batch * seq, D_NSA), F32), raw_shape, raw_shape],
        compiler_params=_cparams(3),
        name="nsa_prompt_cmp_decode",
    )(pt_flat, rel_bias, zb, kc, vct, zb, zb, zb, zb, gates, toep, bias_c, expand_t,
      *([cache_nsa2d] * pps), w_kv)


def _prep_weights(l, w_in, g_q, g_k, w_cmp_k, w_cmp_v):
    n_gate = 3 * H_NSA
    g0 = D_NSA + 6 * D_KV
    wi = w_in[l]
    w_a = wi[:, :g0].astype(BF16)
    w_b = wi[:, g0 + n_gate:].astype(BF16)
    wg = wi[:, g0:g0 + n_gate].reshape(D_MODEL, 3, G_NSA, HPG)
    wg = jnp.transpose(wg, (0, 2, 1, 3)).reshape(D_MODEL, G_NSA, 3 * HPG)
    wg = jnp.pad(wg, ((0, 0), (0, 0), (0, LANE - 3 * HPG))).reshape(D_MODEL, G_NSA * LANE)
    ones = jnp.ones((HEAD_DIM,), F32)
    gain = jnp.concatenate(
        [jnp.tile(g_q[l], H_NSA), jnp.tile(ones, 2 * G_NSA),
         jnp.tile(g_k[l, 1], G_NSA), jnp.tile(ones, G_NSA),
         jnp.tile(g_k[l, 2], G_NSA), jnp.tile(ones, G_NSA),
         jnp.tile(ones, 3 * H_SB)])[None, :]
    wk = w_cmp_k[l].reshape(CMP_BLOCK * HEAD_DIM, HEAD_DIM).astype(BF16)
    wv = w_cmp_v[l].reshape(CMP_BLOCK * HEAD_DIM, HEAD_DIM).astype(BF16)
    return dict(
        w_a=w_a, w_b=w_b, w_gate=wg.astype(BF16), gain=gain, wk=wk, wv=wv,
        w_kv=jnp.concatenate([wk, wv], axis=1))


def _prompt_layer(x2d, l, wt, dense_w, p, tables, rel_bias, batch, seq, sb_args, cmp_args, stacked):
    na, bk, bv, gk0, go, nf = p
    zb, sb_rows, nsa_rows, win_rows, gates = _inproj(
        x2d, na, wt["w_a"], wt["w_b"], wt["w_gate"], wt["gain"], seq, min(WINDOW, seq),
        (l,) + stacked)
    kc, vct = _cmp_prompt(nsa_rows, l, batch, seq, wt["wk"], wt["wv"], bk, bv, gk0)
    o_nsa, kcr, vcr = _nsa_prompt(rel_bias, zb, kc, vct, gates, tables[0], tables[1], batch, seq,
                                  cmp_args)
    o_sb, dense_b = _sb_prompt(zb, batch, seq, l, dense_w)
    wt["w_out"], wt["w_gate_ffn"], wt["w_up"], wt["w_down"] = dense_b
    x1 = _outproj(o_nsa, o_sb, go, wt["w_out"], x2d)
    x2, o_sb_sample = _ffn(x1, nf, wt["w_gate_ffn"], wt["w_up"], wt["w_down"], sb_args)
    return x2, sb_rows, nsa_rows, win_rows, (o_sb_sample, kcr, vcr)


ROWS_S = 16


def _sb_dec_step(ph, c, last_c, q_ref, page_refs, up_ref, upp_ref, o_ref, z_ref, acc_ref, page):
    pps = len(page_refs)
    width = page * H_SB
    r = lax.broadcasted_iota(I32, (ROWS_S, width), 0)
    lane = lax.broadcasted_iota(I32, (ROWS_S, width), 1)
    own_head = (lane % H_SB) == r

    @pl.when(ph == 0)
    def _():
        q = q_ref[...]
        for u in range(pps):
            ks = page_refs[u][...].reshape(width, HEAD_DIM).astype(BF16)
            zt = _dot_nt(q, ks)
            z_ref[pl.ds(c * pps + u, 1), :] = jnp.sum(jnp.where(own_head, zt, 0.0), axis=0,
                                                      keepdims=True)

    @pl.when((ph == 1) & (c == 0))
    def _():
        z = z_ref[...] * SCALE
        log_1m = _neg_softplus(z)
        cs = _dot_split(log_1m, up_ref[...])
        hi, lo = _split_bf16(cs[:, width:])
        later_pages = _dot(upp_ref[...], hi) + _dot(upp_ref[...], lo)
        z_ref[...] = jnp.exp(z + log_1m + cs[:, :width] + later_pages)
        acc_ref[...] = jnp.zeros_like(acc_ref)

    @pl.when(ph == 1)
    def _():
        for u in range(pps):
            vs = page_refs[u][...].reshape(width, HEAD_DIM).astype(BF16)
            a = jnp.broadcast_to(z_ref[pl.ds(c * pps + u, 1), :], (ROWS_S, width))
            acc_ref[...] += _dot(jnp.where(own_head, a, 0.0).astype(BF16), vs)

    @pl.when((ph == 1) & (c == last_c))
    def _():
        o_ref[...] = acc_ref[0:H_SB, :]


def _sb_dec_consts(npg, page):
    assert npg % SB_PAGES_PER_STEP == 0 and npg % 8 == 0
    width = page * H_SB
    r = lax.broadcasted_iota(I32, (width, 2 * width), 0)
    c = lax.broadcasted_iota(I32, (width, 2 * width), 1)
    same_head = (r % H_SB) == (c % H_SB)
    up = (same_head & ((c >= width) | (r // H_SB > c // H_SB))).astype(BF16)
    pr = lax.broadcasted_iota(I32, (npg, npg), 0)
    pc = lax.broadcasted_iota(I32, (npg, npg), 1)
    upp = (pc > pr).astype(BF16)
    return up, upp


NSA_CH = 4 * G_NSA


def _cmp_dec_step(page_refs, w_ref, kc_ref, vc_ref, yk_ref, yv_ref, page):
    pps = len(page_refs)
    halves = page // CMP_BLOCK
    nblk = pps * halves
    acc = jnp.zeros((nblk * NSA_CH, 2 * HEAD_DIM), F32)

    def token_rows(u, hf, l):
        return page_refs[u][pl.ds((hf * CMP_BLOCK + l) * NSA_CH, NSA_CH), :]

    for l in range(0, CMP_BLOCK, 2):
        x_l = jnp.concatenate(
            [jnp.concatenate([token_rows(u, hf, l), token_rows(u, hf, l + 1)], axis=1)
             for u in range(pps) for hf in range(halves)], axis=0).astype(BF16)
        acc = acc + _dot(x_l, w_ref[l * HEAD_DIM:(l + 2) * HEAD_DIM, :])
    yk_ref[...] = acc[:, :HEAD_DIM]
    yv_ref[...] = acc[:, HEAD_DIM:]
    for g in range(G_NSA):
        kc_ref[g] = yk_ref[pl.ds(g, nblk, stride=NSA_CH), :]
        vc_ref[g] = yv_ref[pl.ds(G_NSA + g, nblk, stride=NSA_CH), :]


def _nsa_sel_body(q_ref, kcr_ref, vcr_ref, new_ref, wk0_ref, wv0_ref, bk_ref, bv_ref, gk_ref,
                  sbc_ref, kw_ref, vw_ref, sbw_ref, idx_ref, oc_ref, ow_ref,
                  *, nb0, nbp, past_len, w_buf, wp):
    nb = nb0 + 1
    new = new_ref[...].astype(BF16)
    new_k = _rms(_dot(new, wk0_ref[...]) + bk_ref[...], gk_ref[...])
    new_v = _dot(new, wv0_ref[...]) + bv_ref[...]
    row8 = lax.broadcasted_iota(I32, (8, HEAD_DIM), 0)
    n1 = lax.broadcasted_iota(I32, (1, nbp), 1)
    nq = lax.broadcasted_iota(I32, (ROWS_S, nbp), 1)
    pos_q = past_len
    pad = jnp.zeros((nbp - nb0 - 8, HEAD_DIM), F32)
    for g in range(G_NSA):
        q = q_ref[g]
        kc = jnp.concatenate(
            [_rms(kcr_ref[g] + bk_ref[...], gk_ref[...]),
             jnp.where(row8 == 0, jnp.broadcast_to(new_k[g:g + 1], (8, HEAD_DIM)), 0.0), pad],
            axis=0).astype(BF16)
        vc = jnp.concatenate(
            [vcr_ref[g] + bv_ref[...],
             jnp.where(row8 == 0, jnp.broadcast_to(new_v[G_NSA + g:G_NSA + g + 1], (8, HEAD_DIM)),
                       0.0), pad], axis=0).astype(BF16)
        s_c = _dot_nt(q, kc) * SCALE + sbc_ref[g]
        valid_c = (CMP_BLOCK * nq + (CMP_BLOCK - 1) <= pos_q) & (nq < nb)
        p_c = _masked_softmax(s_c, valid_c)
        oc_ref[g] = _dot(p_c.astype(BF16), vc)
        imp = p_c[0:1]
        for hh in range(1, HPG):
            imp = imp + p_c[hh:hh + 1]

        cur = pos_q // CMP_BLOCK
        forced = (n1 == 0) | (n1 >= cur - (N_LOCAL - 1))
        score = jnp.where(n1 > cur, -jnp.inf, jnp.where(forced, FORCE_SCORE, imp))
        sq = jnp.broadcast_to(score, (LANE, nbp))
        col = jnp.concatenate([sq[:, k * LANE:(k + 1) * LANE].T for k in range(nbp // LANE)],
                              axis=0)
        j_i = lax.broadcasted_iota(I32, (nbp, LANE), 0)
        ranks = []
        for k in range(nbp // LANE):
            mine = jnp.broadcast_to(score[:, k * LANE:(k + 1) * LANE], (nbp, LANE))
            n_i = k * LANE + lax.broadcasted_iota(I32, (nbp, LANE), 1)
            ahead = (col > mine) | ((col == mine) & (j_i < n_i))
            ranks.append(jnp.sum(ahead.astype(F32), axis=0, keepdims=True))
        rank = jnp.concatenate(ranks, axis=1)
        slot = lax.broadcasted_iota(I32, (TOP_N, nbp), 0).astype(F32)
        hit = jnp.broadcast_to(rank, (TOP_N, nbp)) == slot
        ids = jnp.sum(jnp.where(hit, lax.broadcasted_iota(I32, (TOP_N, nbp), 1).astype(F32), 0.0),
                      axis=1, keepdims=True)
        idx_ref[g] = jnp.broadcast_to(ids, (TOP_N, LANE)).astype(I32)

        jw = lax.broadcasted_iota(I32, (ROWS_S, wp), 1)
        dist_w = w_buf - jw
        valid_w = (dist_w >= 0) & (dist_w <= WINDOW) & (past_len - w_buf + jw >= 0)
        s_w = _dot_nt(q, kw_ref[g]) * SCALE + sbw_ref[g]
        p_w = _masked_softmax(s_w, valid_w)
        ow_ref[g] = _dot(p_w.astype(BF16), vw_ref[g])


def _nsa_select(q_nsa, kcr, vcr, new_c, wk0, wv0, bk, bv, gk0, sbc, kw, vw, sbw, past_len, w_buf):
    dec_b, _, nb0, _ = kcr.shape
    nbp = sbc.shape[-1]
    wp = sbw.shape[-1]
    assert nb0 + 1 >= TOP_N and nbp >= nb0 + 8 and nb0 % 8 == 0
    body = functools.partial(_nsa_sel_body, nb0=nb0, nbp=nbp, past_len=past_len, w_buf=w_buf, wp=wp)

    def per_b(shape):
        nd = len(shape)
        return pl.BlockSpec((None,) + tuple(shape), lambda b: (b,) + (0,) * nd)

    def whole(arr):
        nd = arr.ndim
        return pl.BlockSpec(arr.shape, lambda b: (0,) * nd)

    vec = (G_NSA, ROWS_S, HEAD_DIM)
    return pl.pallas_call(
        body,
        grid=(dec_b,),
        in_specs=[per_b(vec), per_b((G_NSA, nb0, HEAD_DIM)), per_b((G_NSA, nb0, HEAD_DIM)),
                  per_b((8, HEAD_DIM)), whole(wk0), whole(wv0), whole(bk), whole(bv), whole(gk0),
                  whole(sbc), per_b((G_NSA, wp, HEAD_DIM)), per_b((G_NSA, wp, HEAD_DIM)), whole(sbw)],
        out_specs=[per_b((G_NSA, TOP_N, LANE)), per_b(vec), per_b(vec)],
        out_shape=[jax.ShapeDtypeStruct((dec_b, G_NSA, TOP_N, LANE), I32),
                   jax.ShapeDtypeStruct((dec_b,) + vec, F32),
                   jax.ShapeDtypeStruct((dec_b,) + vec, F32)],
        compiler_params=_cparams(1),
        name="nsa_select",
    )(q_nsa, kcr, vcr, new_c, wk0, wv0, bk, bv, gk0, sbc, kw, vw, sbw)


GATHER_BLOCKS = 8


def _nsa_gather_body(pt_ref, idx_ref, rb_ref, q_ref, *refs, nb0, past_len):
    n_pages = G_NSA * GATHER_BLOCKS
    page_refs = refs[:n_pages]
    new_ref, gs_ref, oc_ref, ow_ref, o_ref, m_ref, l_ref, a_ref = refs[n_pages:]
    b = pl.program_id(0)
    k = pl.program_id(1)
    pos_q = past_len
    keys = GATHER_BLOCKS * CMP_BLOCK
    row = lax.broadcasted_iota(I32, (CMP_BLOCK, HEAD_DIM), 0)
    l1 = lax.broadcasted_iota(I32, (1, CMP_BLOCK), 1)
    hrow = lax.broadcasted_iota(I32, (ROWS_S, keys), 0)
    for g in range(G_NSA):
        new_k = jnp.broadcast_to(new_ref[g:g + 1, :], (CMP_BLOCK, HEAD_DIM))
        new_v = jnp.broadcast_to(new_ref[G_NSA + g:G_NSA + g + 1, :], (CMP_BLOCK, HEAD_DIM))
        ks, vs, dist = [], [], []
        for kk in range(GATHER_BLOCKS):
            blk = idx_ref[(b * G_NSA + g) * TOP_N + k * GATHER_BLOCKS + kk]
            is_new = blk >= nb0
            page_ref = page_refs[g * GATHER_BLOCKS + kk]
            k_blk = page_ref[pl.ds(2 * G_NSA + g, CMP_BLOCK, stride=NSA_CH), :]
            v_blk = page_ref[pl.ds(3 * G_NSA + g, CMP_BLOCK, stride=NSA_CH), :]
            ks.append(jnp.where(is_new, jnp.where(row == 0, new_k, 0.0), k_blk).astype(BF16))
            vs.append(jnp.where(is_new, jnp.where(row == 0, new_v, 0.0), v_blk).astype(BF16))
            dist.append(pos_q - (blk * CMP_BLOCK + l1))
        ks = jnp.concatenate(ks, axis=0)
        vs = jnp.concatenate(vs, axis=0)
        dist = jnp.concatenate(dist, axis=1)
        bucket = jnp.broadcast_to(_t5_bucket(dist), (ROWS_S, keys))
        bias = jnp.zeros((ROWS_S, keys), F32)
        for hh in range(HPG):
            for bk in range(N_BUCKETS):
                bias = jnp.where((hrow == hh) & (bucket == bk), rb_ref[bk, g * HPG + hh], bias)
        s = _dot_nt(q_ref[g], ks) * SCALE + bias
        logit = jnp.where(jnp.broadcast_to(dist >= 0, (ROWS_S, keys)), s, NEG_INF)
        fl = _Flash(m_ref.at[g], l_ref.at[g], a_ref.at[g])

        @pl.when(k == 0)
        def _():
            fl.first(logit, vs)

        @pl.when(k > 0)
        def _():
            fl.update(logit, vs)

        @pl.when(k == pl.num_programs(1) - 1)
        def _():
            o_ref[g] = gs_ref[g, 0] * oc_ref[g] + gs_ref[g, 1] * fl.result() + gs_ref[g, 2] * ow_ref[g]


def _nsa_gather(l, pt_flat, idx_flat, rel_bias, q_nsa, cache_nsa2d, new_s, gs, oc, ow, nb0,
                past_len, page):
    dec_b = q_nsa.shape[0]
    npg = pt_flat.shape[0] // dec_b
    halves = page // CMP_BLOCK
    body = functools.partial(_nsa_gather_body, nb0=nb0, past_len=past_len)

    def page_spec(g, kk):
        def imap(b, k, pt, idx):
            blk = jnp.minimum(idx[(b * G_NSA + g) * TOP_N + k * GATHER_BLOCKS + kk], nb0 - 1)
            return (l, pt[b * npg + _div(blk, halves)], _mod(blk, halves), 0)
        return pl.BlockSpec((None, None, CMP_BLOCK * NSA_CH, HEAD_DIM), imap)

    def per_b(shape):
        nd = len(shape)
        return pl.BlockSpec((None,) + tuple(shape), lambda b, k, pt, idx: (b,) + (0,) * nd)

    vec = (G_NSA, ROWS_S, HEAD_DIM)
    assert TOP_N % GATHER_BLOCKS == 0
    n_pages = G_NSA * GATHER_BLOCKS
    grid_spec = pltpu.PrefetchScalarGridSpec(
        num_scalar_prefetch=2,
        grid=(dec_b, TOP_N // GATHER_BLOCKS),
        in_specs=[pl.BlockSpec(memory_space=pltpu.SMEM), per_b(vec)]
        + [page_spec(g, kk) for g in range(G_NSA) for kk in range(GATHER_BLOCKS)]
        + [per_b((8, HEAD_DIM)), per_b((G_NSA, 3, ROWS_S, HEAD_DIM)), per_b(vec), per_b(vec)],
        out_specs=per_b(vec),
        scratch_shapes=[pltpu.VMEM((G_NSA, ROWS_S, 1), F32), pltpu.VMEM((G_NSA, ROWS_S, 1), F32),
                        pltpu.VMEM((G_NSA, ROWS_S, HEAD_DIM), F32)],
    )
    return pl.pallas_call(
        body,
        grid_spec=grid_spec,
        out_shape=jax.ShapeDtypeStruct((dec_b,) + vec, F32),
        compiler_params=_cparams(2),
        name="nsa_gather",
    )(pt_flat, idx_flat, rel_bias, q_nsa, *([cache_nsa2d] * n_pages), new_s, gs, oc, ow)


def _pad_rows(x, rows):
    return jnp.pad(x, ((0, rows - x.shape[0]),) + ((0, 0),) * (x.ndim - 1))


def _sample_inproj(xs, wt, p, dec_b):
    rows = xs.shape[0]
    zb, sb_rows, nsa_rows, win_rows, gates = _inproj(
        xs, p[0], wt["w_a"], wt["w_b"], wt["w_gate"], wt["gain"], rows, rows)
    zb8 = zb[:dec_b]
    q_sb = zb8[:, QS_BLK * LANE:KSB_BLK * LANE].reshape(dec_b, H_SB, HEAD_DIM)
    q_sb = jnp.pad(q_sb, ((0, 0), (0, ROWS_S - H_SB), (0, 0)))
    keep = lambda a: a[:dec_b * (a.shape[0] // rows)]
    return q_sb, (zb8, keep(sb_rows), keep(nsa_rows), keep(win_rows), gates)


def _sample_layer(l, xs, projected, walked, wt, p, sb_tabs, rel_bias, pt_flat, cache_nsa2d, win_l,
                  w_cmp0, dec_b, past_len, page):
    na, bk, bv, gk0, go, nf = p
    sbc, sbw = sb_tabs
    w_buf = win_l.shape[1]
    rows = xs.shape[0]
    zb8, sb_rows, nsa_rows, win_rows, gates = projected
    o_sb, kcr, vcr = walked
    o_sb = o_sb.reshape(dec_b, D_SB)

    q_nsa = zb8[:, :D_NSA].reshape(dec_b, G_NSA, HPG, HEAD_DIM)
    q_nsa = jnp.pad(q_nsa, ((0, 0), (0, 0), (0, ROWS_S - HPG), (0, 0)))
    nsa_new = nsa_rows.reshape(dec_b, 4 * G_NSA, HEAD_DIM)
    new_c = jnp.pad(nsa_new[:, :2 * G_NSA], ((0, 0), (0, 8 - 2 * G_NSA), (0, 0)))
    new_s = jnp.pad(nsa_new[:, 2 * G_NSA:], ((0, 0), (0, 8 - 2 * G_NSA), (0, 0)))
    new_w = win_rows.reshape(dec_b, 1, 2, G_NSA, HEAD_DIM)
    win_all = jnp.concatenate([win_l, new_w], axis=1)
    wp = sbw.shape[-1]
    win_t = jnp.transpose(win_all, (2, 0, 3, 1, 4))
    win_t = jnp.pad(win_t, ((0, 0), (0, 0), (0, 0), (0, wp - w_buf - 1), (0, 0))).astype(BF16)
    idx, oc, ow = _nsa_select(q_nsa, kcr, vcr, new_c, w_cmp0[0], w_cmp0[1], bk, bv, gk0, sbc,
                              win_t[0], win_t[1], sbw, past_len, w_buf)
    gs = gates[:dec_b].reshape(dec_b, G_NSA, LANE)[:, :, :3 * HPG].reshape(dec_b, G_NSA, 3, HPG)
    gs = jnp.pad(gs, ((0, 0), (0, 0), (0, 0), (0, ROWS_S - HPG)))
    gs = jnp.broadcast_to(gs[..., None], gs.shape + (HEAD_DIM,))
    o_nsa = _nsa_gather(l, pt_flat, idx[:, :, :, 0].reshape(-1), rel_bias, q_nsa, cache_nsa2d,
                        new_s, gs, oc, ow, kcr.shape[2], past_len, page)
    o_nsa = o_nsa[:, :, :HPG].reshape(dec_b, D_NSA)

    x1 = _outproj(_pad_rows(o_nsa, rows), _pad_rows(o_sb, rows), go, wt["w_out"], xs)
    x2 = _ffn(x1, nf, wt["w_gate_ffn"], wt["w_up"], wt["w_down"])
    new_win = win_all[:, 1:]
    return x2, sb_rows, nsa_rows, new_win


def kernel(x_prompt, x_sample, cache_sb_kv, cache_nsa_kv, state_win_kv, page_table, rel_bias,
           norm_attn, w_in, g_q, g_k, w_cmp_k, b_cmp_k, w_cmp_v, b_cmp_v, g_out, w_out,
           norm_ffn, w_gate, w_up, w_down):
    batch, seq, _ = x_prompt.shape
    dec_b, dec_t, _ = x_sample.shape
    assert dec_t == 1
    depth, n_pool, page = cache_sb_kv.shape[:3]
    npg = page_table.shape[1]
    past_len = npg * page
    w_buf = state_win_kv.shape[2]
    nb_s = past_len // CMP_BLOCK + 1
    nbp_s = -(-(nb_s + 7) // LANE) * LANE
    wp_s = -(-(w_buf + 1) // LANE) * LANE

    toep, bias_c, sc, sw = _bias_tables(rel_bias, seq, past_len, w_buf, nbp_s, wp_s)
    sbc = jnp.pad(sc.reshape(G_NSA, HPG, nbp_s), ((0, 0), (0, ROWS_S - HPG), (0, 0)))
    sbw = jnp.pad(sw.reshape(G_NSA, HPG, wp_s), ((0, 0), (0, ROWS_S - HPG), (0, 0)))
    pt_flat = page_table.reshape(-1).astype(I32)
    cache_nsa2d = cache_nsa_kv.reshape(depth, n_pool, page * NSA_CH, HEAD_DIM)

    xp = x_prompt.reshape(batch * seq, D_MODEL)
    xs = _pad_rows(x_sample.reshape(dec_b, D_MODEL), ROWS_S)
    sb_s, nsa_s, win_s = [], [], []
    w_keep = min(WINDOW, seq)
    rows_p = tuple(jnp.zeros((depth * n * per, LANE), F32) for n, per in
                   ((batch * seq, 2 * H_SB), (batch * seq, 4 * G_NSA), (batch * w_keep, 2 * G_NSA)))
    for l in range(depth):
        wt = _prep_weights(l, w_in, g_q, g_k, w_cmp_k, w_cmp_v)
        p = (norm_attn[l][None], b_cmp_k[l][None], b_cmp_v[l][None], g_k[l, 0][None],
             g_out[l][None], norm_ffn[l][None])
        q_sb, projected = _sample_inproj(xs, wt, p, dec_b)
        xp, *rows_p, walked = _prompt_layer(
            xp, l, wt, (w_out, w_gate, w_up, w_down), p, (toep, bias_c), rel_bias, batch, seq,
            (l, pt_flat, q_sb, cache_sb_kv), (l, pt_flat, cache_nsa2d, wt["w_kv"], dec_b, page),
            (depth, tuple(rows_p)))

        w_cmp0 = (w_cmp_k[l, 0].astype(BF16), w_cmp_v[l, 0].astype(BF16))
        xs, sb_rows, nsa_rows, new_win = _sample_layer(
            l, xs, projected, walked, wt, p, (sbc, sbw), rel_bias, pt_flat, cache_nsa2d,
            state_win_kv[l], w_cmp0, dec_b, past_len, page)
        sb_s.append(sb_rows.reshape(dec_b, 1, 2, H_SB, HEAD_DIM))
        nsa_s.append(nsa_rows.reshape(dec_b, 1, 4, G_NSA, HEAD_DIM))
        win_s.append(new_win)
    sb_p, nsa_p, win_p = rows_p
    return (xp.reshape(batch, seq, D_MODEL), xs[:dec_b].reshape(dec_b, 1, D_MODEL),
            sb_p.reshape(depth, batch, seq, 2, H_SB, HEAD_DIM), jnp.stack(sb_s),
            nsa_p.reshape(depth, batch, seq, 4, G_NSA, HEAD_DIM), jnp.stack(nsa_s),
            win_p.reshape(depth, batch, w_keep, 2, G_NSA, HEAD_DIM), jnp.stack(win_s))
```

```python
import functools
import math

import jax
import jax.numpy as jnp
from jax import lax
from jax.experimental import pallas as pl
from jax.experimental.pallas import tpu as pltpu

F32 = jnp.float32
BF16 = jnp.bfloat16
I32 = jnp.int32

LANE = 128
HEAD_DIM = 128
G_NSA = 2
HPG = 4
H_NSA = G_NSA * HPG
H_SB = 8
D_NSA = H_NSA * HEAD_DIM
D_SB = H_SB * HEAD_DIM
D_MODEL = D_NSA + D_SB
D_KV = G_NSA * HEAD_DIM
CMP_BLOCK = 64
TOP_N = 16
N_LOCAL = 2
WINDOW = 512
N_BUCKETS = 32
MAX_DISTANCE = 128
EPS = 1e-6
FORCE_SCORE = 1e4
NEG_INF = -1e30
SCALE = HEAD_DIM ** -0.5

COL_TILE = 512
N_MAIN = D_NSA + 6 * D_KV + 3 * D_SB
QS_BLK = (D_NSA + 6 * D_KV) // LANE
KSB_BLK = QS_BLK + D_SB // LANE
VSB_BLK = KSB_BLK + D_SB // LANE
KS_BLK = (D_NSA + 2 * D_KV) // LANE
VS_BLK = KS_BLK + G_NSA
KW_BLK = VS_BLK + G_NSA
VW_BLK = KW_BLK + G_NSA
VMEM_LIMIT = 56 * 1024 * 1024

SB_TQ = 256
SB_TK = 256
SB_HB = 8
NSA_TQ = 256
SB_PAGES_PER_STEP = 16
CMP_PAGES_PER_STEP = 16


VMEM_LIMIT_FUSED = 62 * 1024 * 1024


def _cparams(n_axes, vmem_limit=VMEM_LIMIT):
    return pltpu.CompilerParams(dimension_semantics=("arbitrary",) * n_axes,
                                vmem_limit_bytes=vmem_limit)


def _dot(a, b):
    return jnp.dot(a, b, preferred_element_type=F32)


def _dot_nt(a, b):
    return lax.dot_general(a, b, (((1,), (1,)), ((), ())), preferred_element_type=F32)


def _split_bf16(x):
    hi = x.astype(BF16)
    lo = (x - hi.astype(F32)).astype(BF16)
    return hi, lo


def _dot_split(x, u):
    hi, lo = _split_bf16(x)
    return _dot(hi, u) + _dot(lo, u)


def _rms(y, gain):
    ms = jnp.mean(y * y, axis=-1, keepdims=True)
    return y * lax.rsqrt(ms + EPS) * gain


def _neg_softplus(z):
    return -(jnp.maximum(z, 0.0) + jnp.log(1.0 + jnp.exp(-jnp.abs(z))))


def _t5_bucket(dist):
    n = jnp.maximum(dist, 0)
    max_exact = N_BUCKETS // 2
    nf = jnp.maximum(n, 1).astype(F32)
    large = max_exact + (jnp.log(nf / max_exact) / math.log(MAX_DISTANCE / max_exact)
                         * (N_BUCKETS - max_exact)).astype(I32)
    large = jnp.minimum(large, N_BUCKETS - 1)
    return jnp.where(n < max_exact, n, large)


def _div(x, n):
    if n & (n - 1) == 0:
        return lax.shift_right_logical(x, n.bit_length() - 1)
    return x // n


def _mod(x, n):
    if n & (n - 1) == 0:
        return x & (n - 1)
    return x % n


def _pick_tile(m, candidates):
    for t in candidates:
        if m % t == 0:
            return t
    raise ValueError(f"no tile for {m}")


def _bias_body(rb_ref, toep_ref, bc_ref, sc_ref, sw_ref, *, seq, past_len, w_buf):
    h = pl.program_id(0)
    tq = NSA_TQ

    def table(dist):
        bk = _t5_bucket(dist)
        out = jnp.zeros(dist.shape, F32)
        for b in range(N_BUCKETS):
            out = jnp.where(bk == b, rb_ref[b, h], out)
        return out

    k = lax.broadcasted_iota(I32, (tq, tq), 0)
    q = lax.broadcasted_iota(I32, (tq, tq), 1)
    toep_ref[0] = table(tq + q - k)
    toep_ref[1] = table(q - k)
    nbr = -(-(seq // CMP_BLOCK) // 8) * 8
    n = lax.broadcasted_iota(I32, (nbr, tq), 0)
    qc = lax.broadcasted_iota(I32, (nbr, tq), 1)
    for i in range(seq // tq):
        bc_ref[i, 0:nbr, :] = table(i * tq + qc - CMP_BLOCK * n - (CMP_BLOCK - 1))
        if nbr < LANE:
            bc_ref[i, nbr:, :] = jnp.zeros((LANE - nbr, tq), F32)
    n1 = lax.broadcasted_iota(I32, sc_ref.shape, 1)
    sc_ref[...] = table(past_len - CMP_BLOCK * n1 - (CMP_BLOCK - 1))
    j1 = lax.broadcasted_iota(I32, sw_ref.shape, 1)
    sw_ref[...] = table(w_buf - j1)


def _bias_tables(rel_bias, seq, past_len, w_buf, nbp_s, wp_s):
    body = functools.partial(_bias_body, seq=seq, past_len=past_len, w_buf=w_buf)
    nq = seq // NSA_TQ
    return pl.pallas_call(
        body,
        grid=(H_NSA,),
        in_specs=[pl.BlockSpec(memory_space=pltpu.SMEM)],
        out_specs=[
            pl.BlockSpec((2, NSA_TQ, NSA_TQ), lambda h: (0, 0, h)),
            pl.BlockSpec((nq, LANE, NSA_TQ), lambda h: (0, 0, h)),
            pl.BlockSpec((None, 1, nbp_s), lambda h: (h, 0, 0)),
            pl.BlockSpec((None, 1, wp_s), lambda h: (h, 0, 0)),
        ],
        out_shape=[
            jax.ShapeDtypeStruct((2, NSA_TQ, H_NSA * NSA_TQ), F32),
            jax.ShapeDtypeStruct((nq, LANE, H_NSA * NSA_TQ), F32),
            jax.ShapeDtypeStruct((H_NSA, 1, nbp_s), F32),
            jax.ShapeDtypeStruct((H_NSA, 1, wp_s), F32),
        ],
        compiler_params=_cparams(1),
        name="bias_tables",
    )(rel_bias)


Q_TILES = D_NSA // COL_TILE
NSA_TILE0 = Q_TILES
WIN_TILE = NSA_TILE0 + 4 * D_KV // COL_TILE
SB_TILE0 = WIN_TILE + 2 * D_KV // COL_TILE + D_SB // COL_TILE


def _inproj_body(x_ref, na_ref, wa_ref, wb_ref, wg_ref, gain_ref, *rest, w_keep):
    zb_ref, sb_ref, nsa_ref, win_ref, gate_ref, h_ref = rest[-6:]
    j = pl.program_id(1)
    tm = x_ref.shape[0]
    half = COL_TILE // 2

    @pl.when(j == 0)
    def _():
        h = _rms(x_ref[...], na_ref[...]).astype(BF16)
        h_ref[...] = h
        gate_ref[...] = jax.nn.sigmoid(_dot(h, wg_ref[...]))

    gain = gain_ref[...]
    chunks = COL_TILE // LANE

    def emit(w_ref, norm_chunks, f32_ref=None, first=0, per_row=1, row0=0):
        h = h_ref[...]
        accs = [_dot(h, w_ref[:, :half]), _dot(h, w_ref[:, half:])]
        for c in range(chunks):
            sl = slice(c * LANE, (c + 1) * LANE)
            y = accs[c * LANE // half][:, (c * LANE) % half:(c * LANE) % half + LANE]
            if c in norm_chunks:
                y = _rms(y, gain[:, sl])
            zb_ref[:, sl] = y.astype(BF16)
            if f32_ref is not None:
                f32_ref[pl.ds(first + c, tm - row0, stride=per_row), :] = y[row0:, :]

    @pl.when(j < Q_TILES)
    def _():
        emit(wa_ref, (0, 1, 2, 3))

    @pl.when(j == NSA_TILE0)
    def _():
        emit(wa_ref, (), nsa_ref, 0, 2 * chunks)

    @pl.when(j == NSA_TILE0 + 1)
    def _():
        emit(wa_ref, (0, 1), nsa_ref, chunks, 2 * chunks)

    @pl.when(j == WIN_TILE)
    def _():
        emit(wa_ref, (0, 1), win_ref, 0, chunks, tm - w_keep)

    @pl.when((j > WIN_TILE) & (j < SB_TILE0))
    def _():
        emit(wb_ref, ())

    @pl.when(j >= SB_TILE0)
    def _():
        emit(wb_ref, (), sb_ref, (j - SB_TILE0) * chunks, 2 * H_SB)


def _inproj(x2d, na, w_a, w_b, w_gate, gain_cols, seq, w_keep, stacked=None):
    m = x2d.shape[0]
    tm = _pick_tile(seq, (1024, 512, 256, 128, 16))
    assert w_keep <= tm
    tpb = seq // tm
    sb_per, nsa_per, win_per = 2 * H_SB, 4 * G_NSA, 2 * G_NSA
    n_a = w_a.shape[1] // COL_TILE
    n_b = w_b.shape[1] // COL_TILE
    assert n_a == WIN_TILE + 1 and n_a + n_b == N_MAIN // COL_TILE
    body = functools.partial(_inproj_body, w_keep=w_keep)
    layer, depth, bufs = (0, 1, ()) if stacked is None else stacked
    row0, win0 = layer * (m // tm), layer * (m // seq)
    n_in = 6
    return pl.pallas_call(
        body,
        grid=(m // tm, n_a + n_b),
        in_specs=[
            pl.BlockSpec((tm, D_MODEL), lambda i, j: (i, 0)),
            pl.BlockSpec((1, D_MODEL), lambda i, j: (0, 0)),
            pl.BlockSpec((D_MODEL, COL_TILE), lambda i, j: (0, jnp.minimum(j, n_a - 1))),
            pl.BlockSpec((D_MODEL, COL_TILE), lambda i, j: (0, jnp.maximum(j - n_a, 0))),
            pl.BlockSpec((D_MODEL, 2 * LANE), lambda i, j: (0, 0)),
            pl.BlockSpec((1, COL_TILE), lambda i, j: (0, j)),
        ] + [pl.BlockSpec(memory_space=pl.ANY)] * len(bufs),
        out_specs=[
            pl.BlockSpec((tm, COL_TILE), lambda i, j: (i, j)),
            pl.BlockSpec((tm * sb_per, LANE), lambda i, j: (row0 + i, 0),
                         pipeline_mode=pl.Buffered(1)),
            pl.BlockSpec((tm * nsa_per, LANE), lambda i, j: (row0 + i, 0),
                         pipeline_mode=pl.Buffered(1)),
            pl.BlockSpec((w_keep * win_per, LANE), lambda i, j: (win0 + _div(i, tpb), 0)),
            pl.BlockSpec((tm, 2 * LANE), lambda i, j: (i, 0)),
        ],
        out_shape=[
            jax.ShapeDtypeStruct((m, N_MAIN), BF16),
            jax.ShapeDtypeStruct((depth * m * sb_per, LANE), F32),
            jax.ShapeDtypeStruct((depth * m * nsa_per, LANE), F32),
            jax.ShapeDtypeStruct((depth * (m // seq) * w_keep * win_per, LANE), F32),
            jax.ShapeDtypeStruct((m, 2 * LANE), F32),
        ],
        input_output_aliases={n_in + k: 1 + k for k in range(len(bufs))},
        scratch_shapes=[pltpu.VMEM((tm, D_MODEL), BF16)],
        compiler_params=_cparams(2),
        name="inproj",
    )(x2d, na, w_a, w_b, w_gate, gain_cols, *bufs)


def _outproj_body(on_ref, os_ref, go_ref, w_ref, x_ref, y_ref, h_ref):
    j = pl.program_id(1)

    @pl.when(j == 0)
    def _():
        go = go_ref[...]
        h_ref[:, :D_NSA] = _rms(on_ref[...], go[:, :D_NSA]).astype(BF16)
        if os_ref.dtype == BF16:
            h_ref[:, D_NSA:] = os_ref[...]
        else:
            h_ref[:, D_NSA:] = _rms(os_ref[...], go[:, D_NSA:]).astype(BF16)

    y_ref[...] = x_ref[...] + _dot(h_ref[...], w_ref[...])


def _outproj(o_nsa, o_sb, g_out, w_out, x2d):
    m = x2d.shape[0]
    tm = _pick_tile(m, (512, 256, 128, 16))
    tn = D_MODEL
    return pl.pallas_call(
        _outproj_body,
        grid=(m // tm, D_MODEL // tn),
        in_specs=[
            pl.BlockSpec((tm, D_NSA), lambda i, j: (i, 0)),
            pl.BlockSpec((tm, D_SB), lambda i, j: (i, 0)),
            pl.BlockSpec((1, D_MODEL), lambda i, j: (0, 0)),
            pl.BlockSpec((D_MODEL, tn), lambda i, j: (0, j), pipeline_mode=pl.Buffered(1)),
            pl.BlockSpec((tm, tn), lambda i, j: (i, j)),
        ],
        out_specs=pl.BlockSpec((tm, tn), lambda i, j: (i, j)),
        out_shape=jax.ShapeDtypeStruct((m, D_MODEL), F32),
        scratch_shapes=[pltpu.VMEM((tm, D_MODEL), BF16)],
        compiler_params=_cparams(2),
        name="outproj",
    )(o_nsa, o_sb, g_out, w_out, x2d)


def _ffn_body(*refs, sb):
    if sb is None:
        x_ref, nf_ref, wg_ref, wu_ref, wd_ref, y_ref, h_ref = refs
    else:
        pps = sb["pps"]
        x_ref, nf_ref, wg_ref, wu_ref, wd_ref, q_ref = refs[1:7]
        page_refs = refs[7:7 + pps]
        up_ref, upp_ref, y_ref, o_ref, h_ref, z_ref, sacc_ref = refs[7 + pps:]
    i = pl.program_id(0)
    j = pl.program_id(1)

    @pl.when(j == 0)
    def _():
        x = x_ref[...]
        h_ref[...] = _rms(x, nf_ref[...]).astype(BF16)
        y_ref[...] = x

    h = h_ref[...]
    act = jax.nn.silu(_dot(h, wg_ref[...])) * _dot(h, wu_ref[...])
    y_ref[...] += _dot(act.astype(BF16), wd_ref[...])

    if sb is not None:
        nc = sb["n_chunks"]
        s = i * pl.num_programs(1) + j

        @pl.when(s < sb["n_steps"])
        def _():
            _sb_dec_step(_mod(_div(s, nc), 2), _mod(s, nc), nc - 1, q_ref, page_refs, up_ref, upp_ref, o_ref,
                         z_ref, sacc_ref, sb["page"])


def _ffn(x2d, norm_ffn, w_gate, w_up, w_down, sb_args=None):
    m = x2d.shape[0]
    d_ff = w_gate.shape[1]
    if sb_args is None:
        tm = _pick_tile(m, (512, 256, 128, 16))
        tf = _pick_tile(d_ff, (512, 256))
        x_mode = None
    else:
        tm = _pick_tile(m, (1024, 512, 256, 128))
        tf = _pick_tile(d_ff, (256,))
        x_mode = pl.Buffered(1)
    nf = d_ff // tf
    in_specs = [
        pl.BlockSpec((tm, D_MODEL), lambda i, j, *_: (i, 0), pipeline_mode=x_mode),
        pl.BlockSpec((1, D_MODEL), lambda i, j, *_: (0, 0)),
        pl.BlockSpec((D_MODEL, tf), lambda i, j, *_: (0, j)),
        pl.BlockSpec((D_MODEL, tf), lambda i, j, *_: (0, j)),
        pl.BlockSpec((tf, D_MODEL), lambda i, j, *_: (j, 0)),
    ]
    y_spec = pl.BlockSpec((tm, D_MODEL), lambda i, j, *_: (i, 0))
    y_shape = jax.ShapeDtypeStruct((m, D_MODEL), F32)
    h_scratch = pltpu.VMEM((tm, D_MODEL), BF16)
    if sb_args is None:
        return pl.pallas_call(
            functools.partial(_ffn_body, sb=None),
            grid=(m // tm, nf),
            in_specs=in_specs,
            out_specs=y_spec,
            out_shape=y_shape,
            scratch_shapes=[h_scratch],
            compiler_params=_cparams(2),
            name="ffn",
        )(x2d, norm_ffn, w_gate, w_up, w_down)

    l, pt_flat, q_sb, cache_sb = sb_args
    dec_b = q_sb.shape[0]
    page = cache_sb.shape[2]
    npg = pt_flat.shape[0] // dec_b
    pps = SB_PAGES_PER_STEP
    nc = npg // pps
    n_steps = dec_b * 2 * nc
    assert n_steps <= (m // tm) * nf
    width = page * H_SB
    up, upp = _sb_dec_consts(npg, page)

    def pos(i, j):
        s = jnp.minimum(i * nf + j, n_steps - 1)
        return _div(s, 2 * nc), _mod(_div(s, nc), 2), _mod(s, nc)

    def page_spec(u):
        def imap(i, j, pt):
            b, ph, c = pos(i, j)
            return (l, pt[b * npg + c * pps + u], 0, ph, 0, 0)
        return pl.BlockSpec((None, None, page, None, H_SB, HEAD_DIM), imap)

    once = pl.Buffered(1)
    grid_spec = pltpu.PrefetchScalarGridSpec(
        num_scalar_prefetch=1,
        grid=(m // tm, nf),
        in_specs=in_specs
        + [pl.BlockSpec((None, ROWS_S, HEAD_DIM), lambda i, j, pt: (pos(i, j)[0], 0, 0))]
        + [page_spec(u) for u in range(pps)]
        + [pl.BlockSpec((width, 2 * width), lambda i, j, pt: (0, 0), pipeline_mode=once),
           pl.BlockSpec((npg, npg), lambda i, j, pt: (0, 0), pipeline_mode=once)],
        out_specs=[y_spec,
                   pl.BlockSpec((None, H_SB, HEAD_DIM), lambda i, j, pt: (pos(i, j)[0], 0, 0))],
        scratch_shapes=[h_scratch, pltpu.VMEM((npg, width), F32),
                        pltpu.VMEM((ROWS_S, HEAD_DIM), F32)],
    )
    sb = dict(pps=pps, page=page, n_chunks=nc, n_steps=n_steps)
    return pl.pallas_call(
        functools.partial(_ffn_body, sb=sb),
        grid_spec=grid_spec,
        out_shape=[y_shape, jax.ShapeDtypeStruct((dec_b, H_SB, HEAD_DIM), F32)],
        compiler_params=_cparams(2, VMEM_LIMIT_FUSED),
        name="ffn_sb_decode",
    )(pt_flat, x2d, norm_ffn, w_gate, w_up, w_down, q_sb, *([cache_sb] * pps), up, upp)


SB_SUB = 4


def _sb_body(*refs, n_cast, cast_steps):
    n_sub = SB_HB // SB_SUB
    q_refs, k_refs, v_refs = refs[:n_sub], refs[n_sub:2 * n_sub], refs[2 * n_sub:3 * n_sub]
    u_ref = refs[3 * n_sub]
    w_refs = refs[3 * n_sub + 1:3 * n_sub + 1 + n_cast]
    go_ref = refs[3 * n_sub + 1 + n_cast]
    o_ref = refs[3 * n_sub + 2 + n_cast]
    wb_refs = refs[3 * n_sub + 3 + n_cast:3 * n_sub + 3 + 2 * n_cast]
    acc_ref, run_ref = refs[3 * n_sub + 3 + 2 * n_cast:]
    i = pl.program_id(2)
    pos = (pl.program_id(0) * pl.num_programs(1) + pl.program_id(1)) * pl.num_programs(2) + i

    @pl.when(pos < cast_steps)
    def _():
        for w_ref, wb_ref in zip(w_refs, wb_refs):
            wb_ref[...] = w_ref[...].astype(BF16)

    row = lax.broadcasted_iota(I32, (SB_TQ, SB_TK), 0)
    col = lax.broadcasted_iota(I32, (SB_TQ, SB_TK), 1)
    strict = col < row

    def tile(kb, diagonal):
        off = pl.multiple_of(kb * SB_TK, SB_TK)
        heads = range(SB_HB)
        sls = [slice(h % SB_SUB * HEAD_DIM, (h % SB_SUB + 1) * HEAD_DIM) for h in heads]
        q_ref = [q_refs[h // SB_SUB] for h in heads]
        k_ref = [k_refs[h // SB_SUB] for h in heads]
        v_ref = [v_refs[h // SB_SUB] for h in heads]
        z = [_dot_nt(q_ref[h][:, sls[h]], k_ref[h][pl.ds(off, SB_TK), sls[h]]) * SCALE
             for h in heads]
        log_1m = [_neg_softplus(zh) for zh in z]
        if diagonal:
            log_1m = [jnp.where(strict, lh, 0.0) for lh in log_1m]
        parts = [_split_bf16(lh) for lh in log_1m]
        u = u_ref[...]
        later = [_dot(hi, u) + _dot(lo, u) for hi, lo in parts]
        total = [jnp.sum(lh, axis=-1, keepdims=True) for lh in log_1m]
        if diagonal:
            a = [jnp.where(strict, jnp.exp(z[h] + log_1m[h] + later[h]), 0.0) for h in heads]
        else:
            run = [run_ref[h] for h in heads]
            a = [jnp.exp(z[h] + log_1m[h] + later[h] + run[h]) for h in heads]
        pv = [_dot(a[h].astype(BF16), v_ref[h][pl.ds(off, SB_TK), sls[h]]) for h in heads]
        for h in heads:
            if diagonal:
                acc_ref[h] = pv[h]
                run_ref[h] = total[h]
            else:
                acc_ref[h] += pv[h]
                run_ref[h] = run[h] + total[h]

    tile(i, True)

    def step(it, carry):
        tile(i - 1 - it, False)
        return carry

    lax.fori_loop(0, i, step, 0)
    o = jnp.concatenate([acc_ref[h] for h in range(SB_HB)], axis=1)
    o_ref[...] = _rms(o, go_ref[...]).astype(BF16)


def _sb_prompt(zb, batch, seq, l, weights, go_sb):
    assert SB_HB == H_SB
    nq = seq // SB_TQ
    n_grid = batch * (H_SB // SB_HB) * nq
    cast_steps = 1
    while (cast_steps * 2 <= n_grid
           and all(w.shape[1] % (cast_steps * 2 * 16) == 0 for w in weights)):
        cast_steps *= 2

    def w_spec(w, stacked):
        rows = w.shape[1] // cast_steps
        imap = lambda b, h, i: (jnp.minimum((b * (H_SB // SB_HB) + h) * nq + i, cast_steps - 1), 0)
        if stacked:
            return pl.BlockSpec((None, rows, w.shape[2]), lambda b, h, i: (l,) + imap(b, h, i))
        return pl.BlockSpec((rows, w.shape[2]), imap)

    width = SB_HB * HEAD_DIM
    sub_w = SB_SUB * HEAD_DIM
    n_sub = SB_HB // SB_SUB
    assert QS_BLK % SB_SUB == 0 and KSB_BLK % SB_SUB == 0 and VSB_BLK % SB_SUB == 0
    r = lax.broadcasted_iota(I32, (SB_TK, SB_TK), 0)
    c = lax.broadcasted_iota(I32, (SB_TK, SB_TK), 1)
    u = (r > c).astype(BF16)

    def q_spec(n):
        return pl.BlockSpec((SB_TQ, sub_w),
                            lambda b, h, i: (b * nq + i, QS_BLK // SB_SUB + h * n_sub + n))

    def kv_spec(blk, n):
        return pl.BlockSpec((seq, sub_w), lambda b, h, i: (b, blk // SB_SUB + h * n_sub + n))

    subs = range(n_sub)
    outs = pl.pallas_call(
        functools.partial(_sb_body, n_cast=len(weights), cast_steps=cast_steps),
        grid=(batch, H_SB // SB_HB, nq),
        in_specs=[q_spec(n) for n in subs] + [kv_spec(KSB_BLK, n) for n in subs]
        + [kv_spec(VSB_BLK, n) for n in subs]
        + [pl.BlockSpec((SB_TK, SB_TK), lambda b, h, i: (0, 0))]
        + [w_spec(w, True) for w in weights]
        + [pl.BlockSpec((1, D_SB), lambda b, h, i: (0, 0))],
        out_specs=[pl.BlockSpec((SB_TQ, width), lambda b, h, i: (b * nq + i, h))]
        + [w_spec(w, False) for w in weights],
        out_shape=[jax.ShapeDtypeStruct((batch * seq, D_SB), BF16)]
        + [jax.ShapeDtypeStruct(w.shape[1:], BF16) for w in weights],
        scratch_shapes=[pltpu.VMEM((SB_HB, SB_TQ, HEAD_DIM), F32), pltpu.VMEM((SB_HB, SB_TQ, 1), F32)],
        compiler_params=_cparams(3),
        name="sb_prompt_cast",
    )(*([zb] * (3 * n_sub)), u, *weights, go_sb)
    return outs[0], outs[1:]


def _cmp_prompt_body(x_ref, wk_ref, wv_ref, bk_ref, bv_ref, gk_ref, kc_ref, vct_ref, *, nb):
    pad = jnp.zeros((LANE - nb, HEAD_DIM), F32)
    per = 4 * G_NSA
    for c in range(2 * G_NSA):
        rows = jnp.concatenate(
            [x_ref[pl.ds(l * per + c, nb, stride=CMP_BLOCK * per), :] for l in range(CMP_BLOCK)],
            axis=1)
        rows = rows.astype(BF16)
        if c < G_NSA:
            y = _rms(_dot(rows, wk_ref[...]) + bk_ref[...], gk_ref[...])
            kc_ref[c] = jnp.concatenate([y, pad], axis=0).astype(BF16)
        else:
            y = _dot(rows, wv_ref[...]) + bv_ref[...]
            vct_ref[c - G_NSA] = jnp.concatenate([y, pad], axis=0).T.astype(BF16)


def _cmp_prompt(nsa_rows, layer, batch, seq, wk, wv, bk, bv, gk0):
    nb = seq // CMP_BLOCK
    assert nb % 8 == 0 and nb < LANE
    body = functools.partial(_cmp_prompt_body, nb=nb)
    shape = jax.ShapeDtypeStruct((batch, G_NSA, LANE, HEAD_DIM), BF16)

    return pl.pallas_call(
        body,
        grid=(batch,),
        in_specs=[
            pl.BlockSpec((seq * 4 * G_NSA, HEAD_DIM), lambda b: (layer * batch + b, 0)),
            pl.BlockSpec(wk.shape, lambda b: (0, 0)),
            pl.BlockSpec(wv.shape, lambda b: (0, 0)),
            pl.BlockSpec((1, HEAD_DIM), lambda b: (0, 0)),
            pl.BlockSpec((1, HEAD_DIM), lambda b: (0, 0)),
            pl.BlockSpec((1, HEAD_DIM), lambda b: (0, 0)),
        ],
        out_specs=[pl.BlockSpec((None, G_NSA, LANE, HEAD_DIM), lambda b: (b, 0, 0, 0))] * 2,
        out_shape=[shape, shape],
        compiler_params=_cparams(1),
        name="cmp_prompt",
    )(nsa_rows, wk, wv, bk, bv, gk0)


def _masked_softmax(s, valid):
    logit = jnp.where(valid, s, NEG_INF)
    e = jnp.exp(logit - jnp.max(logit, axis=-1, keepdims=True))
    return jnp.where(valid, e / jnp.sum(e, axis=-1, keepdims=True), 0.0)


class _Flash:
    def __init__(self, m_ref, l_ref, a_ref):
        self.m_ref, self.l_ref, self.a_ref = m_ref, l_ref, a_ref

    def first(self, logit, v):
        m = jnp.max(logit, axis=-1, keepdims=True)
        p = jnp.exp(logit - m)
        self.m_ref[...] = m
        self.l_ref[...] = jnp.sum(p, axis=-1, keepdims=True)
        self.a_ref[...] = _dot(p.astype(BF16), v)

    def update(self, logit, v):
        m_old = self.m_ref[...]
        m = jnp.maximum(m_old, jnp.max(logit, axis=-1, keepdims=True))
        alpha = jnp.exp(m_old - m)
        p = jnp.exp(logit - m)
        self.m_ref[...] = m
        self.l_ref[...] = alpha * self.l_ref[...] + jnp.sum(p, axis=-1, keepdims=True)
        self.a_ref[...] = alpha * self.a_ref[...] + _dot(p.astype(BF16), v)

    def result(self):
        return self.a_ref[...] / self.l_ref[...]


class _FlashT:
    def __init__(self, m_ref, l_ref, a_ref):
        self.m_ref, self.l_ref, self.a_ref = m_ref, l_ref, a_ref

    @staticmethod
    def step(states, logits, v_ts, first):
        ms = [jnp.max(lg, axis=0, keepdims=True) for lg in logits]
        if not first:
            olds = [st.m_ref[...] for st in states]
            ms = [jnp.maximum(o, m) for o, m in zip(olds, ms)]
            alphas = [jnp.exp(o - m) for o, m in zip(olds, ms)]
        ps = [jnp.exp(lg - m) for lg, m in zip(logits, ms)]
        sums = [jnp.sum(p, axis=0, keepdims=True) for p in ps]
        pvs = [_dot(v_t, p.astype(BF16)) for v_t, p in zip(v_ts, ps)]
        for n, st in enumerate(states):
            st.m_ref[...] = ms[n]
            if first:
                st.l_ref[...] = sums[n]
                st.a_ref[...] = pvs[n]
            else:
                st.l_ref[...] = alphas[n] * st.l_ref[...] + sums[n]
                st.a_ref[...] = alphas[n] * st.a_ref[...] + pvs[n]

    def update(self, logit, v_t):
        _FlashT.step([self], [logit], [v_t], False)

    def result(self):
        return self.a_ref[...] / self.l_ref[...]


def _nsa_body(pt_ref, rb_ref, q_ref, kc_ref, vct_ref, ks_ref, vs_ref, kw_ref, vw_ref, g_ref,
              toep_ref, bc_ref, et_ref, *rest, nb, seq, cmp):
    page_refs = rest[:cmp["pps"]]
    (wkv_ref, o_ref, kcr_ref, vcr_ref, vst_ref, vwt_ref, m_s, l_s, a_s, m_w, l_w, a_w, yk_ref,
     yv_ref) = rest[cmp["pps"]:]
    g = pl.program_id(1)
    i = pl.program_id(2)
    tq = NSA_TQ
    sub = tq // LANE
    width = HPG * tq
    q0 = i * tq
    step = (pl.program_id(0) * pl.num_programs(1) + g) * pl.num_programs(2) + i

    @pl.when(step < cmp["n_steps"])
    def _():
        _cmp_dec_step(page_refs, wkv_ref, kcr_ref, vcr_ref, yk_ref, yv_ref, cmp["page"])

    @pl.when(i == 0)
    def _():
        for c in range(seq // LANE):
            rows = slice(c * LANE, (c + 1) * LANE)
            cols = slice((c % sub) * LANE, (c % sub + 1) * LANE)
            vst_ref[c // sub, :, cols] = vs_ref[rows, :].astype(F32).T.astype(BF16)
            vwt_ref[c // sub, :, cols] = vw_ref[rows, :].astype(F32).T.astype(BF16)

    qs = jnp.concatenate([q_ref[:, hh * LANE:(hh + 1) * LANE] for hh in range(HPG)], axis=0)

    def q_off(rows):
        return lax.broadcasted_iota(I32, (rows, width), 1) % tq

    n_r = lax.broadcasted_iota(I32, (LANE, width), 0)
    s_c = _dot_nt(kc_ref[...], qs) * SCALE + bc_ref[...]
    valid_c = (CMP_BLOCK * n_r + (CMP_BLOCK - 1) <= q0 + q_off(LANE)) & (n_r < nb)
    logit = jnp.where(valid_c, s_c, NEG_INF)
    e = jnp.exp(logit - jnp.max(logit, axis=0, keepdims=True))
    p_c = jnp.where(valid_c, e / jnp.sum(e, axis=0, keepdims=True), 0.0)
    oc_t = _dot(vct_ref[...], p_c.astype(BF16))
    imp_t = p_c[:, 0:tq]
    for hh in range(1, HPG):
        imp_t = imp_t + p_c[:, hh * tq:(hh + 1) * tq]

    nbr = -(-nb // 8) * 8
    n_i = lax.broadcasted_iota(I32, (nbr, tq), 0)
    cur = (q0 + lax.broadcasted_iota(I32, (nbr, tq), 1)) // CMP_BLOCK
    forced = (n_i == 0) | (n_i >= cur - (N_LOCAL - 1))
    score = jnp.where(n_i > cur, -jnp.inf, jnp.where(forced, FORCE_SCORE, imp_t[0:nbr]))
    rank = jnp.zeros((nbr, tq), I32)
    for j in range(nb):
        sj = jnp.broadcast_to(score[j:j + 1, :], (nbr, tq))
        ahead = (sj > score) | ((sj == score) & (n_i > j))
        rank = rank + ahead.astype(I32)
    sel_t = ((rank < min(TOP_N, nb)) & (score > -jnp.inf)).astype(F32)
    if nbr < LANE:
        sel_t = jnp.concatenate([sel_t, jnp.zeros((LANE - nbr, tq), F32)], axis=0)
    sel_t = sel_t.astype(BF16)

    k_r = lax.broadcasted_iota(I32, (tq, width), 0)
    q_c = q_off(tq)
    causal = k_r <= q_c
    bias_far = jnp.concatenate(
        [jnp.full((1, tq), rb_ref[N_BUCKETS - 1, g * HPG + hh], F32) for hh in range(HPG)], axis=1)

    def chunk(ref, c):
        return ref[pl.ds(pl.multiple_of(c * tq, tq), tq), :]

    def scores(ref, c):
        return _dot_nt(chunk(ref, c), qs) * SCALE

    def selected(c):
        hit = _dot(chunk(et_ref, c), sel_t)
        return jnp.concatenate([hit] * HPG, axis=1) > 0.5

    fs = _FlashT(m_s, l_s, a_s)
    fw = _FlashT(m_w, l_w, a_w)

    def both(c, bias, sel_extra, win_mask, first):
        sel_mask = selected(c) if sel_extra is None else selected(c) & sel_extra
        lg_s = jnp.where(sel_mask, scores(ks_ref, c) + bias, NEG_INF)
        lg_w = scores(kw_ref, c) + bias
        if win_mask is not None:
            lg_w = jnp.where(win_mask, lg_w, NEG_INF)
        _FlashT.step([fs, fw], [lg_s, lg_w], [vst_ref[c], vwt_ref[c]], first)

    both(i, toep_ref[1], causal, causal, True)

    @pl.when(i >= 1)
    def _():
        both(i - 1, toep_ref[0], None, None, False)

    @pl.when(i >= 2)
    def _():
        both(i - 2, bias_far, None, k_r >= q_c, False)

    def far_step(c, carry):
        fs.update(jnp.where(selected(c), scores(ks_ref, c) + bias_far, NEG_INF), vst_ref[c])
        return carry

    lax.fori_loop(0, jnp.maximum(i - 2, 0), far_step, 0)

    g_t = jnp.concatenate([g_ref[t * LANE:(t + 1) * LANE, :].T for t in range(sub)], axis=1)

    def gate(branch):
        return jnp.concatenate(
            [g_t[branch * HPG + hh:branch * HPG + hh + 1, :] for hh in range(HPG)], axis=1)

    o_t = gate(0) * oc_t + gate(1) * fs.result() + gate(2) * fw.result()
    for hh in range(HPG):
        for t in range(sub):
            lanes = slice(hh * tq + t * LANE, hh * tq + (t + 1) * LANE)
            o_ref[t * LANE:(t + 1) * LANE, hh * LANE:(hh + 1) * LANE] = o_t[:, lanes].T


def _nsa_prompt(rel_bias, zb, kc, vct, gates, toep, bias_c, batch, seq, cmp_args):
    tq = NSA_TQ
    assert WINDOW == 2 * tq and tq % LANE == 0 and seq % tq == 0
    nq = seq // tq
    nb = seq // CMP_BLOCK
    assert nb <= LANE
    width = HPG * tq
    k = lax.broadcasted_iota(I32, (seq, LANE), 0)
    n = lax.broadcasted_iota(I32, (seq, LANE), 1)
    expand_t = (k // CMP_BLOCK == n).astype(BF16)

    l, pt_flat, cache_nsa2d, w_kv, dec_b, page = cmp_args
    npg = pt_flat.shape[0] // dec_b
    pps = min(CMP_PAGES_PER_STEP, npg)
    assert npg % pps == 0
    nchunk = npg // pps
    n_cmp = dec_b * nchunk
    assert n_cmp <= batch * G_NSA * nq
    halves = page // CMP_BLOCK
    nblk = pps * halves
    body = functools.partial(_nsa_body, nb=nb, seq=seq,
                             cmp=dict(pps=pps, page=page, n_steps=n_cmp))

    def kv_spec(blk):
        return pl.BlockSpec((seq, HEAD_DIM), lambda b, g, i, pt: (b, blk + g))

    def cmp_pos(b, g, i):
        s = jnp.minimum((b * G_NSA + g) * nq + i, n_cmp - 1)
        return _div(s, nchunk), _mod(s, nchunk)

    def page_spec(u):
        def imap(b, g, i, pt):
            row, c = cmp_pos(b, g, i)
            return (l, pt[row * npg + c * pps + u], 0, 0)
        return pl.BlockSpec((None, None, page * NSA_CH, HEAD_DIM), imap)

    def raw_imap(b, g, i, pt):
        row, c = cmp_pos(b, g, i)
        return (row, 0, c, 0)

    raw_spec = pl.BlockSpec((None, G_NSA, nblk, HEAD_DIM), raw_imap)
    raw_shape = jax.ShapeDtypeStruct((dec_b, G_NSA, npg * halves, HEAD_DIM), F32)
    stat = pltpu.VMEM((1, width), F32)
    accum = pltpu.VMEM((HEAD_DIM, width), F32)
    v_t = pltpu.VMEM((nq, HEAD_DIM, tq), BF16)
    y_raw = pltpu.VMEM((nblk * NSA_CH, HEAD_DIM), F32)
    grid_spec = pltpu.PrefetchScalarGridSpec(
        num_scalar_prefetch=1,
        grid=(batch, G_NSA, nq),
        in_specs=[
            pl.BlockSpec(memory_space=pltpu.SMEM),
            pl.BlockSpec((tq, HPG * HEAD_DIM), lambda b, g, i, pt: (b * nq + i, g)),
            pl.BlockSpec((None, None, LANE, HEAD_DIM), lambda b, g, i, pt: (b, g, 0, 0)),
            pl.BlockSpec((None, None, HEAD_DIM, LANE), lambda b, g, i, pt: (b, g, 0, 0)),
            kv_spec(KS_BLK), kv_spec(VS_BLK), kv_spec(KW_BLK), kv_spec(VW_BLK),
            pl.BlockSpec((tq, LANE), lambda b, g, i, pt: (b * nq + i, g)),
            pl.BlockSpec((2, tq, width), lambda b, g, i, pt: (0, 0, g)),
            pl.BlockSpec((None, LANE, width), lambda b, g, i, pt: (i, 0, g)),
            pl.BlockSpec((seq, LANE), lambda b, g, i, pt: (0, 0)),
        ]
        + [page_spec(u) for u in range(pps)]
        + [pl.BlockSpec(w_kv.shape, lambda b, g, i, pt: (0, 0), pipeline_mode=pl.Buffered(1))],
        out_specs=[pl.BlockSpec((tq, HPG * HEAD_DIM), lambda b, g, i, pt: (b * nq + i, g)),
                   raw_spec, raw_spec],
        scratch_shapes=[v_t, v_t, stat, stat, accum, stat, stat, accum, y_raw, y_raw],
    )
    return pl.pallas_call(
        body,
        grid_spec=grid_spec,
        out_shape=[jax.ShapeDtypeStruct((batch * seq, D_NSA), F32), raw_shape, raw_shape],
        compiler_params=_cparams(3),
        name="nsa_prompt_cmp_decode",
    )(pt_flat, rel_bias, zb, kc, vct, zb, zb, zb, zb, gates, toep, bias_c, expand_t,
      *([cache_nsa2d] * pps), w_kv)


def _prep_weights(l, w_in, g_q, g_k, w_cmp_k, w_cmp_v):
    n_gate = 3 * H_NSA
    g0 = D_NSA + 6 * D_KV
    wi = w_in[l]
    w_a = wi[:, :g0].astype(BF16)
    w_b = wi[:, g0 + n_gate:].astype(BF16)
    wg = wi[:, g0:g0 + n_gate].reshape(D_MODEL, 3, G_NSA, HPG)
    wg = jnp.transpose(wg, (0, 2, 1, 3)).reshape(D_MODEL, G_NSA, 3 * HPG)
    wg = jnp.pad(wg, ((0, 0), (0, 0), (0, LANE - 3 * HPG))).reshape(D_MODEL, G_NSA * LANE)
    ones = jnp.ones((HEAD_DIM,), F32)
    gain = jnp.concatenate(
        [jnp.tile(g_q[l], H_NSA), jnp.tile(ones, 2 * G_NSA),
         jnp.tile(g_k[l, 1], G_NSA), jnp.tile(ones, G_NSA),
         jnp.tile(g_k[l, 2], G_NSA), jnp.tile(ones, G_NSA),
         jnp.tile(ones, 3 * H_SB)])[None, :]
    wk = w_cmp_k[l].reshape(CMP_BLOCK * HEAD_DIM, HEAD_DIM).astype(BF16)
    wv = w_cmp_v[l].reshape(CMP_BLOCK * HEAD_DIM, HEAD_DIM).astype(BF16)
    return dict(
        w_a=w_a, w_b=w_b, w_gate=wg.astype(BF16), gain=gain, wk=wk, wv=wv,
        w_kv=jnp.concatenate([wk, wv], axis=1))


def _prompt_layer(x2d, l, wt, dense_w, p, tables, rel_bias, batch, seq, sb_args, cmp_args, stacked):
    na, bk, bv, gk0, go, nf = p
    zb, sb_rows, nsa_rows, win_rows, gates = _inproj(
        x2d, na, wt["w_a"], wt["w_b"], wt["w_gate"], wt["gain"], seq, min(WINDOW, seq),
        (l,) + stacked)
    kc, vct = _cmp_prompt(nsa_rows, l, batch, seq, wt["wk"], wt["wv"], bk, bv, gk0)
    o_nsa, kcr, vcr = _nsa_prompt(rel_bias, zb, kc, vct, gates, tables[0], tables[1], batch, seq,
                                  cmp_args)
    o_sb, dense_b = _sb_prompt(zb, batch, seq, l, dense_w, go[:, D_NSA:])
    wt["w_out"], wt["w_gate_ffn"], wt["w_up"], wt["w_down"] = dense_b
    x1 = _outproj(o_nsa, o_sb, go, wt["w_out"], x2d)
    x2, o_sb_sample = _ffn(x1, nf, wt["w_gate_ffn"], wt["w_up"], wt["w_down"], sb_args)
    return x2, sb_rows, nsa_rows, win_rows, (o_sb_sample, kcr, vcr)


ROWS_S = 16


def _sb_dec_step(ph, c, last_c, q_ref, page_refs, up_ref, upp_ref, o_ref, z_ref, acc_ref, page):
    pps = len(page_refs)
    width = page * H_SB
    r = lax.broadcasted_iota(I32, (ROWS_S, width), 0)
    lane = lax.broadcasted_iota(I32, (ROWS_S, width), 1)
    own_head = (lane % H_SB) == r

    @pl.when(ph == 0)
    def _():
        q = q_ref[...]
        for u in range(pps):
            ks = page_refs[u][...].reshape(width, HEAD_DIM).astype(BF16)
            zt = _dot_nt(q, ks)
            z_ref[pl.ds(c * pps + u, 1), :] = jnp.sum(jnp.where(own_head, zt, 0.0), axis=0,
                                                      keepdims=True)

    @pl.when((ph == 1) & (c == 0))
    def _():
        z = z_ref[...] * SCALE
        log_1m = _neg_softplus(z)
        cs = _dot_split(log_1m, up_ref[...])
        hi, lo = _split_bf16(cs[:, width:])
        later_pages = _dot(upp_ref[...], hi) + _dot(upp_ref[...], lo)
        z_ref[...] = jnp.exp(z + log_1m + cs[:, :width] + later_pages)
        acc_ref[...] = jnp.zeros_like(acc_ref)

    @pl.when(ph == 1)
    def _():
        for u in range(pps):
            vs = page_refs[u][...].reshape(width, HEAD_DIM).astype(BF16)
            a = jnp.broadcast_to(z_ref[pl.ds(c * pps + u, 1), :], (ROWS_S, width))
            acc_ref[...] += _dot(jnp.where(own_head, a, 0.0).astype(BF16), vs)

    @pl.when((ph == 1) & (c == last_c))
    def _():
        o_ref[...] = acc_ref[0:H_SB, :]


def _sb_dec_consts(npg, page):
    assert npg % SB_PAGES_PER_STEP == 0 and npg % 8 == 0
    width = page * H_SB
    r = lax.broadcasted_iota(I32, (width, 2 * width), 0)
    c = lax.broadcasted_iota(I32, (width, 2 * width), 1)
    same_head = (r % H_SB) == (c % H_SB)
    up = (same_head & ((c >= width) | (r // H_SB > c // H_SB))).astype(BF16)
    pr = lax.broadcasted_iota(I32, (npg, npg), 0)
    pc = lax.broadcasted_iota(I32, (npg, npg), 1)
    upp = (pc > pr).astype(BF16)
    return up, upp


NSA_CH = 4 * G_NSA


def _cmp_dec_step(page_refs, w_ref, kc_ref, vc_ref, yk_ref, yv_ref, page):
    pps = len(page_refs)
    halves = page // CMP_BLOCK
    nblk = pps * halves
    acc = jnp.zeros((nblk * NSA_CH, 2 * HEAD_DIM), F32)

    def token_rows(u, hf, l):
        return page_refs[u][pl.ds((hf * CMP_BLOCK + l) * NSA_CH, NSA_CH), :]

    for l in range(0, CMP_BLOCK, 2):
        x_l = jnp.concatenate(
            [jnp.concatenate([token_rows(u, hf, l), token_rows(u, hf, l + 1)], axis=1)
             for u in range(pps) for hf in range(halves)], axis=0).astype(BF16)
        acc = acc + _dot(x_l, w_ref[l * HEAD_DIM:(l + 2) * HEAD_DIM, :])
    yk_ref[...] = acc[:, :HEAD_DIM]
    yv_ref[...] = acc[:, HEAD_DIM:]
    for g in range(G_NSA):
        kc_ref[g] = yk_ref[pl.ds(g, nblk, stride=NSA_CH), :]
        vc_ref[g] = yv_ref[pl.ds(G_NSA + g, nblk, stride=NSA_CH), :]


def _nsa_sel_body(q_ref, kcr_ref, vcr_ref, new_ref, wk0_ref, wv0_ref, bk_ref, bv_ref, gk_ref,
                  sbc_ref, kw_ref, vw_ref, sbw_ref, idx_ref, oc_ref, ow_ref,
                  *, nb0, nbp, past_len, w_buf, wp):
    nb = nb0 + 1
    new = new_ref[...].astype(BF16)
    new_k = _rms(_dot(new, wk0_ref[...]) + bk_ref[...], gk_ref[...])
    new_v = _dot(new, wv0_ref[...]) + bv_ref[...]
    row8 = lax.broadcasted_iota(I32, (8, HEAD_DIM), 0)
    n1 = lax.broadcasted_iota(I32, (1, nbp), 1)
    nq = lax.broadcasted_iota(I32, (ROWS_S, nbp), 1)
    pos_q = past_len
    pad = jnp.zeros((nbp - nb0 - 8, HEAD_DIM), F32)
    for g in range(G_NSA):
        q = q_ref[g]
        kc = jnp.concatenate(
            [_rms(kcr_ref[g] + bk_ref[...], gk_ref[...]),
             jnp.where(row8 == 0, jnp.broadcast_to(new_k[g:g + 1], (8, HEAD_DIM)), 0.0), pad],
            axis=0).astype(BF16)
        vc = jnp.concatenate(
            [vcr_ref[g] + bv_ref[...],
             jnp.where(row8 == 0, jnp.broadcast_to(new_v[G_NSA + g:G_NSA + g + 1], (8, HEAD_DIM)),
                       0.0), pad], axis=0).astype(BF16)
        s_c = _dot_nt(q, kc) * SCALE + sbc_ref[g]
        valid_c = (CMP_BLOCK * nq + (CMP_BLOCK - 1) <= pos_q) & (nq < nb)
        p_c = _masked_softmax(s_c, valid_c)
        oc_ref[g] = _dot(p_c.astype(BF16), vc)
        imp = p_c[0:1]
        for hh in range(1, HPG):
            imp = imp + p_c[hh:hh + 1]

        cur = pos_q // CMP_BLOCK
        forced = (n1 == 0) | (n1 >= cur - (N_LOCAL - 1))
        score = jnp.where(n1 > cur, -jnp.inf, jnp.where(forced, FORCE_SCORE, imp))
        sq = jnp.broadcast_to(score, (LANE, nbp))
        col = jnp.concatenate([sq[:, k * LANE:(k + 1) * LANE].T for k in range(nbp // LANE)],
                              axis=0)
        j_i = lax.broadcasted_iota(I32, (nbp, LANE), 0)
        ranks = []
        for k in range(nbp // LANE):
            mine = jnp.broadcast_to(score[:, k * LANE:(k + 1) * LANE], (nbp, LANE))
            n_i = k * LANE + lax.broadcasted_iota(I32, (nbp, LANE), 1)
            ahead = (col > mine) | ((col == mine) & (j_i < n_i))
            ranks.append(jnp.sum(ahead.astype(F32), axis=0, keepdims=True))
        rank = jnp.concatenate(ranks, axis=1)
        slot = lax.broadcasted_iota(I32, (TOP_N, nbp), 0).astype(F32)
        hit = jnp.broadcast_to(rank, (TOP_N, nbp)) == slot
        ids = jnp.sum(jnp.where(hit, lax.broadcasted_iota(I32, (TOP_N, nbp), 1).astype(F32), 0.0),
                      axis=1, keepdims=True)
        idx_ref[g] = jnp.broadcast_to(ids, (TOP_N, LANE)).astype(I32)

        jw = lax.broadcasted_iota(I32, (ROWS_S, wp), 1)
        dist_w = w_buf - jw
        valid_w = (dist_w >= 0) & (dist_w <= WINDOW) & (past_len - w_buf + jw >= 0)
        s_w = _dot_nt(q, kw_ref[g]) * SCALE + sbw_ref[g]
        p_w = _masked_softmax(s_w, valid_w)
        ow_ref[g] = _dot(p_w.astype(BF16), vw_ref[g])


def _nsa_select(q_nsa, kcr, vcr, new_c, wk0, wv0, bk, bv, gk0, sbc, kw, vw, sbw, past_len, w_buf):
    dec_b, _, nb0, _ = kcr.shape
    nbp = sbc.shape[-1]
    wp = sbw.shape[-1]
    assert nb0 + 1 >= TOP_N and nbp >= nb0 + 8 and nb0 % 8 == 0
    body = functools.partial(_nsa_sel_body, nb0=nb0, nbp=nbp, past_len=past_len, w_buf=w_buf, wp=wp)

    def per_b(shape):
        nd = len(shape)
        return pl.BlockSpec((None,) + tuple(shape), lambda b: (b,) + (0,) * nd)

    def whole(arr):
        nd = arr.ndim
        return pl.BlockSpec(arr.shape, lambda b: (0,) * nd)

    vec = (G_NSA, ROWS_S, HEAD_DIM)
    return pl.pallas_call(
        body,
        grid=(dec_b,),
        in_specs=[per_b(vec), per_b((G_NSA, nb0, HEAD_DIM)), per_b((G_NSA, nb0, HEAD_DIM)),
                  per_b((8, HEAD_DIM)), whole(wk0), whole(wv0), whole(bk), whole(bv), whole(gk0),
                  whole(sbc), per_b((G_NSA, wp, HEAD_DIM)), per_b((G_NSA, wp, HEAD_DIM)), whole(sbw)],
        out_specs=[per_b((G_NSA, TOP_N, LANE)), per_b(vec), per_b(vec)],
        out_shape=[jax.ShapeDtypeStruct((dec_b, G_NSA, TOP_N, LANE), I32),
                   jax.ShapeDtypeStruct((dec_b,) + vec, F32),
                   jax.ShapeDtypeStruct((dec_b,) + vec, F32)],
        compiler_params=_cparams(1),
        name="nsa_select",
    )(q_nsa, kcr, vcr, new_c, wk0, wv0, bk, bv, gk0, sbc, kw, vw, sbw)


GATHER_BLOCKS = 8


def _nsa_gather_body(pt_ref, idx_ref, rb_ref, q_ref, *refs, nb0, past_len):
    n_pages = G_NSA * GATHER_BLOCKS
    page_refs = refs[:n_pages]
    new_ref, gs_ref, oc_ref, ow_ref, o_ref, m_ref, l_ref, a_ref = refs[n_pages:]
    b = pl.program_id(0)
    k = pl.program_id(1)
    pos_q = past_len
    keys = GATHER_BLOCKS * CMP_BLOCK
    row = lax.broadcasted_iota(I32, (CMP_BLOCK, HEAD_DIM), 0)
    l1 = lax.broadcasted_iota(I32, (1, CMP_BLOCK), 1)
    hrow = lax.broadcasted_iota(I32, (ROWS_S, keys), 0)
    for g in range(G_NSA):
        new_k = jnp.broadcast_to(new_ref[g:g + 1, :], (CMP_BLOCK, HEAD_DIM))
        new_v = jnp.broadcast_to(new_ref[G_NSA + g:G_NSA + g + 1, :], (CMP_BLOCK, HEAD_DIM))
        ks, vs, dist = [], [], []
        for kk in range(GATHER_BLOCKS):
            blk = idx_ref[(b * G_NSA + g) * TOP_N + k * GATHER_BLOCKS + kk]
            is_new = blk >= nb0
            page_ref = page_refs[g * GATHER_BLOCKS + kk]
            k_blk = page_ref[pl.ds(2 * G_NSA + g, CMP_BLOCK, stride=NSA_CH), :]
            v_blk = page_ref[pl.ds(3 * G_NSA + g, CMP_BLOCK, stride=NSA_CH), :]
            ks.append(jnp.where(is_new, jnp.where(row == 0, new_k, 0.0), k_blk).astype(BF16))
            vs.append(jnp.where(is_new, jnp.where(row == 0, new_v, 0.0), v_blk).astype(BF16))
            dist.append(pos_q - (blk * CMP_BLOCK + l1))
        ks = jnp.concatenate(ks, axis=0)
        vs = jnp.concatenate(vs, axis=0)
        dist = jnp.concatenate(dist, axis=1)
        bucket = jnp.broadcast_to(_t5_bucket(dist), (ROWS_S, keys))
        bias = jnp.zeros((ROWS_S, keys), F32)
        for hh in range(HPG):
            for bk in range(N_BUCKETS):
                bias = jnp.where((hrow == hh) & (bucket == bk), rb_ref[bk, g * HPG + hh], bias)
        s = _dot_nt(q_ref[g], ks) * SCALE + bias
        logit = jnp.where(jnp.broadcast_to(dist >= 0, (ROWS_S, keys)), s, NEG_INF)
        fl = _Flash(m_ref.at[g], l_ref.at[g], a_ref.at[g])

        @pl.when(k == 0)
        def _():
            fl.first(logit, vs)

        @pl.when(k > 0)
        def _():
            fl.update(logit, vs)

        @pl.when(k == pl.num_programs(1) - 1)
        def _():
            o_ref[g] = gs_ref[g, 0] * oc_ref[g] + gs_ref[g, 1] * fl.result() + gs_ref[g, 2] * ow_ref[g]


def _nsa_gather(l, pt_flat, idx_flat, rel_bias, q_nsa, cache_nsa2d, new_s, gs, oc, ow, nb0,
                past_len, page):
    dec_b = q_nsa.shape[0]
    npg = pt_flat.shape[0] // dec_b
    halves = page // CMP_BLOCK
    body = functools.partial(_nsa_gather_body, nb0=nb0, past_len=past_len)

    def page_spec(g, kk):
        def imap(b, k, pt, idx):
            blk = jnp.minimum(idx[(b * G_NSA + g) * TOP_N + k * GATHER_BLOCKS + kk], nb0 - 1)
            return (l, pt[b * npg + _div(blk, halves)], _mod(blk, halves), 0)
        return pl.BlockSpec((None, None, CMP_BLOCK * NSA_CH, HEAD_DIM), imap)

    def per_b(shape):
        nd = len(shape)
        return pl.BlockSpec((None,) + tuple(shape), lambda b, k, pt, idx: (b,) + (0,) * nd)

    vec = (G_NSA, ROWS_S, HEAD_DIM)
    assert TOP_N % GATHER_BLOCKS == 0
    n_pages = G_NSA * GATHER_BLOCKS
    grid_spec = pltpu.PrefetchScalarGridSpec(
        num_scalar_prefetch=2,
        grid=(dec_b, TOP_N // GATHER_BLOCKS),
        in_specs=[pl.BlockSpec(memory_space=pltpu.SMEM), per_b(vec)]
        + [page_spec(g, kk) for g in range(G_NSA) for kk in range(GATHER_BLOCKS)]
        + [per_b((8, HEAD_DIM)), per_b((G_NSA, 3, ROWS_S, HEAD_DIM)), per_b(vec), per_b(vec)],
        out_specs=per_b(vec),
        scratch_shapes=[pltpu.VMEM((G_NSA, ROWS_S, 1), F32), pltpu.VMEM((G_NSA, ROWS_S, 1), F32),
                        pltpu.VMEM((G_NSA, ROWS_S, HEAD_DIM), F32)],
    )
    return pl.pallas_call(
        body,
        grid_spec=grid_spec,
        out_shape=jax.ShapeDtypeStruct((dec_b,) + vec, F32),
        compiler_params=_cparams(2),
        name="nsa_gather",
    )(pt_flat, idx_flat, rel_bias, q_nsa, *([cache_nsa2d] * n_pages), new_s, gs, oc, ow)


def _pad_rows(x, rows):
    return jnp.pad(x, ((0, rows - x.shape[0]),) + ((0, 0),) * (x.ndim - 1))


def _sample_inproj(xs, wt, p, dec_b):
    rows = xs.shape[0]
    zb, sb_rows, nsa_rows, win_rows, gates = _inproj(
        xs, p[0], wt["w_a"], wt["w_b"], wt["w_gate"], wt["gain"], rows, rows)
    zb8 = zb[:dec_b]
    q_sb = zb8[:, QS_BLK * LANE:KSB_BLK * LANE].reshape(dec_b, H_SB, HEAD_DIM)
    q_sb = jnp.pad(q_sb, ((0, 0), (0, ROWS_S - H_SB), (0, 0)))
    keep = lambda a: a[:dec_b * (a.shape[0] // rows)]
    return q_sb, (zb8, keep(sb_rows), keep(nsa_rows), keep(win_rows), gates)


def _sample_layer(l, xs, projected, walked, wt, p, sb_tabs, rel_bias, pt_flat, cache_nsa2d, win_l,
                  w_cmp0, dec_b, past_len, page):
    na, bk, bv, gk0, go, nf = p
    sbc, sbw = sb_tabs
    w_buf = win_l.shape[1]
    rows = xs.shape[0]
    zb8, sb_rows, nsa_rows, win_rows, gates = projected
    o_sb, kcr, vcr = walked
    o_sb = o_sb.reshape(dec_b, D_SB)

    q_nsa = zb8[:, :D_NSA].reshape(dec_b, G_NSA, HPG, HEAD_DIM)
    q_nsa = jnp.pad(q_nsa, ((0, 0), (0, 0), (0, ROWS_S - HPG), (0, 0)))
    nsa_new = nsa_rows.reshape(dec_b, 4 * G_NSA, HEAD_DIM)
    new_c = jnp.pad(nsa_new[:, :2 * G_NSA], ((0, 0), (0, 8 - 2 * G_NSA), (0, 0)))
    new_s = jnp.pad(nsa_new[:, 2 * G_NSA:], ((0, 0), (0, 8 - 2 * G_NSA), (0, 0)))
    new_w = win_rows.reshape(dec_b, 1, 2, G_NSA, HEAD_DIM)
    win_all = jnp.concatenate([win_l, new_w], axis=1)
    wp = sbw.shape[-1]
    win_t = jnp.transpose(win_all, (2, 0, 3, 1, 4))
    win_t = jnp.pad(win_t, ((0, 0), (0, 0), (0, 0), (0, wp - w_buf - 1), (0, 0))).astype(BF16)
    idx, oc, ow = _nsa_select(q_nsa, kcr, vcr, new_c, w_cmp0[0], w_cmp0[1], bk, bv, gk0, sbc,
                              win_t[0], win_t[1], sbw, past_len, w_buf)
    gs = gates[:dec_b].reshape(dec_b, G_NSA, LANE)[:, :, :3 * HPG].reshape(dec_b, G_NSA, 3, HPG)
    gs = jnp.pad(gs, ((0, 0), (0, 0), (0, 0), (0, ROWS_S - HPG)))
    gs = jnp.broadcast_to(gs[..., None], gs.shape + (HEAD_DIM,))
    o_nsa = _nsa_gather(l, pt_flat, idx[:, :, :, 0].reshape(-1), rel_bias, q_nsa, cache_nsa2d,
                        new_s, gs, oc, ow, kcr.shape[2], past_len, page)
    o_nsa = o_nsa[:, :, :HPG].reshape(dec_b, D_NSA)

    x1 = _outproj(_pad_rows(o_nsa, rows), _pad_rows(o_sb, rows), go, wt["w_out"], xs)
    x2 = _ffn(x1, nf, wt["w_gate_ffn"], wt["w_up"], wt["w_down"])
    new_win = win_all[:, 1:]
    return x2, sb_rows, nsa_rows, new_win


def kernel(x_prompt, x_sample, cache_sb_kv, cache_nsa_kv, state_win_kv, page_table, rel_bias,
           norm_attn, w_in, g_q, g_k, w_cmp_k, b_cmp_k, w_cmp_v, b_cmp_v, g_out, w_out,
           norm_ffn, w_gate, w_up, w_down):
    batch, seq, _ = x_prompt.shape
    dec_b, dec_t, _ = x_sample.shape
    assert dec_t == 1
    depth, n_pool, page = cache_sb_kv.shape[:3]
    npg = page_table.shape[1]
    past_len = npg * page
    w_buf = state_win_kv.shape[2]
    nb_s = past_len // CMP_BLOCK + 1
    nbp_s = -(-(nb_s + 7) // LANE) * LANE
    wp_s = -(-(w_buf + 1) // LANE) * LANE

    toep, bias_c, sc, sw = _bias_tables(rel_bias, seq, past_len, w_buf, nbp_s, wp_s)
    sbc = jnp.pad(sc.reshape(G_NSA, HPG, nbp_s), ((0, 0), (0, ROWS_S - HPG), (0, 0)))
    sbw = jnp.pad(sw.reshape(G_NSA, HPG, wp_s), ((0, 0), (0, ROWS_S - HPG), (0, 0)))
    pt_flat = page_table.reshape(-1).astype(I32)
    cache_nsa2d = cache_nsa_kv.reshape(depth, n_pool, page * NSA_CH, HEAD_DIM)

    xp = x_prompt.reshape(batch * seq, D_MODEL)
    xs = _pad_rows(x_sample.reshape(dec_b, D_MODEL), ROWS_S)
    sb_s, nsa_s, win_s = [], [], []
    w_keep = min(WINDOW, seq)
    rows_p = tuple(jnp.zeros((depth * n * per, LANE), F32) for n, per in
                   ((batch * seq, 2 * H_SB), (batch * seq, 4 * G_NSA), (batch * w_keep, 2 * G_NSA)))
    for l in range(depth):
        wt = _prep_weights(l, w_in, g_q, g_k, w_cmp_k, w_cmp_v)
        p = (norm_attn[l][None], b_cmp_k[l][None], b_cmp_v[l][None], g_k[l, 0][None],
             g_out[l][None], norm_ffn[l][None])
        q_sb, projected = _sample_inproj(xs, wt, p, dec_b)
        xp, *rows_p, walked = _prompt_layer(
            xp, l, wt, (w_out, w_gate, w_up, w_down), p, (toep, bias_c), rel_bias, batch, seq,
            (l, pt_flat, q_sb, cache_sb_kv), (l, pt_flat, cache_nsa2d, wt["w_kv"], dec_b, page),
            (depth, tuple(rows_p)))

        w_cmp0 = (w_cmp_k[l, 0].astype(BF16), w_cmp_v[l, 0].astype(BF16))
        xs, sb_rows, nsa_rows, new_win = _sample_layer(
            l, xs, projected, walked, wt, p, (sbc, sbw), rel_bias, pt_flat, cache_nsa2d,
            state_win_kv[l], w_cmp0, dec_b, past_len, page)
        sb_s.append(sb_rows.reshape(dec_b, 1, 2, H_SB, HEAD_DIM))
        nsa_s.append(nsa_rows.reshape(dec_b, 1, 4, G_NSA, HEAD_DIM))
        win_s.append(new_win)
    sb_p, nsa_p, win_p = rows_p
    return (xp.reshape(batch, seq, D_MODEL), xs[:dec_b].reshape(dec_b, 1, D_MODEL),
            sb_p.reshape(depth, batch, seq, 2, H_SB, HEAD_DIM), jnp.stack(sb_s),
            nsa_p.reshape(depth, batch, seq, 4, G_NSA, HEAD_DIM), jnp.stack(nsa_s),
            win_p.reshape(depth, batch, w_keep, 2, G_NSA, HEAD_DIM), jnp.stack(win_s))
```
